```python
import math
import numpy as np
import jax
import jax.numpy as jnp
from jax import lax

D_MODEL = 1024
BATCH = 32
SEQ = 2048
DEPTH = 2

N_MEM = 256
HEAD_DIM = 64
MAIN_W = 3 * D_MODEL // 4
N_HEADS = MAIN_W // HEAD_DIM
MEM_HEADS = 4
MEM_W = D_MODEL - MAIN_W
MEM_HEAD_DIM = MEM_W // MEM_HEADS
MIX_W = MAIN_W + MEM_W

NSA_KV_HEADS = 1
NSA_KV_W = NSA_KV_HEADS * HEAD_DIM
CMP_LEN = 32
CMP_STRIDE = 16
CMP_HIDDEN = 4 * HEAD_DIM
SLC_BLOCK = 64
SLC_TOPK = 16
WINDOW = 512
NSA_Q_CHUNK = 64
FORCE_SCORE = 1.0e4

MOBA_BLOCK = 256
MOBA_TOPK = 3
MOBA_Q_CHUNK = 8

REL_BUCKETS = 32
REL_MAX_DIST = 128

D_FF = 2816
CONV_W = 3

N_A = DEPTH // 2
N_B = DEPTH - N_A
ALPHA = (2.0 * DEPTH) ** 0.25
BETA = (8.0 * DEPTH) ** -0.25
LN_EPS = 1e-5
NEG_INF = -1e30
TINY = 1e-30

A_IN_SIZES = (MAIN_W, NSA_KV_W, NSA_KV_W, NSA_KV_W, NSA_KV_W, NSA_KV_W, NSA_KV_W, 3 * N_HEADS, MEM_W)
A_IN = sum(A_IN_SIZES)
B_IN_SIZES = (MAIN_W, MEM_W)
B_IN = sum(B_IN_SIZES)

kernel_name = 'yoco_nsa_moba_hybrid'


def split_cols(h, sizes):
    return jnp.split(h, np.cumsum(sizes)[:-1].tolist(), axis=-1)


def layer_norm(x, g, b):
    xf = x.astype(jnp.float32)
    mu = xf.mean(-1, keepdims=True)
    var = jnp.square(xf - mu).mean(-1, keepdims=True)
    return ((xf - mu) * lax.rsqrt(var + LN_EPS) * g + b).astype(x.dtype)


def rel_bucket(dist):
    n = jnp.maximum(dist, 0)
    max_exact = REL_BUCKETS // 2
    nf = jnp.maximum(n, 1).astype(jnp.float32)
    large = max_exact + (jnp.log(nf / max_exact) / math.log(REL_MAX_DIST / max_exact)
                         * (REL_BUCKETS - max_exact)).astype(jnp.int32)
    large = jnp.minimum(large, REL_BUCKETS - 1)
    return jnp.where(n < max_exact, n, large)


def masked_softmax(logits, mask):
    logits = jnp.where(mask, logits, NEG_INF)
    m = jnp.max(logits, -1, keepdims=True)
    p = jnp.where(mask, jnp.exp(logits - m), 0.0)
    return p / jnp.maximum(p.sum(-1, keepdims=True), TINY)


def cmp_slc_overlap(n_cmp, n_slc):
    start = np.arange(n_cmp) * CMP_STRIDE
    end = start + CMP_LEN
    bs = np.arange(n_slc) * SLC_BLOCK
    m = (start[:, None] < bs[None, :] + SLC_BLOCK) & (end[:, None] > bs[None, :])
    return m.astype(np.float32)


def nsa_compress(k, pe, w1, w2):
    B, S, G, dh = k.shape
    n_cmp = (S - CMP_LEN) // CMP_STRIDE + 1
    idx = (np.arange(n_cmp, dtype=np.int32)[:, None] * CMP_STRIDE
           + np.arange(CMP_LEN, dtype=np.int32)[None, :])
    blocks = k[:, idx] + pe[:, None, :]
    blocks = blocks.transpose(0, 1, 3, 2, 4).reshape(B, n_cmp, G, CMP_LEN * dh)
    return jax.nn.gelu(blocks @ w1) @ w2


def nsa_attention(q, k_cmp, v_cmp, k_slc, v_slc, k_win, v_win, gates, rel_bias):
    B, S, H, dh = q.shape
    G = k_slc.shape[2]
    hpg = H // G
    n_cmp = k_cmp.shape[1]
    n_slc = S // SLC_BLOCK
    top = min(SLC_TOPK, n_slc)
    n_sel = top * SLC_BLOCK
    QC = NSA_Q_CHUNK
    KW = WINDOW + QC
    scale = dh ** -0.5

    cmp_end = jnp.asarray(np.arange(n_cmp) * CMP_STRIDE + CMP_LEN - 1, jnp.int32)
    overlap = jnp.asarray(cmp_slc_overlap(n_cmp, n_slc))
    blk_ids = jnp.arange(n_slc, dtype=jnp.int32)
    offs = jnp.arange(SLC_BLOCK, dtype=jnp.int32)
    k_blk = k_slc.reshape(B, n_slc, SLC_BLOCK, G, dh).transpose(0, 3, 1, 2, 4).reshape(B, G, n_slc, SLC_BLOCK * dh)
    v_blk = v_slc.reshape(B, n_slc, SLC_BLOCK, G, dh).transpose(0, 3, 1, 2, 4).reshape(B, G, n_slc, SLC_BLOCK * dh)
    k_win_pad = jnp.pad(k_win, ((0, 0), (WINDOW, 0), (0, 0), (0, 0)))
    v_win_pad = jnp.pad(v_win, ((0, 0), (WINDOW, 0), (0, 0), (0, 0)))
    table_g = rel_bias.reshape(REL_BUCKETS, G, hpg)
    g_idx = jnp.arange(G)[None, None, :, None]

    def head_bias(bucket):
        return rel_bias[bucket].reshape(bucket.shape + (G, hpg)).transpose(0, 2, 3, 1)

    def chunk(c):
        t0 = c * QC
        t = t0 + jnp.arange(QC, dtype=jnp.int32)
        qc = lax.dynamic_slice_in_dim(q, t0, QC, axis=1).reshape(B, QC, G, hpg, dh)
        gc = lax.dynamic_slice_in_dim(gates, t0, QC, axis=1).reshape(B, QC, G, hpg, 3)

        dist_c = t[:, None] - cmp_end[None, :]
        lg = jnp.einsum('bqgjd,bngd->bqgjn', qc, k_cmp).astype(jnp.float32) * scale + head_bias(rel_bucket(dist_c))
        p_c = masked_softmax(lg, (dist_c >= 0)[None, :, None, None, :])
        o_c = jnp.einsum('bqgjn,bngd->bqgjd', p_c, v_cmp)

        imp = jnp.einsum('bqgjn,ns->bqgs', p_c, overlap)
        cur = t // SLC_BLOCK
        bid = blk_ids[None, :]
        eligible = bid <= cur[:, None]
        forced = (bid == 0) | (bid == cur[:, None]) | (bid == cur[:, None] - 1)
        score = jnp.where(eligible, jnp.where(forced, FORCE_SCORE, 0.0), NEG_INF)[None, :, None, :] + imp
        _, sel = lax.top_k(score, top)
        idx = sel.transpose(0, 2, 1, 3).reshape(B, G, QC * top, 1)
        ks = jnp.take_along_axis(k_blk, idx, axis=2).reshape(B, G, QC, n_sel, dh)
        vs = jnp.take_along_axis(v_blk, idx, axis=2).reshape(B, G, QC, n_sel, dh)
        key_pos = (sel[..., None] * SLC_BLOCK + offs).reshape(B, QC, G, n_sel)
        dist_s = t[None, :, None, None] - key_pos
        bias_s = jnp.swapaxes(table_g[rel_bucket(dist_s), g_idx], -1, -2)
        lg = jnp.einsum('bqgjd,bgqkd->bqgjk', qc, ks).astype(jnp.float32) * scale + bias_s
        p_s = masked_softmax(lg, (dist_s >= 0)[:, :, :, None, :])
        o_s = jnp.einsum('bqgjk,bgqkd->bqgjd', p_s, vs)

        kw = lax.dynamic_slice_in_dim(k_win_pad, t0, KW, axis=1)
        vw = lax.dynamic_slice_in_dim(v_win_pad, t0, KW, axis=1)
        pos_w = t0 - WINDOW + jnp.arange(KW, dtype=jnp.int32)
        dist_w = t[:, None] - pos_w[None, :]
        mask_w = (pos_w[None, :] >= 0) & (dist_w >= 0) & (dist_w < WINDOW)
        lg = jnp.einsum('bqgjd,bkgd->bqgjk', qc, kw).astype(jnp.float32) * scale + head_bias(rel_bucket(dist_w))
        p_w = masked_softmax(lg, mask_w[None, :, None, None, :])
        o_w = jnp.einsum('bqgjk,bkgd->bqgjd', p_w, vw)

        o = gc[..., 0:1] * o_c + gc[..., 1:2] * o_s + gc[..., 2:3] * o_w
        return o.reshape(B, QC, H * dh).astype(q.dtype)

    out = lax.map(chunk, jnp.arange(S // QC, dtype=jnp.int32))
    return out.transpose(1, 0, 2, 3).reshape(B, S, H * dh)


def moba_shared_kv(x, w_kv):
    B, S, _ = x.shape
    k, v = jnp.split(x @ w_kv, 2, axis=-1)
    nb = -(-S // MOBA_BLOCK)
    pad = nb * MOBA_BLOCK - S
    k = jnp.pad(k.reshape(B, S, N_HEADS, HEAD_DIM), ((0, 0), (0, pad), (0, 0), (0, 0)))
    v = jnp.pad(v.reshape(B, S, N_HEADS, HEAD_DIM), ((0, 0), (0, pad), (0, 0), (0, 0)))
    k_blocks = k.reshape(B, nb, MOBA_BLOCK, N_HEADS, HEAD_DIM)
    k_mean = k_blocks.astype(jnp.float32).mean(2).astype(k.dtype)
    k_blk = k_blocks.transpose(0, 3, 1, 2, 4).reshape(B, N_HEADS, nb, MOBA_BLOCK * HEAD_DIM)
    v_blk = v.reshape(B, nb, MOBA_BLOCK, N_HEADS, HEAD_DIM).transpose(0, 3, 1, 2, 4).reshape(B, N_HEADS, nb, MOBA_BLOCK * HEAD_DIM)
    return k_mean, k_blk, v_blk


def moba_attention(q, k_mean, k_blk, v_blk, rel_bias):
    B, S, H, dh = q.shape
    nb = k_mean.shape[1]
    top = min(MOBA_TOPK, nb - 1)
    n_p = top * MOBA_BLOCK
    QC = MOBA_Q_CHUNK
    scale = dh ** -0.5
    blk_ids = jnp.arange(nb, dtype=jnp.int32)
    offs = jnp.arange(MOBA_BLOCK, dtype=jnp.int32)
    h_idx = jnp.arange(H)[None, :, None, None]

    def chunk(c):
        t0 = c * QC
        t = t0 + jnp.arange(QC, dtype=jnp.int32)
        cb = t0 // MOBA_BLOCK
        qc = lax.dynamic_slice_in_dim(q, t0, QC, axis=1)
        ko = lax.dynamic_index_in_dim(k_blk, cb, axis=2, keepdims=False).reshape(B, H, MOBA_BLOCK, dh)
        vo = lax.dynamic_index_in_dim(v_blk, cb, axis=2, keepdims=False).reshape(B, H, MOBA_BLOCK, dh)
        dist_o = t[:, None] - (cb * MOBA_BLOCK + offs)[None, :]
        lg_o = (jnp.einsum('bqhd,bhkd->bhqk', qc, ko).astype(jnp.float32) * scale
                + rel_bias[rel_bucket(dist_o)].transpose(2, 0, 1))
        mask_o = jnp.broadcast_to(dist_o >= 0, lg_o.shape)
        if top == 0:
            p = masked_softmax(lg_o, mask_o)
            o = jnp.einsum('bhqk,bhkd->bqhd', p, vo)
        else:
            gate = jnp.einsum('bqhd,bnhd->bhqn', qc, k_mean).astype(jnp.float32)
            gate = jnp.where(blk_ids < cb, gate, NEG_INF)
            _, sel = lax.top_k(gate, top)
            idx = sel.reshape(B, H, QC * top, 1)
            kp = jnp.take_along_axis(k_blk, idx, axis=2).reshape(B, H, QC, n_p, dh)
            vp = jnp.take_along_axis(v_blk, idx, axis=2).reshape(B, H, QC, n_p, dh)
            pos_p = (sel[..., None] * MOBA_BLOCK + offs).reshape(B, H, QC, n_p)
            lg_p = (jnp.einsum('bqhd,bhqkd->bhqk', qc, kp).astype(jnp.float32) * scale
                    + rel_bias[rel_bucket(t[None, None, :, None] - pos_p), h_idx])
            mask_p = pos_p < cb * MOBA_BLOCK
            p = masked_softmax(jnp.concatenate([lg_p, lg_o], -1), jnp.concatenate([mask_p, mask_o], -1))
            o = (jnp.einsum('bhqk,bhqkd->bqhd', p[..., :n_p], vp)
                 + jnp.einsum('bhqk,bhkd->bqhd', p[..., n_p:], vo))
        return o.reshape(B, QC, H * dh).astype(q.dtype)

    out = lax.map(chunk, jnp.arange(S // QC, dtype=jnp.int32))
    return out.transpose(1, 0, 2, 3).reshape(B, S, H * dh)


def memory_attention(qm, mem, w_mem_kv):
    B, M, _ = mem.shape
    km, vm = jnp.split(mem @ w_mem_kv, 2, axis=-1)
    km = km.reshape(B, M, MEM_HEADS, MEM_HEAD_DIM)
    vm = vm.reshape(B, M, MEM_HEADS, MEM_HEAD_DIM)
    lg = jnp.einsum('bshd,bmhd->bhsm', qm, km).astype(jnp.float32) * (MEM_HEAD_DIM ** -0.5)
    p = jax.nn.softmax(lg, axis=-1)
    o = jnp.einsum('bhsm,bmhd->bshd', p, vm)
    return o.reshape(qm.shape[0], qm.shape[1], MEM_W).astype(qm.dtype)


def conv_ffn(x, w_in, conv_w, conv_b, w_out):
    S = x.shape[1]
    a, b = jnp.split(x @ w_in, 2, axis=-1)
    a_pad = jnp.pad(a, ((0, 0), (CONV_W - 1, 0), (0, 0)))
    a = sum(conv_w[k] * a_pad[:, k:k + S] for k in range(CONV_W)) + conv_b
    return (jax.nn.gelu(a) * b) @ w_out


def setup_inputs(seed: int = 0) -> dict:
    key = jax.random.key(seed)
    keys = iter(jax.random.split(key, 32))

    def nrm(shape, scale):
        return jax.random.normal(next(keys), shape, jnp.float32) * scale

    L = CMP_LEN * HEAD_DIM
    return {
        'x': nrm((BATCH, SEQ, D_MODEL), 1.0),
        'mem': nrm((BATCH, N_MEM, D_MODEL), 1.0),
        'rel_bias': nrm((REL_BUCKETS, N_HEADS), 0.5),
        'a_w_in': nrm((N_A, D_MODEL, A_IN), D_MODEL ** -0.5),
        'a_cmp_pe_k': nrm((N_A, CMP_LEN, HEAD_DIM), 0.1),
        'a_cmp_w1_k': nrm((N_A, L, CMP_HIDDEN), L ** -0.5),
        'a_cmp_w2_k': nrm((N_A, CMP_HIDDEN, HEAD_DIM), CMP_HIDDEN ** -0.5),
        'a_cmp_pe_v': nrm((N_A, CMP_LEN, HEAD_DIM), 0.1),
        'a_cmp_w1_v': nrm((N_A, L, CMP_HIDDEN), L ** -0.5),
        'a_cmp_w2_v': nrm((N_A, CMP_HIDDEN, HEAD_DIM), CMP_HIDDEN ** -0.5),
        'a_w_mem_kv': nrm((N_A, D_MODEL, 2 * MEM_W), D_MODEL ** -0.5),
        'a_w_out': nrm((N_A, MIX_W, D_MODEL), BETA * MIX_W ** -0.5),
        'shared_w_kv': nrm((D_MODEL, 2 * MAIN_W), D_MODEL ** -0.5),
        'b_w_in': nrm((N_B, D_MODEL, B_IN), D_MODEL ** -0.5),
        'b_w_mem_kv': nrm((N_B, D_MODEL, 2 * MEM_W), D_MODEL ** -0.5),
        'b_w_out': nrm((N_B, MIX_W, D_MODEL), BETA * MIX_W ** -0.5),
        'ln1_g': 1.0 + nrm((DEPTH, D_MODEL), 0.02),
        'ln1_b': nrm((DEPTH, D_MODEL), 0.02),
        'ln2_g': 1.0 + nrm((DEPTH, D_MODEL), 0.02),
        'ln2_b': nrm((DEPTH, D_MODEL), 0.02),
        'ffn_w_in': nrm((DEPTH, D_MODEL, 2 * D_FF), D_MODEL ** -0.5),
        'ffn_conv_w': nrm((DEPTH, CONV_W, D_FF), CONV_W ** -0.5),
        'ffn_conv_b': nrm((DEPTH, D_FF), 0.02),
        'ffn_w_out': nrm((DEPTH, D_FF, D_MODEL), BETA * D_FF ** -0.5),
    }


def reference(x, mem, rel_bias, a_w_in, a_cmp_pe_k, a_cmp_w1_k, a_cmp_w2_k, a_cmp_pe_v, a_cmp_w1_v,
              a_cmp_w2_v, a_w_mem_kv, a_w_out, shared_w_kv, b_w_in, b_w_mem_kv, b_w_out,
              ln1_g, ln1_b, ln2_g, ln2_b, ffn_w_in, ffn_conv_w, ffn_conv_b, ffn_w_out):
    B, S, _ = x.shape

    def heads(z):
        return z.reshape(B, S, -1, HEAD_DIM)

    k_mean = k_blk = v_blk = None
    for layer in range(DEPTH):
        if layer < N_A:
            i = layer
            q, kc, vc, ks, vs, kw, vw, g, qm = split_cols(x @ a_w_in[i], A_IN_SIZES)
            k_cmp = nsa_compress(heads(kc), a_cmp_pe_k[i], a_cmp_w1_k[i], a_cmp_w2_k[i])
            v_cmp = nsa_compress(heads(vc), a_cmp_pe_v[i], a_cmp_w1_v[i], a_cmp_w2_v[i])
            gates = jax.nn.sigmoid(g).reshape(B, S, N_HEADS, 3)
            o_main = nsa_attention(heads(q), k_cmp, v_cmp, heads(ks), heads(vs), heads(kw), heads(vw),
                                   gates, rel_bias)
            w_mem_kv, w_out = a_w_mem_kv[i], a_w_out[i]
        else:
            if layer == N_A:
                k_mean, k_blk, v_blk = moba_shared_kv(x, shared_w_kv)
            i = layer - N_A
            q, qm = split_cols(x @ b_w_in[i], B_IN_SIZES)
            o_main = moba_attention(heads(q), k_mean, k_blk, v_blk, rel_bias)
            w_mem_kv, w_out = b_w_mem_kv[i], b_w_out[i]
        o_mem = memory_attention(qm.reshape(B, S, MEM_HEADS, MEM_HEAD_DIM), mem, w_mem_kv)
        mix = jnp.concatenate([o_main, o_mem], axis=-1) @ w_out
        x = layer_norm(ALPHA * x + mix, ln1_g[layer], ln1_b[layer])
        ffn = conv_ffn(x, ffn_w_in[layer], ffn_conv_w[layer], ffn_conv_b[layer], ffn_w_out[layer])
        x = layer_norm(ALPHA * x + ffn, ln2_g[layer], ln2_b[layer])
    return x
```

```python
import functools
import math

import numpy as np
import jax
import jax.numpy as jnp
from jax import lax
from jax.experimental import pallas as pl
from jax.experimental.pallas import tpu as pltpu

F32 = jnp.float32
BF16 = jnp.bfloat16

D_MODEL = 1024
HEAD_DIM = 64
N_HEADS = 12
N_PAIRS = N_HEADS // 2
MAIN_W = N_HEADS * HEAD_DIM
MEM_HEADS = 4
MEM_W = MEM_HEADS * HEAD_DIM
CMP_LEN = 32
CMP_STRIDE = 16
CMP_HIDDEN = 256
SLC_BLOCK = 64
SLC_TOPK = 16
WINDOW = 512
FORCE_SCORE = 1.0e4
MOBA_BLOCK = 256
MOBA_TOPK = 3
REL_BUCKETS = 32
REL_MAX_DIST = 128
D_FF = 2816
CONV_W = 3
DEPTH = 2
ALPHA = (2.0 * DEPTH) ** 0.25
LN_EPS = 1e-5
NEG_INF = -1e30
TINY = 1e-30
SCALE = HEAD_DIM ** -0.5

LANES = 128
TQ = 256
VMEM_LIMIT = 56 * 1024 * 1024


def _dot(a, b):
    return jnp.dot(a, b, preferred_element_type=F32)


def _dot_nt(a, b):
    return lax.dot_general(a, b, (((1,), (1,)), ((), ())), preferred_element_type=F32)


def _params(sem, vmem=None):
    return pltpu.CompilerParams(dimension_semantics=sem, vmem_limit_bytes=vmem)


def _proj_kernel(x_ref, *refs):
    n = len(refs) // 2
    xb = x_ref[...].astype(BF16)
    for w_ref, o_ref in zip(refs[:n], refs[n:]):
        o_ref[...] = _dot(xb, w_ref[...]).astype(o_ref.dtype)


def _proj(x, weights, dtypes, tm, name):
    m, k = x.shape
    in_specs = [pl.BlockSpec((tm, k), lambda i: (i, 0))]
    in_specs += [pl.BlockSpec(w.shape, lambda i: (0, 0)) for w in weights]
    out_specs = [pl.BlockSpec((tm, w.shape[1]), lambda i: (i, 0)) for w in weights]
    out_shape = [jax.ShapeDtypeStruct((m, w.shape[1]), dt) for w, dt in zip(weights, dtypes)]
    return pl.pallas_call(
        _proj_kernel, grid=(m // tm,), in_specs=in_specs, out_specs=out_specs, out_shape=out_shape,
        compiler_params=_params(("arbitrary",), VMEM_LIMIT), name=name)(x, *weights)


def _projb_kernel(x_ref, wq_ref, wm_ref, wk_ref, wv_ref, q_ref, qm_ref, k_ref, v_ref, km_ref):
    xb = x_ref[...].astype(BF16)
    q_ref[...] = _dot(xb, wq_ref[...]).astype(BF16)
    qm_ref[...] = _dot(xb, wm_ref[...]).astype(BF16)
    k = _dot(xb, wk_ref[...])
    k_ref[...] = k.astype(BF16)
    km_ref[...] = jnp.mean(k, axis=0, keepdims=True)
    v_ref[...] = _dot(xb, wv_ref[...]).astype(BF16)


def _proj_b(x, wq, wm, wk, wv):
    m, k = x.shape
    nblk = m // MOBA_BLOCK
    row = lambda i: (i, 0)
    full = lambda i: (0, 0)
    return pl.pallas_call(
        _projb_kernel, grid=(nblk,),
        in_specs=[pl.BlockSpec((MOBA_BLOCK, k), row)] + [pl.BlockSpec(w.shape, full) for w in (wq, wm, wk, wv)],
        out_specs=[pl.BlockSpec((MOBA_BLOCK, MAIN_W), row), pl.BlockSpec((MOBA_BLOCK, MEM_W), row),
                   pl.BlockSpec((MOBA_BLOCK, MAIN_W), row), pl.BlockSpec((MOBA_BLOCK, MAIN_W), row),
                   pl.BlockSpec((None, 1, MAIN_W), lambda i: (i, 0, 0))],
        out_shape=[jax.ShapeDtypeStruct((m, MAIN_W), BF16), jax.ShapeDtypeStruct((m, MEM_W), BF16),
                   jax.ShapeDtypeStruct((m, MAIN_W), BF16), jax.ShapeDtypeStruct((m, MAIN_W), BF16),
                   jax.ShapeDtypeStruct((nblk, 1, MAIN_W), F32)],
        compiler_params=_params(("arbitrary",), VMEM_LIMIT), name="proj_b")(x, wq, wm, wk, wv)


def _bias_kernel(tbl_ref, bb_ref, bc_ref):
    h = pl.program_id(0)

    def bias_of(dist):
        n = jnp.maximum(dist, 0)
        max_exact = REL_BUCKETS // 2
        nf = jnp.maximum(n, 1).astype(F32)
        large = max_exact + (jnp.log(nf / max_exact) / math.log(REL_MAX_DIST / max_exact)
                             * (REL_BUCKETS - max_exact)).astype(jnp.int32)
        large = jnp.minimum(large, REL_BUCKETS - 1)
        bucket = jnp.where(n < max_exact, n, large)
        out = jnp.zeros(dist.shape, F32)
        for kk in range(REL_BUCKETS):
            out = jnp.where(bucket == kk, tbl_ref[h * REL_BUCKETS + kk], out)
        return out

    r = lax.broadcasted_iota(jnp.int32, (TQ, TQ), 0)
    c = lax.broadcasted_iota(jnp.int32, (TQ, TQ), 1)
    d0 = r - c
    bb_ref[0] = jnp.where(d0 >= 0, bias_of(d0), NEG_INF)
    bb_ref[1] = bias_of(d0 + TQ)
    far = bias_of(d0 + 2 * TQ)
    bb_ref[2] = far
    bb_ref[3] = jnp.where(d0 + 2 * TQ < WINDOW, far, NEG_INF)

    rc = lax.broadcasted_iota(jnp.int32, (TQ, LANES), 0)
    nc = lax.broadcasted_iota(jnp.int32, (TQ, LANES), 1)
    n_cmp = (2048 - CMP_LEN) // CMP_STRIDE + 1
    for i in range(bc_ref.shape[0]):
        dc = i * TQ + rc - (nc * CMP_STRIDE + CMP_LEN - 1)
        bc_ref[i] = jnp.where((dc >= 0) & (nc < n_cmp), bias_of(dc), NEG_INF)


def _bias_tables(rel_bias, n_qt):
    tbl = rel_bias.T.reshape(-1)
    return pl.pallas_call(
        _bias_kernel, grid=(N_HEADS,),
        in_specs=[pl.BlockSpec(memory_space=pltpu.SMEM)],
        out_specs=[pl.BlockSpec((None, 4, TQ, TQ), lambda h: (h, 0, 0, 0)),
                   pl.BlockSpec((n_qt, None, TQ, LANES), lambda h: (0, h, 0, 0))],
        out_shape=[jax.ShapeDtypeStruct((N_HEADS, 4, TQ, TQ), F32),
                   jax.ShapeDtypeStruct((n_qt, N_HEADS, TQ, LANES), F32)],
        compiler_params=_params(("arbitrary",)), name="bias_tables")(tbl)


def _cmp_kernel(kv_ref, pe_ref, w1_ref, w2_ref, o_ref):
    x = kv_ref[...].astype(F32)
    lo = _dot((x + pe_ref[0:1, :]).astype(BF16), w1_ref[0])
    hi = _dot((x + pe_ref[1:2, :]).astype(BF16), w1_ref[1])
    nrow = x.shape[0]
    hid = lo + pltpu.roll(hi, nrow - 1, 0)
    o_ref[...] = _dot(jax.nn.gelu(hid).astype(BF16), w2_ref[...]).astype(o_ref.dtype)


def _compress(kvr, pe, w1, w2):
    b, nrow, width = kvr.shape
    return pl.pallas_call(
        _cmp_kernel, grid=(b,),
        in_specs=[pl.BlockSpec((None, nrow, width), lambda i: (i, 0, 0)),
                  pl.BlockSpec(pe.shape, lambda i: (0, 0)),
                  pl.BlockSpec(w1.shape, lambda i: (0, 0, 0)),
                  pl.BlockSpec(w2.shape, lambda i: (0, 0))],
        out_specs=pl.BlockSpec((None, nrow, w2.shape[1]), lambda i: (i, 0, 0)),
        out_shape=jax.ShapeDtypeStruct((b, nrow, w2.shape[1]), BF16),
        compiler_params=_params(("arbitrary",), VMEM_LIMIT), name="nsa_compress")(kvr, pe, w1, w2)


def _flash_step(carry, s, v):
    m, l, acc = carry
    m_new = jnp.maximum(m, jnp.max(s, axis=1, keepdims=True))
    alpha = jnp.exp(m - m_new)
    p = jnp.exp(s - m_new)
    l = alpha * l + jnp.sum(p, axis=1, keepdims=True)
    acc = alpha * acc + _dot(p.astype(BF16), v)
    return m_new, l, acc


def _flash_init():
    return (jnp.full((TQ, 1), NEG_INF, F32), jnp.zeros((TQ, 1), F32), jnp.zeros((TQ, LANES), F32))


def _rows(ref, kt):
    return ref[pl.ds(pl.multiple_of(kt * TQ, TQ), TQ), :]


def _nsa_kernel(qf_ref, qp_ref, g_ref, kvc_ref, ks_ref, vs_ref, kw_ref, vw_ref, bb_ref, bc_ref,
                ovl_ref, exp_ref, o_ref, madd_ref, ocg_ref, g1_ref, g2_ref):
    i = pl.program_id(1)
    p = pl.program_id(2)
    lane = lax.broadcasted_iota(jnp.int32, (TQ, LANES), 1)
    lo = lane < HEAD_DIM
    halves = (lo, jnp.logical_not(lo))

    @pl.when(p == 0)
    def _select():
        gs = jax.nn.sigmoid(g_ref[...])
        kc2 = kvc_ref[:, 0:LANES]
        vc2 = kvc_ref[:, LANES:2 * LANES]
        psum = jnp.zeros((TQ, LANES), F32)
        for pp in range(N_PAIRS):
            q2 = qf_ref[:, pp * LANES:(pp + 1) * LANES] * SCALE
            ocs, g1s, g2s = [], [], []
            for j in range(2):
                h = 2 * pp + j
                qh = jnp.where(halves[j], q2, jnp.zeros_like(q2))
                bias = bc_ref[h]
                s = _dot_nt(qh, kc2) + bias
                valid = bias > 0.5 * NEG_INF
                m = jnp.max(s, axis=1, keepdims=True)
                e = jnp.where(valid, jnp.exp(s - m), 0.0)
                pc = e / jnp.maximum(jnp.sum(e, axis=1, keepdims=True), TINY)
                psum = psum + pc
                oc = _dot(pc.astype(BF16), vc2)
                ocs.append(gs[:, 3 * h:3 * h + 1] * oc)
                g1s.append(jnp.broadcast_to(gs[:, 3 * h + 1:3 * h + 2], (TQ, LANES)))
                g2s.append(jnp.broadcast_to(gs[:, 3 * h + 2:3 * h + 3], (TQ, LANES)))
            ocg_ref[pp] = jnp.where(lo, ocs[0], ocs[1])
            g1_ref[pp] = jnp.where(lo, g1s[0], g1s[1])
            g2_ref[pp] = jnp.where(lo, g2s[0], g2s[1])

        p_hi = psum.astype(BF16)
        p_lo = (psum - p_hi.astype(F32)).astype(BF16)
        imp = _dot_nt(ovl_ref[...], p_hi) + _dot_nt(ovl_ref[...], p_lo)
        n_slc = 2048 // SLC_BLOCK
        imp = imp[0:n_slc, :]
        sidx = lax.broadcasted_iota(jnp.int32, (n_slc, TQ), 0)
        t = i * TQ + lax.broadcasted_iota(jnp.int32, (n_slc, TQ), 1)
        cur = lax.shift_right_logical(t, 6)
        eligible = sidx <= cur
        forced = (sidx == 0) | (sidx == cur) | (sidx == cur - 1)
        score = jnp.where(eligible, jnp.where(forced, FORCE_SCORE, 0.0), NEG_INF) + imp
        cnt = jnp.zeros((n_slc, TQ), jnp.int32)
        for sp in range(n_slc):
            row = score[sp:sp + 1, :]
            better = (row > score) | ((row == score) & (sp < sidx))
            cnt = cnt + better.astype(jnp.int32)
        sel_t = (cnt < min(SLC_TOPK, n_slc)).astype(F32)
        sel_t = jnp.concatenate([sel_t, jnp.zeros((LANES - n_slc, TQ), F32)], axis=0)
        sel = sel_t.T
        selx = _dot(sel.astype(BF16), exp_ref[...])
        rq = i * TQ + lax.broadcasted_iota(jnp.int32, (TQ, TQ), 0)
        ck = lax.broadcasted_iota(jnp.int32, (TQ, TQ), 1)
        for kt in range(madd_ref.shape[0]):
            ok = (selx[:, kt * TQ:(kt + 1) * TQ] > 0.5) & (kt * TQ + ck <= rq)
            madd_ref[kt] = jnp.where(ok, 0.0, NEG_INF)

    q2 = qp_ref[...] * SCALE
    o_s, o_w = [], []
    for j in range(2):
        h = 2 * p + j
        qh = jnp.where(halves[j], q2, jnp.zeros_like(q2))

        def slc_body(jj, carry):
            kt = i - jj
            s = _dot_nt(qh, _rows(ks_ref, kt)) + bb_ref[h, jnp.minimum(jj, 2)] + madd_ref[kt]
            return _flash_step(carry, s, _rows(vs_ref, kt))

        _, l, acc = lax.fori_loop(0, i + 1, slc_body, _flash_init())
        o_s.append(acc / jnp.maximum(l, TINY))

        def win_body(jj, carry):
            kt = i - jj
            s = _dot_nt(qh, _rows(kw_ref, kt)) + bb_ref[h, jnp.where(jj == 2, 3, jj)]
            return _flash_step(carry, s, _rows(vw_ref, kt))

        _, l, acc = lax.fori_loop(0, jnp.minimum(i, WINDOW // TQ) + 1, win_body, _flash_init())
        o_w.append(acc / jnp.maximum(l, TINY))

    os2 = jnp.where(lo, o_s[0], o_s[1])
    ow2 = jnp.where(lo, o_w[0], o_w[1])
    o_ref[...] = (ocg_ref[p] + g1_ref[p] * os2 + g2_ref[p] * ow2).astype(o_ref.dtype)


def _nsa_attention(q, g, kvc, ks2, vs2, kw2, vw2, bb, bc, ovl_t, expand):
    b, s, _ = q.shape
    n_qt = s // TQ
    per_b = lambda bi, i, p: (bi, 0, 0)
    return pl.pallas_call(
        _nsa_kernel, grid=(b, n_qt, N_PAIRS),
        in_specs=[pl.BlockSpec((None, TQ, MAIN_W), lambda bi, i, p: (bi, i, 0)),
                  pl.BlockSpec((None, TQ, LANES), lambda bi, i, p: (bi, i, p)),
                  pl.BlockSpec((None, TQ, LANES), lambda bi, i, p: (bi, i, 0)),
                  pl.BlockSpec((None,) + kvc.shape[1:], per_b),
                  pl.BlockSpec((None, s, LANES), per_b), pl.BlockSpec((None, s, LANES), per_b),
                  pl.BlockSpec((None, s, LANES), per_b), pl.BlockSpec((None, s, LANES), per_b),
                  pl.BlockSpec(bb.shape, lambda bi, i, p: (0, 0, 0, 0)),
                  pl.BlockSpec((None,) + bc.shape[1:], lambda bi, i, p: (i, 0, 0, 0)),
                  pl.BlockSpec(ovl_t.shape, lambda bi, i, p: (0, 0)),
                  pl.BlockSpec(expand.shape, lambda bi, i, p: (0, 0))],
        out_specs=pl.BlockSpec((None, TQ, LANES), lambda bi, i, p: (bi, i, p)),
        out_shape=jax.ShapeDtypeStruct((b, s, MAIN_W), BF16),
        scratch_shapes=[pltpu.VMEM((n_qt, TQ, TQ), F32), pltpu.VMEM((N_PAIRS, TQ, LANES), F32),
                        pltpu.VMEM((N_PAIRS, TQ, LANES), F32), pltpu.VMEM((N_PAIRS, TQ, LANES), F32)],
        compiler_params=_params(("arbitrary", "arbitrary", "arbitrary"), VMEM_LIMIT),
        name="nsa_attention")(q, q, g, kvc, ks2, vs2, kw2, vw2, bb, bc, ovl_t, expand)


def _moba_kernel(q_ref, k_ref, v_ref, km_ref, bb_ref, o_ref, mcol_ref):
    p = pl.program_id(1)
    i = pl.program_id(2)
    lane = lax.broadcasted_iota(jnp.int32, (TQ, LANES), 1)
    lo = lane < HEAD_DIM
    halves = (lo, jnp.logical_not(lo))
    q2 = q_ref[...]
    q2s = q2 * SCALE
    km = km_ref[...].astype(BF16)
    n_blk = mcol_ref.shape[1]
    eligible = lane < i
    outs = []
    for j in range(2):
        h = 2 * p + j
        gate = _dot_nt(jnp.where(halves[j], q2, jnp.zeros_like(q2)), km)
        gate = jnp.where(eligible, gate, NEG_INF)
        cnt = jnp.zeros((TQ, LANES), jnp.int32)
        for n in range(n_blk):
            col = gate[:, n:n + 1]
            better = (col > gate) | ((col == gate) & (n < lane))
            cnt = cnt + better.astype(jnp.int32)
        seladd = jnp.where((cnt < min(MOBA_TOPK, n_blk - 1)) & eligible, 0.0, NEG_INF)
        for n in range(n_blk):
            mcol_ref[j, n] = jnp.broadcast_to(seladd[:, n:n + 1], (TQ, TQ))

        qh = jnp.where(halves[j], q2s, jnp.zeros_like(q2s))
        s = _dot_nt(qh, _rows(k_ref, i)) + bb_ref[h, 0]
        carry = _flash_step(_flash_init(), s, _rows(v_ref, i))

        def body(jj, carry):
            kt = i - jj
            s = _dot_nt(qh, _rows(k_ref, kt)) + bb_ref[h, jnp.minimum(jj, 2)] + mcol_ref[j, kt]
            return _flash_step(carry, s, _rows(v_ref, kt))

        _, l, acc = lax.fori_loop(1, i + 1, body, carry)
        outs.append(acc / jnp.maximum(l, TINY))
    o_ref[...] = jnp.where(lo, outs[0], outs[1]).astype(o_ref.dtype)


def _moba_attention(q, k, v, kmean, bb):
    b, s, _ = q.shape
    n_qt = s // TQ
    per_bp = lambda bi, p, i: (bi, 0, p)
    return pl.pallas_call(
        _moba_kernel, grid=(b, N_PAIRS, n_qt),
        in_specs=[pl.BlockSpec((None, TQ, LANES), lambda bi, p, i: (bi, i, p)),
                  pl.BlockSpec((None, s, LANES), per_bp), pl.BlockSpec((None, s, LANES), per_bp),
                  pl.BlockSpec((None, LANES, LANES), per_bp),
                  pl.BlockSpec(bb.shape, lambda bi, p, i: (0, 0, 0, 0))],
        out_specs=pl.BlockSpec((None, TQ, LANES), lambda bi, p, i: (bi, i, p)),
        out_shape=jax.ShapeDtypeStruct((b, s, MAIN_W), BF16),
        scratch_shapes=[pltpu.VMEM((2, n_qt, TQ, TQ), F32)],
        compiler_params=_params(("arbitrary", "arbitrary", "arbitrary"), VMEM_LIMIT),
        name="moba_attention")(q, k, v, kmean, bb)


def _mem_kernel(q_ref, kv_ref, o_ref):
    lane = lax.broadcasted_iota(jnp.int32, (TQ, LANES), 1)
    lo = lane < HEAD_DIM
    halves = (lo, jnp.logical_not(lo))
    for pp in range(MEM_HEADS // 2):
        q2 = q_ref[:, pp * LANES:(pp + 1) * LANES] * SCALE
        k2 = kv_ref[:, pp * LANES:(pp + 1) * LANES]
        v2 = kv_ref[:, MEM_W + pp * LANES:MEM_W + (pp + 1) * LANES]
        outs = []
        for j in range(2):
            s = _dot_nt(jnp.where(halves[j], q2, jnp.zeros_like(q2)), k2)
            e = jnp.exp(s - jnp.max(s, axis=1, keepdims=True))
            pr = e / jnp.sum(e, axis=1, keepdims=True)
            outs.append(_dot(pr.astype(BF16), v2))
        o_ref[:, pp * LANES:(pp + 1) * LANES] = jnp.where(lo, outs[0], outs[1]).astype(o_ref.dtype)


def _mem_attention(qm, mem_kv):
    b, s, _ = qm.shape
    n_mem = mem_kv.shape[1]
    return pl.pallas_call(
        _mem_kernel, grid=(b, s // TQ),
        in_specs=[pl.BlockSpec((None, TQ, MEM_W), lambda bi, i: (bi, i, 0)),
                  pl.BlockSpec((None, n_mem, 2 * MEM_W), lambda bi, i: (bi, 0, 0))],
        out_specs=pl.BlockSpec((None, TQ, MEM_W), lambda bi, i: (bi, i, 0)),
        out_shape=jax.ShapeDtypeStruct((b, s, MEM_W), BF16),
        compiler_params=_params(("arbitrary", "arbitrary")), name="mem_attention")(qm, mem_kv)


def _layer_norm(y, g, b):
    mu = jnp.mean(y, axis=-1, keepdims=True)
    var = jnp.mean(jnp.square(y - mu), axis=-1, keepdims=True)
    return (y - mu) * lax.rsqrt(var + LN_EPS) * g + b


def _outproj_kernel(om_ref, oe_ref, wa_ref, wb_ref, x_ref, g_ref, b_ref, o_ref):
    mix = _dot(om_ref[...], wa_ref[...]) + _dot(oe_ref[...], wb_ref[...])
    o_ref[...] = _layer_norm(ALPHA * x_ref[...] + mix, g_ref[...], b_ref[...])


def _outproj_ln(o_main, o_mem, w_a, w_b, x, g, b, tm=512):
    m = x.shape[0]
    row = lambda i: (i, 0)
    full = lambda i: (0, 0)
    return pl.pallas_call(
        _outproj_kernel, grid=(m // tm,),
        in_specs=[pl.BlockSpec((tm, MAIN_W), row), pl.BlockSpec((tm, MEM_W), row),
                  pl.BlockSpec(w_a.shape, full), pl.BlockSpec(w_b.shape, full),
                  pl.BlockSpec((tm, D_MODEL), row), pl.BlockSpec((1, D_MODEL), full), pl.BlockSpec((1, D_MODEL), full)],
        out_specs=pl.BlockSpec((tm, D_MODEL), row),
        out_shape=jax.ShapeDtypeStruct((m, D_MODEL), F32),
        compiler_params=_params(("arbitrary",), VMEM_LIMIT), name="outproj_ln")(o_main, o_mem, w_a, w_b, x, g, b)


FF_CHUNK = 256


def _ffn_kernel(x_ref, wa_ref, wb_ref, cw_ref, cb_ref, wo_ref, g_ref, b_ref, o_ref, xb_ref):
    c = pl.program_id(1)

    @pl.when(c == 0)
    def _():
        xb_ref[...] = x_ref[...].astype(BF16)

    xb = xb_ref[...]
    a = _dot(xb, wa_ref[...])
    gate = _dot(xb, wb_ref[...])
    rows = lax.broadcasted_iota(jnp.int32, a.shape, 0)
    nrow = a.shape[0]
    a1 = jnp.where(rows >= 1, pltpu.roll(a, 1, 0), 0.0)
    a2 = jnp.where(rows >= 2, pltpu.roll(a, 2, 0), 0.0)
    conv = cw_ref[0:1, :] * a2 + cw_ref[1:2, :] * a1 + cw_ref[2:3, :] * a + cb_ref[...]
    hid = (jax.nn.gelu(conv) * gate).astype(BF16)
    part = _dot(hid, wo_ref[...])

    @pl.when(c == 0)
    def _():
        o_ref[...] = part

    @pl.when(c > 0)
    def _():
        o_ref[...] += part

    @pl.when(c == pl.num_programs(1) - 1)
    def _():
        o_ref[...] = _layer_norm(ALPHA * x_ref[...] + o_ref[...], g_ref[...], b_ref[...])


def _ffn_ln(x, w_in, conv_w, conv_b, w_out, g, b):
    bsz, s, d = x.shape
    n_c = D_FF // FF_CHUNK
    return pl.pallas_call(
        _ffn_kernel, grid=(bsz, n_c),
        in_specs=[pl.BlockSpec((None, s, d), lambda bi, c: (bi, 0, 0)),
                  pl.BlockSpec((d, FF_CHUNK), lambda bi, c: (0, c)),
                  pl.BlockSpec((d, FF_CHUNK), lambda bi, c: (0, n_c + c)),
                  pl.BlockSpec((CONV_W, FF_CHUNK), lambda bi, c: (0, c)),
                  pl.BlockSpec((1, FF_CHUNK), lambda bi, c: (0, c)),
                  pl.BlockSpec((FF_CHUNK, d), lambda bi, c: (c, 0)),
                  pl.BlockSpec((1, d), lambda bi, c: (0, 0)), pl.BlockSpec((1, d), lambda bi, c: (0, 0))],
        out_specs=pl.BlockSpec((None, s, d), lambda bi, c: (bi, 0, 0)),
        out_shape=jax.ShapeDtypeStruct((bsz, s, d), F32),
        scratch_shapes=[pltpu.VMEM((s, d), BF16)],
        compiler_params=_params(("arbitrary", "arbitrary"), VMEM_LIMIT),
        name="ffn_ln")(x, w_in, w_in, conv_w, conv_b, w_out, g, b)


def _dup(w):
    return jnp.concatenate([w, w], axis=1)


def _cmp_weights(pe_k, w1_k, w2_k, pe_v, w1_v, w2_v):
    half = CMP_LEN // 2
    zk = jnp.zeros((half, HEAD_DIM, CMP_HIDDEN), F32)

    def w1_half(sl):
        wk = w1_k.reshape(CMP_LEN, HEAD_DIM, CMP_HIDDEN)[sl]
        wv = w1_v.reshape(CMP_LEN, HEAD_DIM, CMP_HIDDEN)[sl]
        top = jnp.concatenate([wk, zk], axis=2)
        bot = jnp.concatenate([zk, wv], axis=2)
        return jnp.concatenate([top, bot], axis=1).reshape(half * 2 * HEAD_DIM, 2 * CMP_HIDDEN)

    w1 = jnp.stack([w1_half(slice(0, half)), w1_half(slice(half, CMP_LEN))]).astype(BF16)
    pe = jnp.concatenate([pe_k, pe_v], axis=1)
    pe = jnp.stack([pe[:half].reshape(-1), pe[half:].reshape(-1)])
    zo = jnp.zeros((CMP_HIDDEN, 2 * HEAD_DIM), F32)
    w2 = jnp.concatenate([jnp.concatenate([_dup(w2_k), zo], axis=1),
                          jnp.concatenate([zo, _dup(w2_v)], axis=1)], axis=0).astype(BF16)
    return pe, w1, w2


def _structure_constants(seq):
    n_cmp = (seq - CMP_LEN) // CMP_STRIDE + 1
    n_slc = seq // SLC_BLOCK
    start = np.arange(LANES) * CMP_STRIDE
    bs = np.arange(LANES) * SLC_BLOCK
    ovl_t = ((start[None, :] < bs[:, None] + SLC_BLOCK) & (start[None, :] + CMP_LEN > bs[:, None])
             & (np.arange(LANES)[None, :] < n_cmp) & (np.arange(LANES)[:, None] < n_slc))
    expand = (np.arange(seq)[None, :] // SLC_BLOCK) == np.arange(LANES)[:, None]
    return jnp.asarray(ovl_t, BF16), jnp.asarray(expand, BF16)


def kernel(x, mem, rel_bias, a_w_in, a_cmp_pe_k, a_cmp_w1_k, a_cmp_w2_k, a_cmp_pe_v, a_cmp_w1_v, a_cmp_w2_v,
           a_w_mem_kv, a_w_out, shared_w_kv, b_w_in, b_w_mem_kv, b_w_out, ln1_g, ln1_b, ln2_g, ln2_b,
           ffn_w_in, ffn_conv_w, ffn_conv_b, ffn_w_out):
    bsz, seq, d = x.shape
    n_mem = mem.shape[1]
    m = bsz * seq
    assert (seq, d) == (2048, D_MODEL) and seq % TQ == 0
    n_qt = seq // TQ

    bb, bc = _bias_tables(rel_bias, n_qt)
    ovl_t, expand = _structure_constants(seq)
    memf = mem.reshape(bsz * n_mem, d)
    xf = x.reshape(m, d)

    def ffn(xcur, layer):
        return _ffn_ln(xcur.reshape(bsz, seq, d), ffn_w_in[layer].astype(BF16), ffn_conv_w[layer],
                       ffn_conv_b[layer][None, :], ffn_w_out[layer].astype(BF16),
                       ln2_g[layer][None, :], ln2_b[layer][None, :]).reshape(m, d)

    def out_ln(o_main, o_mem, w_out, xcur, layer):
        w = w_out.astype(BF16)
        return _outproj_ln(o_main.reshape(m, MAIN_W), o_mem.reshape(m, MEM_W), w[:MAIN_W], w[MAIN_W:], xcur,
                           ln1_g[layer][None, :], ln1_b[layer][None, :])

    w = a_w_in[0]
    c0 = MAIN_W
    cols = [w[:, c0 + kk * HEAD_DIM:c0 + (kk + 1) * HEAD_DIM] for kk in range(6)]
    c_g = c0 + 6 * HEAD_DIM
    w_g = jnp.pad(w[:, c_g:c_g + 3 * N_HEADS], ((0, 0), (0, LANES - 3 * N_HEADS)))
    w_qm = w[:, c_g + 3 * N_HEADS:]
    weights = [w[:, :MAIN_W], jnp.concatenate(cols[0:2], axis=1), _dup(cols[2]), _dup(cols[3]),
               _dup(cols[4]), _dup(cols[5]), w_g, w_qm]
    weights = [wi.astype(BF16) for wi in weights]
    q, kvc_tok, ks2, vs2, kw2, vw2, g, qm = _proj(
        xf, weights, [BF16, BF16, BF16, BF16, BF16, BF16, F32, BF16], 512, "proj_a")

    pe, w1, w2 = _cmp_weights(a_cmp_pe_k[0], a_cmp_w1_k[0], a_cmp_w2_k[0],
                              a_cmp_pe_v[0], a_cmp_w1_v[0], a_cmp_w2_v[0])
    kvc = _compress(kvc_tok.reshape(bsz, seq // (CMP_LEN // 2), (CMP_LEN // 2) * LANES), pe, w1, w2)

    r3 = lambda t: t.reshape(bsz, seq, t.shape[-1])
    o_main = _nsa_attention(r3(q), r3(g), kvc, r3(ks2), r3(vs2), r3(kw2), r3(vw2), bb, bc, ovl_t, expand)
    (mkv,) = _proj(memf, [a_w_mem_kv[0].astype(BF16)], [BF16], 512, "proj_mem_a")
    o_mem = _mem_attention(r3(qm), mkv.reshape(bsz, n_mem, 2 * MEM_W))
    x1 = out_ln(o_main, o_mem, a_w_out[0], xf, 0)
    x1 = ffn(x1, 0)

    wb = b_w_in[0].astype(BF16)
    wkv = shared_w_kv.astype(BF16)
    q, qm, k, v, kmean = _proj_b(x1, wb[:, :MAIN_W], wb[:, MAIN_W:], wkv[:, :MAIN_W], wkv[:, MAIN_W:])
    n_blk = seq // MOBA_BLOCK
    kmean = jnp.pad(kmean.reshape(bsz, n_blk, MAIN_W), ((0, 0), (0, LANES - n_blk), (0, 0)))
    o_main = _moba_attention(r3(q), r3(k), r3(v), kmean, bb)
    (mkv,) = _proj(memf, [b_w_mem_kv[0].astype(BF16)], [BF16], 512, "proj_mem_b")
    o_mem = _mem_attention(r3(qm), mkv.reshape(bsz, n_mem, 2 * MEM_W))
    x2 = out_ln(o_main, o_mem, b_w_out[0], x1, 1)
    x2 = ffn(x2, 1)
    return x2.reshape(bsz, seq, d)
```

```python
import math

import numpy as np
import jax
import jax.numpy as jnp
from jax import lax
from jax.experimental import pallas as pl
from jax.experimental.pallas import tpu as pltpu

F32 = jnp.float32
BF16 = jnp.bfloat16

D_MODEL = 1024
HEAD_DIM = 64
N_HEADS = 12
N_PAIRS = N_HEADS // 2
MAIN_W = N_HEADS * HEAD_DIM
MEM_HEADS = 4
MEM_W = MEM_HEADS * HEAD_DIM
CMP_LEN = 32
CMP_STRIDE = 16
CMP_HIDDEN = 256
SLC_BLOCK = 64
SLC_TOPK = 16
WINDOW = 512
FORCE_SCORE = 1.0e4
MOBA_BLOCK = 256
MOBA_TOPK = 3
REL_BUCKETS = 32
REL_MAX_DIST = 128
D_FF = 2816
CONV_W = 3
DEPTH = 2
ALPHA = (2.0 * DEPTH) ** 0.25
LN_EPS = 1e-5
NEG_INF = -1e30
TINY = 1e-30
SCALE = HEAD_DIM ** -0.5

LANES = 128
TQ = 256
VMEM_LIMIT = 56 * 1024 * 1024
QK_LOOKAHEAD = 4


def _dot(a, b):
    return jnp.dot(a, b, preferred_element_type=F32)


def _dot_nt(a, b):
    return lax.dot_general(a, b, (((1,), (1,)), ((), ())), preferred_element_type=F32)


def _params(sem, vmem=None):
    return pltpu.CompilerParams(dimension_semantics=sem, vmem_limit_bytes=vmem)


def _proj_kernel(x_ref, *refs):
    n = len(refs) // 2
    xb = x_ref[...].astype(BF16)
    for w_ref, o_ref in zip(refs[:n], refs[n:]):
        o_ref[...] = _dot(xb, w_ref[...]).astype(o_ref.dtype)


def _proj(x, weights, dtypes, tm, name):
    m, k = x.shape
    in_specs = [pl.BlockSpec((tm, k), lambda i: (i, 0))]
    in_specs += [pl.BlockSpec(w.shape, lambda i: (0, 0)) for w in weights]
    out_specs = [pl.BlockSpec((tm, w.shape[1]), lambda i: (i, 0)) for w in weights]
    out_shape = [jax.ShapeDtypeStruct((m, w.shape[1]), dt) for w, dt in zip(weights, dtypes)]
    return pl.pallas_call(
        _proj_kernel, grid=(m // tm,), in_specs=in_specs, out_specs=out_specs, out_shape=out_shape,
        compiler_params=_params(("arbitrary",), VMEM_LIMIT), name=name)(x, *weights)


def _projb_kernel(x_ref, wq_ref, wm_ref, wk_ref, wv_ref, q_ref, qm_ref, k_ref, v_ref, km_ref):
    xb = x_ref[...].astype(BF16)
    q_ref[...] = _dot(xb, wq_ref[...]).astype(BF16)
    qm_ref[...] = _dot(xb, wm_ref[...]).astype(BF16)
    k = _dot(xb, wk_ref[...])
    k_ref[...] = k.astype(BF16)
    km_ref[...] = jnp.mean(k, axis=0, keepdims=True)
    v_ref[...] = _dot(xb, wv_ref[...]).astype(BF16)


def _proj_b(x, wq, wm, wk, wv):
    m, k = x.shape
    nblk = m // MOBA_BLOCK
    row = lambda i: (i, 0)
    full = lambda i: (0, 0)
    return pl.pallas_call(
        _projb_kernel, grid=(nblk,),
        in_specs=[pl.BlockSpec((MOBA_BLOCK, k), row)] + [pl.BlockSpec(w.shape, full) for w in (wq, wm, wk, wv)],
        out_specs=[pl.BlockSpec((MOBA_BLOCK, MAIN_W), row), pl.BlockSpec((MOBA_BLOCK, MEM_W), row),
                   pl.BlockSpec((MOBA_BLOCK, MAIN_W), row), pl.BlockSpec((MOBA_BLOCK, MAIN_W), row),
                   pl.BlockSpec((None, 1, MAIN_W), lambda i: (i, 0, 0))],
        out_shape=[jax.ShapeDtypeStruct((m, MAIN_W), BF16), jax.ShapeDtypeStruct((m, MEM_W), BF16),
                   jax.ShapeDtypeStruct((m, MAIN_W), BF16), jax.ShapeDtypeStruct((m, MAIN_W), BF16),
                   jax.ShapeDtypeStruct((nblk, 1, MAIN_W), F32)],
        compiler_params=_params(("arbitrary",), VMEM_LIMIT), name="proj_b")(x, wq, wm, wk, wv)


def _bias_kernel(tbl_ref, bb_ref, bc_ref, far_ref):
    h = pl.program_id(0)

    def bias_of(dist):
        n = jnp.maximum(dist, 0)
        max_exact = REL_BUCKETS // 2
        nf = jnp.maximum(n, 1).astype(F32)
        large = max_exact + (jnp.log(nf / max_exact) / math.log(REL_MAX_DIST / max_exact)
                             * (REL_BUCKETS - max_exact)).astype(jnp.int32)
        large = jnp.minimum(large, REL_BUCKETS - 1)
        bucket = jnp.where(n < max_exact, n, large)
        out = jnp.zeros(dist.shape, F32)
        for kk in range(REL_BUCKETS):
            out = jnp.where(bucket == kk, tbl_ref[h * REL_BUCKETS + kk], out)
        return out

    key = lax.broadcasted_iota(jnp.int32, (TQ, TQ), 0)
    qry = lax.broadcasted_iota(jnp.int32, (TQ, TQ), 1)
    d0 = qry - key
    bb_ref[0] = jnp.where(d0 >= 0, bias_of(d0), NEG_INF)
    bb_ref[1] = bias_of(d0 + TQ)
    far = bias_of(d0 + 2 * TQ)
    bb_ref[2] = far
    bb_ref[3] = jnp.where(d0 + 2 * TQ < WINDOW, far, NEG_INF)
    far_ref[...] = far[0:1, :]

    nc = lax.broadcasted_iota(jnp.int32, (LANES, TQ), 0)
    qc = lax.broadcasted_iota(jnp.int32, (LANES, TQ), 1)
    n_cmp = (2048 - CMP_LEN) // CMP_STRIDE + 1
    for i in range(bc_ref.shape[0]):
        dc = i * TQ + qc - (nc * CMP_STRIDE + CMP_LEN - 1)
        bc_ref[i] = jnp.where((dc >= 0) & (nc < n_cmp), bias_of(dc), NEG_INF)


def _bias_tables(rel_bias, n_qt):
    tbl = rel_bias.T.reshape(-1)
    return pl.pallas_call(
        _bias_kernel, grid=(N_HEADS,),
        in_specs=[pl.BlockSpec(memory_space=pltpu.SMEM)],
        out_specs=[pl.BlockSpec((None, 4, TQ, TQ), lambda h: (h, 0, 0, 0)),
                   pl.BlockSpec((n_qt, None, LANES, TQ), lambda h: (0, h, 0, 0)),
                   pl.BlockSpec((None, 1, TQ), lambda h: (h, 0, 0))],
        out_shape=[jax.ShapeDtypeStruct((N_HEADS, 4, TQ, TQ), F32),
                   jax.ShapeDtypeStruct((n_qt, N_HEADS, LANES, TQ), F32),
                   jax.ShapeDtypeStruct((N_HEADS, 1, TQ), F32)],
        compiler_params=_params(("arbitrary",)), name="bias_tables")(tbl)


def _cmp_kernel(kv_ref, pe_ref, w1_ref, w2_ref, o_ref):
    x = kv_ref[...].astype(F32)
    lo = _dot((x + pe_ref[0:1, :]).astype(BF16), w1_ref[0])
    hi = _dot((x + pe_ref[1:2, :]).astype(BF16), w1_ref[1])
    nrow = x.shape[0]
    hid = lo + pltpu.roll(hi, nrow - 1, 0)
    o_ref[...] = _dot(jax.nn.gelu(hid).astype(BF16), w2_ref[...])


def _compress(kvr, pe, w1, w2):
    b, nrow, width = kvr.shape
    return pl.pallas_call(
        _cmp_kernel, grid=(b,),
        in_specs=[pl.BlockSpec((None, nrow, width), lambda i: (i, 0, 0)),
                  pl.BlockSpec(pe.shape, lambda i: (0, 0)),
                  pl.BlockSpec(w1.shape, lambda i: (0, 0, 0)),
                  pl.BlockSpec(w2.shape, lambda i: (0, 0))],
        out_specs=pl.BlockSpec((None, nrow, w2.shape[1]), lambda i: (i, 0, 0)),
        out_shape=jax.ShapeDtypeStruct((b, nrow, w2.shape[1]), F32),
        compiler_params=_params(("arbitrary",), VMEM_LIMIT), name="nsa_compress")(kvr, pe, w1, w2)


def _softmax_tile(state, c, s_t, v_t, vis=None, cbias=None):
    m_ref, l_ref, acc_ref = state
    m = m_ref[c]
    n_blk = 1 if vis is None else vis.shape[0]
    rows = s_t.shape[0] // n_blk
    parts = [s_t[b * rows:(b + 1) * rows] for b in range(n_blk)]
    tile_max = None
    for b in range(n_blk):
        mb = jnp.max(parts[b], axis=0, keepdims=True)
        if vis is not None:
            mb = jnp.where(vis[b:b + 1] > 0.5, mb, NEG_INF)
        tile_max = mb if tile_max is None else jnp.maximum(tile_max, mb)
    if cbias is not None:
        tile_max = tile_max + cbias
    m_new = jnp.maximum(m, tile_max)
    shift = m_new if cbias is None else m_new - cbias
    p_parts = []
    for b in range(n_blk):
        off = shift if vis is None else jnp.where(vis[b:b + 1] > 0.5, shift, -NEG_INF)
        p_parts.append(jnp.exp(parts[b] - off))
    p_t = p_parts[0] if n_blk == 1 else jnp.concatenate(p_parts, axis=0)
    alpha = jnp.exp(m - m_new)
    l_ref[c] = alpha * l_ref[c] + jnp.sum(p_t, axis=0, keepdims=True)
    acc_ref[c] = alpha * acc_ref[c] + _dot(v_t, p_t.astype(BF16))
    m_ref[c] = m_new


def _run_chains(chains, logits_fn, update_fn):
    pending = {}
    for idx, c in enumerate(chains[:QK_LOOKAHEAD]):
        pending[c] = logits_fn(c, idx)
    for idx, c in enumerate(chains):
        update_fn(c, pending.pop(c), idx % QK_LOOKAHEAD)
        if idx + QK_LOOKAHEAD < len(chains):
            nxt = chains[idx + QK_LOOKAHEAD]
            pending[nxt] = logits_fn(nxt, idx % QK_LOOKAHEAD)


def _stage_logits(s_ref, slot, s_t):
    s_ref[slot] = s_t


def _init_state(state):
    m_ref, l_ref, acc_ref = state
    m_ref[...] = jnp.full(m_ref.shape, NEG_INF, F32)
    l_ref[...] = jnp.zeros(l_ref.shape, F32)
    acc_ref[...] = jnp.zeros(acc_ref.shape, F32)


def _chain_out(state, c):
    _, l_ref, acc_ref = state
    return acc_ref[c] / jnp.maximum(l_ref[c], TINY)


def _rows(ref, kt, cols=slice(None)):
    return ref[pl.ds(pl.multiple_of(kt * TQ, TQ), TQ), cols]


def _head_masks():
    lane = lax.broadcasted_iota(jnp.int32, (TQ, LANES), 1)
    lo = lane < HEAD_DIM
    return lo, jnp.logical_not(lo)


def _store_head_queries(q_ref, qh_ref):
    halves = _head_masks()
    for pp in range(N_PAIRS):
        q2 = q_ref[:, pp * LANES:(pp + 1) * LANES] * SCALE
        for j in range(2):
            qh_ref[2 * pp + j] = jnp.where(halves[j], q2, jnp.zeros_like(q2))


def _nsa_kernel(q_ref, g_ref, kvc_ref, ks_ref, vs_ref, kw_ref, vw_ref, bb_ref, far_ref, bc_ref, ovl_ref,
                o_ref, vst_ref, vwt_ref, qh_ref, gs_ref, oc_ref, sel_ref, m_ref, l_ref, acc_ref, s_ref):
    i = pl.program_id(1)
    n_kt = vst_ref.shape[0]
    state = (m_ref, l_ref, acc_ref)

    @pl.when(i == 0)
    def _values():
        for kt in range(n_kt):
            rows = slice(kt * TQ, (kt + 1) * TQ)
            vst_ref[kt] = vs_ref[rows, :].astype(F32).T[0:HEAD_DIM].astype(BF16)
            vwt_ref[kt] = vw_ref[rows, :].astype(F32).T[0:HEAD_DIM].astype(BF16)

    _store_head_queries(q_ref, qh_ref)
    gs_ref[...] = jax.nn.sigmoid(g_ref[...]).T

    kc2 = kvc_ref[:, 0:LANES].astype(BF16)
    vc_t = kvc_ref[:, LANES:2 * LANES].T[0:HEAD_DIM].astype(BF16)
    psum = jnp.zeros((LANES, TQ), F32)
    for h in range(N_HEADS):
        bias = bc_ref[h]
        s = _dot_nt(kc2, qh_ref[h]) + bias
        valid = bias > 0.5 * NEG_INF
        m = jnp.max(s, axis=0, keepdims=True)
        e = jnp.where(valid, jnp.exp(s - m), 0.0)
        pc = e / jnp.maximum(jnp.sum(e, axis=0, keepdims=True), TINY)
        psum = psum + pc
        oc_ref[h] = gs_ref[3 * h:3 * h + 1, :] * _dot(vc_t, pc.astype(BF16))

    p_hi = psum.astype(BF16)
    p_lo = (psum - p_hi.astype(F32)).astype(BF16)
    imp = _dot(ovl_ref[...], p_hi) + _dot(ovl_ref[...], p_lo)
    per_tile = TQ // SLC_BLOCK
    n_slc = n_kt * per_tile
    imp = imp[0:n_slc, :]
    sidx = lax.broadcasted_iota(jnp.int32, (n_slc, TQ), 0)
    t = i * TQ + lax.broadcasted_iota(jnp.int32, (n_slc, TQ), 1)
    cur = lax.shift_right_logical(t, 6)
    eligible = sidx <= cur
    forced = (sidx == 0) | (sidx == cur) | (sidx == cur - 1)
    score = jnp.where(eligible, jnp.where(forced, FORCE_SCORE, 0.0), NEG_INF) + imp
    cnt = jnp.zeros((n_slc, TQ), jnp.int32)
    for sp in range(n_slc):
        row = score[sp:sp + 1, :]
        better = (row > score) | ((row == score) & (sp < sidx))
        cnt = cnt + better.astype(jnp.int32)
    chosen = jnp.where(cnt < min(SLC_TOPK, n_slc), 1.0, 0.0)
    for kt in range(n_kt):
        sel_ref[kt, 0:per_tile, :] = chosen[kt * per_tile:(kt + 1) * per_tile, :]

    _init_state(state)
    slc = [("s", h) for h in range(N_HEADS)]
    win = [("w", h) for h in range(N_HEADS)]
    n_near = jnp.minimum(i, WINDOW // TQ) + 1

    def near_body(jj, _):
        kt = i - jj
        k_s, k_w, v_s, v_w = _rows(ks_ref, kt), _rows(kw_ref, kt), vst_ref[kt], vwt_ref[kt]
        vis = sel_ref[kt, 0:per_tile, :]
        b_s = jnp.minimum(jj, 2)
        b_w = jnp.where(jj == 2, 3, jj)

        def logits(c, slot):
            _stage_logits(s_ref, slot, _dot_nt(k_s if c[0] == "s" else k_w, qh_ref[c[1]]))

        def update(c, _s, slot):
            kind, h = c
            if kind == "s":
                _softmax_tile(state, h, s_ref[slot] + bb_ref[h, b_s], v_s, vis=vis)
            else:
                _softmax_tile(state, N_HEADS + h, s_ref[slot] + bb_ref[h, b_w], v_w)

        _run_chains([c for pair in zip(slc, win) for c in pair], logits, update)
        return 0

    lax.fori_loop(0, n_near, near_body, 0)

    def far_body(jj, _):
        kt = i - jj
        k_s, v_s = _rows(ks_ref, kt), vst_ref[kt]
        vis = sel_ref[kt, 0:per_tile, :]
        _run_chains(slc, lambda c, slot: _dot_nt(k_s, qh_ref[c[1]]),
                    lambda c, s, slot: _softmax_tile(state, c[1], s, v_s, vis=vis, cbias=far_ref[c[1]]))
        return 0

    lax.fori_loop(n_near, i + 1, far_body, 0)

    for pp in range(N_PAIRS):
        hs = (2 * pp, 2 * pp + 1)
        outs = []
        for h in hs:
            g1 = gs_ref[3 * h + 1:3 * h + 2, :]
            g2 = gs_ref[3 * h + 2:3 * h + 3, :]
            outs.append(oc_ref[h] + g1 * _chain_out(state, h) + g2 * _chain_out(state, N_HEADS + h))
        o_ref[:, pp * LANES:(pp + 1) * LANES] = jnp.concatenate(outs, axis=0).T.astype(o_ref.dtype)


def _nsa_attention(q, g, kvc, ks2, vs2, kw2, vw2, bb, far, bc, ovl_t):
    b, s, _ = q.shape
    n_qt = s // TQ
    per_b = lambda bi, i: (bi, 0, 0)
    return pl.pallas_call(
        _nsa_kernel, grid=(b, n_qt),
        in_specs=[pl.BlockSpec((None, TQ, MAIN_W), lambda bi, i: (bi, i, 0)),
                  pl.BlockSpec((None, TQ, LANES), lambda bi, i: (bi, i, 0)),
                  pl.BlockSpec((None,) + kvc.shape[1:], per_b),
                  pl.BlockSpec((None, s, LANES), per_b), pl.BlockSpec((None, s, LANES), per_b),
                  pl.BlockSpec((None, s, LANES), per_b), pl.BlockSpec((None, s, LANES), per_b),
                  pl.BlockSpec(bb.shape, lambda bi, i: (0, 0, 0, 0)),
                  pl.BlockSpec(far.shape, lambda bi, i: (0, 0, 0)),
                  pl.BlockSpec((None,) + bc.shape[1:], lambda bi, i: (i, 0, 0, 0)),
                  pl.BlockSpec(ovl_t.shape, lambda bi, i: (0, 0))],
        out_specs=pl.BlockSpec((None, TQ, MAIN_W), lambda bi, i: (bi, i, 0)),
        out_shape=jax.ShapeDtypeStruct((b, s, MAIN_W), BF16),
        scratch_shapes=[pltpu.VMEM((n_qt, HEAD_DIM, TQ), BF16), pltpu.VMEM((n_qt, HEAD_DIM, TQ), BF16),
                        pltpu.VMEM((N_HEADS, TQ, LANES), BF16), pltpu.VMEM((LANES, TQ), F32),
                        pltpu.VMEM((N_HEADS, HEAD_DIM, TQ), F32), pltpu.VMEM((n_qt, 8, TQ), F32),
                        pltpu.VMEM((2 * N_HEADS, 1, TQ), F32), pltpu.VMEM((2 * N_HEADS, 1, TQ), F32),
                        pltpu.VMEM((2 * N_HEADS, HEAD_DIM, TQ), F32), pltpu.VMEM((QK_LOOKAHEAD, TQ, TQ), F32)],
        compiler_params=_params(("arbitrary", "arbitrary"), VMEM_LIMIT),
        name="nsa_attention")(q, g, kvc, ks2, vs2, kw2, vw2, bb, far, bc, ovl_t)


def _moba_kernel(q_ref, k_ref, v_ref, km_ref, bb_ref, far_ref, o_ref, vt_ref, qh_ref, sel_ref,
                 m_ref, l_ref, acc_ref, s_ref):
    i = pl.program_id(1)
    halves = _head_masks()
    n_blk = vt_ref.shape[1]
    state = (m_ref, l_ref, acc_ref)
    heads = list(range(N_HEADS))

    @pl.when(i == 0)
    def _values():
        for pp in range(N_PAIRS):
            for kt in range(n_blk):
                tile = v_ref[kt * TQ:(kt + 1) * TQ, pp * LANES:(pp + 1) * LANES]
                vt_ref[pp, kt] = tile.astype(F32).T.astype(BF16)

    blk = lax.broadcasted_iota(jnp.int32, (n_blk, TQ), 0)
    eligible = blk < i
    for pp in range(N_PAIRS):
        q2 = q_ref[:, pp * LANES:(pp + 1) * LANES]
        km = km_ref[:, pp * LANES:(pp + 1) * LANES].astype(BF16)
        for j in range(2):
            gate = _dot_nt(km, jnp.where(halves[j], q2, jnp.zeros_like(q2)))[0:n_blk, :]
            gate = jnp.where(eligible, gate, NEG_INF)
            cnt = jnp.zeros((n_blk, TQ), jnp.int32)
            for n in range(n_blk):
                row = gate[n:n + 1, :]
                better = (row > gate) | ((row == gate) & (n < blk))
                cnt = cnt + better.astype(jnp.int32)
            chosen = (cnt < min(MOBA_TOPK, n_blk - 1)) & eligible
            vis = jnp.where(chosen | (blk == i), 1.0, 0.0)
            for n in range(n_blk):
                sel_ref[2 * pp + j, n] = vis[n:n + 1, :]
    _store_head_queries(q_ref, qh_ref)
    _init_state(state)

    def tile_inputs(kt, h):
        pp, j = divmod(h, 2)
        k = _rows(k_ref, kt, slice(pp * LANES, (pp + 1) * LANES))
        v_t = vt_ref[pp, kt, j * HEAD_DIM:(j + 1) * HEAD_DIM, :]
        return k, v_t, sel_ref[h, kt]

    def near_tile(jj):
        kt = i - jj
        def update(h, _s, slot):
            _, v_t, vis = tile_inputs(kt, h)
            _softmax_tile(state, h, s_ref[slot] + bb_ref[h, jj], v_t, vis=vis)

        def logits(h, slot):
            _stage_logits(s_ref, slot, _dot_nt(tile_inputs(kt, h)[0], qh_ref[h]))

        _run_chains(heads, logits, update)

    near_tile(0)
    n_near = jnp.minimum(i, 1) + 1

    @pl.when(i >= 1)
    def _():
        near_tile(1)

    def far_body(jj, _):
        kt = i - jj
        def update(h, s, slot):
            _, v_t, vis = tile_inputs(kt, h)
            _softmax_tile(state, h, s, v_t, vis=vis, cbias=far_ref[h])
        _run_chains(heads, lambda h, slot: _dot_nt(tile_inputs(kt, h)[0], qh_ref[h]), update)
        return 0

    lax.fori_loop(n_near, i + 1, far_body, 0)

    for pp in range(N_PAIRS):
        out_t = jnp.concatenate([_chain_out(state, 2 * pp), _chain_out(state, 2 * pp + 1)], axis=0)
        o_ref[:, pp * LANES:(pp + 1) * LANES] = out_t.T.astype(o_ref.dtype)


def _moba_attention(q, k, v, kmean, bb, far):
    b, s, _ = q.shape
    n_qt = s // TQ
    per_b = lambda bi, i: (bi, 0, 0)
    return pl.pallas_call(
        _moba_kernel, grid=(b, n_qt),
        in_specs=[pl.BlockSpec((None, TQ, MAIN_W), lambda bi, i: (bi, i, 0)),
                  pl.BlockSpec((None, s, MAIN_W), per_b), pl.BlockSpec((None, s, MAIN_W), per_b),
                  pl.BlockSpec((None, LANES, MAIN_W), per_b),
                  pl.BlockSpec((N_HEADS, 2, TQ, TQ), lambda bi, i: (0, 0, 0, 0)),
                  pl.BlockSpec(far.shape, lambda bi, i: (0, 0, 0))],
        out_specs=pl.BlockSpec((None, TQ, MAIN_W), lambda bi, i: (bi, i, 0)),
        out_shape=jax.ShapeDtypeStruct((b, s, MAIN_W), BF16),
        scratch_shapes=[pltpu.VMEM((N_PAIRS, n_qt, LANES, TQ), BF16), pltpu.VMEM((N_HEADS, TQ, LANES), BF16),
                        pltpu.VMEM((N_HEADS, n_qt, 1, TQ), F32),
                        pltpu.VMEM((N_HEADS, 1, TQ), F32), pltpu.VMEM((N_HEADS, 1, TQ), F32),
                        pltpu.VMEM((N_HEADS, HEAD_DIM, TQ), F32), pltpu.VMEM((QK_LOOKAHEAD, TQ, TQ), F32)],
        compiler_params=_params(("arbitrary", "arbitrary"), VMEM_LIMIT),
        name="moba_attention")(q, k, v, kmean, bb, far)


def _mem_kernel(q_ref, kv_ref, o_ref):
    lo, hi = _head_masks()
    halves = (lo, hi)
    for pp in range(MEM_HEADS // 2):
        q2 = q_ref[:, pp * LANES:(pp + 1) * LANES] * SCALE
        k2 = kv_ref[:, pp * LANES:(pp + 1) * LANES]
        v2 = kv_ref[:, MEM_W + pp * LANES:MEM_W + (pp + 1) * LANES]
        outs = []
        for j in range(2):
            s = _dot_nt(jnp.where(halves[j], q2, jnp.zeros_like(q2)), k2)
            e = jnp.exp(s - jnp.max(s, axis=1, keepdims=True))
            pr = e / jnp.sum(e, axis=1, keepdims=True)
            outs.append(_dot(pr.astype(BF16), v2))
        o_ref[:, pp * LANES:(pp + 1) * LANES] = jnp.where(lo, outs[0], outs[1]).astype(o_ref.dtype)


def _mem_attention(qm, mem_kv):
    b, s, _ = qm.shape
    n_mem = mem_kv.shape[1]
    return pl.pallas_call(
        _mem_kernel, grid=(b, s // TQ),
        in_specs=[pl.BlockSpec((None, TQ, MEM_W), lambda bi, i: (bi, i, 0)),
                  pl.BlockSpec((None, n_mem, 2 * MEM_W), lambda bi, i: (bi, 0, 0))],
        out_specs=pl.BlockSpec((None, TQ, MEM_W), lambda bi, i: (bi, i, 0)),
        out_shape=jax.ShapeDtypeStruct((b, s, MEM_W), BF16),
        compiler_params=_params(("arbitrary", "arbitrary")), name="mem_attention")(qm, mem_kv)


def _layer_norm(y, g, b):
    mu = jnp.mean(y, axis=-1, keepdims=True)
    var = jnp.mean(jnp.square(y - mu), axis=-1, keepdims=True)
    return (y - mu) * lax.rsqrt(var + LN_EPS) * g + b


def _outproj_kernel(om_ref, oe_ref, wa_ref, wb_ref, x_ref, g_ref, b_ref, o_ref):
    mix = _dot(om_ref[...], wa_ref[...]) + _dot(oe_ref[...], wb_ref[...])
    o_ref[...] = _layer_norm(ALPHA * x_ref[...] + mix, g_ref[...], b_ref[...])


def _outproj_ln(o_main, o_mem, w_a, w_b, x, g, b, tm=512):
    m = x.shape[0]
    row = lambda i: (i, 0)
    full = lambda i: (0, 0)
    return pl.pallas_call(
        _outproj_kernel, grid=(m // tm,),
        in_specs=[pl.BlockSpec((tm, MAIN_W), row), pl.BlockSpec((tm, MEM_W), row),
                  pl.BlockSpec(w_a.shape, full), pl.BlockSpec(w_b.shape, full),
                  pl.BlockSpec((tm, D_MODEL), row), pl.BlockSpec((1, D_MODEL), full), pl.BlockSpec((1, D_MODEL), full)],
        out_specs=pl.BlockSpec((tm, D_MODEL), row),
        out_shape=jax.ShapeDtypeStruct((m, D_MODEL), F32),
        compiler_params=_params(("arbitrary",), VMEM_LIMIT), name="outproj_ln")(o_main, o_mem, w_a, w_b, x, g, b)


FF_CHUNK = 256


def _ffn_kernel(x_ref, wa_ref, wb_ref, cw_ref, cb_ref, wo_ref, g_ref, b_ref, o_ref, xb_ref):
    c = pl.program_id(1)

    @pl.when(c == 0)
    def _():
        xb_ref[...] = x_ref[...].astype(BF16)

    xb = xb_ref[...]
    a = _dot(xb, wa_ref[...])
    gate = _dot(xb, wb_ref[...])
    rows = lax.broadcasted_iota(jnp.int32, a.shape, 0)
    a1 = jnp.where(rows >= 1, pltpu.roll(a, 1, 0), 0.0)
    a2 = jnp.where(rows >= 2, pltpu.roll(a, 2, 0), 0.0)
    conv = cw_ref[0:1, :] * a2 + cw_ref[1:2, :] * a1 + cw_ref[2:3, :] * a + cb_ref[...]
    hid = (jax.nn.gelu(conv) * gate).astype(BF16)
    part = _dot(hid, wo_ref[...])

    @pl.when(c == 0)
    def _():
        o_ref[...] = part

    @pl.when(c > 0)
    def _():
        o_ref[...] += part

    @pl.when(c == pl.num_programs(1) - 1)
    def _():
        o_ref[...] = _layer_norm(ALPHA * x_ref[...] + o_ref[...], g_ref[...], b_ref[...])


def _ffn_ln(x, w_in, conv_w, conv_b, w_out, g, b):
    bsz, s, d = x.shape
    n_c = D_FF // FF_CHUNK
    return pl.pallas_call(
        _ffn_kernel, grid=(bsz, n_c),
        in_specs=[pl.BlockSpec((None, s, d), lambda bi, c: (bi, 0, 0)),
                  pl.BlockSpec((d, FF_CHUNK), lambda bi, c: (0, c)),
                  pl.BlockSpec((d, FF_CHUNK), lambda bi, c: (0, n_c + c)),
                  pl.BlockSpec((CONV_W, FF_CHUNK), lambda bi, c: (0, c)),
                  pl.BlockSpec((1, FF_CHUNK), lambda bi, c: (0, c)),
                  pl.BlockSpec((FF_CHUNK, d), lambda bi, c: (c, 0)),
                  pl.BlockSpec((1, d), lambda bi, c: (0, 0)), pl.BlockSpec((1, d), lambda bi, c: (0, 0))],
        out_specs=pl.BlockSpec((None, s, d), lambda bi, c: (bi, 0, 0)),
        out_shape=jax.ShapeDtypeStruct((bsz, s, d), F32),
        scratch_shapes=[pltpu.VMEM((s, d), BF16)],
        compiler_params=_params(("arbitrary", "arbitrary"), VMEM_LIMIT),
        name="ffn_ln")(x, w_in, w_in, conv_w, conv_b, w_out, g, b)


def _dup(w):
    return jnp.concatenate([w, w], axis=1)


def _cmp_weights(pe_k, w1_k, w2_k, pe_v, w1_v, w2_v):
    half = CMP_LEN // 2
    zk = jnp.zeros((half, HEAD_DIM, CMP_HIDDEN), F32)

    def w1_half(sl):
        wk = w1_k.reshape(CMP_LEN, HEAD_DIM, CMP_HIDDEN)[sl]
        wv = w1_v.reshape(CMP_LEN, HEAD_DIM, CMP_HIDDEN)[sl]
        top = jnp.concatenate([wk, zk], axis=2)
        bot = jnp.concatenate([zk, wv], axis=2)
        return jnp.concatenate([top, bot], axis=1).reshape(half * 2 * HEAD_DIM, 2 * CMP_HIDDEN)

    w1 = jnp.stack([w1_half(slice(0, half)), w1_half(slice(half, CMP_LEN))]).astype(BF16)
    pe = jnp.concatenate([pe_k, pe_v], axis=1)
    pe = jnp.stack([pe[:half].reshape(-1), pe[half:].reshape(-1)])
    zo = jnp.zeros((CMP_HIDDEN, 2 * HEAD_DIM), F32)
    w2 = jnp.concatenate([jnp.concatenate([_dup(w2_k), zo], axis=1),
                          jnp.concatenate([zo, _dup(w2_v)], axis=1)], axis=0).astype(BF16)
    return pe, w1, w2


def _overlap_matrix(seq):
    n_cmp = (seq - CMP_LEN) // CMP_STRIDE + 1
    n_slc = seq // SLC_BLOCK
    start = np.arange(LANES) * CMP_STRIDE
    bs = np.arange(LANES) * SLC_BLOCK
    ovl_t = ((start[None, :] < bs[:, None] + SLC_BLOCK) & (start[None, :] + CMP_LEN > bs[:, None])
             & (np.arange(LANES)[None, :] < n_cmp) & (np.arange(LANES)[:, None] < n_slc))
    return jnp.asarray(ovl_t, BF16)


def kernel(x, mem, rel_bias, a_w_in, a_cmp_pe_k, a_cmp_w1_k, a_cmp_w2_k, a_cmp_pe_v, a_cmp_w1_v, a_cmp_w2_v,
           a_w_mem_kv, a_w_out, shared_w_kv, b_w_in, b_w_mem_kv, b_w_out, ln1_g, ln1_b, ln2_g, ln2_b,
           ffn_w_in, ffn_conv_w, ffn_conv_b, ffn_w_out):
    bsz, seq, d = x.shape
    n_mem = mem.shape[1]
    m = bsz * seq
    assert (seq, d) == (2048, D_MODEL) and seq % TQ == 0
    n_qt = seq // TQ

    bb, bc, far = _bias_tables(rel_bias, n_qt)
    ovl_t = _overlap_matrix(seq)
    memf = mem.reshape(bsz * n_mem, d)
    xf = x.reshape(m, d)

    def ffn(xcur, layer):
        return _ffn_ln(xcur.reshape(bsz, seq, d), ffn_w_in[layer].astype(BF16), ffn_conv_w[layer],
                       ffn_conv_b[layer][None, :], ffn_w_out[layer].astype(BF16),
                       ln2_g[layer][None, :], ln2_b[layer][None, :]).reshape(m, d)

    def out_ln(o_main, o_mem, w_out, xcur, layer):
        w = w_out.astype(BF16)
        return _outproj_ln(o_main.reshape(m, MAIN_W), o_mem.reshape(m, MEM_W), w[:MAIN_W], w[MAIN_W:], xcur,
                           ln1_g[layer][None, :], ln1_b[layer][None, :])

    w = a_w_in[0]
    c0 = MAIN_W
    cols = [w[:, c0 + kk * HEAD_DIM:c0 + (kk + 1) * HEAD_DIM] for kk in range(6)]
    c_g = c0 + 6 * HEAD_DIM
    w_g = jnp.pad(w[:, c_g:c_g + 3 * N_HEADS], ((0, 0), (0, LANES - 3 * N_HEADS)))
    w_qm = w[:, c_g + 3 * N_HEADS:]
    weights = [w[:, :MAIN_W], jnp.concatenate(cols[0:2], axis=1), _dup(cols[2]), _dup(cols[3]),
               _dup(cols[4]), _dup(cols[5]), w_g, w_qm]
    weights = [wi.astype(BF16) for wi in weights]
    q, kvc_tok, ks2, vs2, kw2, vw2, g, qm = _proj(
        xf, weights, [BF16, BF16, BF16, BF16, BF16, BF16, F32, BF16], 512, "proj_a")

    pe, w1, w2 = _cmp_weights(a_cmp_pe_k[0], a_cmp_w1_k[0], a_cmp_w2_k[0],
                              a_cmp_pe_v[0], a_cmp_w1_v[0], a_cmp_w2_v[0])
    kvc = _compress(kvc_tok.reshape(bsz, seq // (CMP_LEN // 2), (CMP_LEN // 2) * LANES), pe, w1, w2)

    r3 = lambda t: t.reshape(bsz, seq, t.shape[-1])
    o_main = _nsa_attention(r3(q), r3(g), kvc, r3(ks2), r3(vs2), r3(kw2), r3(vw2), bb, far, bc, ovl_t)
    (mkv,) = _proj(memf, [a_w_mem_kv[0].astype(BF16)], [BF16], 512, "proj_mem_a")
    o_mem = _mem_attention(r3(qm), mkv.reshape(bsz, n_mem, 2 * MEM_W))
    x1 = out_ln(o_main, o_mem, a_w_out[0], xf, 0)
    x1 = ffn(x1, 0)

    wb = b_w_in[0].astype(BF16)
    wkv = shared_w_kv.astype(BF16)
    q, qm, k, v, kmean = _proj_b(x1, wb[:, :MAIN_W], wb[:, MAIN_W:], wkv[:, :MAIN_W], wkv[:, MAIN_W:])
    n_blk = seq // MOBA_BLOCK
    kmean = jnp.pad(kmean.reshape(bsz, n_blk, MAIN_W), ((0, 0), (0, LANES - n_blk), (0, 0)))
    o_main = _moba_attention(r3(q), r3(k), r3(v), kmean, bb, far)
    (mkv,) = _proj(memf, [b_w_mem_kv[0].astype(BF16)], [BF16], 512, "proj_mem_b")
    o_mem = _mem_attention(r3(qm), mkv.reshape(bsz, n_mem, 2 * MEM_W))
    x2 = out_ln(o_main, o_mem, b_w_out[0], x1, 1)
    x2 = ffn(x2, 1)
    return x2.reshape(bsz, seq, d)
```

```python
import math

import numpy as np
import jax
import jax.numpy as jnp
from jax import lax
from jax.experimental import pallas as pl
from jax.experimental.pallas import tpu as pltpu

F32 = jnp.float32
BF16 = jnp.bfloat16

D_MODEL = 1024
HEAD_DIM = 64
N_HEADS = 12
N_PAIRS = N_HEADS // 2
MAIN_W = N_HEADS * HEAD_DIM
MEM_HEADS = 4
MEM_W = MEM_HEADS * HEAD_DIM
CMP_LEN = 32
CMP_STRIDE = 16
CMP_HIDDEN = 256
SLC_BLOCK = 64
SLC_TOPK = 16
WINDOW = 512
FORCE_SCORE = 1.0e4
MOBA_BLOCK = 256
MOBA_TOPK = 3
REL_BUCKETS = 32
REL_MAX_DIST = 128
D_FF = 2816
CONV_W = 3
DEPTH = 2
ALPHA = (2.0 * DEPTH) ** 0.25
LN_EPS = 1e-5
NEG_INF = -1e30
TINY = 1e-30
SCALE = HEAD_DIM ** -0.5

LANES = 128
TQ = 256
VMEM_LIMIT = 60 * 1024 * 1024
QK_LOOKAHEAD = 4


def _dot(a, b):
    return jnp.dot(a, b, preferred_element_type=F32)


def _dot_nt(a, b):
    return lax.dot_general(a, b, (((1,), (1,)), ((), ())), preferred_element_type=F32)


def _params(sem, vmem=None):
    return pltpu.CompilerParams(dimension_semantics=sem, vmem_limit_bytes=vmem)


def _proj_kernel(x_ref, *refs):
    n = len(refs) // 2
    xb = x_ref[...].astype(BF16)
    for w_ref, o_ref in zip(refs[:n], refs[n:]):
        o_ref[...] = _dot(xb, w_ref[...]).astype(o_ref.dtype)


def _proj(x, weights, dtypes, tm, name):
    m, k = x.shape
    in_specs = [pl.BlockSpec((tm, k), lambda i: (i, 0))]
    in_specs += [pl.BlockSpec(w.shape, lambda i: (0, 0)) for w in weights]
    out_specs = [pl.BlockSpec((tm, w.shape[1]), lambda i: (i, 0)) for w in weights]
    out_shape = [jax.ShapeDtypeStruct((m, w.shape[1]), dt) for w, dt in zip(weights, dtypes)]
    return pl.pallas_call(
        _proj_kernel, grid=(m // tm,), in_specs=in_specs, out_specs=out_specs, out_shape=out_shape,
        compiler_params=_params(("arbitrary",), VMEM_LIMIT), name=name)(x, *weights)


def _projb_kernel(x_ref, wq_ref, wm_ref, wk_ref, wv_ref, q_ref, qm_ref, k_ref, v_ref, km_ref):
    xb = x_ref[...].astype(BF16)
    q_ref[...] = _dot(xb, wq_ref[...]).astype(BF16)
    qm_ref[...] = _dot(xb, wm_ref[...]).astype(BF16)
    k = _dot(xb, wk_ref[...])
    k_ref[...] = k.astype(BF16)
    km_ref[...] = jnp.mean(k, axis=0, keepdims=True)
    v_ref[...] = _dot(xb, wv_ref[...]).astype(BF16)


def _proj_b(x, wq, wm, wk, wv):
    m, k = x.shape
    nblk = m // MOBA_BLOCK
    row = lambda i: (i, 0)
    full = lambda i: (0, 0)
    return pl.pallas_call(
        _projb_kernel, grid=(nblk,),
        in_specs=[pl.BlockSpec((MOBA_BLOCK, k), row)] + [pl.BlockSpec(w.shape, full) for w in (wq, wm, wk, wv)],
        out_specs=[pl.BlockSpec((MOBA_BLOCK, MAIN_W), row), pl.BlockSpec((MOBA_BLOCK, MEM_W), row),
                   pl.BlockSpec((MOBA_BLOCK, MAIN_W), row), pl.BlockSpec((MOBA_BLOCK, MAIN_W), row),
                   pl.BlockSpec((None, 1, MAIN_W), lambda i: (i, 0, 0))],
        out_shape=[jax.ShapeDtypeStruct((m, MAIN_W), BF16), jax.ShapeDtypeStruct((m, MEM_W), BF16),
                   jax.ShapeDtypeStruct((m, MAIN_W), BF16), jax.ShapeDtypeStruct((m, MAIN_W), BF16),
                   jax.ShapeDtypeStruct((nblk, 1, MAIN_W), F32)],
        compiler_params=_params(("arbitrary",), VMEM_LIMIT), name="proj_b")(x, wq, wm, wk, wv)


def _bias_kernel(tbl_ref, bb_ref, bc_ref, far_ref):
    h = pl.program_id(0)

    def bias_of(dist):
        n = jnp.maximum(dist, 0)
        max_exact = REL_BUCKETS // 2
        nf = jnp.maximum(n, 1).astype(F32)
        large = max_exact + (jnp.log(nf / max_exact) / math.log(REL_MAX_DIST / max_exact)
                             * (REL_BUCKETS - max_exact)).astype(jnp.int32)
        large = jnp.minimum(large, REL_BUCKETS - 1)
        bucket = jnp.where(n < max_exact, n, large)
        out = jnp.zeros(dist.shape, F32)
        for kk in range(REL_BUCKETS):
            out = jnp.where(bucket == kk, tbl_ref[h * REL_BUCKETS + kk], out)
        return out

    key = lax.broadcasted_iota(jnp.int32, (TQ, TQ), 0)
    qry = lax.broadcasted_iota(jnp.int32, (TQ, TQ), 1)
    d0 = qry - key
    bb_ref[0] = jnp.where(d0 >= 0, bias_of(d0), NEG_INF)
    bb_ref[1] = bias_of(d0 + TQ)
    far = bias_of(d0 + 2 * TQ)
    bb_ref[2] = far
    bb_ref[3] = jnp.where(d0 + 2 * TQ < WINDOW, far, NEG_INF)
    far_ref[...] = far[0:1, :]

    nc = lax.broadcasted_iota(jnp.int32, (LANES, TQ), 0)
    qc = lax.broadcasted_iota(jnp.int32, (LANES, TQ), 1)
    n_cmp = (2048 - CMP_LEN) // CMP_STRIDE + 1
    for i in range(bc_ref.shape[0]):
        dc = i * TQ + qc - (nc * CMP_STRIDE + CMP_LEN - 1)
        bc_ref[i] = jnp.where((dc >= 0) & (nc < n_cmp), bias_of(dc), NEG_INF)


def _bias_tables(rel_bias, n_qt):
    tbl = rel_bias.T.reshape(-1)
    return pl.pallas_call(
        _bias_kernel, grid=(N_HEADS,),
        in_specs=[pl.BlockSpec(memory_space=pltpu.SMEM)],
        out_specs=[pl.BlockSpec((None, 4, TQ, TQ), lambda h: (h, 0, 0, 0)),
                   pl.BlockSpec((n_qt, None, LANES, TQ), lambda h: (0, h, 0, 0)),
                   pl.BlockSpec((None, 1, TQ), lambda h: (h, 0, 0))],
        out_shape=[jax.ShapeDtypeStruct((N_HEADS, 4, TQ, TQ), F32),
                   jax.ShapeDtypeStruct((n_qt, N_HEADS, LANES, TQ), F32),
                   jax.ShapeDtypeStruct((N_HEADS, 1, TQ), F32)],
        compiler_params=_params(("arbitrary",)), name="bias_tables")(tbl)


def _cmp_kernel(kv_ref, pe_ref, w1_ref, w2_ref, o_ref):
    x = kv_ref[...].astype(F32)
    lo = _dot((x + pe_ref[0:1, :]).astype(BF16), w1_ref[0])
    hi = _dot((x + pe_ref[1:2, :]).astype(BF16), w1_ref[1])
    nrow = x.shape[0]
    hid = lo + pltpu.roll(hi, nrow - 1, 0)
    o_ref[...] = _dot(jax.nn.gelu(hid).astype(BF16), w2_ref[...])


def _compress(kvr, pe, w1, w2):
    b, nrow, width = kvr.shape
    return pl.pallas_call(
        _cmp_kernel, grid=(b,),
        in_specs=[pl.BlockSpec((None, nrow, width), lambda i: (i, 0, 0)),
                  pl.BlockSpec(pe.shape, lambda i: (0, 0)),
                  pl.BlockSpec(w1.shape, lambda i: (0, 0, 0)),
                  pl.BlockSpec(w2.shape, lambda i: (0, 0))],
        out_specs=pl.BlockSpec((None, nrow, w2.shape[1]), lambda i: (i, 0, 0)),
        out_shape=jax.ShapeDtypeStruct((b, nrow, w2.shape[1]), F32),
        compiler_params=_params(("arbitrary",), VMEM_LIMIT), name="nsa_compress")(kvr, pe, w1, w2)


def _softmax_tile(state, c, s_t, v_t, vis=None, cbias=None):
    m_ref, l_ref, acc_ref = state
    m = m_ref[c]
    n_blk = 1 if vis is None else vis.shape[0]
    rows = s_t.shape[0] // n_blk
    parts = [s_t[b * rows:(b + 1) * rows] for b in range(n_blk)]
    tile_max = None
    for b in range(n_blk):
        mb = jnp.max(parts[b], axis=0, keepdims=True)
        if vis is not None:
            mb = jnp.where(vis[b:b + 1] > 0.5, mb, NEG_INF)
        tile_max = mb if tile_max is None else jnp.maximum(tile_max, mb)
    if cbias is not None:
        tile_max = tile_max + cbias
    m_new = jnp.maximum(m, tile_max)
    shift = m_new if cbias is None else m_new - cbias
    p_parts = []
    for b in range(n_blk):
        off = shift if vis is None else jnp.where(vis[b:b + 1] > 0.5, shift, -NEG_INF)
        p_parts.append(jnp.exp(parts[b] - off))
    p_t = p_parts[0] if n_blk == 1 else jnp.concatenate(p_parts, axis=0)
    alpha = jnp.exp(m - m_new)
    l_ref[c] = alpha * l_ref[c] + jnp.sum(p_t, axis=0, keepdims=True)
    acc_ref[c] = alpha * acc_ref[c] + _dot(v_t, p_t.astype(BF16))
    m_ref[c] = m_new


def _run_chains(chains, logits_fn, update_fn):
    pending = {}
    for idx, c in enumerate(chains[:QK_LOOKAHEAD]):
        pending[c] = logits_fn(c, idx)
    for idx, c in enumerate(chains):
        update_fn(c, pending.pop(c), idx % QK_LOOKAHEAD)
        if idx + QK_LOOKAHEAD < len(chains):
            nxt = chains[idx + QK_LOOKAHEAD]
            pending[nxt] = logits_fn(nxt, idx % QK_LOOKAHEAD)


def _stage_logits(s_ref, slot, s_t):
    s_ref[slot] = s_t


def _init_state(state):
    m_ref, l_ref, acc_ref = state
    m_ref[...] = jnp.full(m_ref.shape, NEG_INF, F32)
    l_ref[...] = jnp.zeros(l_ref.shape, F32)
    acc_ref[...] = jnp.zeros(acc_ref.shape, F32)


def _chain_out(state, c):
    _, l_ref, acc_ref = state
    return acc_ref[c] / jnp.maximum(l_ref[c], TINY)


def _rows(ref, kt, cols=slice(None)):
    return ref[pl.ds(pl.multiple_of(kt * TQ, TQ), TQ), cols]


def _head_masks():
    lane = lax.broadcasted_iota(jnp.int32, (TQ, LANES), 1)
    lo = lane < HEAD_DIM
    return lo, jnp.logical_not(lo)


def _store_head_queries(q_ref, qh_ref):
    halves = _head_masks()
    for pp in range(N_PAIRS):
        q2 = q_ref[:, pp * LANES:(pp + 1) * LANES] * SCALE
        for j in range(2):
            qh_ref[2 * pp + j] = jnp.where(halves[j], q2, jnp.zeros_like(q2))


def _nsa_kernel(q_ref, g_ref, kvc_ref, ks_ref, vs_ref, kw_ref, vw_ref, bb_ref, far_ref, bc_ref, ovl_ref,
                o_ref, vst_ref, vwt_ref, qh_ref, gs_ref, oc_ref, sel_ref, m_ref, l_ref, acc_ref, s_ref):
    i = pl.program_id(1)
    n_kt = vst_ref.shape[0]
    state = (m_ref, l_ref, acc_ref)

    @pl.when(i == 0)
    def _values():
        for kt in range(n_kt):
            rows = slice(kt * TQ, (kt + 1) * TQ)
            vst_ref[kt] = vs_ref[rows, :].astype(F32).T[0:HEAD_DIM].astype(BF16)
            vwt_ref[kt] = vw_ref[rows, :].astype(F32).T[0:HEAD_DIM].astype(BF16)

    _store_head_queries(q_ref, qh_ref)
    gs_ref[...] = jax.nn.sigmoid(g_ref[...]).T

    kc2 = kvc_ref[:, 0:LANES].astype(BF16)
    vc_t = kvc_ref[:, LANES:2 * LANES].T[0:HEAD_DIM].astype(BF16)
    psums = []

    def cmp_logits(h, slot):
        s_ref[slot, 0:LANES, :] = _dot_nt(kc2, qh_ref[h])

    def cmp_update(h, _s, slot):
        bias = bc_ref[h]
        s = s_ref[slot, 0:LANES, :] + bias
        valid = bias > 0.5 * NEG_INF
        m = jnp.max(s, axis=0, keepdims=True)
        e = jnp.where(valid, jnp.exp(s - m), 0.0)
        pc = e / jnp.maximum(jnp.sum(e, axis=0, keepdims=True), TINY)
        psums[:] = [pc if not psums else psums[0] + pc]
        oc_ref[h] = gs_ref[3 * h:3 * h + 1, :] * _dot(vc_t, pc.astype(BF16))

    _run_chains(list(range(N_HEADS)), cmp_logits, cmp_update)
    psum = psums[0]
    p_hi = psum.astype(BF16)
    p_lo = (psum - p_hi.astype(F32)).astype(BF16)
    imp = _dot(ovl_ref[...], p_hi) + _dot(ovl_ref[...], p_lo)
    per_tile = TQ // SLC_BLOCK
    n_slc = n_kt * per_tile
    imp = imp[0:n_slc, :]
    sidx = lax.broadcasted_iota(jnp.int32, (n_slc, TQ), 0)
    t = i * TQ + lax.broadcasted_iota(jnp.int32, (n_slc, TQ), 1)
    cur = lax.shift_right_logical(t, 6)
    eligible = sidx <= cur
    forced = (sidx == 0) | (sidx == cur) | (sidx == cur - 1)
    score = jnp.where(eligible, jnp.where(forced, FORCE_SCORE, 0.0), NEG_INF) + imp
    cnt = jnp.zeros((n_slc, TQ), jnp.int32)
    for sp in range(n_slc):
        row = score[sp:sp + 1, :]
        better = (row > score) | ((row == score) & (sp < sidx))
        cnt = cnt + better.astype(jnp.int32)
    chosen = jnp.where(cnt < min(SLC_TOPK, n_slc), 1.0, 0.0)
    for kt in range(n_kt):
        sel_ref[kt, 0:per_tile, :] = chosen[kt * per_tile:(kt + 1) * per_tile, :]

    _init_state(state)
    slc = [("s", h) for h in range(N_HEADS)]
    win = [("w", h) for h in range(N_HEADS)]
    n_near = jnp.minimum(i, WINDOW // TQ) + 1

    def near_body(jj, _):
        kt = i - jj
        k_s, k_w, v_s, v_w = _rows(ks_ref, kt), _rows(kw_ref, kt), vst_ref[kt], vwt_ref[kt]
        vis = sel_ref[kt, 0:per_tile, :]
        b_s = jnp.minimum(jj, 2)
        b_w = jnp.where(jj == 2, 3, jj)

        def logits(c, slot):
            _stage_logits(s_ref, slot, _dot_nt(k_s if c[0] == "s" else k_w, qh_ref[c[1]]))

        def update(c, _s, slot):
            kind, h = c
            if kind == "s":
                _softmax_tile(state, h, s_ref[slot] + bb_ref[h, b_s], v_s, vis=vis)
            else:
                _softmax_tile(state, N_HEADS + h, s_ref[slot] + bb_ref[h, b_w], v_w)

        _run_chains([c for pair in zip(slc, win) for c in pair], logits, update)
        return 0

    lax.fori_loop(0, n_near, near_body, 0)

    def far_body(jj, _):
        kt = i - jj
        k_s, v_s = _rows(ks_ref, kt), vst_ref[kt]
        vis = sel_ref[kt, 0:per_tile, :]
        _run_chains(slc, lambda c, slot: _dot_nt(k_s, qh_ref[c[1]]),
                    lambda c, s, slot: _softmax_tile(state, c[1], s, v_s, vis=vis, cbias=far_ref[c[1]]))
        return 0

    lax.fori_loop(n_near, i + 1, far_body, 0)

    for pp in range(N_PAIRS):
        hs = (2 * pp, 2 * pp + 1)
        outs = []
        for h in hs:
            g1 = gs_ref[3 * h + 1:3 * h + 2, :]
            g2 = gs_ref[3 * h + 2:3 * h + 3, :]
            outs.append(oc_ref[h] + g1 * _chain_out(state, h) + g2 * _chain_out(state, N_HEADS + h))
        o_ref[:, pp * LANES:(pp + 1) * LANES] = jnp.concatenate(outs, axis=0).T.astype(o_ref.dtype)


def _nsa_attention(q, g, kvc, ks2, vs2, kw2, vw2, bb, far, bc, ovl_t):
    b, s, _ = q.shape
    n_qt = s // TQ
    per_b = lambda bi, i: (bi, 0, 0)
    return pl.pallas_call(
        _nsa_kernel, grid=(b, n_qt),
        in_specs=[pl.BlockSpec((None, TQ, MAIN_W), lambda bi, i: (bi, i, 0)),
                  pl.BlockSpec((None, TQ, LANES), lambda bi, i: (bi, i, 0)),
                  pl.BlockSpec((None,) + kvc.shape[1:], per_b),
                  pl.BlockSpec((None, s, LANES), per_b), pl.BlockSpec((None, s, LANES), per_b),
                  pl.BlockSpec((None, s, LANES), per_b), pl.BlockSpec((None, s, LANES), per_b),
                  pl.BlockSpec(bb.shape, lambda bi, i: (0, 0, 0, 0)),
                  pl.BlockSpec(far.shape, lambda bi, i: (0, 0, 0)),
                  pl.BlockSpec((None,) + bc.shape[1:], lambda bi, i: (i, 0, 0, 0)),
                  pl.BlockSpec(ovl_t.shape, lambda bi, i: (0, 0))],
        out_specs=pl.BlockSpec((None, TQ, MAIN_W), lambda bi, i: (bi, i, 0)),
        out_shape=jax.ShapeDtypeStruct((b, s, MAIN_W), BF16),
        scratch_shapes=[pltpu.VMEM((n_qt, HEAD_DIM, TQ), BF16), pltpu.VMEM((n_qt, HEAD_DIM, TQ), BF16),
                        pltpu.VMEM((N_HEADS, TQ, LANES), BF16), pltpu.VMEM((LANES, TQ), F32),
                        pltpu.VMEM((N_HEADS, HEAD_DIM, TQ), F32), pltpu.VMEM((n_qt, 8, TQ), F32),
                        pltpu.VMEM((2 * N_HEADS, 1, TQ), F32), pltpu.VMEM((2 * N_HEADS, 1, TQ), F32),
                        pltpu.VMEM((2 * N_HEADS, HEAD_DIM, TQ), F32), pltpu.VMEM((QK_LOOKAHEAD, TQ, TQ), F32)],
        compiler_params=_params(("arbitrary", "arbitrary"), VMEM_LIMIT),
        name="nsa_attention")(q, g, kvc, ks2, vs2, kw2, vw2, bb, far, bc, ovl_t)


def _moba_kernel(q_ref, k_ref, v_ref, km_ref, bb_ref, far_ref, o_ref, vt_ref, qh_ref, sel_ref,
                 m_ref, l_ref, acc_ref, s_ref):
    i = pl.program_id(1)
    halves = _head_masks()
    n_blk = vt_ref.shape[1]
    state = (m_ref, l_ref, acc_ref)
    heads = list(range(N_HEADS))

    @pl.when(i == 0)
    def _values():
        for pp in range(N_PAIRS):
            for kt in range(n_blk):
                tile = v_ref[kt * TQ:(kt + 1) * TQ, pp * LANES:(pp + 1) * LANES]
                vt_ref[pp, kt] = tile.astype(F32).T.astype(BF16)

    blk = lax.broadcasted_iota(jnp.int32, (n_blk, TQ), 0)
    eligible = blk < i
    for pp in range(N_PAIRS):
        q2 = q_ref[:, pp * LANES:(pp + 1) * LANES]
        km = km_ref[:, pp * LANES:(pp + 1) * LANES].astype(BF16)
        for j in range(2):
            gate = _dot_nt(km, jnp.where(halves[j], q2, jnp.zeros_like(q2)))[0:n_blk, :]
            gate = jnp.where(eligible, gate, NEG_INF)
            cnt = jnp.zeros((n_blk, TQ), jnp.int32)
            for n in range(n_blk):
                row = gate[n:n + 1, :]
                better = (row > gate) | ((row == gate) & (n < blk))
                cnt = cnt + better.astype(jnp.int32)
            chosen = (cnt < min(MOBA_TOPK, n_blk - 1)) & eligible
            vis = jnp.where(chosen | (blk == i), 1.0, 0.0)
            for n in range(n_blk):
                sel_ref[2 * pp + j, n] = vis[n:n + 1, :]
    _store_head_queries(q_ref, qh_ref)
    _init_state(state)

    def tile_inputs(kt, h):
        pp, j = divmod(h, 2)
        k = _rows(k_ref, kt, slice(pp * LANES, (pp + 1) * LANES))
        v_t = vt_ref[pp, kt, j * HEAD_DIM:(j + 1) * HEAD_DIM, :]
        return k, v_t, sel_ref[h, kt]

    def near_tile(jj):
        kt = i - jj
        def update(h, _s, slot):
            _, v_t, vis = tile_inputs(kt, h)
            _softmax_tile(state, h, s_ref[slot] + bb_ref[h, jj], v_t, vis=vis)

        def logits(h, slot):
            _stage_logits(s_ref, slot, _dot_nt(tile_inputs(kt, h)[0], qh_ref[h]))

        _run_chains(heads, logits, update)

    near_tile(0)
    n_near = jnp.minimum(i, 1) + 1

    @pl.when(i >= 1)
    def _():
        near_tile(1)

    def far_body(jj, _):
        kt = i - jj
        def update(h, s, slot):
            _, v_t, vis = tile_inputs(kt, h)
            _softmax_tile(state, h, s, v_t, vis=vis, cbias=far_ref[h])
        _run_chains(heads, lambda h, slot: _dot_nt(tile_inputs(kt, h)[0], qh_ref[h]), update)
        return 0

    lax.fori_loop(n_near, i + 1, far_body, 0)

    for pp in range(N_PAIRS):
        out_t = jnp.concatenate([_chain_out(state, 2 * pp), _chain_out(state, 2 * pp + 1)], axis=0)
        o_ref[:, pp * LANES:(pp + 1) * LANES] = out_t.T.astype(o_ref.dtype)


def _moba_attention(q, k, v, kmean, bb, far):
    b, s, _ = q.shape
    n_qt = s // TQ
    per_b = lambda bi, i: (bi, 0, 0)
    return pl.pallas_call(
        _moba_kernel, grid=(b, n_qt),
        in_specs=[pl.BlockSpec((None, TQ, MAIN_W), lambda bi, i: (bi, i, 0)),
                  pl.BlockSpec((None, s, MAIN_W), per_b), pl.BlockSpec((None, s, MAIN_W), per_b),
                  pl.BlockSpec((None, LANES, MAIN_W), per_b),
                  pl.BlockSpec((N_HEADS, 2, TQ, TQ), lambda bi, i: (0, 0, 0, 0)),
                  pl.BlockSpec(far.shape, lambda bi, i: (0, 0, 0))],
        out_specs=pl.BlockSpec((None, TQ, MAIN_W), lambda bi, i: (bi, i, 0)),
        out_shape=jax.ShapeDtypeStruct((b, s, MAIN_W), BF16),
        scratch_shapes=[pltpu.VMEM((N_PAIRS, n_qt, LANES, TQ), BF16), pltpu.VMEM((N_HEADS, TQ, LANES), BF16),
                        pltpu.VMEM((N_HEADS, n_qt, 1, TQ), F32),
                        pltpu.VMEM((N_HEADS, 1, TQ), F32), pltpu.VMEM((N_HEADS, 1, TQ), F32),
                        pltpu.VMEM((N_HEADS, HEAD_DIM, TQ), F32), pltpu.VMEM((QK_LOOKAHEAD, TQ, TQ), F32)],
        compiler_params=_params(("arbitrary", "arbitrary"), VMEM_LIMIT),
        name="moba_attention")(q, k, v, kmean, bb, far)


def _mem_kernel(q_ref, kv_ref, o_ref):
    lo, hi = _head_masks()
    halves = (lo, hi)
    for pp in range(MEM_HEADS // 2):
        q2 = q_ref[:, pp * LANES:(pp + 1) * LANES] * SCALE
        k2 = kv_ref[:, pp * LANES:(pp + 1) * LANES]
        v2 = kv_ref[:, MEM_W + pp * LANES:MEM_W + (pp + 1) * LANES]
        outs = []
        for j in range(2):
            s = _dot_nt(jnp.where(halves[j], q2, jnp.zeros_like(q2)), k2)
            e = jnp.exp(s - jnp.max(s, axis=1, keepdims=True))
            pr = e / jnp.sum(e, axis=1, keepdims=True)
            outs.append(_dot(pr.astype(BF16), v2))
        o_ref[:, pp * LANES:(pp + 1) * LANES] = jnp.where(lo, outs[0], outs[1]).astype(o_ref.dtype)


def _mem_attention(qm, mem_kv):
    b, s, _ = qm.shape
    n_mem = mem_kv.shape[1]
    return pl.pallas_call(
        _mem_kernel, grid=(b, s // TQ),
        in_specs=[pl.BlockSpec((None, TQ, MEM_W), lambda bi, i: (bi, i, 0)),
                  pl.BlockSpec((None, n_mem, 2 * MEM_W), lambda bi, i: (bi, 0, 0))],
        out_specs=pl.BlockSpec((None, TQ, MEM_W), lambda bi, i: (bi, i, 0)),
        out_shape=jax.ShapeDtypeStruct((b, s, MEM_W), BF16),
        compiler_params=_params(("arbitrary", "arbitrary")), name="mem_attention")(qm, mem_kv)


def _layer_norm(y, g, b):
    mu = jnp.mean(y, axis=-1, keepdims=True)
    var = jnp.mean(jnp.square(y - mu), axis=-1, keepdims=True)
    return (y - mu) * lax.rsqrt(var + LN_EPS) * g + b


def _outproj_kernel(om_ref, oe_ref, wa_ref, wb_ref, x_ref, g_ref, b_ref, o_ref):
    mix = _dot(om_ref[...], wa_ref[...]) + _dot(oe_ref[...], wb_ref[...])
    o_ref[...] = _layer_norm(ALPHA * x_ref[...] + mix, g_ref[...], b_ref[...])


def _outproj_ln(o_main, o_mem, w_a, w_b, x, g, b, tm=512):
    m = x.shape[0]
    row = lambda i: (i, 0)
    full = lambda i: (0, 0)
    return pl.pallas_call(
        _outproj_kernel, grid=(m // tm,),
        in_specs=[pl.BlockSpec((tm, MAIN_W), row), pl.BlockSpec((tm, MEM_W), row),
                  pl.BlockSpec(w_a.shape, full), pl.BlockSpec(w_b.shape, full),
                  pl.BlockSpec((tm, D_MODEL), row), pl.BlockSpec((1, D_MODEL), full), pl.BlockSpec((1, D_MODEL), full)],
        out_specs=pl.BlockSpec((tm, D_MODEL), row),
        out_shape=jax.ShapeDtypeStruct((m, D_MODEL), F32),
        compiler_params=_params(("arbitrary",), VMEM_LIMIT), name="outproj_ln")(o_main, o_mem, w_a, w_b, x, g, b)


FF_CHUNK = 256
FF_ROWS = 128
SUBLANES = 8


def _ffn_kernel(x_ref, wa_ref, wb_ref, cw_ref, cb_ref, wo_ref, g_ref, b_ref, o_ref, xb_ref, hid_ref):
    c = pl.program_id(1)
    last = pl.num_programs(1) - 1

    @pl.when(c == 0)
    def _():
        xb_ref[...] = x_ref[...].astype(BF16)
        hid_ref[1] = jnp.zeros(hid_ref.shape[1:], BF16)
        o_ref[...] = jnp.zeros(o_ref.shape, F32)

    blocks = [slice(r0, r0 + FF_ROWS) for r0 in range(0, xb_ref.shape[0], FF_ROWS)]

    def matmuls(rows):
        xb = xb_ref[rows, :]
        a = _dot(xb, wa_ref[...])
        gate = _dot(xb, wb_ref[...])
        o_ref[rows, :] += _dot(hid_ref[(c + 1) % 2, rows, :], wo_ref[...])
        return a, gate

    def elementwise(rows, a, gate, tail):
        ext = jnp.concatenate([tail, a], axis=0)
        a1 = pltpu.roll(ext, 1, 0)[SUBLANES:]
        a2 = pltpu.roll(ext, 2, 0)[SUBLANES:]
        conv = cw_ref[0:1, :] * a2 + cw_ref[1:2, :] * a1 + cw_ref[2:3, :] * a + cb_ref[...]
        hid_ref[c % 2, rows, :] = (jax.nn.gelu(conv) * gate).astype(BF16)
        return a[FF_ROWS - SUBLANES:]

    tail = jnp.zeros((SUBLANES, wa_ref.shape[1]), F32)
    ready = matmuls(blocks[0])
    for r, rows in enumerate(blocks):
        cur = ready
        if r + 1 < len(blocks):
            ready = matmuls(blocks[r + 1])
        tail = elementwise(rows, cur[0], cur[1], tail)

    @pl.when(c == last)
    def _():
        o_ref[...] = _layer_norm(ALPHA * x_ref[...] + o_ref[...], g_ref[...], b_ref[...])


def _ffn_ln(x, w_in, conv_w, conv_b, w_out, g, b):
    bsz, s, d = x.shape
    n_c = D_FF // FF_CHUNK
    cur = lambda c: jnp.minimum(c, n_c - 1)
    prev = lambda c: jnp.maximum(c - 1, 0)
    return pl.pallas_call(
        _ffn_kernel, grid=(bsz, n_c + 1),
        in_specs=[pl.BlockSpec((None, s, d), lambda bi, c: (bi, 0, 0)),
                  pl.BlockSpec((d, FF_CHUNK), lambda bi, c: (0, cur(c))),
                  pl.BlockSpec((d, FF_CHUNK), lambda bi, c: (0, n_c + cur(c))),
                  pl.BlockSpec((CONV_W, FF_CHUNK), lambda bi, c: (0, cur(c))),
                  pl.BlockSpec((1, FF_CHUNK), lambda bi, c: (0, cur(c))),
                  pl.BlockSpec((FF_CHUNK, d), lambda bi, c: (prev(c), 0)),
                  pl.BlockSpec((1, d), lambda bi, c: (0, 0)), pl.BlockSpec((1, d), lambda bi, c: (0, 0))],
        out_specs=pl.BlockSpec((None, s, d), lambda bi, c: (bi, 0, 0)),
        out_shape=jax.ShapeDtypeStruct((bsz, s, d), F32),
        scratch_shapes=[pltpu.VMEM((s, d), BF16), pltpu.VMEM((2, s, FF_CHUNK), BF16)],
        compiler_params=_params(("arbitrary", "arbitrary"), VMEM_LIMIT),
        name="ffn_ln")(x, w_in, w_in, conv_w, conv_b, w_out, g, b)


def _dup(w):
    return jnp.concatenate([w, w], axis=1)


def _cmp_weights(pe_k, w1_k, w2_k, pe_v, w1_v, w2_v):
    half = CMP_LEN // 2
    zk = jnp.zeros((half, HEAD_DIM, CMP_HIDDEN), F32)

    def w1_half(sl):
        wk = w1_k.reshape(CMP_LEN, HEAD_DIM, CMP_HIDDEN)[sl]
        wv = w1_v.reshape(CMP_LEN, HEAD_DIM, CMP_HIDDEN)[sl]
        top = jnp.concatenate([wk, zk], axis=2)
        bot = jnp.concatenate([zk, wv], axis=2)
        return jnp.concatenate([top, bot], axis=1).reshape(half * 2 * HEAD_DIM, 2 * CMP_HIDDEN)

    w1 = jnp.stack([w1_half(slice(0, half)), w1_half(slice(half, CMP_LEN))]).astype(BF16)
    pe = jnp.concatenate([pe_k, pe_v], axis=1)
    pe = jnp.stack([pe[:half].reshape(-1), pe[half:].reshape(-1)])
    zo = jnp.zeros((CMP_HIDDEN, 2 * HEAD_DIM), F32)
    w2 = jnp.concatenate([jnp.concatenate([_dup(w2_k), zo], axis=1),
                          jnp.concatenate([zo, _dup(w2_v)], axis=1)], axis=0).astype(BF16)
    return pe, w1, w2


def _overlap_matrix(seq):
    n_cmp = (seq - CMP_LEN) // CMP_STRIDE + 1
    n_slc = seq // SLC_BLOCK
    start = np.arange(LANES) * CMP_STRIDE
    bs = np.arange(LANES) * SLC_BLOCK
    ovl_t = ((start[None, :] < bs[:, None] + SLC_BLOCK) & (start[None, :] + CMP_LEN > bs[:, None])
             & (np.arange(LANES)[None, :] < n_cmp) & (np.arange(LANES)[:, None] < n_slc))
    return jnp.asarray(ovl_t, BF16)


def kernel(x, mem, rel_bias, a_w_in, a_cmp_pe_k, a_cmp_w1_k, a_cmp_w2_k, a_cmp_pe_v, a_cmp_w1_v, a_cmp_w2_v,
           a_w_mem_kv, a_w_out, shared_w_kv, b_w_in, b_w_mem_kv, b_w_out, ln1_g, ln1_b, ln2_g, ln2_b,
           ffn_w_in, ffn_conv_w, ffn_conv_b, ffn_w_out):
    bsz, seq, d = x.shape
    n_mem = mem.shape[1]
    m = bsz * seq
    assert (seq, d) == (2048, D_MODEL) and seq % TQ == 0
    n_qt = seq // TQ

    bb, bc, far = _bias_tables(rel_bias, n_qt)
    ovl_t = _overlap_matrix(seq)
    memf = mem.reshape(bsz * n_mem, d)
    xf = x.reshape(m, d)

    def ffn(xcur, layer):
        return _ffn_ln(xcur.reshape(bsz, seq, d), ffn_w_in[layer].astype(BF16), ffn_conv_w[layer],
                       ffn_conv_b[layer][None, :], ffn_w_out[layer].astype(BF16),
                       ln2_g[layer][None, :], ln2_b[layer][None, :]).reshape(m, d)

    def out_ln(o_main, o_mem, w_out, xcur, layer):
        w = w_out.astype(BF16)
        return _outproj_ln(o_main.reshape(m, MAIN_W), o_mem.reshape(m, MEM_W), w[:MAIN_W], w[MAIN_W:], xcur,
                           ln1_g[layer][None, :], ln1_b[layer][None, :])

    w = a_w_in[0]
    c0 = MAIN_W
    cols = [w[:, c0 + kk * HEAD_DIM:c0 + (kk + 1) * HEAD_DIM] for kk in range(6)]
    c_g = c0 + 6 * HEAD_DIM
    w_g = jnp.pad(w[:, c_g:c_g + 3 * N_HEADS], ((0, 0), (0, LANES - 3 * N_HEADS)))
    w_qm = w[:, c_g + 3 * N_HEADS:]
    weights = [w[:, :MAIN_W], jnp.concatenate(cols[0:2], axis=1), _dup(cols[2]), _dup(cols[3]),
               _dup(cols[4]), _dup(cols[5]), w_g, w_qm]
    weights = [wi.astype(BF16) for wi in weights]
    q, kvc_tok, ks2, vs2, kw2, vw2, g, qm = _proj(
        xf, weights, [BF16, BF16, BF16, BF16, BF16, BF16, F32, BF16], 512, "proj_a")

    pe, w1, w2 = _cmp_weights(a_cmp_pe_k[0], a_cmp_w1_k[0], a_cmp_w2_k[0],
                              a_cmp_pe_v[0], a_cmp_w1_v[0], a_cmp_w2_v[0])
    kvc = _compress(kvc_tok.reshape(bsz, seq // (CMP_LEN // 2), (CMP_LEN // 2) * LANES), pe, w1, w2)

    r3 = lambda t: t.reshape(bsz, seq, t.shape[-1])
    o_main = _nsa_attention(r3(q), r3(g), kvc, r3(ks2), r3(vs2), r3(kw2), r3(vw2), bb, far, bc, ovl_t)
    (mkv,) = _proj(memf, [a_w_mem_kv[0].astype(BF16)], [BF16], 512, "proj_mem_a")
    o_mem = _mem_attention(r3(qm), mkv.reshape(bsz, n_mem, 2 * MEM_W))
    x1 = out_ln(o_main, o_mem, a_w_out[0], xf, 0)
    x1 = ffn(x1, 0)

    wb = b_w_in[0].astype(BF16)
    wkv = shared_w_kv.astype(BF16)
    q, qm, k, v, kmean = _proj_b(x1, wb[:, :MAIN_W], wb[:, MAIN_W:], wkv[:, :MAIN_W], wkv[:, MAIN_W:])
    n_blk = seq // MOBA_BLOCK
    kmean = jnp.pad(kmean.reshape(bsz, n_blk, MAIN_W), ((0, 0), (0, LANES - n_blk), (0, 0)))
    o_main = _moba_attention(r3(q), r3(k), r3(v), kmean, bb, far)
    (mkv,) = _proj(memf, [b_w_mem_kv[0].astype(BF16)], [BF16], 512, "proj_mem_b")
    o_mem = _mem_attention(r3(qm), mkv.reshape(bsz, n_mem, 2 * MEM_W))
    x2 = out_ln(o_main, o_mem, b_w_out[0], x1, 1)
    x2 = ffn(x2, 1)
    return x2.reshape(bsz, seq, d)
```

```python
import functools
import math

import numpy as np
import jax
import jax.numpy as jnp
from jax import lax
from jax.experimental import pallas as pl
from jax.experimental.pallas import tpu as pltpu

F32 = jnp.float32
BF16 = jnp.bfloat16

D_MODEL = 1024
HEAD_DIM = 64
N_HEADS = 12
N_PAIRS = N_HEADS // 2
MAIN_W = N_HEADS * HEAD_DIM
MEM_HEADS = 4
MEM_W = MEM_HEADS * HEAD_DIM
CMP_LEN = 32
CMP_STRIDE = 16
CMP_HIDDEN = 256
SLC_BLOCK = 64
SLC_TOPK = 16
WINDOW = 512
FORCE_SCORE = 1.0e4
MOBA_BLOCK = 256
MOBA_TOPK = 3
REL_BUCKETS = 32
REL_MAX_DIST = 128
D_FF = 2816
CONV_W = 3
DEPTH = 2
ALPHA = (2.0 * DEPTH) ** 0.25
LN_EPS = 1e-5
NEG_INF = -1e30
TINY = 1e-30
LOG2E = math.log2(math.e)
Q_SCALE = HEAD_DIM ** -0.5 * LOG2E

LANES = 128
TQ = 256
VMEM_LIMIT = 60 * 1024 * 1024
QK_LOOKAHEAD = 4


def _dot(a, b):
    return jnp.dot(a, b, preferred_element_type=F32)


def _dot_nt(a, b):
    return lax.dot_general(a, b, (((1,), (1,)), ((), ())), preferred_element_type=F32)


def _params(sem, vmem=None):
    return pltpu.CompilerParams(dimension_semantics=sem, vmem_limit_bytes=vmem)


def _proj_kernel(x_ref, *refs, scales):
    n = len(refs) // 2
    xb = x_ref[...].astype(BF16)
    for w_ref, o_ref, scale in zip(refs[:n], refs[n:], scales):
        y = _dot(xb, w_ref[...])
        o_ref[...] = (y if scale is None else y * scale).astype(o_ref.dtype)


def _proj(x, weights, dtypes, tm, name, scales=None):
    m, k = x.shape
    scales = tuple(scales) if scales is not None else (None,) * len(weights)
    in_specs = [pl.BlockSpec((tm, k), lambda i: (i, 0))]
    in_specs += [pl.BlockSpec(w.shape, lambda i: (0, 0)) for w in weights]
    out_specs = [pl.BlockSpec((tm, w.shape[1]), lambda i: (i, 0)) for w in weights]
    out_shape = [jax.ShapeDtypeStruct((m, w.shape[1]), dt) for w, dt in zip(weights, dtypes)]
    return pl.pallas_call(
        functools.partial(_proj_kernel, scales=scales), grid=(m // tm,), in_specs=in_specs,
        out_specs=out_specs, out_shape=out_shape,
        compiler_params=_params(("arbitrary",), VMEM_LIMIT), name=name)(x, *weights)


def _projb_kernel(x_ref, wq_ref, wm_ref, wk_ref, wv_ref, q_ref, qm_ref, k_ref, v_ref, km_ref):
    xb = x_ref[...].astype(BF16)
    q_ref[...] = (_dot(xb, wq_ref[...]) * Q_SCALE).astype(BF16)
    qm_ref[...] = (_dot(xb, wm_ref[...]) * Q_SCALE).astype(BF16)
    k = _dot(xb, wk_ref[...])
    k_ref[...] = k.astype(BF16)
    km_ref[...] = jnp.mean(k, axis=0, keepdims=True)
    v_ref[...] = _dot(xb, wv_ref[...]).astype(BF16)


def _proj_b(x, wq, wm, wk, wv):
    m, k = x.shape
    nblk = m // MOBA_BLOCK
    row = lambda i: (i, 0)
    full = lambda i: (0, 0)
    return pl.pallas_call(
        _projb_kernel, grid=(nblk,),
        in_specs=[pl.BlockSpec((MOBA_BLOCK, k), row)] + [pl.BlockSpec(w.shape, full) for w in (wq, wm, wk, wv)],
        out_specs=[pl.BlockSpec((MOBA_BLOCK, MAIN_W), row), pl.BlockSpec((MOBA_BLOCK, MEM_W), row),
                   pl.BlockSpec((MOBA_BLOCK, MAIN_W), row), pl.BlockSpec((MOBA_BLOCK, MAIN_W), row),
                   pl.BlockSpec((None, 1, MAIN_W), lambda i: (i, 0, 0))],
        out_shape=[jax.ShapeDtypeStruct((m, MAIN_W), BF16), jax.ShapeDtypeStruct((m, MEM_W), BF16),
                   jax.ShapeDtypeStruct((m, MAIN_W), BF16), jax.ShapeDtypeStruct((m, MAIN_W), BF16),
                   jax.ShapeDtypeStruct((nblk, 1, MAIN_W), F32)],
        compiler_params=_params(("arbitrary",), VMEM_LIMIT), name="proj_b")(x, wq, wm, wk, wv)


def _bias_kernel(tbl_ref, bb_ref, bc_ref, far_ref):
    h = pl.program_id(0)

    def bias_of(dist):
        n = jnp.maximum(dist, 0)
        max_exact = REL_BUCKETS // 2
        nf = jnp.maximum(n, 1).astype(F32)
        large = max_exact + (jnp.log(nf / max_exact) / math.log(REL_MAX_DIST / max_exact)
                             * (REL_BUCKETS - max_exact)).astype(jnp.int32)
        large = jnp.minimum(large, REL_BUCKETS - 1)
        bucket = jnp.where(n < max_exact, n, large)
        out = jnp.zeros(dist.shape, F32)
        for kk in range(REL_BUCKETS):
            out = jnp.where(bucket == kk, tbl_ref[h * REL_BUCKETS + kk], out)
        return out * LOG2E

    key = lax.broadcasted_iota(jnp.int32, (TQ, TQ), 0)
    qry = lax.broadcasted_iota(jnp.int32, (TQ, TQ), 1)
    d0 = qry - key
    bb_ref[0] = jnp.where(d0 >= 0, bias_of(d0), NEG_INF)
    bb_ref[1] = bias_of(d0 + TQ)
    far = bias_of(d0 + 2 * TQ)
    bb_ref[2] = far
    bb_ref[3] = jnp.where(d0 + 2 * TQ < WINDOW, far, NEG_INF)
    far_ref[...] = far[0:1, :]

    nc = lax.broadcasted_iota(jnp.int32, (LANES, TQ), 0)
    qc = lax.broadcasted_iota(jnp.int32, (LANES, TQ), 1)
    n_cmp = (2048 - CMP_LEN) // CMP_STRIDE + 1
    for i in range(bc_ref.shape[0]):
        dc = i * TQ + qc - (nc * CMP_STRIDE + CMP_LEN - 1)
        bc_ref[i] = jnp.where((dc >= 0) & (nc < n_cmp), bias_of(dc), NEG_INF)


def _bias_tables(rel_bias, n_qt):
    tbl = rel_bias.T.reshape(-1)
    return pl.pallas_call(
        _bias_kernel, grid=(N_HEADS,),
        in_specs=[pl.BlockSpec(memory_space=pltpu.SMEM)],
        out_specs=[pl.BlockSpec((None, 4, TQ, TQ), lambda h: (h, 0, 0, 0)),
                   pl.BlockSpec((n_qt, None, LANES, TQ), lambda h: (0, h, 0, 0)),
                   pl.BlockSpec((None, 1, TQ), lambda h: (h, 0, 0))],
        out_shape=[jax.ShapeDtypeStruct((N_HEADS, 4, TQ, TQ), F32),
                   jax.ShapeDtypeStruct((n_qt, N_HEADS, LANES, TQ), F32),
                   jax.ShapeDtypeStruct((N_HEADS, 1, TQ), F32)],
        compiler_params=_params(("arbitrary",)), name="bias_tables")(tbl)


def _cmp_kernel(kv_ref, pe_ref, w1_ref, w2_ref, o_ref):
    x = kv_ref[...].astype(F32)
    lo = _dot((x + pe_ref[0:1, :]).astype(BF16), w1_ref[0])
    hi = _dot((x + pe_ref[1:2, :]).astype(BF16), w1_ref[1])
    nrow = x.shape[0]
    hid = lo + pltpu.roll(hi, nrow - 1, 0)
    o_ref[...] = _dot(jax.nn.gelu(hid).astype(BF16), w2_ref[...])


def _compress(kvr, pe, w1, w2):
    b, nrow, width = kvr.shape
    return pl.pallas_call(
        _cmp_kernel, grid=(b,),
        in_specs=[pl.BlockSpec((None, nrow, width), lambda i: (i, 0, 0)),
                  pl.BlockSpec(pe.shape, lambda i: (0, 0)),
                  pl.BlockSpec(w1.shape, lambda i: (0, 0, 0)),
                  pl.BlockSpec(w2.shape, lambda i: (0, 0))],
        out_specs=pl.BlockSpec((None, nrow, w2.shape[1]), lambda i: (i, 0, 0)),
        out_shape=jax.ShapeDtypeStruct((b, nrow, w2.shape[1]), F32),
        compiler_params=_params(("arbitrary",), VMEM_LIMIT), name="nsa_compress")(kvr, pe, w1, w2)


def _softmax_tile(state, c, s_t, v_t, vis=None, cbias=None):
    m_ref, l_ref, acc_ref = state
    m = m_ref[c]
    n_blk = 1 if vis is None else vis.shape[0]
    rows = s_t.shape[0] // n_blk
    parts = [s_t[b * rows:(b + 1) * rows] for b in range(n_blk)]
    tile_max = None
    for b in range(n_blk):
        mb = jnp.max(parts[b], axis=0, keepdims=True)
        if vis is not None:
            mb = jnp.where(vis[b:b + 1] > 0.5, mb, NEG_INF)
        tile_max = mb if tile_max is None else jnp.maximum(tile_max, mb)
    if cbias is not None:
        tile_max = tile_max + cbias
    m_new = jnp.maximum(m, tile_max)
    shift = m_new if cbias is None else m_new - cbias
    p_parts = []
    for b in range(n_blk):
        off = shift if vis is None else jnp.where(vis[b:b + 1] > 0.5, shift, -NEG_INF)
        p_parts.append(jnp.exp2(parts[b] - off))
    p_t = p_parts[0] if n_blk == 1 else jnp.concatenate(p_parts, axis=0)
    alpha = jnp.exp2(m - m_new)
    l_ref[c] = alpha * l_ref[c] + jnp.sum(p_t, axis=0, keepdims=True)
    acc_ref[c] = alpha * acc_ref[c] + _dot(v_t, p_t.astype(BF16))
    m_ref[c] = m_new


def _run_chains(chains, logits_fn, update_fn):
    pending = {}
    for idx, c in enumerate(chains[:QK_LOOKAHEAD]):
        pending[c] = logits_fn(c, idx)
    for idx, c in enumerate(chains):
        update_fn(c, pending.pop(c), idx % QK_LOOKAHEAD)
        if idx + QK_LOOKAHEAD < len(chains):
            nxt = chains[idx + QK_LOOKAHEAD]
            pending[nxt] = logits_fn(nxt, idx % QK_LOOKAHEAD)


def _stage_logits(s_ref, slot, s_t):
    s_ref[slot] = s_t


def _init_state(state):
    m_ref, l_ref, acc_ref = state
    m_ref[...] = jnp.full(m_ref.shape, NEG_INF, F32)
    l_ref[...] = jnp.zeros(l_ref.shape, F32)
    acc_ref[...] = jnp.zeros(acc_ref.shape, F32)


def _chain_out(state, c):
    _, l_ref, acc_ref = state
    return acc_ref[c] / jnp.maximum(l_ref[c], TINY)


def _rows(ref, kt, cols=slice(None)):
    return ref[pl.ds(pl.multiple_of(kt * TQ, TQ), TQ), cols]


def _store_head_queries(q_ref, qh_ref, n_pairs):
    row = lax.broadcasted_iota(jnp.int32, (LANES, TQ), 0)
    halves = (row < HEAD_DIM, row >= HEAD_DIM)
    for pp in range(n_pairs):
        q2t = q_ref[:, pp * LANES:(pp + 1) * LANES].astype(F32).T
        for j in range(2):
            qh_ref[2 * pp + j] = jnp.where(halves[j], q2t, 0.0).astype(BF16)


def _nsa_kernel(q_ref, g_ref, kvc_ref, ks_ref, vs_ref, kw_ref, vw_ref, bb_ref, far_ref, bc_ref, ovl_ref,
                o_ref, vst_ref, vwt_ref, qh_ref, gs_ref, oc_ref, sel_ref, m_ref, l_ref, acc_ref, s_ref):
    i = pl.program_id(1)
    n_kt = vst_ref.shape[0]
    state = (m_ref, l_ref, acc_ref)

    @pl.when(i == 0)
    def _values():
        for kt in range(n_kt):
            rows = slice(kt * TQ, (kt + 1) * TQ)
            vst_ref[kt] = vs_ref[rows, :].astype(F32).T[0:HEAD_DIM].astype(BF16)
            vwt_ref[kt] = vw_ref[rows, :].astype(F32).T[0:HEAD_DIM].astype(BF16)

    _store_head_queries(q_ref, qh_ref, N_PAIRS)
    gs_ref[...] = jax.nn.sigmoid(g_ref[...]).T

    kc2 = kvc_ref[:, 0:LANES].astype(BF16)
    vc_t = kvc_ref[:, LANES:2 * LANES].T[0:HEAD_DIM].astype(BF16)
    psums = []

    def cmp_logits(h, slot):
        s_ref[slot, 0:LANES, :] = _dot(kc2, qh_ref[h])

    def cmp_update(h, _s, slot):
        bias = bc_ref[h]
        s = s_ref[slot, 0:LANES, :] + bias
        valid = bias > 0.5 * NEG_INF
        m = jnp.max(s, axis=0, keepdims=True)
        e = jnp.where(valid, jnp.exp2(s - m), 0.0)
        pc = e / jnp.maximum(jnp.sum(e, axis=0, keepdims=True), TINY)
        psums[:] = [pc if not psums else psums[0] + pc]
        oc_ref[h] = gs_ref[3 * h:3 * h + 1, :] * _dot(vc_t, pc.astype(BF16))

    _run_chains(list(range(N_HEADS)), cmp_logits, cmp_update)
    psum = psums[0]
    p_hi = psum.astype(BF16)
    p_lo = (psum - p_hi.astype(F32)).astype(BF16)
    imp = _dot(ovl_ref[...], p_hi) + _dot(ovl_ref[...], p_lo)
    per_tile = TQ // SLC_BLOCK
    n_slc = n_kt * per_tile
    imp = imp[0:n_slc, :]
    sidx = lax.broadcasted_iota(jnp.int32, (n_slc, TQ), 0)
    t = i * TQ + lax.broadcasted_iota(jnp.int32, (n_slc, TQ), 1)
    cur = lax.shift_right_logical(t, 6)
    eligible = sidx <= cur
    forced = (sidx == 0) | (sidx == cur) | (sidx == cur - 1)
    score = jnp.where(eligible, jnp.where(forced, FORCE_SCORE, 0.0), NEG_INF) + imp
    cnt = jnp.zeros((n_slc, TQ), jnp.int32)
    for sp in range(n_slc):
        row = score[sp:sp + 1, :]
        better = (row > score) | ((row == score) & (sp < sidx))
        cnt = cnt + better.astype(jnp.int32)
    chosen = jnp.where(cnt < min(SLC_TOPK, n_slc), 1.0, 0.0)
    for kt in range(n_kt):
        sel_ref[kt, 0:per_tile, :] = chosen[kt * per_tile:(kt + 1) * per_tile, :]

    _init_state(state)
    slc = [("s", h) for h in range(N_HEADS)]
    win = [("w", h) for h in range(N_HEADS)]
    n_near = jnp.minimum(i, WINDOW // TQ) + 1

    def near_body(jj, _):
        kt = i - jj
        k_s, k_w, v_s, v_w = _rows(ks_ref, kt), _rows(kw_ref, kt), vst_ref[kt], vwt_ref[kt]
        vis = sel_ref[kt, 0:per_tile, :]
        b_s = jnp.minimum(jj, 2)
        b_w = jnp.where(jj == 2, 3, jj)

        def logits(c, slot):
            _stage_logits(s_ref, slot, _dot(k_s if c[0] == "s" else k_w, qh_ref[c[1]]))

        def update(c, _s, slot):
            kind, h = c
            if kind == "s":
                _softmax_tile(state, h, s_ref[slot] + bb_ref[h, b_s], v_s, vis=vis)
            else:
                _softmax_tile(state, N_HEADS + h, s_ref[slot] + bb_ref[h, b_w], v_w)

        _run_chains([c for pair in zip(slc, win) for c in pair], logits, update)
        return 0

    lax.fori_loop(0, n_near, near_body, 0)

    def far_body(jj, _):
        kt = i - jj
        k_s, v_s = _rows(ks_ref, kt), vst_ref[kt]
        vis = sel_ref[kt, 0:per_tile, :]
        _run_chains(slc, lambda c, slot: _dot(k_s, qh_ref[c[1]]),
                    lambda c, s, slot: _softmax_tile(state, c[1], s, v_s, vis=vis, cbias=far_ref[c[1]]))
        return 0

    lax.fori_loop(n_near, i + 1, far_body, 0)

    for pp in range(N_PAIRS):
        hs = (2 * pp, 2 * pp + 1)
        outs = []
        for h in hs:
            g1 = gs_ref[3 * h + 1:3 * h + 2, :]
            g2 = gs_ref[3 * h + 2:3 * h + 3, :]
            outs.append(oc_ref[h] + g1 * _chain_out(state, h) + g2 * _chain_out(state, N_HEADS + h))
        o_ref[:, pp * LANES:(pp + 1) * LANES] = jnp.concatenate(outs, axis=0).T.astype(o_ref.dtype)


def _nsa_attention(q, g, kvc, ks2, vs2, kw2, vw2, bb, far, bc, ovl_t):
    b, s, _ = q.shape
    n_qt = s // TQ
    per_b = lambda bi, i: (bi, 0, 0)
    return pl.pallas_call(
        _nsa_kernel, grid=(b, n_qt),
        in_specs=[pl.BlockSpec((None, TQ, MAIN_W), lambda bi, i: (bi, i, 0)),
                  pl.BlockSpec((None, TQ, LANES), lambda bi, i: (bi, i, 0)),
                  pl.BlockSpec((None,) + kvc.shape[1:], per_b),
                  pl.BlockSpec((None, s, LANES), per_b), pl.BlockSpec((None, s, LANES), per_b),
                  pl.BlockSpec((None, s, LANES), per_b), pl.BlockSpec((None, s, LANES), per_b),
                  pl.BlockSpec(bb.shape, lambda bi, i: (0, 0, 0, 0)),
                  pl.BlockSpec(far.shape, lambda bi, i: (0, 0, 0)),
                  pl.BlockSpec((None,) + bc.shape[1:], lambda bi, i: (i, 0, 0, 0)),
                  pl.BlockSpec(ovl_t.shape, lambda bi, i: (0, 0))],
        out_specs=pl.BlockSpec((None, TQ, MAIN_W), lambda bi, i: (bi, i, 0)),
        out_shape=jax.ShapeDtypeStruct((b, s, MAIN_W), BF16),
        scratch_shapes=[pltpu.VMEM((n_qt, HEAD_DIM, TQ), BF16), pltpu.VMEM((n_qt, HEAD_DIM, TQ), BF16),
                        pltpu.VMEM((N_HEADS, LANES, TQ), BF16), pltpu.VMEM((LANES, TQ), F32),
                        pltpu.VMEM((N_HEADS, HEAD_DIM, TQ), F32), pltpu.VMEM((n_qt, 8, TQ), F32),
                        pltpu.VMEM((2 * N_HEADS, 1, TQ), F32), pltpu.VMEM((2 * N_HEADS, 1, TQ), F32),
                        pltpu.VMEM((2 * N_HEADS, HEAD_DIM, TQ), F32), pltpu.VMEM((QK_LOOKAHEAD, TQ, TQ), F32)],
        compiler_params=_params(("arbitrary", "arbitrary"), VMEM_LIMIT),
        name="nsa_attention")(q, g, kvc, ks2, vs2, kw2, vw2, bb, far, bc, ovl_t)


def _moba_kernel(q_ref, k_ref, v_ref, km_ref, bb_ref, far_ref, o_ref, vt_ref, qh_ref, sel_ref,
                 m_ref, l_ref, acc_ref, s_ref):
    i = pl.program_id(1)
    n_blk = vt_ref.shape[1]
    state = (m_ref, l_ref, acc_ref)
    heads = list(range(N_HEADS))

    @pl.when(i == 0)
    def _values():
        for pp in range(N_PAIRS):
            for kt in range(n_blk):
                tile = v_ref[kt * TQ:(kt + 1) * TQ, pp * LANES:(pp + 1) * LANES]
                vt_ref[pp, kt] = tile.astype(F32).T.astype(BF16)

    _store_head_queries(q_ref, qh_ref, N_PAIRS)
    blk = lax.broadcasted_iota(jnp.int32, (n_blk, TQ), 0)
    eligible = blk < i
    for pp in range(N_PAIRS):
        km = km_ref[:, pp * LANES:(pp + 1) * LANES].astype(BF16)
        for j in range(2):
            gate = _dot(km, qh_ref[2 * pp + j])[0:n_blk, :]
            gate = jnp.where(eligible, gate, NEG_INF)
            cnt = jnp.zeros((n_blk, TQ), jnp.int32)
            for n in range(n_blk):
                row = gate[n:n + 1, :]
                better = (row > gate) | ((row == gate) & (n < blk))
                cnt = cnt + better.astype(jnp.int32)
            chosen = (cnt < min(MOBA_TOPK, n_blk - 1)) & eligible
            vis = jnp.where(chosen | (blk == i), 1.0, 0.0)
            for n in range(n_blk):
                sel_ref[2 * pp + j, n] = vis[n:n + 1, :]
    _init_state(state)

    def tile_inputs(kt, h):
        pp, j = divmod(h, 2)
        k = _rows(k_ref, kt, slice(pp * LANES, (pp + 1) * LANES))
        v_t = vt_ref[pp, kt, j * HEAD_DIM:(j + 1) * HEAD_DIM, :]
        return k, v_t, sel_ref[h, kt]

    def near_tile(jj):
        kt = i - jj
        def update(h, _s, slot):
            _, v_t, vis = tile_inputs(kt, h)
            _softmax_tile(state, h, s_ref[slot] + bb_ref[h, jj], v_t, vis=vis)

        def logits(h, slot):
            _stage_logits(s_ref, slot, _dot(tile_inputs(kt, h)[0], qh_ref[h]))

        _run_chains(heads, logits, update)

    near_tile(0)
    n_near = jnp.minimum(i, 1) + 1

    @pl.when(i >= 1)
    def _():
        near_tile(1)

    def far_body(jj, _):
        kt = i - jj
        def update(h, s, slot):
            _, v_t, vis = tile_inputs(kt, h)
            _softmax_tile(state, h, s, v_t, vis=vis, cbias=far_ref[h])
        _run_chains(heads, lambda h, slot: _dot(tile_inputs(kt, h)[0], qh_ref[h]), update)
        return 0

    lax.fori_loop(n_near, i + 1, far_body, 0)

    for pp in range(N_PAIRS):
        out_t = jnp.concatenate([_chain_out(state, 2 * pp), _chain_out(state, 2 * pp + 1)], axis=0)
        o_ref[:, pp * LANES:(pp + 1) * LANES] = out_t.T.astype(o_ref.dtype)


def _moba_attention(q, k, v, kmean, bb, far):
    b, s, _ = q.shape
    n_qt = s // TQ
    per_b = lambda bi, i: (bi, 0, 0)
    return pl.pallas_call(
        _moba_kernel, grid=(b, n_qt),
        in_specs=[pl.BlockSpec((None, TQ, MAIN_W), lambda bi, i: (bi, i, 0)),
                  pl.BlockSpec((None, s, MAIN_W), per_b), pl.BlockSpec((None, s, MAIN_W), per_b),
                  pl.BlockSpec((None, LANES, MAIN_W), per_b),
                  pl.BlockSpec((N_HEADS, 2, TQ, TQ), lambda bi, i: (0, 0, 0, 0)),
                  pl.BlockSpec(far.shape, lambda bi, i: (0, 0, 0))],
        out_specs=pl.BlockSpec((None, TQ, MAIN_W), lambda bi, i: (bi, i, 0)),
        out_shape=jax.ShapeDtypeStruct((b, s, MAIN_W), BF16),
        scratch_shapes=[pltpu.VMEM((N_PAIRS, n_qt, LANES, TQ), BF16), pltpu.VMEM((N_HEADS, LANES, TQ), BF16),
                        pltpu.VMEM((N_HEADS, n_qt, 1, TQ), F32),
                        pltpu.VMEM((N_HEADS, 1, TQ), F32), pltpu.VMEM((N_HEADS, 1, TQ), F32),
                        pltpu.VMEM((N_HEADS, HEAD_DIM, TQ), F32), pltpu.VMEM((QK_LOOKAHEAD, TQ, TQ), F32)],
        compiler_params=_params(("arbitrary", "arbitrary"), VMEM_LIMIT),
        name="moba_attention")(q, k, v, kmean, bb, far)


def _mem_kernel(q_ref, kv_ref, o_ref, qh_ref):
    _store_head_queries(q_ref, qh_ref, MEM_HEADS // 2)
    outs = {}

    def logits(h, slot):
        return _dot(kv_ref[:, (h // 2) * LANES:(h // 2 + 1) * LANES], qh_ref[h])

    def update(h, s, slot):
        lanes = slice(MEM_W + (h // 2) * LANES, MEM_W + (h // 2 + 1) * LANES)
        v_t = kv_ref[:, lanes].astype(F32).T[(h % 2) * HEAD_DIM:(h % 2 + 1) * HEAD_DIM].astype(BF16)
        e = jnp.exp2(s - jnp.max(s, axis=0, keepdims=True))
        pr = e / jnp.sum(e, axis=0, keepdims=True)
        outs[h] = _dot(v_t, pr.astype(BF16))

    _run_chains(list(range(MEM_HEADS)), logits, update)
    for pp in range(MEM_HEADS // 2):
        pair = jnp.concatenate([outs[2 * pp], outs[2 * pp + 1]], axis=0)
        o_ref[:, pp * LANES:(pp + 1) * LANES] = pair.T.astype(o_ref.dtype)


def _mem_attention(qm, mem_kv):
    b, s, _ = qm.shape
    n_mem = mem_kv.shape[1]
    return pl.pallas_call(
        _mem_kernel, grid=(b, s // TQ),
        in_specs=[pl.BlockSpec((None, TQ, MEM_W), lambda bi, i: (bi, i, 0)),
                  pl.BlockSpec((None, n_mem, 2 * MEM_W), lambda bi, i: (bi, 0, 0))],
        out_specs=pl.BlockSpec((None, TQ, MEM_W), lambda bi, i: (bi, i, 0)),
        out_shape=jax.ShapeDtypeStruct((b, s, MEM_W), BF16),
        scratch_shapes=[pltpu.VMEM((MEM_HEADS, LANES, TQ), BF16)],
        compiler_params=_params(("arbitrary", "arbitrary")), name="mem_attention")(qm, mem_kv)


def _layer_norm(y, g, b):
    mu = jnp.mean(y, axis=-1, keepdims=True)
    var = jnp.mean(jnp.square(y - mu), axis=-1, keepdims=True)
    return (y - mu) * lax.rsqrt(var + LN_EPS) * g + b


def _outproj_kernel(om_ref, oe_ref, wa_ref, wb_ref, x_ref, g_ref, b_ref, o_ref):
    mix = _dot(om_ref[...], wa_ref[...]) + _dot(oe_ref[...], wb_ref[...])
    o_ref[...] = _layer_norm(ALPHA * x_ref[...] + mix, g_ref[...], b_ref[...])


def _outproj_ln(o_main, o_mem, w_a, w_b, x, g, b, tm=512):
    m = x.shape[0]
    row = lambda i: (i, 0)
    full = lambda i: (0, 0)
    return pl.pallas_call(
        _outproj_kernel, grid=(m // tm,),
        in_specs=[pl.BlockSpec((tm, MAIN_W), row), pl.BlockSpec((tm, MEM_W), row),
                  pl.BlockSpec(w_a.shape, full), pl.BlockSpec(w_b.shape, full),
                  pl.BlockSpec((tm, D_MODEL), row), pl.BlockSpec((1, D_MODEL), full), pl.BlockSpec((1, D_MODEL), full)],
        out_specs=pl.BlockSpec((tm, D_MODEL), row),
        out_shape=jax.ShapeDtypeStruct((m, D_MODEL), F32),
        compiler_params=_params(("arbitrary",), VMEM_LIMIT), name="outproj_ln")(o_main, o_mem, w_a, w_b, x, g, b)


FF_CHUNK = 256
FF_ROWS = 2048
SUBLANES = 8


def _ffn_kernel(x_ref, wa_ref, wb_ref, cw_ref, cb_ref, wo_ref, g_ref, b_ref, o_ref, xb_ref, hid_ref):
    c = pl.program_id(1)
    last = pl.num_programs(1) - 1

    @pl.when(c == 0)
    def _():
        xb_ref[...] = x_ref[...].astype(BF16)
        hid_ref[1] = jnp.zeros(hid_ref.shape[1:], BF16)
        o_ref[...] = jnp.zeros(o_ref.shape, F32)

    blocks = [slice(r0, r0 + FF_ROWS) for r0 in range(0, xb_ref.shape[0], FF_ROWS)]

    def matmuls(rows):
        xb = xb_ref[rows, :]
        a = _dot(xb, wa_ref[...])
        o_ref[rows, :] += _dot(hid_ref[(c + 1) % 2, rows, :], wo_ref[...])
        gate = _dot(xb, wb_ref[...])
        return a, gate

    def elementwise(rows, a, gate, tail):
        ext = jnp.concatenate([tail, a], axis=0)
        a1 = pltpu.roll(ext, 1, 0)[SUBLANES:]
        a2 = pltpu.roll(ext, 2, 0)[SUBLANES:]
        conv = cw_ref[0:1, :] * a2 + cw_ref[1:2, :] * a1 + cw_ref[2:3, :] * a + cb_ref[...]
        hid_ref[c % 2, rows, :] = (jax.nn.gelu(conv) * gate).astype(BF16)
        return a[FF_ROWS - SUBLANES:]

    tail = jnp.zeros((SUBLANES, wa_ref.shape[1]), F32)
    ready = matmuls(blocks[0])
    for r, rows in enumerate(blocks):
        cur = ready
        if r + 1 < len(blocks):
            ready = matmuls(blocks[r + 1])
        tail = elementwise(rows, cur[0], cur[1], tail)

    @pl.when(c == last)
    def _():
        o_ref[...] = _layer_norm(ALPHA * x_ref[...] + o_ref[...], g_ref[...], b_ref[...])


def _ffn_ln(x, w_in, conv_w, conv_b, w_out, g, b):
    bsz, s, d = x.shape
    n_c = D_FF // FF_CHUNK
    cur = lambda c: jnp.minimum(c, n_c - 1)
    prev = lambda c: jnp.maximum(c - 1, 0)
    return pl.pallas_call(
        _ffn_kernel, grid=(bsz, n_c + 1),
        in_specs=[pl.BlockSpec((None, s, d), lambda bi, c: (bi, 0, 0)),
                  pl.BlockSpec((d, FF_CHUNK), lambda bi, c: (0, cur(c))),
                  pl.BlockSpec((d, FF_CHUNK), lambda bi, c: (0, n_c + cur(c))),
                  pl.BlockSpec((CONV_W, FF_CHUNK), lambda bi, c: (0, cur(c))),
                  pl.BlockSpec((1, FF_CHUNK), lambda bi, c: (0, cur(c))),
                  pl.BlockSpec((FF_CHUNK, d), lambda bi, c: (prev(c), 0)),
                  pl.BlockSpec((1, d), lambda bi, c: (0, 0)), pl.BlockSpec((1, d), lambda bi, c: (0, 0))],
        out_specs=pl.BlockSpec((None, s, d), lambda bi, c: (bi, 0, 0)),
        out_shape=jax.ShapeDtypeStruct((bsz, s, d), F32),
        scratch_shapes=[pltpu.VMEM((s, d), BF16), pltpu.VMEM((2, s, FF_CHUNK), BF16)],
        compiler_params=_params(("arbitrary", "arbitrary"), VMEM_LIMIT),
        name="ffn_ln")(x, w_in, w_in, conv_w, conv_b, w_out, g, b)


def _dup(w):
    return jnp.concatenate([w, w], axis=1)


def _cmp_weights(pe_k, w1_k, w2_k, pe_v, w1_v, w2_v):
    half = CMP_LEN // 2
    zk = jnp.zeros((half, HEAD_DIM, CMP_HIDDEN), F32)

    def w1_half(sl):
        wk = w1_k.reshape(CMP_LEN, HEAD_DIM, CMP_HIDDEN)[sl]
        wv = w1_v.reshape(CMP_LEN, HEAD_DIM, CMP_HIDDEN)[sl]
        top = jnp.concatenate([wk, zk], axis=2)
        bot = jnp.concatenate([zk, wv], axis=2)
        return jnp.concatenate([top, bot], axis=1).reshape(half * 2 * HEAD_DIM, 2 * CMP_HIDDEN)

    w1 = jnp.stack([w1_half(slice(0, half)), w1_half(slice(half, CMP_LEN))]).astype(BF16)
    pe = jnp.concatenate([pe_k, pe_v], axis=1)
    pe = jnp.stack([pe[:half].reshape(-1), pe[half:].reshape(-1)])
    zo = jnp.zeros((CMP_HIDDEN, 2 * HEAD_DIM), F32)
    w2 = jnp.concatenate([jnp.concatenate([_dup(w2_k), zo], axis=1),
                          jnp.concatenate([zo, _dup(w2_v)], axis=1)], axis=0).astype(BF16)
    return pe, w1, w2


def _overlap_matrix(seq):
    n_cmp = (seq - CMP_LEN) // CMP_STRIDE + 1
    n_slc = seq // SLC_BLOCK
    start = np.arange(LANES) * CMP_STRIDE
    bs = np.arange(LANES) * SLC_BLOCK
    ovl_t = ((start[None, :] < bs[:, None] + SLC_BLOCK) & (start[None, :] + CMP_LEN > bs[:, None])
             & (np.arange(LANES)[None, :] < n_cmp) & (np.arange(LANES)[:, None] < n_slc))
    return jnp.asarray(ovl_t, BF16)


def kernel(x, mem, rel_bias, a_w_in, a_cmp_pe_k, a_cmp_w1_k, a_cmp_w2_k, a_cmp_pe_v, a_cmp_w1_v, a_cmp_w2_v,
           a_w_mem_kv, a_w_out, shared_w_kv, b_w_in, b_w_mem_kv, b_w_out, ln1_g, ln1_b, ln2_g, ln2_b,
           ffn_w_in, ffn_conv_w, ffn_conv_b, ffn_w_out):
    bsz, seq, d = x.shape
    n_mem = mem.shape[1]
    m = bsz * seq
    assert (seq, d) == (2048, D_MODEL) and seq % TQ == 0
    n_qt = seq // TQ

    bb, bc, far = _bias_tables(rel_bias, n_qt)
    ovl_t = _overlap_matrix(seq)
    memf = mem.reshape(bsz * n_mem, d)
    xf = x.reshape(m, d)

    def ffn(xcur, layer):
        return _ffn_ln(xcur.reshape(bsz, seq, d), ffn_w_in[layer].astype(BF16), ffn_conv_w[layer],
                       ffn_conv_b[layer][None, :], ffn_w_out[layer].astype(BF16),
                       ln2_g[layer][None, :], ln2_b[layer][None, :]).reshape(m, d)

    def out_ln(o_main, o_mem, w_out, xcur, layer):
        w = w_out.astype(BF16)
        return _outproj_ln(o_main.reshape(m, MAIN_W), o_mem.reshape(m, MEM_W), w[:MAIN_W], w[MAIN_W:], xcur,
                           ln1_g[layer][None, :], ln1_b[layer][None, :])

    w = a_w_in[0]
    c0 = MAIN_W
    cols = [w[:, c0 + kk * HEAD_DIM:c0 + (kk + 1) * HEAD_DIM] for kk in range(6)]
    c_g = c0 + 6 * HEAD_DIM
    w_g = jnp.pad(w[:, c_g:c_g + 3 * N_HEADS], ((0, 0), (0, LANES - 3 * N_HEADS)))
    w_qm = w[:, c_g + 3 * N_HEADS:]
    weights = [w[:, :MAIN_W], jnp.concatenate(cols[0:2], axis=1), _dup(cols[2]), _dup(cols[3]),
               _dup(cols[4]), _dup(cols[5]), w_g, w_qm]
    weights = [wi.astype(BF16) for wi in weights]
    q, kvc_tok, ks2, vs2, kw2, vw2, g, qm = _proj(
        xf, weights, [BF16, BF16, BF16, BF16, BF16, BF16, F32, BF16], 512, "proj_a",
        scales=[Q_SCALE, None, None, None, None, None, None, Q_SCALE])

    pe, w1, w2 = _cmp_weights(a_cmp_pe_k[0], a_cmp_w1_k[0], a_cmp_w2_k[0],
                              a_cmp_pe_v[0], a_cmp_w1_v[0], a_cmp_w2_v[0])
    kvc = _compress(kvc_tok.reshape(bsz, seq // (CMP_LEN // 2), (CMP_LEN // 2) * LANES), pe, w1, w2)

    r3 = lambda t: t.reshape(bsz, seq, t.shape[-1])
    o_main = _nsa_attention(r3(q), r3(g), kvc, r3(ks2), r3(vs2), r3(kw2), r3(vw2), bb, far, bc, ovl_t)
    (mkv,) = _proj(memf, [a_w_mem_kv[0].astype(BF16)], [BF16], 512, "proj_mem_a")
    o_mem = _mem_attention(r3(qm), mkv.reshape(bsz, n_mem, 2 * MEM_W))
    x1 = out_ln(o_main, o_mem, a_w_out[0], xf, 0)
    x1 = ffn(x1, 0)

    wb = b_w_in[0].astype(BF16)
    wkv = shared_w_kv.astype(BF16)
    q, qm, k, v, kmean = _proj_b(x1, wb[:, :MAIN_W], wb[:, MAIN_W:], wkv[:, :MAIN_W], wkv[:, MAIN_W:])
    n_blk = seq // MOBA_BLOCK
    kmean = jnp.pad(kmean.reshape(bsz, n_blk, MAIN_W), ((0, 0), (0, LANES - n_blk), (0, 0)))
    o_main = _moba_attention(r3(q), r3(k), r3(v), kmean, bb, far)
    (mkv,) = _proj(memf, [b_w_mem_kv[0].astype(BF16)], [BF16], 512, "proj_mem_b")
    o_mem = _mem_attention(r3(qm), mkv.reshape(bsz, n_mem, 2 * MEM_W))
    x2 = out_ln(o_main, o_mem, b_w_out[0], x1, 1)
    x2 = ffn(x2, 1)
    return x2.reshape(bsz, seq, d)
```

```python
import functools
import math

import numpy as np
import jax
import jax.numpy as jnp
from jax import lax
from jax.experimental import pallas as pl
from jax.experimental.pallas import tpu as pltpu

F32 = jnp.float32
BF16 = jnp.bfloat16

D_MODEL = 1024
HEAD_DIM = 64
N_HEADS = 12
N_PAIRS = N_HEADS // 2
MAIN_W = N_HEADS * HEAD_DIM
MEM_HEADS = 4
MEM_W = MEM_HEADS * HEAD_DIM
CMP_LEN = 32
CMP_STRIDE = 16
CMP_HIDDEN = 256
SLC_BLOCK = 64
SLC_TOPK = 16
WINDOW = 512
FORCE_SCORE = 1.0e4
MOBA_BLOCK = 256
MOBA_TOPK = 3
REL_BUCKETS = 32
REL_MAX_DIST = 128
D_FF = 2816
CONV_W = 3
DEPTH = 2
ALPHA = (2.0 * DEPTH) ** 0.25
LN_EPS = 1e-5
NEG_INF = -1e30
TINY = 1e-30
LOG2E = math.log2(math.e)
Q_SCALE = HEAD_DIM ** -0.5 * LOG2E

LANES = 128
TQ = 256
VMEM_LIMIT = 60 * 1024 * 1024
V_ROWS = HEAD_DIM + 8
QK_LOOKAHEAD = 6


def _dot(a, b):
    return jnp.dot(a, b, preferred_element_type=F32)


def _dot_nt(a, b):
    return lax.dot_general(a, b, (((1,), (1,)), ((), ())), preferred_element_type=F32)


def _params(sem, vmem=None):
    return pltpu.CompilerParams(dimension_semantics=sem, vmem_limit_bytes=vmem)


def _proj_kernel(x_ref, *refs, scales):
    n = len(refs) // 2
    xb = x_ref[...].astype(BF16)
    for w_ref, o_ref, scale in zip(refs[:n], refs[n:], scales):
        y = _dot(xb, w_ref[...])
        o_ref[...] = (y if scale is None else y * scale).astype(o_ref.dtype)


def _proj(x, weights, dtypes, tm, name, scales=None):
    m, k = x.shape
    scales = tuple(scales) if scales is not None else (None,) * len(weights)
    in_specs = [pl.BlockSpec((tm, k), lambda i: (i, 0))]
    in_specs += [pl.BlockSpec(w.shape, lambda i: (0, 0)) for w in weights]
    out_specs = [pl.BlockSpec((tm, w.shape[1]), lambda i: (i, 0)) for w in weights]
    out_shape = [jax.ShapeDtypeStruct((m, w.shape[1]), dt) for w, dt in zip(weights, dtypes)]
    return pl.pallas_call(
        functools.partial(_proj_kernel, scales=scales), grid=(m // tm,), in_specs=in_specs,
        out_specs=out_specs, out_shape=out_shape,
        compiler_params=_params(("arbitrary",), VMEM_LIMIT), name=name)(x, *weights)


def _projb_kernel(x_ref, wq_ref, wm_ref, wk_ref, wv_ref, q_ref, qm_ref, k_ref, v_ref, km_ref):
    xb = x_ref[...].astype(BF16)
    q_ref[...] = (_dot(xb, wq_ref[...]) * Q_SCALE).astype(BF16)
    qm_ref[...] = (_dot(xb, wm_ref[...]) * Q_SCALE).astype(BF16)
    k = _dot(xb, wk_ref[...])
    k_ref[...] = k.astype(BF16)
    km_ref[...] = jnp.mean(k, axis=0, keepdims=True)
    v_ref[...] = _dot(xb, wv_ref[...]).astype(BF16)


def _proj_b(x, wq, wm, wk, wv):
    m, k = x.shape
    nblk = m // MOBA_BLOCK
    row = lambda i: (i, 0)
    full = lambda i: (0, 0)
    return pl.pallas_call(
        _projb_kernel, grid=(nblk,),
        in_specs=[pl.BlockSpec((MOBA_BLOCK, k), row)] + [pl.BlockSpec(w.shape, full) for w in (wq, wm, wk, wv)],
        out_specs=[pl.BlockSpec((MOBA_BLOCK, MAIN_W), row), pl.BlockSpec((MOBA_BLOCK, MEM_W), row),
                   pl.BlockSpec((MOBA_BLOCK, MAIN_W), row), pl.BlockSpec((MOBA_BLOCK, MAIN_W), row),
                   pl.BlockSpec((None, 1, MAIN_W), lambda i: (i, 0, 0))],
        out_shape=[jax.ShapeDtypeStruct((m, MAIN_W), BF16), jax.ShapeDtypeStruct((m, MEM_W), BF16),
                   jax.ShapeDtypeStruct((m, MAIN_W), BF16), jax.ShapeDtypeStruct((m, MAIN_W), BF16),
                   jax.ShapeDtypeStruct((nblk, 1, MAIN_W), F32)],
        compiler_params=_params(("arbitrary",), VMEM_LIMIT), name="proj_b")(x, wq, wm, wk, wv)


def _bias_kernel(tbl_ref, bb_ref, bc_ref, far_ref):
    h = pl.program_id(0)

    def bias_of(dist):
        n = jnp.maximum(dist, 0)
        max_exact = REL_BUCKETS // 2
        nf = jnp.maximum(n, 1).astype(F32)
        large = max_exact + (jnp.log(nf / max_exact) / math.log(REL_MAX_DIST / max_exact)
                             * (REL_BUCKETS - max_exact)).astype(jnp.int32)
        large = jnp.minimum(large, REL_BUCKETS - 1)
        bucket = jnp.where(n < max_exact, n, large)
        out = jnp.zeros(dist.shape, F32)
        for kk in range(REL_BUCKETS):
            out = jnp.where(bucket == kk, tbl_ref[h * REL_BUCKETS + kk], out)
        return out * LOG2E

    key = lax.broadcasted_iota(jnp.int32, (TQ, TQ), 0)
    qry = lax.broadcasted_iota(jnp.int32, (TQ, TQ), 1)
    d0 = qry - key
    bb_ref[0] = jnp.where(d0 >= 0, bias_of(d0), NEG_INF)
    bb_ref[1] = bias_of(d0 + TQ)
    far = bias_of(d0 + 2 * TQ)
    bb_ref[2] = far
    bb_ref[3] = jnp.where(d0 + 2 * TQ < WINDOW, far, NEG_INF)
    far_ref[...] = far[0:1, :]

    nc = lax.broadcasted_iota(jnp.int32, (LANES, TQ), 0)
    qc = lax.broadcasted_iota(jnp.int32, (LANES, TQ), 1)
    n_cmp = (2048 - CMP_LEN) // CMP_STRIDE + 1
    for i in range(bc_ref.shape[0]):
        dc = i * TQ + qc - (nc * CMP_STRIDE + CMP_LEN - 1)
        bc_ref[i] = jnp.where((dc >= 0) & (nc < n_cmp), bias_of(dc), NEG_INF)


def _bias_tables(rel_bias, n_qt):
    tbl = rel_bias.T.reshape(-1)
    return pl.pallas_call(
        _bias_kernel, grid=(N_HEADS,),
        in_specs=[pl.BlockSpec(memory_space=pltpu.SMEM)],
        out_specs=[pl.BlockSpec((None, 4, TQ, TQ), lambda h: (h, 0, 0, 0)),
                   pl.BlockSpec((n_qt, None, LANES, TQ), lambda h: (0, h, 0, 0)),
                   pl.BlockSpec((None, 1, TQ), lambda h: (h, 0, 0))],
        out_shape=[jax.ShapeDtypeStruct((N_HEADS, 4, TQ, TQ), F32),
                   jax.ShapeDtypeStruct((n_qt, N_HEADS, LANES, TQ), F32),
                   jax.ShapeDtypeStruct((N_HEADS, 1, TQ), F32)],
        compiler_params=_params(("arbitrary",)), name="bias_tables")(tbl)


def _cmp_kernel(kv_ref, pe_ref, w1_ref, w2_ref, o_ref):
    x = kv_ref[...].astype(F32)
    lo = _dot((x + pe_ref[0:1, :]).astype(BF16), w1_ref[0])
    hi = _dot((x + pe_ref[1:2, :]).astype(BF16), w1_ref[1])
    nrow = x.shape[0]
    hid = lo + pltpu.roll(hi, nrow - 1, 0)
    o_ref[...] = _dot(jax.nn.gelu(hid).astype(BF16), w2_ref[...])


def _compress(kvr, pe, w1, w2):
    b, nrow, width = kvr.shape
    return pl.pallas_call(
        _cmp_kernel, grid=(b,),
        in_specs=[pl.BlockSpec((None, nrow, width), lambda i: (i, 0, 0)),
                  pl.BlockSpec(pe.shape, lambda i: (0, 0)),
                  pl.BlockSpec(w1.shape, lambda i: (0, 0, 0)),
                  pl.BlockSpec(w2.shape, lambda i: (0, 0))],
        out_specs=pl.BlockSpec((None, nrow, w2.shape[1]), lambda i: (i, 0, 0)),
        out_shape=jax.ShapeDtypeStruct((b, nrow, w2.shape[1]), F32),
        compiler_params=_params(("arbitrary",), VMEM_LIMIT), name="nsa_compress")(kvr, pe, w1, w2)


def _softmax_tile(state, c, s_t, v_t, vis=None, cbias=None):
    m_ref, acc_ref = state
    m = m_ref[c]
    n_blk = 1 if vis is None else vis.shape[0]
    rows = s_t.shape[0] // n_blk
    parts = [s_t[b * rows:(b + 1) * rows] for b in range(n_blk)]
    tile_max = None
    for b in range(n_blk):
        mb = jnp.max(parts[b], axis=0, keepdims=True)
        if vis is not None:
            mb = jnp.where(vis[b:b + 1] > 0.5, mb, NEG_INF)
        tile_max = mb if tile_max is None else jnp.maximum(tile_max, mb)
    if cbias is not None:
        tile_max = tile_max + cbias
    m_new = jnp.maximum(m, tile_max)
    shift = m_new if cbias is None else m_new - cbias
    p_parts = []
    for b in range(n_blk):
        off = shift if vis is None else jnp.where(vis[b:b + 1] > 0.5, shift, -NEG_INF)
        p_parts.append(jnp.exp2(parts[b] - off))
    p_t = p_parts[0] if n_blk == 1 else jnp.concatenate(p_parts, axis=0)
    alpha = jnp.exp2(m - m_new)
    acc_ref[c] = alpha * acc_ref[c] + _dot(v_t, p_t.astype(BF16))
    m_ref[c] = m_new


def _run_chains(chains, logits_fn, update_fn):
    pending = {}
    for idx, c in enumerate(chains[:QK_LOOKAHEAD]):
        pending[c] = logits_fn(c, idx)
    for idx, c in enumerate(chains):
        update_fn(c, pending.pop(c), idx % QK_LOOKAHEAD)
        if idx + QK_LOOKAHEAD < len(chains):
            nxt = chains[idx + QK_LOOKAHEAD]
            pending[nxt] = logits_fn(nxt, idx % QK_LOOKAHEAD)


def _stage_logits(s_ref, slot, s_t):
    s_ref[slot] = s_t


def _init_state(state):
    m_ref, acc_ref = state
    m_ref[...] = jnp.full(m_ref.shape, NEG_INF, F32)
    acc_ref[...] = jnp.zeros(acc_ref.shape, F32)


def _chain_out(state, c):
    _, acc_ref = state
    return acc_ref[c, 0:HEAD_DIM, :] / jnp.maximum(acc_ref[c, HEAD_DIM:HEAD_DIM + 1, :], TINY)


def _values_with_ones(tile, row0):
    v_t = tile.astype(F32).T[row0:row0 + HEAD_DIM]
    pad = lax.broadcasted_iota(jnp.int32, (V_ROWS - HEAD_DIM, tile.shape[0]), 0)
    return jnp.concatenate([v_t, jnp.where(pad == 0, 1.0, 0.0)], axis=0).astype(BF16)


def _rows(ref, kt, cols=slice(None)):
    return ref[pl.ds(pl.multiple_of(kt * TQ, TQ), TQ), cols]


def _store_head_queries(q_ref, qh_ref, n_pairs):
    row = lax.broadcasted_iota(jnp.int32, (LANES, TQ), 0)
    halves = (row < HEAD_DIM, row >= HEAD_DIM)
    for pp in range(n_pairs):
        q2t = q_ref[:, pp * LANES:(pp + 1) * LANES].astype(F32).T
        for j in range(2):
            qh_ref[2 * pp + j] = jnp.where(halves[j], q2t, 0.0).astype(BF16)


def _nsa_kernel(q_ref, g_ref, kvc_ref, ks_ref, vs_ref, kw_ref, vw_ref, bb_ref, far_ref, bc_ref, ovl_ref,
                o_ref, vst_ref, vwt_ref, qh_ref, gs_ref, oc_ref, sel_ref, m_ref, acc_ref, s_ref):
    i = pl.program_id(1)
    n_kt = vst_ref.shape[0]
    state = (m_ref, acc_ref)

    @pl.when(i == 0)
    def _values():
        for kt in range(n_kt):
            rows = slice(kt * TQ, (kt + 1) * TQ)
            vst_ref[kt] = _values_with_ones(vs_ref[rows, :], 0)
            vwt_ref[kt] = _values_with_ones(vw_ref[rows, :], 0)

    _store_head_queries(q_ref, qh_ref, N_PAIRS)
    gs_ref[...] = jax.nn.sigmoid(g_ref[...]).T

    kc2 = kvc_ref[:, 0:LANES].astype(BF16)
    vc_t = kvc_ref[:, LANES:2 * LANES].T[0:HEAD_DIM].astype(BF16)
    psums = []

    def cmp_logits(h, slot):
        s_ref[slot, 0:LANES, :] = _dot(kc2, qh_ref[h])

    def cmp_update(h, _s, slot):
        bias = bc_ref[h]
        s = s_ref[slot, 0:LANES, :] + bias
        valid = bias > 0.5 * NEG_INF
        m = jnp.max(s, axis=0, keepdims=True)
        e = jnp.where(valid, jnp.exp2(s - m), 0.0)
        pc = e / jnp.maximum(jnp.sum(e, axis=0, keepdims=True), TINY)
        psums[:] = [pc if not psums else psums[0] + pc]
        oc_ref[h] = gs_ref[3 * h:3 * h + 1, :] * _dot(vc_t, pc.astype(BF16))

    _run_chains(list(range(N_HEADS)), cmp_logits, cmp_update)
    psum = psums[0]
    p_hi = psum.astype(BF16)
    p_lo = (psum - p_hi.astype(F32)).astype(BF16)
    imp = _dot(ovl_ref[...], p_hi) + _dot(ovl_ref[...], p_lo)
    per_tile = TQ // SLC_BLOCK
    n_slc = n_kt * per_tile
    imp = imp[0:n_slc, :]
    sidx = lax.broadcasted_iota(jnp.int32, (n_slc, TQ), 0)
    t = i * TQ + lax.broadcasted_iota(jnp.int32, (n_slc, TQ), 1)
    cur = lax.shift_right_logical(t, 6)
    eligible = sidx <= cur
    forced = (sidx == 0) | (sidx == cur) | (sidx == cur - 1)
    score = jnp.where(eligible, jnp.where(forced, FORCE_SCORE, 0.0), NEG_INF) + imp
    cnt = jnp.zeros((n_slc, TQ), jnp.int32)
    for sp in range(n_slc):
        row = score[sp:sp + 1, :]
        better = (row > score) | ((row == score) & (sp < sidx))
        cnt = cnt + better.astype(jnp.int32)
    chosen = jnp.where(cnt < min(SLC_TOPK, n_slc), 1.0, 0.0)
    for kt in range(n_kt):
        sel_ref[kt, 0:per_tile, :] = chosen[kt * per_tile:(kt + 1) * per_tile, :]

    _init_state(state)
    slc = [("s", h) for h in range(N_HEADS)]
    win = [("w", h) for h in range(N_HEADS)]
    n_near = jnp.minimum(i, WINDOW // TQ) + 1

    def near_body(jj, _):
        kt = i - jj
        k_s, k_w, v_s, v_w = _rows(ks_ref, kt), _rows(kw_ref, kt), vst_ref[kt], vwt_ref[kt]
        vis = sel_ref[kt, 0:per_tile, :]
        b_s = jnp.minimum(jj, 2)
        b_w = jnp.where(jj == 2, 3, jj)

        def logits(c, slot):
            _stage_logits(s_ref, slot, _dot(k_s if c[0] == "s" else k_w, qh_ref[c[1]]))

        def update(c, _s, slot):
            kind, h = c
            if kind == "s":
                _softmax_tile(state, h, s_ref[slot] + bb_ref[h, b_s], v_s, vis=vis)
            else:
                _softmax_tile(state, N_HEADS + h, s_ref[slot] + bb_ref[h, b_w], v_w)

        _run_chains([c for pair in zip(slc, win) for c in pair], logits, update)
        return 0

    lax.fori_loop(0, n_near, near_body, 0)

    def far_body(jj, _):
        kt = i - jj
        k_s, v_s = _rows(ks_ref, kt), vst_ref[kt]
        vis = sel_ref[kt, 0:per_tile, :]
        _run_chains(slc, lambda c, slot: _dot(k_s, qh_ref[c[1]]),
                    lambda c, s, slot: _softmax_tile(state, c[1], s, v_s, vis=vis, cbias=far_ref[c[1]]))
        return 0

    lax.fori_loop(n_near, i + 1, far_body, 0)

    for pp in range(N_PAIRS):
        hs = (2 * pp, 2 * pp + 1)
        outs = []
        for h in hs:
            g1 = gs_ref[3 * h + 1:3 * h + 2, :]
            g2 = gs_ref[3 * h + 2:3 * h + 3, :]
            outs.append(oc_ref[h] + g1 * _chain_out(state, h) + g2 * _chain_out(state, N_HEADS + h))
        o_ref[:, pp * LANES:(pp + 1) * LANES] = jnp.concatenate(outs, axis=0).T.astype(o_ref.dtype)


def _nsa_attention(q, g, kvc, ks2, vs2, kw2, vw2, bb, far, bc, ovl_t):
    b, s, _ = q.shape
    n_qt = s // TQ
    per_b = lambda bi, i: (bi, 0, 0)
    return pl.pallas_call(
        _nsa_kernel, grid=(b, n_qt),
        in_specs=[pl.BlockSpec((None, TQ, MAIN_W), lambda bi, i: (bi, i, 0)),
                  pl.BlockSpec((None, TQ, LANES), lambda bi, i: (bi, i, 0)),
                  pl.BlockSpec((None,) + kvc.shape[1:], per_b),
                  pl.BlockSpec((None, s, LANES), per_b), pl.BlockSpec((None, s, LANES), per_b),
                  pl.BlockSpec((None, s, LANES), per_b), pl.BlockSpec((None, s, LANES), per_b),
                  pl.BlockSpec(bb.shape, lambda bi, i: (0, 0, 0, 0)),
                  pl.BlockSpec(far.shape, lambda bi, i: (0, 0, 0)),
                  pl.BlockSpec((None,) + bc.shape[1:], lambda bi, i: (i, 0, 0, 0)),
                  pl.BlockSpec(ovl_t.shape, lambda bi, i: (0, 0))],
        out_specs=pl.BlockSpec((None, TQ, MAIN_W), lambda bi, i: (bi, i, 0)),
        out_shape=jax.ShapeDtypeStruct((b, s, MAIN_W), BF16),
        scratch_shapes=[pltpu.VMEM((n_qt, V_ROWS, TQ), BF16), pltpu.VMEM((n_qt, V_ROWS, TQ), BF16),
                        pltpu.VMEM((N_HEADS, LANES, TQ), BF16), pltpu.VMEM((LANES, TQ), F32),
                        pltpu.VMEM((N_HEADS, HEAD_DIM, TQ), F32), pltpu.VMEM((n_qt, 8, TQ), F32),
                        pltpu.VMEM((2 * N_HEADS, 1, TQ), F32),
                        pltpu.VMEM((2 * N_HEADS, V_ROWS, TQ), F32), pltpu.VMEM((QK_LOOKAHEAD, TQ, TQ), F32)],
        compiler_params=_params(("arbitrary", "arbitrary"), VMEM_LIMIT),
        name="nsa_attention")(q, g, kvc, ks2, vs2, kw2, vw2, bb, far, bc, ovl_t)


def _moba_kernel(q_ref, k_ref, v_ref, km_ref, bb_ref, far_ref, o_ref, vt_ref, qh_ref, sel_ref,
                 m_ref, acc_ref, s_ref):
    i = pl.program_id(1)
    n_blk = vt_ref.shape[1]
    state = (m_ref, acc_ref)
    heads = list(range(N_HEADS))

    @pl.when(i == 0)
    def _values():
        for pp in range(N_PAIRS):
            for kt in range(n_blk):
                tile = v_ref[kt * TQ:(kt + 1) * TQ, pp * LANES:(pp + 1) * LANES]
                for j in range(2):
                    vt_ref[2 * pp + j, kt] = _values_with_ones(tile, j * HEAD_DIM)

    _store_head_queries(q_ref, qh_ref, N_PAIRS)
    blk = lax.broadcasted_iota(jnp.int32, (n_blk, TQ), 0)
    eligible = blk < i
    for pp in range(N_PAIRS):
        km = km_ref[:, pp * LANES:(pp + 1) * LANES].astype(BF16)
        for j in range(2):
            gate = _dot(km, qh_ref[2 * pp + j])[0:n_blk, :]
            gate = jnp.where(eligible, gate, NEG_INF)
            cnt = jnp.zeros((n_blk, TQ), jnp.int32)
            for n in range(n_blk):
                row = gate[n:n + 1, :]
                better = (row > gate) | ((row == gate) & (n < blk))
                cnt = cnt + better.astype(jnp.int32)
            chosen = (cnt < min(MOBA_TOPK, n_blk - 1)) & eligible
            vis = jnp.where(chosen | (blk == i), 1.0, 0.0)
            for n in range(n_blk):
                sel_ref[2 * pp + j, n] = vis[n:n + 1, :]
    _init_state(state)

    def tile_inputs(kt, h):
        pp = h // 2
        k = _rows(k_ref, kt, slice(pp * LANES, (pp + 1) * LANES))
        return k, vt_ref[h, kt], sel_ref[h, kt]

    def near_tile(jj):
        kt = i - jj
        def update(h, _s, slot):
            _, v_t, vis = tile_inputs(kt, h)
            _softmax_tile(state, h, s_ref[slot] + bb_ref[h, jj], v_t, vis=vis)

        def logits(h, slot):
            _stage_logits(s_ref, slot, _dot(tile_inputs(kt, h)[0], qh_ref[h]))

        _run_chains(heads, logits, update)

    near_tile(0)
    n_near = jnp.minimum(i, 1) + 1

    @pl.when(i >= 1)
    def _():
        near_tile(1)

    def far_body(jj, _):
        kt = i - jj
        def update(h, s, slot):
            _, v_t, vis = tile_inputs(kt, h)
            _softmax_tile(state, h, s, v_t, vis=vis, cbias=far_ref[h])
        _run_chains(heads, lambda h, slot: _dot(tile_inputs(kt, h)[0], qh_ref[h]), update)
        return 0

    lax.fori_loop(n_near, i + 1, far_body, 0)

    for pp in range(N_PAIRS):
        out_t = jnp.concatenate([_chain_out(state, 2 * pp), _chain_out(state, 2 * pp + 1)], axis=0)
        o_ref[:, pp * LANES:(pp + 1) * LANES] = out_t.T.astype(o_ref.dtype)


def _moba_attention(q, k, v, kmean, bb, far):
    b, s, _ = q.shape
    n_qt = s // TQ
    per_b = lambda bi, i: (bi, 0, 0)
    return pl.pallas_call(
        _moba_kernel, grid=(b, n_qt),
        in_specs=[pl.BlockSpec((None, TQ, MAIN_W), lambda bi, i: (bi, i, 0)),
                  pl.BlockSpec((None, s, MAIN_W), per_b), pl.BlockSpec((None, s, MAIN_W), per_b),
                  pl.BlockSpec((None, LANES, MAIN_W), per_b),
                  pl.BlockSpec((N_HEADS, 2, TQ, TQ), lambda bi, i: (0, 0, 0, 0)),
                  pl.BlockSpec(far.shape, lambda bi, i: (0, 0, 0))],
        out_specs=pl.BlockSpec((None, TQ, MAIN_W), lambda bi, i: (bi, i, 0)),
        out_shape=jax.ShapeDtypeStruct((b, s, MAIN_W), BF16),
        scratch_shapes=[pltpu.VMEM((N_HEADS, n_qt, V_ROWS, TQ), BF16), pltpu.VMEM((N_HEADS, LANES, TQ), BF16),
                        pltpu.VMEM((N_HEADS, n_qt, 1, TQ), F32),
                        pltpu.VMEM((N_HEADS, 1, TQ), F32),
                        pltpu.VMEM((N_HEADS, V_ROWS, TQ), F32), pltpu.VMEM((QK_LOOKAHEAD, TQ, TQ), F32)],
        compiler_params=_params(("arbitrary", "arbitrary"), VMEM_LIMIT),
        name="moba_attention")(q, k, v, kmean, bb, far)


def _mem_kernel(q_ref, kv_ref, o_ref, qh_ref):
    _store_head_queries(q_ref, qh_ref, MEM_HEADS // 2)
    outs = {}

    def logits(h, slot):
        return _dot(kv_ref[:, (h // 2) * LANES:(h // 2 + 1) * LANES], qh_ref[h])

    def update(h, s, slot):
        lanes = slice(MEM_W + (h // 2) * LANES, MEM_W + (h // 2 + 1) * LANES)
        v_t = kv_ref[:, lanes].astype(F32).T[(h % 2) * HEAD_DIM:(h % 2 + 1) * HEAD_DIM].astype(BF16)
        e = jnp.exp2(s - jnp.max(s, axis=0, keepdims=True))
        pr = e / jnp.sum(e, axis=0, keepdims=True)
        outs[h] = _dot(v_t, pr.astype(BF16))

    _run_chains(list(range(MEM_HEADS)), logits, update)
    for pp in range(MEM_HEADS // 2):
        pair = jnp.concatenate([outs[2 * pp], outs[2 * pp + 1]], axis=0)
        o_ref[:, pp * LANES:(pp + 1) * LANES] = pair.T.astype(o_ref.dtype)


def _mem_attention(qm, mem_kv):
    b, s, _ = qm.shape
    n_mem = mem_kv.shape[1]
    return pl.pallas_call(
        _mem_kernel, grid=(b, s // TQ),
        in_specs=[pl.BlockSpec((None, TQ, MEM_W), lambda bi, i: (bi, i, 0)),
                  pl.BlockSpec((None, n_mem, 2 * MEM_W), lambda bi, i: (bi, 0, 0))],
        out_specs=pl.BlockSpec((None, TQ, MEM_W), lambda bi, i: (bi, i, 0)),
        out_shape=jax.ShapeDtypeStruct((b, s, MEM_W), BF16),
        scratch_shapes=[pltpu.VMEM((MEM_HEADS, LANES, TQ), BF16)],
        compiler_params=_params(("arbitrary", "arbitrary")), name="mem_attention")(qm, mem_kv)


def _layer_norm(y, g, b):
    mu = jnp.mean(y, axis=-1, keepdims=True)
    var = jnp.mean(jnp.square(y - mu), axis=-1, keepdims=True)
    return (y - mu) * lax.rsqrt(var + LN_EPS) * g + b


def _outproj_kernel(om_ref, oe_ref, wa_ref, wb_ref, x_ref, g_ref, b_ref, o_ref):
    mix = _dot(om_ref[...], wa_ref[...]) + _dot(oe_ref[...], wb_ref[...])
    o_ref[...] = _layer_norm(ALPHA * x_ref[...] + mix, g_ref[...], b_ref[...])


def _outproj_ln(o_main, o_mem, w_a, w_b, x, g, b, tm=512):
    m = x.shape[0]
    row = lambda i: (i, 0)
    full = lambda i: (0, 0)
    return pl.pallas_call(
        _outproj_kernel, grid=(m // tm,),
        in_specs=[pl.BlockSpec((tm, MAIN_W), row), pl.BlockSpec((tm, MEM_W), row),
                  pl.BlockSpec(w_a.shape, full), pl.BlockSpec(w_b.shape, full),
                  pl.BlockSpec((tm, D_MODEL), row), pl.BlockSpec((1, D_MODEL), full), pl.BlockSpec((1, D_MODEL), full)],
        out_specs=pl.BlockSpec((tm, D_MODEL), row),
        out_shape=jax.ShapeDtypeStruct((m, D_MODEL), F32),
        compiler_params=_params(("arbitrary",), VMEM_LIMIT), name="outproj_ln")(o_main, o_mem, w_a, w_b, x, g, b)


FF_CHUNK = 256
FF_ROWS = 2048
SUBLANES = 8


def _ffn_kernel(x_ref, wa_ref, wb_ref, cw_ref, cb_ref, wo_ref, g_ref, b_ref, o_ref, xb_ref, hid_ref):
    c = pl.program_id(1)
    last = pl.num_programs(1) - 1

    @pl.when(c == 0)
    def _():
        xb_ref[...] = x_ref[...].astype(BF16)
        hid_ref[1] = jnp.zeros(hid_ref.shape[1:], BF16)
        o_ref[...] = jnp.zeros(o_ref.shape, F32)

    blocks = [slice(r0, r0 + FF_ROWS) for r0 in range(0, xb_ref.shape[0], FF_ROWS)]

    def matmuls(rows):
        xb = xb_ref[rows, :]
        a = _dot(xb, wa_ref[...])
        o_ref[rows, :] += _dot(hid_ref[(c + 1) % 2, rows, :], wo_ref[...])
        gate = _dot(xb, wb_ref[...])
        return a, gate

    def elementwise(rows, a, gate, tail):
        ext = jnp.concatenate([tail, a], axis=0)
        a1 = pltpu.roll(ext, 1, 0)[SUBLANES:]
        a2 = pltpu.roll(ext, 2, 0)[SUBLANES:]
        conv = cw_ref[0:1, :] * a2 + cw_ref[1:2, :] * a1 + cw_ref[2:3, :] * a + cb_ref[...]
        hid_ref[c % 2, rows, :] = (jax.nn.gelu(conv) * gate).astype(BF16)
        return a[FF_ROWS - SUBLANES:]

    tail = jnp.zeros((SUBLANES, wa_ref.shape[1]), F32)
    ready = matmuls(blocks[0])
    for r, rows in enumerate(blocks):
        cur = ready
        if r + 1 < len(blocks):
            ready = matmuls(blocks[r + 1])
        tail = elementwise(rows, cur[0], cur[1], tail)

    @pl.when(c == last)
    def _():
        o_ref[...] = _layer_norm(ALPHA * x_ref[...] + o_ref[...], g_ref[...], b_ref[...])


def _ffn_ln(x, w_in, conv_w, conv_b, w_out, g, b):
    bsz, s, d = x.shape
    n_c = D_FF // FF_CHUNK
    cur = lambda c: jnp.minimum(c, n_c - 1)
    prev = lambda c: jnp.maximum(c - 1, 0)
    return pl.pallas_call(
        _ffn_kernel, grid=(bsz, n_c + 1),
        in_specs=[pl.BlockSpec((None, s, d), lambda bi, c: (bi, 0, 0)),
                  pl.BlockSpec((d, FF_CHUNK), lambda bi, c: (0, cur(c))),
                  pl.BlockSpec((d, FF_CHUNK), lambda bi, c: (0, n_c + cur(c))),
                  pl.BlockSpec((CONV_W, FF_CHUNK), lambda bi, c: (0, cur(c))),
                  pl.BlockSpec((1, FF_CHUNK), lambda bi, c: (0, cur(c))),
                  pl.BlockSpec((FF_CHUNK, d), lambda bi, c: (prev(c), 0)),
                  pl.BlockSpec((1, d), lambda bi, c: (0, 0)), pl.BlockSpec((1, d), lambda bi, c: (0, 0))],
        out_specs=pl.BlockSpec((None, s, d), lambda bi, c: (bi, 0, 0)),
        out_shape=jax.ShapeDtypeStruct((bsz, s, d), F32),
        scratch_shapes=[pltpu.VMEM((s, d), BF16), pltpu.VMEM((2, s, FF_CHUNK), BF16)],
        compiler_params=_params(("arbitrary", "arbitrary"), VMEM_LIMIT),
        name="ffn_ln")(x, w_in, w_in, conv_w, conv_b, w_out, g, b)


def _dup(w):
    return jnp.concatenate([w, w], axis=1)


def _cmp_weights(pe_k, w1_k, w2_k, pe_v, w1_v, w2_v):
    half = CMP_LEN // 2
    zk = jnp.zeros((half, HEAD_DIM, CMP_HIDDEN), F32)

    def w1_half(sl):
        wk = w1_k.reshape(CMP_LEN, HEAD_DIM, CMP_HIDDEN)[sl]
        wv = w1_v.reshape(CMP_LEN, HEAD_DIM, CMP_HIDDEN)[sl]
        top = jnp.concatenate([wk, zk], axis=2)
        bot = jnp.concatenate([zk, wv], axis=2)
        return jnp.concatenate([top, bot], axis=1).reshape(half * 2 * HEAD_DIM, 2 * CMP_HIDDEN)

    w1 = jnp.stack([w1_half(slice(0, half)), w1_half(slice(half, CMP_LEN))]).astype(BF16)
    pe = jnp.concatenate([pe_k, pe_v], axis=1)
    pe = jnp.stack([pe[:half].reshape(-1), pe[half:].reshape(-1)])
    zo = jnp.zeros((CMP_HIDDEN, 2 * HEAD_DIM), F32)
    w2 = jnp.concatenate([jnp.concatenate([_dup(w2_k), zo], axis=1),
                          jnp.concatenate([zo, _dup(w2_v)], axis=1)], axis=0).astype(BF16)
    return pe, w1, w2


def _overlap_matrix(seq):
    n_cmp = (seq - CMP_LEN) // CMP_STRIDE + 1
    n_slc = seq // SLC_BLOCK
    start = np.arange(LANES) * CMP_STRIDE
    bs = np.arange(LANES) * SLC_BLOCK
    ovl_t = ((start[None, :] < bs[:, None] + SLC_BLOCK) & (start[None, :] + CMP_LEN > bs[:, None])
             & (np.arange(LANES)[None, :] < n_cmp) & (np.arange(LANES)[:, None] < n_slc))
    return jnp.asarray(ovl_t, BF16)


def kernel(x, mem, rel_bias, a_w_in, a_cmp_pe_k, a_cmp_w1_k, a_cmp_w2_k, a_cmp_pe_v, a_cmp_w1_v, a_cmp_w2_v,
           a_w_mem_kv, a_w_out, shared_w_kv, b_w_in, b_w_mem_kv, b_w_out, ln1_g, ln1_b, ln2_g, ln2_b,
           ffn_w_in, ffn_conv_w, ffn_conv_b, ffn_w_out):
    bsz, seq, d = x.shape
    n_mem = mem.shape[1]
    m = bsz * seq
    assert (seq, d) == (2048, D_MODEL) and seq % TQ == 0
    n_qt = seq // TQ

    bb, bc, far = _bias_tables(rel_bias, n_qt)
    ovl_t = _overlap_matrix(seq)
    memf = mem.reshape(bsz * n_mem, d)
    xf = x.reshape(m, d)

    def ffn(xcur, layer):
        return _ffn_ln(xcur.reshape(bsz, seq, d), ffn_w_in[layer].astype(BF16), ffn_conv_w[layer],
                       ffn_conv_b[layer][None, :], ffn_w_out[layer].astype(BF16),
                       ln2_g[layer][None, :], ln2_b[layer][None, :]).reshape(m, d)

    def out_ln(o_main, o_mem, w_out, xcur, layer):
        w = w_out.astype(BF16)
        return _outproj_ln(o_main.reshape(m, MAIN_W), o_mem.reshape(m, MEM_W), w[:MAIN_W], w[MAIN_W:], xcur,
                           ln1_g[layer][None, :], ln1_b[layer][None, :])

    w = a_w_in[0]
    c0 = MAIN_W
    cols = [w[:, c0 + kk * HEAD_DIM:c0 + (kk + 1) * HEAD_DIM] for kk in range(6)]
    c_g = c0 + 6 * HEAD_DIM
    w_g = jnp.pad(w[:, c_g:c_g + 3 * N_HEADS], ((0, 0), (0, LANES - 3 * N_HEADS)))
    w_qm = w[:, c_g + 3 * N_HEADS:]
    weights = [w[:, :MAIN_W], jnp.concatenate(cols[0:2], axis=1), _dup(cols[2]), _dup(cols[3]),
               _dup(cols[4]), _dup(cols[5]), w_g, w_qm]
    weights = [wi.astype(BF16) for wi in weights]
    q, kvc_tok, ks2, vs2, kw2, vw2, g, qm = _proj(
        xf, weights, [BF16, BF16, BF16, BF16, BF16, BF16, F32, BF16], 512, "proj_a",
        scales=[Q_SCALE, None, None, None, None, None, None, Q_SCALE])

    pe, w1, w2 = _cmp_weights(a_cmp_pe_k[0], a_cmp_w1_k[0], a_cmp_w2_k[0],
                              a_cmp_pe_v[0], a_cmp_w1_v[0], a_cmp_w2_v[0])
    kvc = _compress(kvc_tok.reshape(bsz, seq // (CMP_LEN // 2), (CMP_LEN // 2) * LANES), pe, w1, w2)

    r3 = lambda t: t.reshape(bsz, seq, t.shape[-1])
    o_main = _nsa_attention(r3(q), r3(g), kvc, r3(ks2), r3(vs2), r3(kw2), r3(vw2), bb, far, bc, ovl_t)
    (mkv,) = _proj(memf, [a_w_mem_kv[0].astype(BF16)], [BF16], 512, "proj_mem_a")
    o_mem = _mem_attention(r3(qm), mkv.reshape(bsz, n_mem, 2 * MEM_W))
    x1 = out_ln(o_main, o_mem, a_w_out[0], xf, 0)
    x1 = ffn(x1, 0)

    wb = b_w_in[0].astype(BF16)
    wkv = shared_w_kv.astype(BF16)
    q, qm, k, v, kmean = _proj_b(x1, wb[:, :MAIN_W], wb[:, MAIN_W:], wkv[:, :MAIN_W], wkv[:, MAIN_W:])
    n_blk = seq // MOBA_BLOCK
    kmean = jnp.pad(kmean.reshape(bsz, n_blk, MAIN_W), ((0, 0), (0, LANES - n_blk), (0, 0)))
    o_main = _moba_attention(r3(q), r3(k), r3(v), kmean, bb, far)
    (mkv,) = _proj(memf, [b_w_mem_kv[0].astype(BF16)], [BF16], 512, "proj_mem_b")
    o_mem = _mem_attention(r3(qm), mkv.reshape(bsz, n_mem, 2 * MEM_W))
    x2 = out_ln(o_main, o_mem, b_w_out[0], x1, 1)
    x2 = ffn(x2, 1)
    return x2.reshape(bsz, seq, d)
```

```python
import functools
import math

import numpy as np
import jax
import jax.numpy as jnp
from jax import lax
from jax.experimental import pallas as pl
from jax.experimental.pallas import tpu as pltpu

F32 = jnp.float32
BF16 = jnp.bfloat16

D_MODEL = 1024
HEAD_DIM = 64
N_HEADS = 12
N_PAIRS = N_HEADS // 2
MAIN_W = N_HEADS * HEAD_DIM
MEM_HEADS = 4
MEM_W = MEM_HEADS * HEAD_DIM
CMP_LEN = 32
CMP_STRIDE = 16
CMP_HIDDEN = 256
SLC_BLOCK = 64
SLC_TOPK = 16
WINDOW = 512
FORCE_SCORE = 1.0e4
MOBA_BLOCK = 256
MOBA_TOPK = 3
REL_BUCKETS = 32
REL_MAX_DIST = 128
D_FF = 2816
CONV_W = 3
DEPTH = 2
ALPHA = (2.0 * DEPTH) ** 0.25
LN_EPS = 1e-5
NEG_INF = -1e30
TINY = 1e-30
LOG2E = math.log2(math.e)
Q_SCALE = HEAD_DIM ** -0.5 * LOG2E

LANES = 128
TQ = 256
VMEM_LIMIT = 60 * 1024 * 1024
V_ROWS = HEAD_DIM + 8
QK_LOOKAHEAD = 6


def _dot(a, b):
    return jnp.dot(a, b, preferred_element_type=F32)


def _dot_nt(a, b):
    return lax.dot_general(a, b, (((1,), (1,)), ((), ())), preferred_element_type=F32)


def _params(sem, vmem=None):
    return pltpu.CompilerParams(dimension_semantics=sem, vmem_limit_bytes=vmem)


def _proj_kernel(x_ref, w_ref, *o_refs, scales):
    y = _dot(x_ref[...].astype(BF16), w_ref[...])
    col = 0
    for o_ref, scale in zip(o_refs, scales):
        part = y[:, col:col + o_ref.shape[1]]
        o_ref[...] = (part if scale is None else part * scale).astype(o_ref.dtype)
        col += o_ref.shape[1]


def _proj(x, weights, dtypes, tm, name, scales=None):
    m, k = x.shape
    scales = tuple(scales) if scales is not None else (None,) * len(weights)
    assert all(w.shape[1] % LANES == 0 for w in weights)
    w_all = jnp.concatenate(weights, axis=1) if len(weights) > 1 else weights[0]
    out_specs = [pl.BlockSpec((tm, w.shape[1]), lambda i: (i, 0)) for w in weights]
    out_shape = [jax.ShapeDtypeStruct((m, w.shape[1]), dt) for w, dt in zip(weights, dtypes)]
    return pl.pallas_call(
        functools.partial(_proj_kernel, scales=scales), grid=(m // tm,),
        in_specs=[pl.BlockSpec((tm, k), lambda i: (i, 0)), pl.BlockSpec(w_all.shape, lambda i: (0, 0))],
        out_specs=out_specs, out_shape=out_shape,
        compiler_params=_params(("arbitrary",), VMEM_LIMIT), name=name)(x, w_all)


def _projb_kernel(x_ref, wq_ref, wm_ref, wk_ref, wv_ref, q_ref, qm_ref, k_ref, v_ref, km_ref):
    xb = x_ref[...].astype(BF16)
    q_ref[...] = (_dot(xb, wq_ref[...]) * Q_SCALE).astype(BF16)
    qm_ref[...] = (_dot(xb, wm_ref[...]) * Q_SCALE).astype(BF16)
    k = _dot(xb, wk_ref[...])
    k_ref[...] = k.astype(BF16)
    km_ref[...] = jnp.mean(k, axis=0, keepdims=True)
    v_ref[...] = _dot(xb, wv_ref[...]).astype(BF16)


def _proj_b(x, wq, wm, wk, wv):
    m, k = x.shape
    nblk = m // MOBA_BLOCK
    row = lambda i: (i, 0)
    full = lambda i: (0, 0)
    return pl.pallas_call(
        _projb_kernel, grid=(nblk,),
        in_specs=[pl.BlockSpec((MOBA_BLOCK, k), row)] + [pl.BlockSpec(w.shape, full) for w in (wq, wm, wk, wv)],
        out_specs=[pl.BlockSpec((MOBA_BLOCK, MAIN_W), row), pl.BlockSpec((MOBA_BLOCK, MEM_W), row),
                   pl.BlockSpec((MOBA_BLOCK, MAIN_W), row), pl.BlockSpec((MOBA_BLOCK, MAIN_W), row),
                   pl.BlockSpec((None, 1, MAIN_W), lambda i: (i, 0, 0))],
        out_shape=[jax.ShapeDtypeStruct((m, MAIN_W), BF16), jax.ShapeDtypeStruct((m, MEM_W), BF16),
                   jax.ShapeDtypeStruct((m, MAIN_W), BF16), jax.ShapeDtypeStruct((m, MAIN_W), BF16),
                   jax.ShapeDtypeStruct((nblk, 1, MAIN_W), F32)],
        compiler_params=_params(("arbitrary",), VMEM_LIMIT), name="proj_b")(x, wq, wm, wk, wv)


def _bias_kernel(tbl_ref, bb_ref, bc_ref, far_ref):
    h = pl.program_id(0)

    def bias_of(dist):
        n = jnp.maximum(dist, 0)
        max_exact = REL_BUCKETS // 2
        nf = jnp.maximum(n, 1).astype(F32)
        large = max_exact + (jnp.log(nf / max_exact) / math.log(REL_MAX_DIST / max_exact)
                             * (REL_BUCKETS - max_exact)).astype(jnp.int32)
        large = jnp.minimum(large, REL_BUCKETS - 1)
        bucket = jnp.where(n < max_exact, n, large)
        out = jnp.zeros(dist.shape, F32)
        for kk in range(REL_BUCKETS):
            out = jnp.where(bucket == kk, tbl_ref[h * REL_BUCKETS + kk], out)
        return out * LOG2E

    key = lax.broadcasted_iota(jnp.int32, (TQ, TQ), 0)
    qry = lax.broadcasted_iota(jnp.int32, (TQ, TQ), 1)
    d0 = qry - key
    bb_ref[0] = jnp.where(d0 >= 0, bias_of(d0), NEG_INF)
    bb_ref[1] = bias_of(d0 + TQ)
    far = bias_of(d0 + 2 * TQ)
    bb_ref[2] = far
    bb_ref[3] = jnp.where(d0 + 2 * TQ < WINDOW, far, NEG_INF)
    far_ref[...] = far[0:1, :]

    nc = lax.broadcasted_iota(jnp.int32, (LANES, TQ), 0)
    qc = lax.broadcasted_iota(jnp.int32, (LANES, TQ), 1)
    n_cmp = (2048 - CMP_LEN) // CMP_STRIDE + 1
    for i in range(bc_ref.shape[0]):
        dc = i * TQ + qc - (nc * CMP_STRIDE + CMP_LEN - 1)
        bc_ref[i] = jnp.where((dc >= 0) & (nc < n_cmp), bias_of(dc), NEG_INF)


def _bias_tables(rel_bias, n_qt):
    tbl = rel_bias.T.reshape(-1)
    return pl.pallas_call(
        _bias_kernel, grid=(N_HEADS,),
        in_specs=[pl.BlockSpec(memory_space=pltpu.SMEM)],
        out_specs=[pl.BlockSpec((None, 4, TQ, TQ), lambda h: (h, 0, 0, 0)),
                   pl.BlockSpec((n_qt, None, LANES, TQ), lambda h: (0, h, 0, 0)),
                   pl.BlockSpec((None, 1, TQ), lambda h: (h, 0, 0))],
        out_shape=[jax.ShapeDtypeStruct((N_HEADS, 4, TQ, TQ), F32),
                   jax.ShapeDtypeStruct((n_qt, N_HEADS, LANES, TQ), F32),
                   jax.ShapeDtypeStruct((N_HEADS, 1, TQ), F32)],
        compiler_params=_params(("arbitrary",)), name="bias_tables")(tbl)


def _cmp_kernel(kv_ref, pe_ref, w1_ref, w2_ref, o_ref):
    x = kv_ref[...].astype(F32)
    lo = _dot((x + pe_ref[0:1, :]).astype(BF16), w1_ref[0])
    hi = _dot((x + pe_ref[1:2, :]).astype(BF16), w1_ref[1])
    nrow = x.shape[0]
    hid = lo + pltpu.roll(hi, nrow - 1, 0)
    o_ref[...] = _dot(jax.nn.gelu(hid).astype(BF16), w2_ref[...])


def _compress(kvr, pe, w1, w2):
    b, nrow, width = kvr.shape
    return pl.pallas_call(
        _cmp_kernel, grid=(b,),
        in_specs=[pl.BlockSpec((None, nrow, width), lambda i: (i, 0, 0)),
                  pl.BlockSpec(pe.shape, lambda i: (0, 0)),
                  pl.BlockSpec(w1.shape, lambda i: (0, 0, 0)),
                  pl.BlockSpec(w2.shape, lambda i: (0, 0))],
        out_specs=pl.BlockSpec((None, nrow, w2.shape[1]), lambda i: (i, 0, 0)),
        out_shape=jax.ShapeDtypeStruct((b, nrow, w2.shape[1]), F32),
        compiler_params=_params(("arbitrary",), VMEM_LIMIT), name="nsa_compress")(kvr, pe, w1, w2)


def _softmax_tile(state, c, s_t, v_t, vis=None, cbias=None):
    m_ref, acc_ref = state
    m = m_ref[c]
    n_blk = 1 if vis is None else vis.shape[0]
    rows = s_t.shape[0] // n_blk
    parts = [s_t[b * rows:(b + 1) * rows] for b in range(n_blk)]
    tile_max = None
    for b in range(n_blk):
        mb = jnp.max(parts[b], axis=0, keepdims=True)
        if vis is not None:
            mb = jnp.where(vis[b:b + 1] > 0.5, mb, NEG_INF)
        tile_max = mb if tile_max is None else jnp.maximum(tile_max, mb)
    if cbias is not None:
        tile_max = tile_max + cbias
    m_new = jnp.maximum(m, tile_max)
    shift = m_new if cbias is None else m_new - cbias
    p_parts = []
    for b in range(n_blk):
        off = shift if vis is None else jnp.where(vis[b:b + 1] > 0.5, shift, -NEG_INF)
        p_parts.append(jnp.exp2(parts[b] - off))
    p_t = p_parts[0] if n_blk == 1 else jnp.concatenate(p_parts, axis=0)
    alpha = jnp.exp2(m - m_new)
    acc_ref[c] = alpha * acc_ref[c] + _dot(v_t, p_t.astype(BF16))
    m_ref[c] = m_new


def _run_chains(chains, logits_fn, update_fn):
    pending = {}
    for idx, c in enumerate(chains[:QK_LOOKAHEAD]):
        pending[c] = logits_fn(c, idx)
    for idx, c in enumerate(chains):
        update_fn(c, pending.pop(c), idx % QK_LOOKAHEAD)
        if idx + QK_LOOKAHEAD < len(chains):
            nxt = chains[idx + QK_LOOKAHEAD]
            pending[nxt] = logits_fn(nxt, idx % QK_LOOKAHEAD)


def _stage_logits(s_ref, slot, s_t):
    s_ref[slot] = s_t


def _init_state(state):
    m_ref, acc_ref = state
    m_ref[...] = jnp.full(m_ref.shape, NEG_INF, F32)
    acc_ref[...] = jnp.zeros(acc_ref.shape, F32)


def _chain_out(state, c):
    _, acc_ref = state
    return acc_ref[c, 0:HEAD_DIM, :] / jnp.maximum(acc_ref[c, HEAD_DIM:HEAD_DIM + 1, :], TINY)


def _values_with_ones(tile, row0):
    v_t = tile.astype(F32).T[row0:row0 + HEAD_DIM]
    pad = lax.broadcasted_iota(jnp.int32, (V_ROWS - HEAD_DIM, tile.shape[0]), 0)
    return jnp.concatenate([v_t, jnp.where(pad == 0, 1.0, 0.0)], axis=0).astype(BF16)


def _rows(ref, kt, cols=slice(None)):
    return ref[pl.ds(pl.multiple_of(kt * TQ, TQ), TQ), cols]


def _store_head_queries(q_ref, qh_ref, n_pairs):
    row = lax.broadcasted_iota(jnp.int32, (LANES, TQ), 0)
    halves = (row < HEAD_DIM, row >= HEAD_DIM)
    for pp in range(n_pairs):
        q2t = q_ref[:, pp * LANES:(pp + 1) * LANES].astype(F32).T
        for j in range(2):
            qh_ref[2 * pp + j] = jnp.where(halves[j], q2t, 0.0).astype(BF16)


def _nsa_kernel(q_ref, g_ref, kvc_ref, ks_ref, vs_ref, kw_ref, vw_ref, bb_ref, far_ref, bc_ref, ovl_ref,
                o_ref, vst_ref, vwt_ref, qh_ref, gs_ref, oc_ref, sel_ref, m_ref, acc_ref, s_ref):
    i = pl.program_id(1)
    n_kt = vst_ref.shape[0]
    state = (m_ref, acc_ref)

    @pl.when(i == 0)
    def _values():
        for kt in range(n_kt):
            rows = slice(kt * TQ, (kt + 1) * TQ)
            vst_ref[kt] = _values_with_ones(vs_ref[rows, :], 0)
            vwt_ref[kt] = _values_with_ones(vw_ref[rows, :], 0)

    _store_head_queries(q_ref, qh_ref, N_PAIRS)
    gs_ref[...] = jax.nn.sigmoid(g_ref[...]).T

    kc2 = kvc_ref[:, 0:LANES].astype(BF16)
    vc_t = kvc_ref[:, LANES:2 * LANES].T[0:HEAD_DIM].astype(BF16)
    psums = []

    def cmp_logits(h, slot):
        s_ref[slot, 0:LANES, :] = _dot(kc2, qh_ref[h])

    def cmp_update(h, _s, slot):
        bias = bc_ref[h]
        s = s_ref[slot, 0:LANES, :] + bias
        valid = bias > 0.5 * NEG_INF
        m = jnp.max(s, axis=0, keepdims=True)
        e = jnp.where(valid, jnp.exp2(s - m), 0.0)
        pc = e / jnp.maximum(jnp.sum(e, axis=0, keepdims=True), TINY)
        psums[:] = [pc if not psums else psums[0] + pc]
        oc_ref[h] = gs_ref[3 * h:3 * h + 1, :] * _dot(vc_t, pc.astype(BF16))

    _run_chains(list(range(N_HEADS)), cmp_logits, cmp_update)
    psum = psums[0]
    p_hi = psum.astype(BF16)
    p_lo = (psum - p_hi.astype(F32)).astype(BF16)
    imp = _dot(ovl_ref[...], p_hi) + _dot(ovl_ref[...], p_lo)
    per_tile = TQ // SLC_BLOCK
    n_slc = n_kt * per_tile
    imp = imp[0:n_slc, :]
    sidx = lax.broadcasted_iota(jnp.int32, (n_slc, TQ), 0)
    t = i * TQ + lax.broadcasted_iota(jnp.int32, (n_slc, TQ), 1)
    cur = lax.shift_right_logical(t, 6)
    eligible = sidx <= cur
    forced = (sidx == 0) | (sidx == cur) | (sidx == cur - 1)
    score = jnp.where(eligible, jnp.where(forced, FORCE_SCORE, 0.0), NEG_INF) + imp
    cnt = jnp.zeros((n_slc, TQ), jnp.int32)
    for sp in range(n_slc):
        row = score[sp:sp + 1, :]
        better = (row > score) | ((row == score) & (sp < sidx))
        cnt = cnt + better.astype(jnp.int32)
    chosen = jnp.where(cnt < min(SLC_TOPK, n_slc), 1.0, 0.0)
    for kt in range(n_kt):
        sel_ref[kt, 0:per_tile, :] = chosen[kt * per_tile:(kt + 1) * per_tile, :]

    _init_state(state)
    slc = [("s", h) for h in range(N_HEADS)]
    win = [("w", h) for h in range(N_HEADS)]
    n_near = jnp.minimum(i, WINDOW // TQ) + 1

    def near_body(jj, _):
        kt = i - jj
        k_s, k_w, v_s, v_w = _rows(ks_ref, kt), _rows(kw_ref, kt), vst_ref[kt], vwt_ref[kt]
        vis = sel_ref[kt, 0:per_tile, :]
        b_s = jnp.minimum(jj, 2)
        b_w = jnp.where(jj == 2, 3, jj)

        def logits(c, slot):
            _stage_logits(s_ref, slot, _dot(k_s if c[0] == "s" else k_w, qh_ref[c[1]]))

        def update(c, _s, slot):
            kind, h = c
            if kind == "s":
                _softmax_tile(state, h, s_ref[slot] + bb_ref[h, b_s], v_s, vis=vis)
            else:
                _softmax_tile(state, N_HEADS + h, s_ref[slot] + bb_ref[h, b_w], v_w)

        _run_chains([c for pair in zip(slc, win) for c in pair], logits, update)
        return 0

    lax.fori_loop(0, n_near, near_body, 0)

    def far_body(jj, _):
        kt = i - jj
        k_s, v_s = _rows(ks_ref, kt), vst_ref[kt]
        vis = sel_ref[kt, 0:per_tile, :]
        _run_chains(slc, lambda c, slot: _dot(k_s, qh_ref[c[1]]),
                    lambda c, s, slot: _softmax_tile(state, c[1], s, v_s, vis=vis, cbias=far_ref[c[1]]))
        return 0

    lax.fori_loop(n_near, i + 1, far_body, 0)

    for pp in range(N_PAIRS):
        hs = (2 * pp, 2 * pp + 1)
        outs = []
        for h in hs:
            g1 = gs_ref[3 * h + 1:3 * h + 2, :]
            g2 = gs_ref[3 * h + 2:3 * h + 3, :]
            outs.append(oc_ref[h] + g1 * _chain_out(state, h) + g2 * _chain_out(state, N_HEADS + h))
        o_ref[:, pp * LANES:(pp + 1) * LANES] = jnp.concatenate(outs, axis=0).T.astype(o_ref.dtype)


def _nsa_attention(q, g, kvc, ks2, vs2, kw2, vw2, bb, far, bc, ovl_t):
    b, s, _ = q.shape
    n_qt = s // TQ
    per_b = lambda bi, i: (bi, 0, 0)
    return pl.pallas_call(
        _nsa_kernel, grid=(b, n_qt),
        in_specs=[pl.BlockSpec((None, TQ, MAIN_W), lambda bi, i: (bi, i, 0)),
                  pl.BlockSpec((None, TQ, LANES), lambda bi, i: (bi, i, 0)),
                  pl.BlockSpec((None,) + kvc.shape[1:], per_b),
                  pl.BlockSpec((None, s, LANES), per_b), pl.BlockSpec((None, s, LANES), per_b),
                  pl.BlockSpec((None, s, LANES), per_b), pl.BlockSpec((None, s, LANES), per_b),
                  pl.BlockSpec(bb.shape, lambda bi, i: (0, 0, 0, 0)),
                  pl.BlockSpec(far.shape, lambda bi, i: (0, 0, 0)),
                  pl.BlockSpec((None,) + bc.shape[1:], lambda bi, i: (i, 0, 0, 0)),
                  pl.BlockSpec(ovl_t.shape, lambda bi, i: (0, 0))],
        out_specs=pl.BlockSpec((None, TQ, MAIN_W), lambda bi, i: (bi, i, 0)),
        out_shape=jax.ShapeDtypeStruct((b, s, MAIN_W), BF16),
        scratch_shapes=[pltpu.VMEM((n_qt, V_ROWS, TQ), BF16), pltpu.VMEM((n_qt, V_ROWS, TQ), BF16),
                        pltpu.VMEM((N_HEADS, LANES, TQ), BF16), pltpu.VMEM((LANES, TQ), F32),
                        pltpu.VMEM((N_HEADS, HEAD_DIM, TQ), F32), pltpu.VMEM((n_qt, 8, TQ), F32),
                        pltpu.VMEM((2 * N_HEADS, 1, TQ), F32),
                        pltpu.VMEM((2 * N_HEADS, V_ROWS, TQ), F32), pltpu.VMEM((QK_LOOKAHEAD, TQ, TQ), F32)],
        compiler_params=_params(("arbitrary", "arbitrary"), VMEM_LIMIT),
        name="nsa_attention")(q, g, kvc, ks2, vs2, kw2, vw2, bb, far, bc, ovl_t)


def _moba_kernel(q_ref, k_ref, v_ref, km_ref, bb_ref, far_ref, o_ref, vt_ref, qh_ref, sel_ref,
                 m_ref, acc_ref, s_ref):
    i = pl.program_id(1)
    n_blk = vt_ref.shape[1]
    state = (m_ref, acc_ref)
    heads = list(range(N_HEADS))

    @pl.when(i == 0)
    def _values():
        for pp in range(N_PAIRS):
            for kt in range(n_blk):
                tile = v_ref[kt * TQ:(kt + 1) * TQ, pp * LANES:(pp + 1) * LANES]
                for j in range(2):
                    vt_ref[2 * pp + j, kt] = _values_with_ones(tile, j * HEAD_DIM)

    _store_head_queries(q_ref, qh_ref, N_PAIRS)
    blk = lax.broadcasted_iota(jnp.int32, (n_blk, TQ), 0)
    eligible = blk < i
    for pp in range(N_PAIRS):
        km = km_ref[:, pp * LANES:(pp + 1) * LANES].astype(BF16)
        for j in range(2):
            gate = _dot(km, qh_ref[2 * pp + j])[0:n_blk, :]
            gate = jnp.where(eligible, gate, NEG_INF)
            cnt = jnp.zeros((n_blk, TQ), jnp.int32)
            for n in range(n_blk):
                row = gate[n:n + 1, :]
                better = (row > gate) | ((row == gate) & (n < blk))
                cnt = cnt + better.astype(jnp.int32)
            chosen = (cnt < min(MOBA_TOPK, n_blk - 1)) & eligible
            vis = jnp.where(chosen | (blk == i), 1.0, 0.0)
            for n in range(n_blk):
                sel_ref[2 * pp + j, n] = vis[n:n + 1, :]
    _init_state(state)

    def tile_inputs(kt, h):
        pp = h // 2
        k = _rows(k_ref, kt, slice(pp * LANES, (pp + 1) * LANES))
        return k, vt_ref[h, kt], sel_ref[h, kt]

    def near_tile(jj):
        kt = i - jj
        def update(h, _s, slot):
            _, v_t, vis = tile_inputs(kt, h)
            _softmax_tile(state, h, s_ref[slot] + bb_ref[h, jj], v_t, vis=vis)

        def logits(h, slot):
            _stage_logits(s_ref, slot, _dot(tile_inputs(kt, h)[0], qh_ref[h]))

        _run_chains(heads, logits, update)

    near_tile(0)
    n_near = jnp.minimum(i, 1) + 1

    @pl.when(i >= 1)
    def _():
        near_tile(1)

    def far_body(jj, _):
        kt = i - jj
        def update(h, s, slot):
            _, v_t, vis = tile_inputs(kt, h)
            _softmax_tile(state, h, s, v_t, vis=vis, cbias=far_ref[h])
        _run_chains(heads, lambda h, slot: _dot(tile_inputs(kt, h)[0], qh_ref[h]), update)
        return 0

    lax.fori_loop(n_near, i + 1, far_body, 0)

    for pp in range(N_PAIRS):
        out_t = jnp.concatenate([_chain_out(state, 2 * pp), _chain_out(state, 2 * pp + 1)], axis=0)
        o_ref[:, pp * LANES:(pp + 1) * LANES] = out_t.T.astype(o_ref.dtype)


def _moba_attention(q, k, v, kmean, bb, far):
    b, s, _ = q.shape
    n_qt = s // TQ
    per_b = lambda bi, i: (bi, 0, 0)
    return pl.pallas_call(
        _moba_kernel, grid=(b, n_qt),
        in_specs=[pl.BlockSpec((None, TQ, MAIN_W), lambda bi, i: (bi, i, 0)),
                  pl.BlockSpec((None, s, MAIN_W), per_b), pl.BlockSpec((None, s, MAIN_W), per_b),
                  pl.BlockSpec((None, LANES, MAIN_W), per_b),
                  pl.BlockSpec((N_HEADS, 2, TQ, TQ), lambda bi, i: (0, 0, 0, 0)),
                  pl.BlockSpec(far.shape, lambda bi, i: (0, 0, 0))],
        out_specs=pl.BlockSpec((None, TQ, MAIN_W), lambda bi, i: (bi, i, 0)),
        out_shape=jax.ShapeDtypeStruct((b, s, MAIN_W), BF16),
        scratch_shapes=[pltpu.VMEM((N_HEADS, n_qt, V_ROWS, TQ), BF16), pltpu.VMEM((N_HEADS, LANES, TQ), BF16),
                        pltpu.VMEM((N_HEADS, n_qt, 1, TQ), F32),
                        pltpu.VMEM((N_HEADS, 1, TQ), F32),
                        pltpu.VMEM((N_HEADS, V_ROWS, TQ), F32), pltpu.VMEM((QK_LOOKAHEAD, TQ, TQ), F32)],
        compiler_params=_params(("arbitrary", "arbitrary"), VMEM_LIMIT),
        name="moba_attention")(q, k, v, kmean, bb, far)


def _mem_kernel(q_ref, kv_ref, o_ref, qh_ref):
    _store_head_queries(q_ref, qh_ref, MEM_HEADS // 2)
    outs = {}

    def logits(h, slot):
        return _dot(kv_ref[:, (h // 2) * LANES:(h // 2 + 1) * LANES], qh_ref[h])

    def update(h, s, slot):
        lanes = slice(MEM_W + (h // 2) * LANES, MEM_W + (h // 2 + 1) * LANES)
        v_t = kv_ref[:, lanes].astype(F32).T[(h % 2) * HEAD_DIM:(h % 2 + 1) * HEAD_DIM].astype(BF16)
        e = jnp.exp2(s - jnp.max(s, axis=0, keepdims=True))
        pr = e / jnp.sum(e, axis=0, keepdims=True)
        outs[h] = _dot(v_t, pr.astype(BF16))

    _run_chains(list(range(MEM_HEADS)), logits, update)
    for pp in range(MEM_HEADS // 2):
        pair = jnp.concatenate([outs[2 * pp], outs[2 * pp + 1]], axis=0)
        o_ref[:, pp * LANES:(pp + 1) * LANES] = pair.T.astype(o_ref.dtype)


def _mem_attention(qm, mem_kv):
    b, s, _ = qm.shape
    n_mem = mem_kv.shape[1]
    return pl.pallas_call(
        _mem_kernel, grid=(b, s // TQ),
        in_specs=[pl.BlockSpec((None, TQ, MEM_W), lambda bi, i: (bi, i, 0)),
                  pl.BlockSpec((None, n_mem, 2 * MEM_W), lambda bi, i: (bi, 0, 0))],
        out_specs=pl.BlockSpec((None, TQ, MEM_W), lambda bi, i: (bi, i, 0)),
        out_shape=jax.ShapeDtypeStruct((b, s, MEM_W), BF16),
        scratch_shapes=[pltpu.VMEM((MEM_HEADS, LANES, TQ), BF16)],
        compiler_params=_params(("arbitrary", "arbitrary")), name="mem_attention")(qm, mem_kv)


def _layer_norm(y, g, b):
    mu = jnp.mean(y, axis=-1, keepdims=True)
    var = jnp.mean(jnp.square(y - mu), axis=-1, keepdims=True)
    return (y - mu) * lax.rsqrt(var + LN_EPS) * g + b


def _mm_ln_kernel(*refs, n_in, emit_bf16):
    a_refs, w_refs = refs[:n_in], refs[n_in:2 * n_in]
    x_ref, g_ref, b_ref = refs[2 * n_in:2 * n_in + 3]
    outs = refs[2 * n_in + 3:-1]
    y_ref = refs[-1]

    @pl.when(pl.program_id(0) == 0)
    def _():
        y_ref[...] = jnp.zeros(y_ref.shape, F32)

    y = _layer_norm(ALPHA * x_ref[...] + y_ref[...], g_ref[...], b_ref[...])
    outs[0][...] = y
    if emit_bf16:
        outs[1][...] = y.astype(BF16)
    acc = _dot(a_refs[0][...], w_refs[0][...])
    for a_ref, w_ref in zip(a_refs[1:], w_refs[1:]):
        acc = acc + _dot(a_ref[...], w_ref[...])
    y_ref[...] = acc


def _mm_ln(acts, weights, x, g, b, tm, name, emit_bf16=False):
    m, d = x.shape
    n_t = m // tm
    cur = lambda t: (jnp.minimum(t, n_t - 1), 0)
    prev = lambda t: (jnp.maximum(t - 1, 0), 0)
    full = lambda t: (0, 0)
    out_shape = [jax.ShapeDtypeStruct((m, d), F32)] + ([jax.ShapeDtypeStruct((m, d), BF16)] if emit_bf16 else [])
    return pl.pallas_call(
        functools.partial(_mm_ln_kernel, n_in=len(acts), emit_bf16=emit_bf16), grid=(n_t + 1,),
        in_specs=[pl.BlockSpec((tm, a.shape[1]), cur) for a in acts]
                 + [pl.BlockSpec(w.shape, full) for w in weights]
                 + [pl.BlockSpec((tm, d), prev), pl.BlockSpec((1, d), full), pl.BlockSpec((1, d), full)],
        out_specs=[pl.BlockSpec((tm, d), prev) for _ in out_shape],
        out_shape=out_shape,
        scratch_shapes=[pltpu.VMEM((tm, d), F32)],
        compiler_params=_params(("arbitrary",), VMEM_LIMIT), name=name)(*acts, *weights, x, g, b)


FF_CHUNK = 256
FF_ROWS = 256
SUBLANES = 8


def _ffn_hidden_kernel(xb_ref, wa_ref, wb_ref, cw_ref, cb_ref, h_ref):
    tail = jnp.zeros((SUBLANES, wa_ref.shape[1]), F32)
    for r0 in range(0, xb_ref.shape[0], FF_ROWS):
        rows = slice(r0, r0 + FF_ROWS)
        xb = xb_ref[rows, :]
        a = _dot(xb, wa_ref[...])
        ext = jnp.concatenate([tail, a], axis=0)
        a1 = pltpu.roll(ext, 1, 0)[SUBLANES:]
        a2 = pltpu.roll(ext, 2, 0)[SUBLANES:]
        act = jax.nn.gelu(cw_ref[0:1, :] * a2 + cw_ref[1:2, :] * a1 + cw_ref[2:3, :] * a + cb_ref[...])
        h_ref[rows, :] = (act * _dot(xb, wb_ref[...])).astype(BF16)
        tail = a[FF_ROWS - SUBLANES:]


def _ffn_hidden(xb, w_in, conv_w, conv_b):
    bsz, s, d = xb.shape
    n_c = D_FF // FF_CHUNK
    return pl.pallas_call(
        _ffn_hidden_kernel, grid=(bsz, n_c),
        in_specs=[pl.BlockSpec((None, s, d), lambda bi, c: (bi, 0, 0)),
                  pl.BlockSpec((d, FF_CHUNK), lambda bi, c: (0, c)),
                  pl.BlockSpec((d, FF_CHUNK), lambda bi, c: (0, n_c + c)),
                  pl.BlockSpec((CONV_W, FF_CHUNK), lambda bi, c: (0, c)),
                  pl.BlockSpec((1, FF_CHUNK), lambda bi, c: (0, c))],
        out_specs=pl.BlockSpec((None, s, FF_CHUNK), lambda bi, c: (bi, 0, c)),
        out_shape=jax.ShapeDtypeStruct((bsz, s, D_FF), BF16),
        compiler_params=_params(("arbitrary", "arbitrary"), VMEM_LIMIT),
        name="ffn_hidden")(xb, w_in, w_in, conv_w, conv_b)


def _dup(w):
    return jnp.concatenate([w, w], axis=1)


def _cmp_weights(pe_k, w1_k, w2_k, pe_v, w1_v, w2_v):
    half = CMP_LEN // 2
    zk = jnp.zeros((half, HEAD_DIM, CMP_HIDDEN), F32)

    def w1_half(sl):
        wk = w1_k.reshape(CMP_LEN, HEAD_DIM, CMP_HIDDEN)[sl]
        wv = w1_v.reshape(CMP_LEN, HEAD_DIM, CMP_HIDDEN)[sl]
        top = jnp.concatenate([wk, zk], axis=2)
        bot = jnp.concatenate([zk, wv], axis=2)
        return jnp.concatenate([top, bot], axis=1).reshape(half * 2 * HEAD_DIM, 2 * CMP_HIDDEN)

    w1 = jnp.stack([w1_half(slice(0, half)), w1_half(slice(half, CMP_LEN))]).astype(BF16)
    pe = jnp.concatenate([pe_k, pe_v], axis=1)
    pe = jnp.stack([pe[:half].reshape(-1), pe[half:].reshape(-1)])
    zo = jnp.zeros((CMP_HIDDEN, 2 * HEAD_DIM), F32)
    w2 = jnp.concatenate([jnp.concatenate([_dup(w2_k), zo], axis=1),
                          jnp.concatenate([zo, _dup(w2_v)], axis=1)], axis=0).astype(BF16)
    return pe, w1, w2


def _overlap_matrix(seq):
    n_cmp = (seq - CMP_LEN) // CMP_STRIDE + 1
    n_slc = seq // SLC_BLOCK
    start = np.arange(LANES) * CMP_STRIDE
    bs = np.arange(LANES) * SLC_BLOCK
    ovl_t = ((start[None, :] < bs[:, None] + SLC_BLOCK) & (start[None, :] + CMP_LEN > bs[:, None])
             & (np.arange(LANES)[None, :] < n_cmp) & (np.arange(LANES)[:, None] < n_slc))
    return jnp.asarray(ovl_t, BF16)


def kernel(x, mem, rel_bias, a_w_in, a_cmp_pe_k, a_cmp_w1_k, a_cmp_w2_k, a_cmp_pe_v, a_cmp_w1_v, a_cmp_w2_v,
           a_w_mem_kv, a_w_out, shared_w_kv, b_w_in, b_w_mem_kv, b_w_out, ln1_g, ln1_b, ln2_g, ln2_b,
           ffn_w_in, ffn_conv_w, ffn_conv_b, ffn_w_out):
    bsz, seq, d = x.shape
    n_mem = mem.shape[1]
    m = bsz * seq
    assert (seq, d) == (2048, D_MODEL) and seq % TQ == 0
    n_qt = seq // TQ

    bb, bc, far = _bias_tables(rel_bias, n_qt)
    ovl_t = _overlap_matrix(seq)
    memf = mem.reshape(bsz * n_mem, d)
    xf = x.reshape(m, d)

    def ffn(xcur, xcur_bf16, layer):
        hidden = _ffn_hidden(xcur_bf16.reshape(bsz, seq, d), ffn_w_in[layer].astype(BF16), ffn_conv_w[layer],
                             ffn_conv_b[layer][None, :])
        (out,) = _mm_ln([hidden.reshape(m, D_FF)], [ffn_w_out[layer].astype(BF16)], xcur,
                        ln2_g[layer][None, :], ln2_b[layer][None, :], 512, "ffn_out_ln")
        return out

    def out_ln(o_main, o_mem, w_out, xcur, layer):
        w = w_out.astype(BF16)
        return _mm_ln([o_main.reshape(m, MAIN_W), o_mem.reshape(m, MEM_W)], [w[:MAIN_W], w[MAIN_W:]], xcur,
                      ln1_g[layer][None, :], ln1_b[layer][None, :], 512, "outproj_ln", emit_bf16=True)

    w = a_w_in[0]
    c0 = MAIN_W
    cols = [w[:, c0 + kk * HEAD_DIM:c0 + (kk + 1) * HEAD_DIM] for kk in range(6)]
    c_g = c0 + 6 * HEAD_DIM
    w_g = jnp.pad(w[:, c_g:c_g + 3 * N_HEADS], ((0, 0), (0, LANES - 3 * N_HEADS)))
    w_qm = w[:, c_g + 3 * N_HEADS:]
    weights = [w[:, :MAIN_W], jnp.concatenate(cols[0:2], axis=1), _dup(cols[2]), _dup(cols[3]),
               _dup(cols[4]), _dup(cols[5]), w_g, w_qm]
    weights = [wi.astype(BF16) for wi in weights]
    q, kvc_tok, ks2, vs2, kw2, vw2, g, qm = _proj(
        xf, weights, [BF16, BF16, BF16, BF16, BF16, BF16, F32, BF16], 512, "proj_a",
        scales=[Q_SCALE, None, None, None, None, None, None, Q_SCALE])

    pe, w1, w2 = _cmp_weights(a_cmp_pe_k[0], a_cmp_w1_k[0], a_cmp_w2_k[0],
                              a_cmp_pe_v[0], a_cmp_w1_v[0], a_cmp_w2_v[0])
    kvc = _compress(kvc_tok.reshape(bsz, seq // (CMP_LEN // 2), (CMP_LEN // 2) * LANES), pe, w1, w2)

    r3 = lambda t: t.reshape(bsz, seq, t.shape[-1])
    o_main = _nsa_attention(r3(q), r3(g), kvc, r3(ks2), r3(vs2), r3(kw2), r3(vw2), bb, far, bc, ovl_t)
    (mkv,) = _proj(memf, [a_w_mem_kv[0].astype(BF16)], [BF16], 512, "proj_mem_a")
    o_mem = _mem_attention(r3(qm), mkv.reshape(bsz, n_mem, 2 * MEM_W))
    x1, x1b = out_ln(o_main, o_mem, a_w_out[0], xf, 0)
    x1 = ffn(x1, x1b, 0)

    wb = b_w_in[0].astype(BF16)
    wkv = shared_w_kv.astype(BF16)
    q, qm, k, v, kmean = _proj_b(x1, wb[:, :MAIN_W], wb[:, MAIN_W:], wkv[:, :MAIN_W], wkv[:, MAIN_W:])
    n_blk = seq // MOBA_BLOCK
    kmean = jnp.pad(kmean.reshape(bsz, n_blk, MAIN_W), ((0, 0), (0, LANES - n_blk), (0, 0)))
    o_main = _moba_attention(r3(q), r3(k), r3(v), kmean, bb, far)
    (mkv,) = _proj(memf, [b_w_mem_kv[0].astype(BF16)], [BF16], 512, "proj_mem_b")
    o_mem = _mem_attention(r3(qm), mkv.reshape(bsz, n_mem, 2 * MEM_W))
    x2, x2b = out_ln(o_main, o_mem, b_w_out[0], x1, 1)
    x2 = ffn(x2, x2b, 1)
    return x2.reshape(bsz, seq, d)
```

```python
import functools
import math

import numpy as np
import jax
import jax.numpy as jnp
from jax import lax
from jax.experimental import pallas as pl
from jax.experimental.pallas import tpu as pltpu

F32 = jnp.float32
BF16 = jnp.bfloat16

D_MODEL = 1024
HEAD_DIM = 64
N_HEADS = 12
N_PAIRS = N_HEADS // 2
MAIN_W = N_HEADS * HEAD_DIM
MEM_HEADS = 4
MEM_W = MEM_HEADS * HEAD_DIM
CMP_LEN = 32
CMP_STRIDE = 16
CMP_HIDDEN = 256
SLC_BLOCK = 64
SLC_TOPK = 16
WINDOW = 512
FORCE_SCORE = 1.0e4
MOBA_BLOCK = 256
MOBA_TOPK = 3
REL_BUCKETS = 32
REL_MAX_DIST = 128
D_FF = 2816
CONV_W = 3
DEPTH = 2
ALPHA = (2.0 * DEPTH) ** 0.25
LN_EPS = 1e-5
NEG_INF = -1e30
TINY = 1e-30
LOG2E = math.log2(math.e)
Q_SCALE = HEAD_DIM ** -0.5 * LOG2E

LANES = 128
TQ = 256
VMEM_LIMIT = 60 * 1024 * 1024
V_ROWS = HEAD_DIM + 8
QK_LOOKAHEAD = 6


def _dot(a, b):
    return jnp.dot(a, b, preferred_element_type=F32)


def _dot_nt(a, b):
    return lax.dot_general(a, b, (((1,), (1,)), ((), ())), preferred_element_type=F32)


def _params(sem, vmem=None):
    return pltpu.CompilerParams(dimension_semantics=sem, vmem_limit_bytes=vmem)


def _proj_kernel(x_ref, w_ref, *o_refs, scales):
    y = _dot(x_ref[...].astype(BF16), w_ref[...])
    col = 0
    for o_ref, scale in zip(o_refs, scales):
        part = y[:, col:col + o_ref.shape[1]]
        o_ref[...] = (part if scale is None else part * scale).astype(o_ref.dtype)
        col += o_ref.shape[1]


def _proj(x, weights, dtypes, tm, name, scales=None):
    m, k = x.shape
    scales = tuple(scales) if scales is not None else (None,) * len(weights)
    assert all(w.shape[1] % LANES == 0 for w in weights)
    w_all = jnp.concatenate(weights, axis=1) if len(weights) > 1 else weights[0]
    out_specs = [pl.BlockSpec((tm, w.shape[1]), lambda i: (i, 0)) for w in weights]
    out_shape = [jax.ShapeDtypeStruct((m, w.shape[1]), dt) for w, dt in zip(weights, dtypes)]
    return pl.pallas_call(
        functools.partial(_proj_kernel, scales=scales), grid=(m // tm,),
        in_specs=[pl.BlockSpec((tm, k), lambda i: (i, 0)), pl.BlockSpec(w_all.shape, lambda i: (0, 0))],
        out_specs=out_specs, out_shape=out_shape,
        compiler_params=_params(("arbitrary",), VMEM_LIMIT), name=name)(x, w_all)


def _projb_kernel(x_ref, wq_ref, wm_ref, wk_ref, wv_ref, q_ref, qm_ref, k_ref, v_ref, km_ref):
    xb = x_ref[...].astype(BF16)
    q_ref[...] = (_dot(xb, wq_ref[...]) * Q_SCALE).astype(BF16)
    qm_ref[...] = (_dot(xb, wm_ref[...]) * Q_SCALE).astype(BF16)
    k = _dot(xb, wk_ref[...])
    k_ref[...] = k.astype(BF16)
    km_ref[...] = jnp.mean(k, axis=0, keepdims=True)
    v_ref[...] = _dot(xb, wv_ref[...]).astype(BF16)


def _proj_b(x, wq, wm, wk, wv):
    m, k = x.shape
    nblk = m // MOBA_BLOCK
    row = lambda i: (i, 0)
    full = lambda i: (0, 0)
    return pl.pallas_call(
        _projb_kernel, grid=(nblk,),
        in_specs=[pl.BlockSpec((MOBA_BLOCK, k), row)] + [pl.BlockSpec(w.shape, full) for w in (wq, wm, wk, wv)],
        out_specs=[pl.BlockSpec((MOBA_BLOCK, MAIN_W), row), pl.BlockSpec((MOBA_BLOCK, MEM_W), row),
                   pl.BlockSpec((MOBA_BLOCK, MAIN_W), row), pl.BlockSpec((MOBA_BLOCK, MAIN_W), row),
                   pl.BlockSpec((None, 1, MAIN_W), lambda i: (i, 0, 0))],
        out_shape=[jax.ShapeDtypeStruct((m, MAIN_W), BF16), jax.ShapeDtypeStruct((m, MEM_W), BF16),
                   jax.ShapeDtypeStruct((m, MAIN_W), BF16), jax.ShapeDtypeStruct((m, MAIN_W), BF16),
                   jax.ShapeDtypeStruct((nblk, 1, MAIN_W), F32)],
        compiler_params=_params(("arbitrary",), VMEM_LIMIT), name="proj_b")(x, wq, wm, wk, wv)


def _bias_kernel(tbl_ref, bb_ref, bc_ref, far_ref):
    h = pl.program_id(0)

    def bias_of(dist):
        n = jnp.maximum(dist, 0)
        max_exact = REL_BUCKETS // 2
        nf = jnp.maximum(n, 1).astype(F32)
        large = max_exact + (jnp.log(nf / max_exact) / math.log(REL_MAX_DIST / max_exact)
                             * (REL_BUCKETS - max_exact)).astype(jnp.int32)
        large = jnp.minimum(large, REL_BUCKETS - 1)
        bucket = jnp.where(n < max_exact, n, large)
        out = jnp.zeros(dist.shape, F32)
        for kk in range(REL_BUCKETS):
            out = jnp.where(bucket == kk, tbl_ref[h * REL_BUCKETS + kk], out)
        return out * LOG2E

    key = lax.broadcasted_iota(jnp.int32, (TQ, TQ), 0)
    qry = lax.broadcasted_iota(jnp.int32, (TQ, TQ), 1)
    d0 = qry - key
    bb_ref[0] = jnp.where(d0 >= 0, bias_of(d0), NEG_INF)
    bb_ref[1] = bias_of(d0 + TQ)
    far = bias_of(d0 + 2 * TQ)
    bb_ref[2] = far
    bb_ref[3] = jnp.where(d0 + 2 * TQ < WINDOW, far, NEG_INF)
    far_ref[...] = far[0:1, :]

    nc = lax.broadcasted_iota(jnp.int32, (LANES, TQ), 0)
    qc = lax.broadcasted_iota(jnp.int32, (LANES, TQ), 1)
    n_cmp = (2048 - CMP_LEN) // CMP_STRIDE + 1
    for i in range(bc_ref.shape[0]):
        dc = i * TQ + qc - (nc * CMP_STRIDE + CMP_LEN - 1)
        bc_ref[i] = jnp.where((dc >= 0) & (nc < n_cmp), bias_of(dc), NEG_INF)


def _bias_tables(rel_bias, n_qt):
    tbl = rel_bias.T.reshape(-1)
    return pl.pallas_call(
        _bias_kernel, grid=(N_HEADS,),
        in_specs=[pl.BlockSpec(memory_space=pltpu.SMEM)],
        out_specs=[pl.BlockSpec((None, 4, TQ, TQ), lambda h: (h, 0, 0, 0)),
                   pl.BlockSpec((n_qt, None, LANES, TQ), lambda h: (0, h, 0, 0)),
                   pl.BlockSpec((None, 1, TQ), lambda h: (h, 0, 0))],
        out_shape=[jax.ShapeDtypeStruct((N_HEADS, 4, TQ, TQ), F32),
                   jax.ShapeDtypeStruct((n_qt, N_HEADS, LANES, TQ), F32),
                   jax.ShapeDtypeStruct((N_HEADS, 1, TQ), F32)],
        compiler_params=_params(("arbitrary",)), name="bias_tables")(tbl)


def _cmp_kernel(kv_ref, pe_ref, w1_ref, w2_ref, o_ref):
    x = kv_ref[...].astype(F32)
    lo = _dot((x + pe_ref[0:1, :]).astype(BF16), w1_ref[0])
    hi = _dot((x + pe_ref[1:2, :]).astype(BF16), w1_ref[1])
    nrow = x.shape[0]
    hid = lo + pltpu.roll(hi, nrow - 1, 0)
    o_ref[...] = _dot(jax.nn.gelu(hid).astype(BF16), w2_ref[...])


def _compress(kvr, pe, w1, w2):
    b, nrow, width = kvr.shape
    return pl.pallas_call(
        _cmp_kernel, grid=(b,),
        in_specs=[pl.BlockSpec((None, nrow, width), lambda i: (i, 0, 0)),
                  pl.BlockSpec(pe.shape, lambda i: (0, 0)),
                  pl.BlockSpec(w1.shape, lambda i: (0, 0, 0)),
                  pl.BlockSpec(w2.shape, lambda i: (0, 0))],
        out_specs=pl.BlockSpec((None, nrow, w2.shape[1]), lambda i: (i, 0, 0)),
        out_shape=jax.ShapeDtypeStruct((b, nrow, w2.shape[1]), F32),
        compiler_params=_params(("arbitrary",), VMEM_LIMIT), name="nsa_compress")(kvr, pe, w1, w2)


def _softmax_tile(state, c, s_t, v_t, vis=None, cbias=None):
    m_ref, acc_ref = state
    m = m_ref[c]
    n_blk = 1 if vis is None else vis.shape[0]
    rows = s_t.shape[0] // n_blk
    parts = [s_t[b * rows:(b + 1) * rows] for b in range(n_blk)]
    tile_max = None
    for b in range(n_blk):
        mb = jnp.max(parts[b], axis=0, keepdims=True)
        if vis is not None:
            mb = jnp.where(vis[b:b + 1] > 0.5, mb, NEG_INF)
        tile_max = mb if tile_max is None else jnp.maximum(tile_max, mb)
    if cbias is not None:
        tile_max = tile_max + cbias
    m_new = jnp.maximum(m, tile_max)
    shift = m_new if cbias is None else m_new - cbias
    p_parts = []
    for b in range(n_blk):
        off = shift if vis is None else jnp.where(vis[b:b + 1] > 0.5, shift, -NEG_INF)
        p_parts.append(jnp.exp2(parts[b] - off))
    p_t = p_parts[0] if n_blk == 1 else jnp.concatenate(p_parts, axis=0)
    alpha = jnp.exp2(m - m_new)
    acc_ref[c] = alpha * acc_ref[c] + _dot(v_t, p_t.astype(BF16))
    m_ref[c] = m_new


def _run_chains(chains, logits_fn, update_fn):
    pending = {}
    for idx, c in enumerate(chains[:QK_LOOKAHEAD]):
        pending[c] = logits_fn(c, idx)
    for idx, c in enumerate(chains):
        update_fn(c, pending.pop(c), idx % QK_LOOKAHEAD)
        if idx + QK_LOOKAHEAD < len(chains):
            nxt = chains[idx + QK_LOOKAHEAD]
            pending[nxt] = logits_fn(nxt, idx % QK_LOOKAHEAD)


def _stage_logits(s_ref, slot, s_t):
    s_ref[slot] = s_t


def _init_state(state):
    m_ref, acc_ref = state
    m_ref[...] = jnp.full(m_ref.shape, NEG_INF, F32)
    acc_ref[...] = jnp.zeros(acc_ref.shape, F32)


def _chain_out(state, c):
    _, acc_ref = state
    return acc_ref[c, 0:HEAD_DIM, :] / jnp.maximum(acc_ref[c, HEAD_DIM:HEAD_DIM + 1, :], TINY)


def _values_with_ones(tile, row0):
    v_t = tile.astype(F32).T[row0:row0 + HEAD_DIM]
    pad = lax.broadcasted_iota(jnp.int32, (V_ROWS - HEAD_DIM, tile.shape[0]), 0)
    return jnp.concatenate([v_t, jnp.where(pad == 0, 1.0, 0.0)], axis=0).astype(BF16)


def _rows(ref, kt, cols=slice(None)):
    return ref[pl.ds(pl.multiple_of(kt * TQ, TQ), TQ), cols]


def _store_head_queries(q_ref, qh_ref, n_pairs):
    row = lax.broadcasted_iota(jnp.int32, (LANES, TQ), 0)
    halves = (row < HEAD_DIM, row >= HEAD_DIM)
    for pp in range(n_pairs):
        q2t = q_ref[:, pp * LANES:(pp + 1) * LANES].astype(F32).T
        for j in range(2):
            qh_ref[2 * pp + j] = jnp.where(halves[j], q2t, 0.0).astype(BF16)


def _nsa_kernel(q_ref, g_ref, kvc_ref, ks_ref, vs_ref, kw_ref, vw_ref, bb_ref, far_ref, bc_ref, ovl_ref,
                o_ref, vst_ref, vwt_ref, qh_ref, gs_ref, oc_ref, sel_ref, m_ref, acc_ref, s_ref):
    i = pl.program_id(1)
    n_kt = vst_ref.shape[0]
    state = (m_ref, acc_ref)

    @pl.when(i == 0)
    def _values():
        for kt in range(n_kt):
            rows = slice(kt * TQ, (kt + 1) * TQ)
            vst_ref[kt] = _values_with_ones(vs_ref[rows, :], 0)
            vwt_ref[kt] = _values_with_ones(vw_ref[rows, :], 0)

    _store_head_queries(q_ref, qh_ref, N_PAIRS)
    gs_ref[...] = jax.nn.sigmoid(g_ref[...]).T

    kc2 = kvc_ref[:, 0:LANES].astype(BF16)
    vc_t = kvc_ref[:, LANES:2 * LANES].T[0:HEAD_DIM].astype(BF16)
    psums = []

    def cmp_logits(h, slot):
        s_ref[slot, 0:LANES, :] = _dot(kc2, qh_ref[h])

    def cmp_update(h, _s, slot):
        bias = bc_ref[h]
        s = s_ref[slot, 0:LANES, :] + bias
        valid = bias > 0.5 * NEG_INF
        m = jnp.max(s, axis=0, keepdims=True)
        e = jnp.where(valid, jnp.exp2(s - m), 0.0)
        pc = e / jnp.maximum(jnp.sum(e, axis=0, keepdims=True), TINY)
        psums[:] = [pc if not psums else psums[0] + pc]
        oc_ref[h] = gs_ref[3 * h:3 * h + 1, :] * _dot(vc_t, pc.astype(BF16))

    _run_chains(list(range(N_HEADS)), cmp_logits, cmp_update)
    psum = psums[0]
    p_hi = psum.astype(BF16)
    p_lo = (psum - p_hi.astype(F32)).astype(BF16)
    imp = _dot(ovl_ref[...], p_hi) + _dot(ovl_ref[...], p_lo)
    per_tile = TQ // SLC_BLOCK
    n_slc = n_kt * per_tile
    imp = imp[0:n_slc, :]
    sidx = lax.broadcasted_iota(jnp.int32, (n_slc, TQ), 0)
    t = i * TQ + lax.broadcasted_iota(jnp.int32, (n_slc, TQ), 1)
    cur = lax.shift_right_logical(t, 6)
    eligible = sidx <= cur
    forced = (sidx == 0) | (sidx == cur) | (sidx == cur - 1)
    score = jnp.where(eligible, jnp.where(forced, FORCE_SCORE, 0.0), NEG_INF) + imp
    cnt = jnp.zeros((n_slc, TQ), jnp.int32)
    for sp in range(n_slc):
        row = score[sp:sp + 1, :]
        better = (row > score) | ((row == score) & (sp < sidx))
        cnt = cnt + better.astype(jnp.int32)
    chosen = jnp.where(cnt < min(SLC_TOPK, n_slc), 1.0, 0.0)
    for kt in range(n_kt):
        sel_ref[kt, 0:per_tile, :] = chosen[kt * per_tile:(kt + 1) * per_tile, :]

    _init_state(state)
    slc = [("s", h) for h in range(N_HEADS)]
    win = [("w", h) for h in range(N_HEADS)]
    n_near = jnp.minimum(i, WINDOW // TQ) + 1

    def near_body(jj, _):
        kt = i - jj
        k_s, k_w, v_s, v_w = _rows(ks_ref, kt), _rows(kw_ref, kt), vst_ref[kt], vwt_ref[kt]
        vis = sel_ref[kt, 0:per_tile, :]
        b_s = jnp.minimum(jj, 2)
        b_w = jnp.where(jj == 2, 3, jj)

        def logits(c, slot):
            _stage_logits(s_ref, slot, _dot(k_s if c[0] == "s" else k_w, qh_ref[c[1]]))

        def update(c, _s, slot):
            kind, h = c
            if kind == "s":
                _softmax_tile(state, h, s_ref[slot] + bb_ref[h, b_s], v_s, vis=vis)
            else:
                _softmax_tile(state, N_HEADS + h, s_ref[slot] + bb_ref[h, b_w], v_w)

        _run_chains([c for pair in zip(slc, win) for c in pair], logits, update)
        return 0

    lax.fori_loop(0, n_near, near_body, 0)

    def far_body(jj, _):
        kt = i - jj
        k_s, v_s = _rows(ks_ref, kt), vst_ref[kt]
        vis = sel_ref[kt, 0:per_tile, :]
        _run_chains(slc, lambda c, slot: _dot(k_s, qh_ref[c[1]]),
                    lambda c, s, slot: _softmax_tile(state, c[1], s, v_s, vis=vis, cbias=far_ref[c[1]]))
        return 0

    lax.fori_loop(n_near, i + 1, far_body, 0)

    for pp in range(N_PAIRS):
        hs = (2 * pp, 2 * pp + 1)
        outs = []
        for h in hs:
            g1 = gs_ref[3 * h + 1:3 * h + 2, :]
            g2 = gs_ref[3 * h + 2:3 * h + 3, :]
            outs.append(oc_ref[h] + g1 * _chain_out(state, h) + g2 * _chain_out(state, N_HEADS + h))
        o_ref[:, pp * LANES:(pp + 1) * LANES] = jnp.concatenate(outs, axis=0).T.astype(o_ref.dtype)


def _nsa_attention(q, g, kvc, ks2, vs2, kw2, vw2, bb, far, bc, ovl_t):
    b, s, _ = q.shape
    n_qt = s // TQ
    per_b = lambda bi, i: (bi, 0, 0)
    return pl.pallas_call(
        _nsa_kernel, grid=(b, n_qt),
        in_specs=[pl.BlockSpec((None, TQ, MAIN_W), lambda bi, i: (bi, i, 0)),
                  pl.BlockSpec((None, TQ, LANES), lambda bi, i: (bi, i, 0)),
                  pl.BlockSpec((None,) + kvc.shape[1:], per_b),
                  pl.BlockSpec((None, s, LANES), per_b), pl.BlockSpec((None, s, LANES), per_b),
                  pl.BlockSpec((None, s, LANES), per_b), pl.BlockSpec((None, s, LANES), per_b),
                  pl.BlockSpec(bb.shape, lambda bi, i: (0, 0, 0, 0)),
                  pl.BlockSpec(far.shape, lambda bi, i: (0, 0, 0)),
                  pl.BlockSpec((None,) + bc.shape[1:], lambda bi, i: (i, 0, 0, 0)),
                  pl.BlockSpec(ovl_t.shape, lambda bi, i: (0, 0))],
        out_specs=pl.BlockSpec((None, TQ, MAIN_W), lambda bi, i: (bi, i, 0)),
        out_shape=jax.ShapeDtypeStruct((b, s, MAIN_W), BF16),
        scratch_shapes=[pltpu.VMEM((n_qt, V_ROWS, TQ), BF16), pltpu.VMEM((n_qt, V_ROWS, TQ), BF16),
                        pltpu.VMEM((N_HEADS, LANES, TQ), BF16), pltpu.VMEM((LANES, TQ), F32),
                        pltpu.VMEM((N_HEADS, HEAD_DIM, TQ), F32), pltpu.VMEM((n_qt, 8, TQ), F32),
                        pltpu.VMEM((2 * N_HEADS, 1, TQ), F32),
                        pltpu.VMEM((2 * N_HEADS, V_ROWS, TQ), F32), pltpu.VMEM((QK_LOOKAHEAD, TQ, TQ), F32)],
        compiler_params=_params(("arbitrary", "arbitrary"), VMEM_LIMIT),
        name="nsa_attention")(q, g, kvc, ks2, vs2, kw2, vw2, bb, far, bc, ovl_t)


def _moba_kernel(q_ref, k_ref, v_ref, km_ref, bb_ref, far_ref, o_ref, vt_ref, qh_ref, sel_ref,
                 m_ref, acc_ref, s_ref):
    i = pl.program_id(1)
    n_blk = vt_ref.shape[1]
    state = (m_ref, acc_ref)
    heads = list(range(N_HEADS))

    @pl.when(i == 0)
    def _values():
        for pp in range(N_PAIRS):
            for kt in range(n_blk):
                tile = v_ref[kt * TQ:(kt + 1) * TQ, pp * LANES:(pp + 1) * LANES]
                for j in range(2):
                    vt_ref[2 * pp + j, kt] = _values_with_ones(tile, j * HEAD_DIM)

    _store_head_queries(q_ref, qh_ref, N_PAIRS)
    blk = lax.broadcasted_iota(jnp.int32, (n_blk, TQ), 0)
    eligible = blk < i
    for pp in range(N_PAIRS):
        km = km_ref[:, pp * LANES:(pp + 1) * LANES].astype(BF16)
        for j in range(2):
            gate = _dot(km, qh_ref[2 * pp + j])[0:n_blk, :]
            gate = jnp.where(eligible, gate, NEG_INF)
            cnt = jnp.zeros((n_blk, TQ), jnp.int32)
            for n in range(n_blk):
                row = gate[n:n + 1, :]
                better = (row > gate) | ((row == gate) & (n < blk))
                cnt = cnt + better.astype(jnp.int32)
            chosen = (cnt < min(MOBA_TOPK, n_blk - 1)) & eligible
            vis = jnp.where(chosen | (blk == i), 1.0, 0.0)
            for n in range(n_blk):
                sel_ref[2 * pp + j, n] = vis[n:n + 1, :]
    _init_state(state)

    def tile_inputs(kt, h):
        pp = h // 2
        k = _rows(k_ref, kt, slice(pp * LANES, (pp + 1) * LANES))
        return k, vt_ref[h, kt], sel_ref[h, kt]

    def near_tile(jj):
        kt = i - jj
        def update(h, _s, slot):
            _, v_t, vis = tile_inputs(kt, h)
            _softmax_tile(state, h, s_ref[slot] + bb_ref[h, jj], v_t, vis=vis)

        def logits(h, slot):
            _stage_logits(s_ref, slot, _dot(tile_inputs(kt, h)[0], qh_ref[h]))

        _run_chains(heads, logits, update)

    near_tile(0)
    n_near = jnp.minimum(i, 1) + 1

    @pl.when(i >= 1)
    def _():
        near_tile(1)

    def far_body(jj, _):
        kt = i - jj
        def update(h, s, slot):
            _, v_t, vis = tile_inputs(kt, h)
            _softmax_tile(state, h, s, v_t, vis=vis, cbias=far_ref[h])
        _run_chains(heads, lambda h, slot: _dot(tile_inputs(kt, h)[0], qh_ref[h]), update)
        return 0

    lax.fori_loop(n_near, i + 1, far_body, 0)

    for pp in range(N_PAIRS):
        out_t = jnp.concatenate([_chain_out(state, 2 * pp), _chain_out(state, 2 * pp + 1)], axis=0)
        o_ref[:, pp * LANES:(pp + 1) * LANES] = out_t.T.astype(o_ref.dtype)


def _moba_attention(q, k, v, kmean, bb, far):
    b, s, _ = q.shape
    n_qt = s // TQ
    per_b = lambda bi, i: (bi, 0, 0)
    return pl.pallas_call(
        _moba_kernel, grid=(b, n_qt),
        in_specs=[pl.BlockSpec((None, TQ, MAIN_W), lambda bi, i: (bi, i, 0)),
                  pl.BlockSpec((None, s, MAIN_W), per_b), pl.BlockSpec((None, s, MAIN_W), per_b),
                  pl.BlockSpec((None, LANES, MAIN_W), per_b),
                  pl.BlockSpec((N_HEADS, 2, TQ, TQ), lambda bi, i: (0, 0, 0, 0)),
                  pl.BlockSpec(far.shape, lambda bi, i: (0, 0, 0))],
        out_specs=pl.BlockSpec((None, TQ, MAIN_W), lambda bi, i: (bi, i, 0)),
        out_shape=jax.ShapeDtypeStruct((b, s, MAIN_W), BF16),
        scratch_shapes=[pltpu.VMEM((N_HEADS, n_qt, V_ROWS, TQ), BF16), pltpu.VMEM((N_HEADS, LANES, TQ), BF16),
                        pltpu.VMEM((N_HEADS, n_qt, 1, TQ), F32),
                        pltpu.VMEM((N_HEADS, 1, TQ), F32),
                        pltpu.VMEM((N_HEADS, V_ROWS, TQ), F32), pltpu.VMEM((QK_LOOKAHEAD, TQ, TQ), F32)],
        compiler_params=_params(("arbitrary", "arbitrary"), VMEM_LIMIT),
        name="moba_attention")(q, k, v, kmean, bb, far)


def _mem_kernel(q_ref, kv_ref, o_ref, qh_ref):
    n_sub = q_ref.shape[0] // TQ
    for sub in range(n_sub):
        _store_head_queries(q_ref.at[sub * TQ:(sub + 1) * TQ, :], qh_ref.at[sub], MEM_HEADS // 2)
    outs = {}

    def logits(c, slot):
        sub, h = c
        return _dot(kv_ref[:, (h // 2) * LANES:(h // 2 + 1) * LANES], qh_ref[sub, h])

    def update(c, s, slot):
        h = c[1]
        lanes = slice(MEM_W + (h // 2) * LANES, MEM_W + (h // 2 + 1) * LANES)
        v_t = kv_ref[:, lanes].astype(F32).T[(h % 2) * HEAD_DIM:(h % 2 + 1) * HEAD_DIM].astype(BF16)
        e = jnp.exp2(s - jnp.max(s, axis=0, keepdims=True))
        pr = e / jnp.sum(e, axis=0, keepdims=True)
        outs[c] = _dot(v_t, pr.astype(BF16))

    _run_chains([(sub, h) for sub in range(n_sub) for h in range(MEM_HEADS)], logits, update)
    for sub in range(n_sub):
        for pp in range(MEM_HEADS // 2):
            pair = jnp.concatenate([outs[(sub, 2 * pp)], outs[(sub, 2 * pp + 1)]], axis=0)
            o_ref[sub * TQ:(sub + 1) * TQ, pp * LANES:(pp + 1) * LANES] = pair.T.astype(o_ref.dtype)


MEM_TILES = 2


def _mem_attention(qm, mem_kv):
    b, s, _ = qm.shape
    n_mem = mem_kv.shape[1]
    rows = MEM_TILES * TQ
    return pl.pallas_call(
        _mem_kernel, grid=(b, s // rows),
        in_specs=[pl.BlockSpec((None, rows, MEM_W), lambda bi, i: (bi, i, 0)),
                  pl.BlockSpec((None, n_mem, 2 * MEM_W), lambda bi, i: (bi, 0, 0))],
        out_specs=pl.BlockSpec((None, rows, MEM_W), lambda bi, i: (bi, i, 0)),
        out_shape=jax.ShapeDtypeStruct((b, s, MEM_W), BF16),
        scratch_shapes=[pltpu.VMEM((MEM_TILES, MEM_HEADS, LANES, TQ), BF16)],
        compiler_params=_params(("arbitrary", "arbitrary")), name="mem_attention")(qm, mem_kv)


def _layer_norm(y, g, b):
    mu = jnp.mean(y, axis=-1, keepdims=True)
    var = jnp.mean(jnp.square(y - mu), axis=-1, keepdims=True)
    return (y - mu) * lax.rsqrt(var + LN_EPS) * g + b


def _mm_ln_kernel(*refs, n_in, emit_bf16):
    a_refs, w_refs = refs[:n_in], refs[n_in:2 * n_in]
    x_ref, g_ref, b_ref = refs[2 * n_in:2 * n_in + 3]
    outs = refs[2 * n_in + 3:-1]
    y_ref = refs[-1]

    @pl.when(pl.program_id(0) == 0)
    def _():
        y_ref[...] = jnp.zeros(y_ref.shape, F32)

    y = _layer_norm(ALPHA * x_ref[...] + y_ref[...], g_ref[...], b_ref[...])
    outs[0][...] = y
    if emit_bf16:
        outs[1][...] = y.astype(BF16)
    acc = _dot(a_refs[0][...], w_refs[0][...])
    for a_ref, w_ref in zip(a_refs[1:], w_refs[1:]):
        acc = acc + _dot(a_ref[...], w_ref[...])
    y_ref[...] = acc


def _mm_ln(acts, weights, x, g, b, tm, name, emit_bf16=False):
    m, d = x.shape
    n_t = m // tm
    cur = lambda t: (jnp.minimum(t, n_t - 1), 0)
    prev = lambda t: (jnp.maximum(t - 1, 0), 0)
    full = lambda t: (0, 0)
    out_shape = [jax.ShapeDtypeStruct((m, d), F32)] + ([jax.ShapeDtypeStruct((m, d), BF16)] if emit_bf16 else [])
    return pl.pallas_call(
        functools.partial(_mm_ln_kernel, n_in=len(acts), emit_bf16=emit_bf16), grid=(n_t + 1,),
        in_specs=[pl.BlockSpec((tm, a.shape[1]), cur) for a in acts]
                 + [pl.BlockSpec(w.shape, full) for w in weights]
                 + [pl.BlockSpec((tm, d), prev), pl.BlockSpec((1, d), full), pl.BlockSpec((1, d), full)],
        out_specs=[pl.BlockSpec((tm, d), prev) for _ in out_shape],
        out_shape=out_shape,
        scratch_shapes=[pltpu.VMEM((tm, d), F32)],
        compiler_params=_params(("arbitrary",), VMEM_LIMIT), name=name)(*acts, *weights, x, g, b)


FF_CHUNK = 256
FF_ROWS = 256
SUBLANES = 8


def _gelu_tanh(x):
    k = -2.0 * math.sqrt(2.0 / math.pi) * LOG2E
    w = x * (x * x * (0.044715 * k) + k)
    return x / (1.0 + jnp.exp2(w))


def _ffn_kernel(x_ref, wa_ref, wb_ref, cw_ref, cb_ref, wo_ref, g_ref, b_ref, o_ref, xb_ref, hid_ref, a_ref):
    c = pl.program_id(1)
    last = pl.num_programs(1) - 1

    @pl.when(c == 0)
    def _():
        xb_ref[...] = x_ref[...].astype(BF16)
        hid_ref[1] = jnp.zeros(hid_ref.shape[1:], BF16)
        o_ref[...] = jnp.zeros(o_ref.shape, F32)
        a_ref[0:SUBLANES, :] = jnp.zeros((SUBLANES, a_ref.shape[1]), F32)

    xb = xb_ref[...]
    a = _dot(xb, wa_ref[...])
    o_ref[...] += _dot(hid_ref[(c + 1) % 2], wo_ref[...])
    gate = _dot(xb, wb_ref[...])
    n_rows = a.shape[0]
    a_ref[SUBLANES:, :] = a
    a1 = a_ref[SUBLANES - 1:SUBLANES - 1 + n_rows, :]
    a2 = a_ref[SUBLANES - 2:SUBLANES - 2 + n_rows, :]
    conv = cw_ref[0:1, :] * a2 + cw_ref[1:2, :] * a1 + cw_ref[2:3, :] * a + cb_ref[...]
    hid_ref[c % 2] = (_gelu_tanh(conv) * gate).astype(BF16)

    @pl.when(c == last)
    def _():
        o_ref[...] = _layer_norm(ALPHA * x_ref[...] + o_ref[...], g_ref[...], b_ref[...])


def _ffn_ln(x, w_in, conv_w, conv_b, w_out, g, b):
    bsz, s, d = x.shape
    n_c = D_FF // FF_CHUNK
    cur = lambda c: jnp.minimum(c, n_c - 1)
    prev = lambda c: jnp.maximum(c - 1, 0)
    return pl.pallas_call(
        _ffn_kernel, grid=(bsz, n_c + 1),
        in_specs=[pl.BlockSpec((None, s, d), lambda bi, c: (bi, 0, 0)),
                  pl.BlockSpec((d, FF_CHUNK), lambda bi, c: (0, cur(c))),
                  pl.BlockSpec((d, FF_CHUNK), lambda bi, c: (0, n_c + cur(c))),
                  pl.BlockSpec((CONV_W, FF_CHUNK), lambda bi, c: (0, cur(c))),
                  pl.BlockSpec((1, FF_CHUNK), lambda bi, c: (0, cur(c))),
                  pl.BlockSpec((FF_CHUNK, d), lambda bi, c: (prev(c), 0)),
                  pl.BlockSpec((1, d), lambda bi, c: (0, 0)), pl.BlockSpec((1, d), lambda bi, c: (0, 0))],
        out_specs=pl.BlockSpec((None, s, d), lambda bi, c: (bi, 0, 0)),
        out_shape=jax.ShapeDtypeStruct((bsz, s, d), F32),
        scratch_shapes=[pltpu.VMEM((s, d), BF16), pltpu.VMEM((2, s, FF_CHUNK), BF16),
                        pltpu.VMEM((SUBLANES + s, FF_CHUNK), F32)],
        compiler_params=_params(("arbitrary", "arbitrary"), VMEM_LIMIT),
        name="ffn_ln")(x, w_in, w_in, conv_w, conv_b, w_out, g, b)


def _ffn_hidden_kernel(xb_ref, wa_ref, wb_ref, cw_ref, cb_ref, h_ref):
    tail = jnp.zeros((SUBLANES, wa_ref.shape[1]), F32)
    for r0 in range(0, xb_ref.shape[0], FF_ROWS):
        rows = slice(r0, r0 + FF_ROWS)
        xb = xb_ref[rows, :]
        a = _dot(xb, wa_ref[...])
        ext = jnp.concatenate([tail, a], axis=0)
        a1 = pltpu.roll(ext, 1, 0)[SUBLANES:]
        a2 = pltpu.roll(ext, 2, 0)[SUBLANES:]
        act = jax.nn.gelu(cw_ref[0:1, :] * a2 + cw_ref[1:2, :] * a1 + cw_ref[2:3, :] * a + cb_ref[...])
        h_ref[rows, :] = (act * _dot(xb, wb_ref[...])).astype(BF16)
        tail = a[FF_ROWS - SUBLANES:]


def _ffn_hidden(xb, w_in, conv_w, conv_b):
    bsz, s, d = xb.shape
    n_c = D_FF // FF_CHUNK
    return pl.pallas_call(
        _ffn_hidden_kernel, grid=(bsz, n_c),
        in_specs=[pl.BlockSpec((None, s, d), lambda bi, c: (bi, 0, 0)),
                  pl.BlockSpec((d, FF_CHUNK), lambda bi, c: (0, c)),
                  pl.BlockSpec((d, FF_CHUNK), lambda bi, c: (0, n_c + c)),
                  pl.BlockSpec((CONV_W, FF_CHUNK), lambda bi, c: (0, c)),
                  pl.BlockSpec((1, FF_CHUNK), lambda bi, c: (0, c))],
        out_specs=pl.BlockSpec((None, s, FF_CHUNK), lambda bi, c: (bi, 0, c)),
        out_shape=jax.ShapeDtypeStruct((bsz, s, D_FF), BF16),
        compiler_params=_params(("arbitrary", "arbitrary"), VMEM_LIMIT),
        name="ffn_hidden")(xb, w_in, w_in, conv_w, conv_b)


def _dup(w):
    return jnp.concatenate([w, w], axis=1)


def _cmp_weights(pe_k, w1_k, w2_k, pe_v, w1_v, w2_v):
    half = CMP_LEN // 2
    zk = jnp.zeros((half, HEAD_DIM, CMP_HIDDEN), F32)

    def w1_half(sl):
        wk = w1_k.reshape(CMP_LEN, HEAD_DIM, CMP_HIDDEN)[sl]
        wv = w1_v.reshape(CMP_LEN, HEAD_DIM, CMP_HIDDEN)[sl]
        top = jnp.concatenate([wk, zk], axis=2)
        bot = jnp.concatenate([zk, wv], axis=2)
        return jnp.concatenate([top, bot], axis=1).reshape(half * 2 * HEAD_DIM, 2 * CMP_HIDDEN)

    w1 = jnp.stack([w1_half(slice(0, half)), w1_half(slice(half, CMP_LEN))]).astype(BF16)
    pe = jnp.concatenate([pe_k, pe_v], axis=1)
    pe = jnp.stack([pe[:half].reshape(-1), pe[half:].reshape(-1)])
    zo = jnp.zeros((CMP_HIDDEN, 2 * HEAD_DIM), F32)
    w2 = jnp.concatenate([jnp.concatenate([_dup(w2_k), zo], axis=1),
                          jnp.concatenate([zo, _dup(w2_v)], axis=1)], axis=0).astype(BF16)
    return pe, w1, w2


def _overlap_matrix(seq):
    n_cmp = (seq - CMP_LEN) // CMP_STRIDE + 1
    n_slc = seq // SLC_BLOCK
    start = np.arange(LANES) * CMP_STRIDE
    bs = np.arange(LANES) * SLC_BLOCK
    ovl_t = ((start[None, :] < bs[:, None] + SLC_BLOCK) & (start[None, :] + CMP_LEN > bs[:, None])
             & (np.arange(LANES)[None, :] < n_cmp) & (np.arange(LANES)[:, None] < n_slc))
    return jnp.asarray(ovl_t, BF16)


def kernel(x, mem, rel_bias, a_w_in, a_cmp_pe_k, a_cmp_w1_k, a_cmp_w2_k, a_cmp_pe_v, a_cmp_w1_v, a_cmp_w2_v,
           a_w_mem_kv, a_w_out, shared_w_kv, b_w_in, b_w_mem_kv, b_w_out, ln1_g, ln1_b, ln2_g, ln2_b,
           ffn_w_in, ffn_conv_w, ffn_conv_b, ffn_w_out):
    bsz, seq, d = x.shape
    n_mem = mem.shape[1]
    m = bsz * seq
    assert (seq, d) == (2048, D_MODEL) and seq % TQ == 0
    n_qt = seq // TQ

    bb, bc, far = _bias_tables(rel_bias, n_qt)
    ovl_t = _overlap_matrix(seq)
    memf = mem.reshape(bsz * n_mem, d)
    xf = x.reshape(m, d)

    def ffn(xcur, layer):
        return _ffn_ln(xcur.reshape(bsz, seq, d), ffn_w_in[layer].astype(BF16), ffn_conv_w[layer],
                       ffn_conv_b[layer][None, :], ffn_w_out[layer].astype(BF16),
                       ln2_g[layer][None, :], ln2_b[layer][None, :]).reshape(m, d)

    def out_ln(o_main, o_mem, w_out, xcur, layer):
        w = w_out.astype(BF16)
        (out,) = _mm_ln([o_main.reshape(m, MAIN_W), o_mem.reshape(m, MEM_W)], [w[:MAIN_W], w[MAIN_W:]], xcur,
                        ln1_g[layer][None, :], ln1_b[layer][None, :], 512, "outproj_ln")
        return out

    w = a_w_in[0]
    c0 = MAIN_W
    cols = [w[:, c0 + kk * HEAD_DIM:c0 + (kk + 1) * HEAD_DIM] for kk in range(6)]
    c_g = c0 + 6 * HEAD_DIM
    w_g = jnp.pad(w[:, c_g:c_g + 3 * N_HEADS], ((0, 0), (0, LANES - 3 * N_HEADS)))
    w_qm = w[:, c_g + 3 * N_HEADS:]
    weights = [w[:, :MAIN_W], jnp.concatenate(cols[0:2], axis=1), _dup(cols[2]), _dup(cols[3]),
               _dup(cols[4]), _dup(cols[5]), w_g, w_qm]
    weights = [wi.astype(BF16) for wi in weights]
    q, kvc_tok, ks2, vs2, kw2, vw2, g, qm = _proj(
        xf, weights, [BF16, BF16, BF16, BF16, BF16, BF16, F32, BF16], 512, "proj_a",
        scales=[Q_SCALE, None, None, None, None, None, None, Q_SCALE])

    pe, w1, w2 = _cmp_weights(a_cmp_pe_k[0], a_cmp_w1_k[0], a_cmp_w2_k[0],
                              a_cmp_pe_v[0], a_cmp_w1_v[0], a_cmp_w2_v[0])
    kvc = _compress(kvc_tok.reshape(bsz, seq // (CMP_LEN // 2), (CMP_LEN // 2) * LANES), pe, w1, w2)

    r3 = lambda t: t.reshape(bsz, seq, t.shape[-1])
    o_main = _nsa_attention(r3(q), r3(g), kvc, r3(ks2), r3(vs2), r3(kw2), r3(vw2), bb, far, bc, ovl_t)
    (mkv,) = _proj(memf, [a_w_mem_kv[0].astype(BF16)], [BF16], 512, "proj_mem_a")
    o_mem = _mem_attention(r3(qm), mkv.reshape(bsz, n_mem, 2 * MEM_W))
    x1 = out_ln(o_main, o_mem, a_w_out[0], xf, 0)
    x1 = ffn(x1, 0)

    wb = b_w_in[0].astype(BF16)
    wkv = shared_w_kv.astype(BF16)
    q, qm, k, v, kmean = _proj_b(x1, wb[:, :MAIN_W], wb[:, MAIN_W:], wkv[:, :MAIN_W], wkv[:, MAIN_W:])
    n_blk = seq // MOBA_BLOCK
    kmean = jnp.pad(kmean.reshape(bsz, n_blk, MAIN_W), ((0, 0), (0, LANES - n_blk), (0, 0)))
    o_main = _moba_attention(r3(q), r3(k), r3(v), kmean, bb, far)
    (mkv,) = _proj(memf, [b_w_mem_kv[0].astype(BF16)], [BF16], 512, "proj_mem_b")
    o_mem = _mem_attention(r3(qm), mkv.reshape(bsz, n_mem, 2 * MEM_W))
    x2 = out_ln(o_main, o_mem, b_w_out[0], x1, 1)
    x2 = ffn(x2, 1)
    return x2.reshape(bsz, seq, d)
```

```python
import functools
import math

import numpy as np
import jax
import jax.numpy as jnp
from jax import lax
from jax.experimental import pallas as pl
from jax.experimental.pallas import tpu as pltpu

F32 = jnp.float32
BF16 = jnp.bfloat16

D_MODEL = 1024
HEAD_DIM = 64
N_HEADS = 12
N_PAIRS = N_HEADS // 2
MAIN_W = N_HEADS * HEAD_DIM
MEM_HEADS = 4
MEM_W = MEM_HEADS * HEAD_DIM
CMP_LEN = 32
CMP_STRIDE = 16
CMP_HIDDEN = 256
SLC_BLOCK = 64
SLC_TOPK = 16
WINDOW = 512
FORCE_SCORE = 1.0e4
MOBA_BLOCK = 256
MOBA_TOPK = 3
REL_BUCKETS = 32
REL_MAX_DIST = 128
D_FF = 2816
CONV_W = 3
DEPTH = 2
ALPHA = (2.0 * DEPTH) ** 0.25
LN_EPS = 1e-5
NEG_INF = -1e30
TINY = 1e-30
LOG2E = math.log2(math.e)
Q_SCALE = HEAD_DIM ** -0.5 * LOG2E

LANES = 128
TQ = 256
VMEM_LIMIT = 60 * 1024 * 1024
V_ROWS = HEAD_DIM + 8
QK_LOOKAHEAD = 6


def _dot(a, b):
    return jnp.dot(a, b, preferred_element_type=F32)


def _dot_nt(a, b):
    return lax.dot_general(a, b, (((1,), (1,)), ((), ())), preferred_element_type=F32)


def _params(sem, vmem=None):
    return pltpu.CompilerParams(dimension_semantics=sem, vmem_limit_bytes=vmem)


def _proj_kernel(x_ref, w_ref, *o_refs, scales):
    y = _dot(x_ref[...].astype(BF16), w_ref[...])
    col = 0
    for o_ref, scale in zip(o_refs, scales):
        part = y[:, col:col + o_ref.shape[1]]
        o_ref[...] = (part if scale is None else part * scale).astype(o_ref.dtype)
        col += o_ref.shape[1]


def _proj(x, weights, dtypes, tm, name, scales=None):
    m, k = x.shape
    scales = tuple(scales) if scales is not None else (None,) * len(weights)
    assert all(w.shape[1] % LANES == 0 for w in weights)
    w_all = jnp.concatenate(weights, axis=1) if len(weights) > 1 else weights[0]
    out_specs = [pl.BlockSpec((tm, w.shape[1]), lambda i: (i, 0)) for w in weights]
    out_shape = [jax.ShapeDtypeStruct((m, w.shape[1]), dt) for w, dt in zip(weights, dtypes)]
    return pl.pallas_call(
        functools.partial(_proj_kernel, scales=scales), grid=(m // tm,),
        in_specs=[pl.BlockSpec((tm, k), lambda i: (i, 0)), pl.BlockSpec(w_all.shape, lambda i: (0, 0))],
        out_specs=out_specs, out_shape=out_shape,
        compiler_params=_params(("arbitrary",), VMEM_LIMIT), name=name)(x, w_all)


def _projb_kernel(x_ref, wq_ref, wm_ref, wk_ref, wv_ref, q_ref, qm_ref, k_ref, v_ref, km_ref):
    xb = x_ref[...].astype(BF16)
    q_ref[...] = (_dot(xb, wq_ref[...]) * Q_SCALE).astype(BF16)
    qm_ref[...] = (_dot(xb, wm_ref[...]) * Q_SCALE).astype(BF16)
    k = _dot(xb, wk_ref[...])
    k_ref[...] = k.astype(BF16)
    km_ref[...] = jnp.mean(k, axis=0, keepdims=True)
    v_ref[...] = _dot(xb, wv_ref[...]).astype(BF16)


def _proj_b(x, wq, wm, wk, wv):
    m, k = x.shape
    nblk = m // MOBA_BLOCK
    row = lambda i: (i, 0)
    full = lambda i: (0, 0)
    return pl.pallas_call(
        _projb_kernel, grid=(nblk,),
        in_specs=[pl.BlockSpec((MOBA_BLOCK, k), row)] + [pl.BlockSpec(w.shape, full) for w in (wq, wm, wk, wv)],
        out_specs=[pl.BlockSpec((MOBA_BLOCK, MAIN_W), row), pl.BlockSpec((MOBA_BLOCK, MEM_W), row),
                   pl.BlockSpec((MOBA_BLOCK, MAIN_W), row), pl.BlockSpec((MOBA_BLOCK, MAIN_W), row),
                   pl.BlockSpec((None, 1, MAIN_W), lambda i: (i, 0, 0))],
        out_shape=[jax.ShapeDtypeStruct((m, MAIN_W), BF16), jax.ShapeDtypeStruct((m, MEM_W), BF16),
                   jax.ShapeDtypeStruct((m, MAIN_W), BF16), jax.ShapeDtypeStruct((m, MAIN_W), BF16),
                   jax.ShapeDtypeStruct((nblk, 1, MAIN_W), F32)],
        compiler_params=_params(("arbitrary",), VMEM_LIMIT), name="proj_b")(x, wq, wm, wk, wv)


def _bias_kernel(tbl_ref, bb_ref, bc_ref, far_ref):
    h = pl.program_id(0)

    def bias_of(dist):
        n = jnp.maximum(dist, 0)
        max_exact = REL_BUCKETS // 2
        nf = jnp.maximum(n, 1).astype(F32)
        large = max_exact + (jnp.log(nf / max_exact) / math.log(REL_MAX_DIST / max_exact)
                             * (REL_BUCKETS - max_exact)).astype(jnp.int32)
        large = jnp.minimum(large, REL_BUCKETS - 1)
        bucket = jnp.where(n < max_exact, n, large)
        out = jnp.zeros(dist.shape, F32)
        for kk in range(REL_BUCKETS):
            out = jnp.where(bucket == kk, tbl_ref[h * REL_BUCKETS + kk], out)
        return out * LOG2E

    key = lax.broadcasted_iota(jnp.int32, (TQ, TQ), 0)
    qry = lax.broadcasted_iota(jnp.int32, (TQ, TQ), 1)
    d0 = qry - key
    bb_ref[0] = jnp.where(d0 >= 0, bias_of(d0), NEG_INF)
    bb_ref[1] = bias_of(d0 + TQ)
    far = bias_of(d0 + 2 * TQ)
    bb_ref[2] = far
    bb_ref[3] = jnp.where(d0 + 2 * TQ < WINDOW, far, NEG_INF)
    far_ref[...] = far[0:1, :]

    nc = lax.broadcasted_iota(jnp.int32, (LANES, TQ), 0)
    qc = lax.broadcasted_iota(jnp.int32, (LANES, TQ), 1)
    n_cmp = (2048 - CMP_LEN) // CMP_STRIDE + 1
    for i in range(bc_ref.shape[0]):
        dc = i * TQ + qc - (nc * CMP_STRIDE + CMP_LEN - 1)
        bc_ref[i] = jnp.where((dc >= 0) & (nc < n_cmp), bias_of(dc), NEG_INF)


def _bias_tables(rel_bias, n_qt):
    tbl = rel_bias.T.reshape(-1)
    return pl.pallas_call(
        _bias_kernel, grid=(N_HEADS,),
        in_specs=[pl.BlockSpec(memory_space=pltpu.SMEM)],
        out_specs=[pl.BlockSpec((None, 4, TQ, TQ), lambda h: (h, 0, 0, 0)),
                   pl.BlockSpec((n_qt, None, LANES, TQ), lambda h: (0, h, 0, 0)),
                   pl.BlockSpec((None, 1, TQ), lambda h: (h, 0, 0))],
        out_shape=[jax.ShapeDtypeStruct((N_HEADS, 4, TQ, TQ), F32),
                   jax.ShapeDtypeStruct((n_qt, N_HEADS, LANES, TQ), F32),
                   jax.ShapeDtypeStruct((N_HEADS, 1, TQ), F32)],
        compiler_params=_params(("arbitrary",)), name="bias_tables")(tbl)


def _cmp_kernel(kv_ref, pe_ref, w1_ref, w2_ref, o_ref):
    x = kv_ref[...].astype(F32)
    lo = _dot((x + pe_ref[0:1, :]).astype(BF16), w1_ref[0])
    hi = _dot((x + pe_ref[1:2, :]).astype(BF16), w1_ref[1])
    nrow = x.shape[0]
    hid = lo + pltpu.roll(hi, nrow - 1, 0)
    o_ref[...] = _dot(jax.nn.gelu(hid).astype(BF16), w2_ref[...])


def _compress(kvr, pe, w1, w2):
    b, nrow, width = kvr.shape
    return pl.pallas_call(
        _cmp_kernel, grid=(b,),
        in_specs=[pl.BlockSpec((None, nrow, width), lambda i: (i, 0, 0)),
                  pl.BlockSpec(pe.shape, lambda i: (0, 0)),
                  pl.BlockSpec(w1.shape, lambda i: (0, 0, 0)),
                  pl.BlockSpec(w2.shape, lambda i: (0, 0))],
        out_specs=pl.BlockSpec((None, nrow, w2.shape[1]), lambda i: (i, 0, 0)),
        out_shape=jax.ShapeDtypeStruct((b, nrow, w2.shape[1]), F32),
        compiler_params=_params(("arbitrary",), VMEM_LIMIT), name="nsa_compress")(kvr, pe, w1, w2)


def _softmax_tile(state, c, s_t, v_t, vis=None, cbias=None):
    m_ref, acc_ref = state
    m = m_ref[c]
    n_blk = 1 if vis is None else vis.shape[0]
    rows = s_t.shape[0] // n_blk
    parts = [s_t[b * rows:(b + 1) * rows] for b in range(n_blk)]
    tile_max = None
    for b in range(n_blk):
        mb = jnp.max(parts[b], axis=0, keepdims=True)
        if vis is not None:
            mb = jnp.where(vis[b:b + 1] > 0.5, mb, NEG_INF)
        tile_max = mb if tile_max is None else jnp.maximum(tile_max, mb)
    if cbias is not None:
        tile_max = tile_max + cbias
    m_new = jnp.maximum(m, tile_max)
    shift = m_new if cbias is None else m_new - cbias
    p_parts = []
    for b in range(n_blk):
        off = shift if vis is None else jnp.where(vis[b:b + 1] > 0.5, shift, -NEG_INF)
        p_parts.append(jnp.exp2(parts[b] - off))
    p_t = p_parts[0] if n_blk == 1 else jnp.concatenate(p_parts, axis=0)
    alpha = jnp.exp2(m - m_new)
    acc_ref[c] = alpha * acc_ref[c] + _dot(v_t, p_t.astype(BF16))
    m_ref[c] = m_new


def _run_chains(chains, logits_fn, update_fn):
    pending = {}
    for idx, c in enumerate(chains[:QK_LOOKAHEAD]):
        pending[c] = logits_fn(c, idx)
    for idx, c in enumerate(chains):
        update_fn(c, pending.pop(c), idx % QK_LOOKAHEAD)
        if idx + QK_LOOKAHEAD < len(chains):
            nxt = chains[idx + QK_LOOKAHEAD]
            pending[nxt] = logits_fn(nxt, idx % QK_LOOKAHEAD)


def _stage_logits(s_ref, slot, s_t):
    s_ref[slot] = s_t


def _init_state(state):
    m_ref, acc_ref = state
    m_ref[...] = jnp.full(m_ref.shape, NEG_INF, F32)
    acc_ref[...] = jnp.zeros(acc_ref.shape, F32)


def _chain_out(state, c):
    _, acc_ref = state
    return acc_ref[c, 0:HEAD_DIM, :] / jnp.maximum(acc_ref[c, HEAD_DIM:HEAD_DIM + 1, :], TINY)


def _values_with_ones(tile, row0):
    v_t = tile.astype(F32).T[row0:row0 + HEAD_DIM]
    pad = lax.broadcasted_iota(jnp.int32, (V_ROWS - HEAD_DIM, tile.shape[0]), 0)
    return jnp.concatenate([v_t, jnp.where(pad == 0, 1.0, 0.0)], axis=0).astype(BF16)


def _rows(ref, kt, cols=slice(None)):
    return ref[pl.ds(pl.multiple_of(kt * TQ, TQ), TQ), cols]


def _store_head_queries(q_ref, qh_ref, n_pairs):
    row = lax.broadcasted_iota(jnp.int32, (LANES, TQ), 0)
    halves = (row < HEAD_DIM, row >= HEAD_DIM)
    for pp in range(n_pairs):
        q2t = q_ref[:, pp * LANES:(pp + 1) * LANES].astype(F32).T
        for j in range(2):
            qh_ref[2 * pp + j] = jnp.where(halves[j], q2t, 0.0).astype(BF16)


def _nsa_kernel(q_ref, g_ref, kvc_ref, ks_ref, vs_ref, kw_ref, vw_ref, bb_ref, far_ref, bc_ref, ovl_ref,
                o_ref, vst_ref, vwt_ref, qh_ref, gs_ref, oc_ref, sel_ref, m_ref, acc_ref, s_ref):
    i = pl.program_id(1)
    n_kt = vst_ref.shape[0]
    state = (m_ref, acc_ref)

    @pl.when(i == 0)
    def _values():
        for kt in range(n_kt):
            rows = slice(kt * TQ, (kt + 1) * TQ)
            vst_ref[kt] = _values_with_ones(vs_ref[rows, :], 0)
            vwt_ref[kt] = _values_with_ones(vw_ref[rows, :], 0)

    _store_head_queries(q_ref, qh_ref, N_PAIRS)
    gs_ref[...] = jax.nn.sigmoid(g_ref[...]).T

    kc2 = kvc_ref[:, 0:LANES].astype(BF16)
    vc_t = kvc_ref[:, LANES:2 * LANES].T[0:HEAD_DIM].astype(BF16)
    psums = []

    def cmp_logits(h, slot):
        s_ref[slot, 0:LANES, :] = _dot(kc2, qh_ref[h])

    def cmp_update(h, _s, slot):
        bias = bc_ref[h]
        s = s_ref[slot, 0:LANES, :] + bias
        valid = bias > 0.5 * NEG_INF
        m = jnp.max(s, axis=0, keepdims=True)
        e = jnp.where(valid, jnp.exp2(s - m), 0.0)
        pc = e / jnp.maximum(jnp.sum(e, axis=0, keepdims=True), TINY)
        psums[:] = [pc if not psums else psums[0] + pc]
        oc_ref[h] = gs_ref[3 * h:3 * h + 1, :] * _dot(vc_t, pc.astype(BF16))

    _run_chains(list(range(N_HEADS)), cmp_logits, cmp_update)
    psum = psums[0]
    p_hi = psum.astype(BF16)
    p_lo = (psum - p_hi.astype(F32)).astype(BF16)
    imp = _dot(ovl_ref[...], p_hi) + _dot(ovl_ref[...], p_lo)
    per_tile = TQ // SLC_BLOCK
    n_slc = n_kt * per_tile
    imp = imp[0:n_slc, :]
    sidx = lax.broadcasted_iota(jnp.int32, (n_slc, TQ), 0)
    t = i * TQ + lax.broadcasted_iota(jnp.int32, (n_slc, TQ), 1)
    cur = lax.shift_right_logical(t, 6)
    eligible = sidx <= cur
    forced = (sidx == 0) | (sidx == cur) | (sidx == cur - 1)
    score = jnp.where(eligible, jnp.where(forced, FORCE_SCORE, 0.0), NEG_INF) + imp
    cnt = jnp.zeros((n_slc, TQ), jnp.int32)
    for sp in range(n_slc):
        row = score[sp:sp + 1, :]
        better = (row > score) | ((row == score) & (sp < sidx))
        cnt = cnt + better.astype(jnp.int32)
    chosen = jnp.where(cnt < min(SLC_TOPK, n_slc), 1.0, 0.0)
    for kt in range(n_kt):
        sel_ref[kt, 0:per_tile, :] = chosen[kt * per_tile:(kt + 1) * per_tile, :]

    _init_state(state)
    slc = [("s", h) for h in range(N_HEADS)]
    win = [("w", h) for h in range(N_HEADS)]
    n_near = jnp.minimum(i, WINDOW // TQ) + 1

    def near_body(jj, _):
        kt = i - jj
        k_s, k_w, v_s, v_w = _rows(ks_ref, kt), _rows(kw_ref, kt), vst_ref[kt], vwt_ref[kt]
        vis = sel_ref[kt, 0:per_tile, :]
        b_s = jnp.minimum(jj, 2)
        b_w = jnp.where(jj == 2, 3, jj)

        def logits(c, slot):
            _stage_logits(s_ref, slot, _dot(k_s if c[0] == "s" else k_w, qh_ref[c[1]]))

        def update(c, _s, slot):
            kind, h = c
            if kind == "s":
                _softmax_tile(state, h, s_ref[slot] + bb_ref[h, b_s], v_s, vis=vis)
            else:
                _softmax_tile(state, N_HEADS + h, s_ref[slot] + bb_ref[h, b_w], v_w)

        _run_chains([c for pair in zip(slc, win) for c in pair], logits, update)
        return 0

    lax.fori_loop(0, n_near, near_body, 0)

    def far_body(jj, _):
        kt = i - jj
        k_s, v_s = _rows(ks_ref, kt), vst_ref[kt]
        vis = sel_ref[kt, 0:per_tile, :]
        _run_chains(slc, lambda c, slot: _dot(k_s, qh_ref[c[1]]),
                    lambda c, s, slot: _softmax_tile(state, c[1], s, v_s, vis=vis, cbias=far_ref[c[1]]))
        return 0

    lax.fori_loop(n_near, i + 1, far_body, 0)

    for pp in range(N_PAIRS):
        hs = (2 * pp, 2 * pp + 1)
        outs = []
        for h in hs:
            g1 = gs_ref[3 * h + 1:3 * h + 2, :]
            g2 = gs_ref[3 * h + 2:3 * h + 3, :]
            outs.append(oc_ref[h] + g1 * _chain_out(state, h) + g2 * _chain_out(state, N_HEADS + h))
        o_ref[:, pp * LANES:(pp + 1) * LANES] = jnp.concatenate(outs, axis=0).T.astype(o_ref.dtype)


def _nsa_attention(q, g, kvc, ks2, vs2, kw2, vw2, bb, far, bc, ovl_t):
    b, s, _ = q.shape
    n_qt = s // TQ
    per_b = lambda bi, i: (bi, 0, 0)
    return pl.pallas_call(
        _nsa_kernel, grid=(b, n_qt),
        in_specs=[pl.BlockSpec((None, TQ, MAIN_W), lambda bi, i: (bi, i, 0)),
                  pl.BlockSpec((None, TQ, LANES), lambda bi, i: (bi, i, 0)),
                  pl.BlockSpec((None,) + kvc.shape[1:], per_b),
                  pl.BlockSpec((None, s, LANES), per_b), pl.BlockSpec((None, s, LANES), per_b),
                  pl.BlockSpec((None, s, LANES), per_b), pl.BlockSpec((None, s, LANES), per_b),
                  pl.BlockSpec(bb.shape, lambda bi, i: (0, 0, 0, 0)),
                  pl.BlockSpec(far.shape, lambda bi, i: (0, 0, 0)),
                  pl.BlockSpec((None,) + bc.shape[1:], lambda bi, i: (i, 0, 0, 0)),
                  pl.BlockSpec(ovl_t.shape, lambda bi, i: (0, 0))],
        out_specs=pl.BlockSpec((None, TQ, MAIN_W), lambda bi, i: (bi, i, 0)),
        out_shape=jax.ShapeDtypeStruct((b, s, MAIN_W), BF16),
        scratch_shapes=[pltpu.VMEM((n_qt, V_ROWS, TQ), BF16), pltpu.VMEM((n_qt, V_ROWS, TQ), BF16),
                        pltpu.VMEM((N_HEADS, LANES, TQ), BF16), pltpu.VMEM((LANES, TQ), F32),
                        pltpu.VMEM((N_HEADS, HEAD_DIM, TQ), F32), pltpu.VMEM((n_qt, 8, TQ), F32),
                        pltpu.VMEM((2 * N_HEADS, 1, TQ), F32),
                        pltpu.VMEM((2 * N_HEADS, V_ROWS, TQ), F32), pltpu.VMEM((QK_LOOKAHEAD, TQ, TQ), F32)],
        compiler_params=_params(("arbitrary", "arbitrary"), VMEM_LIMIT),
        name="nsa_attention")(q, g, kvc, ks2, vs2, kw2, vw2, bb, far, bc, ovl_t)


def _moba_kernel(q_ref, k_ref, v_ref, km_ref, bb_ref, far_ref, o_ref, vt_ref, qh_ref, sel_ref,
                 m_ref, acc_ref, s_ref):
    i = pl.program_id(1)
    n_blk = vt_ref.shape[1]
    state = (m_ref, acc_ref)
    heads = list(range(N_HEADS))

    @pl.when(i == 0)
    def _values():
        for pp in range(N_PAIRS):
            for kt in range(n_blk):
                tile = v_ref[kt * TQ:(kt + 1) * TQ, pp * LANES:(pp + 1) * LANES]
                for j in range(2):
                    vt_ref[2 * pp + j, kt] = _values_with_ones(tile, j * HEAD_DIM)

    _store_head_queries(q_ref, qh_ref, N_PAIRS)
    blk = lax.broadcasted_iota(jnp.int32, (n_blk, TQ), 0)
    eligible = blk < i
    for pp in range(N_PAIRS):
        km = km_ref[:, pp * LANES:(pp + 1) * LANES].astype(BF16)
        for j in range(2):
            gate = _dot(km, qh_ref[2 * pp + j])[0:n_blk, :]
            gate = jnp.where(eligible, gate, NEG_INF)
            cnt = jnp.zeros((n_blk, TQ), jnp.int32)
            for n in range(n_blk):
                row = gate[n:n + 1, :]
                better = (row > gate) | ((row == gate) & (n < blk))
                cnt = cnt + better.astype(jnp.int32)
            chosen = (cnt < min(MOBA_TOPK, n_blk - 1)) & eligible
            vis = jnp.where(chosen | (blk == i), 1.0, 0.0)
            for n in range(n_blk):
                sel_ref[2 * pp + j, n] = vis[n:n + 1, :]
    _init_state(state)

    def tile_inputs(kt, h):
        pp = h // 2
        k = _rows(k_ref, kt, slice(pp * LANES, (pp + 1) * LANES))
        return k, vt_ref[h, kt], sel_ref[h, kt]

    def near_tile(jj):
        kt = i - jj
        def update(h, _s, slot):
            _, v_t, vis = tile_inputs(kt, h)
            _softmax_tile(state, h, s_ref[slot] + bb_ref[h, jj], v_t, vis=vis)

        def logits(h, slot):
            _stage_logits(s_ref, slot, _dot(tile_inputs(kt, h)[0], qh_ref[h]))

        _run_chains(heads, logits, update)

    near_tile(0)
    n_near = jnp.minimum(i, 1) + 1

    @pl.when(i >= 1)
    def _():
        near_tile(1)

    def far_body(jj, _):
        kt = i - jj
        def update(h, s, slot):
            _, v_t, vis = tile_inputs(kt, h)
            _softmax_tile(state, h, s, v_t, vis=vis, cbias=far_ref[h])
        _run_chains(heads, lambda h, slot: _dot(tile_inputs(kt, h)[0], qh_ref[h]), update)
        return 0

    lax.fori_loop(n_near, i + 1, far_body, 0)

    for pp in range(N_PAIRS):
        out_t = jnp.concatenate([_chain_out(state, 2 * pp), _chain_out(state, 2 * pp + 1)], axis=0)
        o_ref[:, pp * LANES:(pp + 1) * LANES] = out_t.T.astype(o_ref.dtype)


def _moba_attention(q, k, v, kmean, bb, far):
    b, s, _ = q.shape
    n_qt = s // TQ
    per_b = lambda bi, i: (bi, 0, 0)
    return pl.pallas_call(
        _moba_kernel, grid=(b, n_qt),
        in_specs=[pl.BlockSpec((None, TQ, MAIN_W), lambda bi, i: (bi, i, 0)),
                  pl.BlockSpec((None, s, MAIN_W), per_b), pl.BlockSpec((None, s, MAIN_W), per_b),
                  pl.BlockSpec((None, LANES, MAIN_W), per_b),
                  pl.BlockSpec((N_HEADS, 2, TQ, TQ), lambda bi, i: (0, 0, 0, 0)),
                  pl.BlockSpec(far.shape, lambda bi, i: (0, 0, 0))],
        out_specs=pl.BlockSpec((None, TQ, MAIN_W), lambda bi, i: (bi, i, 0)),
        out_shape=jax.ShapeDtypeStruct((b, s, MAIN_W), BF16),
        scratch_shapes=[pltpu.VMEM((N_HEADS, n_qt, V_ROWS, TQ), BF16), pltpu.VMEM((N_HEADS, LANES, TQ), BF16),
                        pltpu.VMEM((N_HEADS, n_qt, 1, TQ), F32),
                        pltpu.VMEM((N_HEADS, 1, TQ), F32),
                        pltpu.VMEM((N_HEADS, V_ROWS, TQ), F32), pltpu.VMEM((QK_LOOKAHEAD, TQ, TQ), F32)],
        compiler_params=_params(("arbitrary", "arbitrary"), VMEM_LIMIT),
        name="moba_attention")(q, k, v, kmean, bb, far)


def _mem_kernel(q_ref, kv_ref, o_ref, qh_ref):
    n_sub = q_ref.shape[0] // TQ
    for sub in range(n_sub):
        _store_head_queries(q_ref.at[sub * TQ:(sub + 1) * TQ, :], qh_ref.at[sub], MEM_HEADS // 2)
    outs = {}

    def logits(c, slot):
        sub, h = c
        return _dot(kv_ref[:, (h // 2) * LANES:(h // 2 + 1) * LANES], qh_ref[sub, h])

    def update(c, s, slot):
        h = c[1]
        lanes = slice(MEM_W + (h // 2) * LANES, MEM_W + (h // 2 + 1) * LANES)
        v_t = kv_ref[:, lanes].astype(F32).T[(h % 2) * HEAD_DIM:(h % 2 + 1) * HEAD_DIM].astype(BF16)
        e = jnp.exp2(s - jnp.max(s, axis=0, keepdims=True))
        pr = e / jnp.sum(e, axis=0, keepdims=True)
        outs[c] = _dot(v_t, pr.astype(BF16))

    _run_chains([(sub, h) for sub in range(n_sub) for h in range(MEM_HEADS)], logits, update)
    for sub in range(n_sub):
        for pp in range(MEM_HEADS // 2):
            pair = jnp.concatenate([outs[(sub, 2 * pp)], outs[(sub, 2 * pp + 1)]], axis=0)
            o_ref[sub * TQ:(sub + 1) * TQ, pp * LANES:(pp + 1) * LANES] = pair.T.astype(o_ref.dtype)


MEM_TILES = 2


def _mem_attention(qm, mem_kv):
    b, s, _ = qm.shape
    n_mem = mem_kv.shape[1]
    rows = MEM_TILES * TQ
    return pl.pallas_call(
        _mem_kernel, grid=(b, s // rows),
        in_specs=[pl.BlockSpec((None, rows, MEM_W), lambda bi, i: (bi, i, 0)),
                  pl.BlockSpec((None, n_mem, 2 * MEM_W), lambda bi, i: (bi, 0, 0))],
        out_specs=pl.BlockSpec((None, rows, MEM_W), lambda bi, i: (bi, i, 0)),
        out_shape=jax.ShapeDtypeStruct((b, s, MEM_W), BF16),
        scratch_shapes=[pltpu.VMEM((MEM_TILES, MEM_HEADS, LANES, TQ), BF16)],
        compiler_params=_params(("arbitrary", "arbitrary")), name="mem_attention")(qm, mem_kv)


def _layer_norm(y, g, b):
    mu = jnp.mean(y, axis=-1, keepdims=True)
    var = jnp.mean(jnp.square(y - mu), axis=-1, keepdims=True)
    return (y - mu) * lax.rsqrt(var + LN_EPS) * g + b


def _mm_ln_kernel(*refs, n_in):
    a_refs, w_refs = refs[:n_in], refs[n_in:2 * n_in]
    x_ref, g_ref, b_ref, o_ref, y_ref = refs[2 * n_in:]

    @pl.when(pl.program_id(0) == 0)
    def _():
        y_ref[...] = jnp.zeros(y_ref.shape, F32)

    o_ref[...] = _layer_norm(ALPHA * x_ref[...] + y_ref[...], g_ref[...], b_ref[...])
    acc = _dot(a_refs[0][...], w_refs[0][...])
    for a_ref, w_ref in zip(a_refs[1:], w_refs[1:]):
        acc = acc + _dot(a_ref[...], w_ref[...])
    y_ref[...] = acc


def _mm_ln(acts, weights, x, g, b, tm, name):
    m, d = x.shape
    n_t = m // tm
    cur = lambda t: (jnp.minimum(t, n_t - 1), 0)
    prev = lambda t: (jnp.maximum(t - 1, 0), 0)
    full = lambda t: (0, 0)
    return pl.pallas_call(
        functools.partial(_mm_ln_kernel, n_in=len(acts)), grid=(n_t + 1,),
        in_specs=[pl.BlockSpec((tm, a.shape[1]), cur) for a in acts]
                 + [pl.BlockSpec(w.shape, full) for w in weights]
                 + [pl.BlockSpec((tm, d), prev), pl.BlockSpec((1, d), full), pl.BlockSpec((1, d), full)],
        out_specs=pl.BlockSpec((tm, d), prev),
        out_shape=jax.ShapeDtypeStruct((m, d), F32),
        scratch_shapes=[pltpu.VMEM((tm, d), F32)],
        compiler_params=_params(("arbitrary",), VMEM_LIMIT), name=name)(*acts, *weights, x, g, b)


FF_CHUNK = 256
FF_ROWS = 512
SUBLANES = 8


def _ffn_kernel(x_ref, wa_ref, wb_ref, cw_ref, cb_ref, wo_ref, wol_ref, g_ref, b_ref, o_ref, xb_ref, hid_ref, *,
                n_c):
    c = pl.program_id(1)

    @pl.when(c == 0)
    def _():
        xb_ref[...] = x_ref[...].astype(BF16)
        hid_ref[1] = jnp.zeros(hid_ref.shape[1:], BF16)
        o_ref[...] = jnp.zeros(o_ref.shape, F32)

    xb = xb_ref[...]
    a = _dot(xb, wa_ref[...])
    o_ref[...] += _dot(hid_ref[(c + 1) % 2], wo_ref[...])
    gate = _dot(xb, wb_ref[...])
    ext = jnp.concatenate([jnp.zeros((SUBLANES, a.shape[1]), F32), a], axis=0)
    a1 = pltpu.roll(ext, 1, 0)[SUBLANES:]
    a2 = pltpu.roll(ext, 2, 0)[SUBLANES:]
    conv = cw_ref[0:1, :] * a2 + cw_ref[1:2, :] * a1 + cw_ref[2:3, :] * a + cb_ref[...]
    hid_ref[c % 2] = (jax.nn.gelu(conv) * gate).astype(BF16)

    @pl.when(c == n_c - 1)
    def _():
        for r0 in range(0, o_ref.shape[0], FF_ROWS):
            rows = slice(r0, r0 + FF_ROWS)
            y = o_ref[rows, :] + _dot(hid_ref[(n_c - 1) % 2, rows, :], wol_ref[...])
            o_ref[rows, :] = _layer_norm(ALPHA * x_ref[rows, :] + y, g_ref[...], b_ref[...])


def _ffn_ln(x, w_in, conv_w, conv_b, w_out, g, b):
    bsz, s, d = x.shape
    n_c = D_FF // FF_CHUNK
    prev = lambda c: jnp.maximum(c - 1, 0)
    return pl.pallas_call(
        functools.partial(_ffn_kernel, n_c=n_c), grid=(bsz, n_c),
        in_specs=[pl.BlockSpec((None, s, d), lambda bi, c: (bi, 0, 0)),
                  pl.BlockSpec((d, FF_CHUNK), lambda bi, c: (0, c)),
                  pl.BlockSpec((d, FF_CHUNK), lambda bi, c: (0, n_c + c)),
                  pl.BlockSpec((CONV_W, FF_CHUNK), lambda bi, c: (0, c)),
                  pl.BlockSpec((1, FF_CHUNK), lambda bi, c: (0, c)),
                  pl.BlockSpec((FF_CHUNK, d), lambda bi, c: (prev(c), 0)),
                  pl.BlockSpec((FF_CHUNK, d), lambda bi, c: (n_c - 1, 0)),
                  pl.BlockSpec((1, d), lambda bi, c: (0, 0)), pl.BlockSpec((1, d), lambda bi, c: (0, 0))],
        out_specs=pl.BlockSpec((None, s, d), lambda bi, c: (bi, 0, 0)),
        out_shape=jax.ShapeDtypeStruct((bsz, s, d), F32),
        scratch_shapes=[pltpu.VMEM((s, d), BF16), pltpu.VMEM((2, s, FF_CHUNK), BF16)],
        compiler_params=_params(("arbitrary", "arbitrary"), VMEM_LIMIT),
        name="ffn_ln")(x, w_in, w_in, conv_w, conv_b, w_out, w_out, g, b)


def _dup(w):
    return jnp.concatenate([w, w], axis=1)


def _cmp_weights(pe_k, w1_k, w2_k, pe_v, w1_v, w2_v):
    half = CMP_LEN // 2
    zk = jnp.zeros((half, HEAD_DIM, CMP_HIDDEN), F32)

    def w1_half(sl):
        wk = w1_k.reshape(CMP_LEN, HEAD_DIM, CMP_HIDDEN)[sl]
        wv = w1_v.reshape(CMP_LEN, HEAD_DIM, CMP_HIDDEN)[sl]
        top = jnp.concatenate([wk, zk], axis=2)
        bot = jnp.concatenate([zk, wv], axis=2)
        return jnp.concatenate([top, bot], axis=1).reshape(half * 2 * HEAD_DIM, 2 * CMP_HIDDEN)

    w1 = jnp.stack([w1_half(slice(0, half)), w1_half(slice(half, CMP_LEN))]).astype(BF16)
    pe = jnp.concatenate([pe_k, pe_v], axis=1)
    pe = jnp.stack([pe[:half].reshape(-1), pe[half:].reshape(-1)])
    zo = jnp.zeros((CMP_HIDDEN, 2 * HEAD_DIM), F32)
    w2 = jnp.concatenate([jnp.concatenate([_dup(w2_k), zo], axis=1),
                          jnp.concatenate([zo, _dup(w2_v)], axis=1)], axis=0).astype(BF16)
    return pe, w1, w2


def _overlap_matrix(seq):
    n_cmp = (seq - CMP_LEN) // CMP_STRIDE + 1
    n_slc = seq // SLC_BLOCK
    start = np.arange(LANES) * CMP_STRIDE
    bs = np.arange(LANES) * SLC_BLOCK
    ovl_t = ((start[None, :] < bs[:, None] + SLC_BLOCK) & (start[None, :] + CMP_LEN > bs[:, None])
             & (np.arange(LANES)[None, :] < n_cmp) & (np.arange(LANES)[:, None] < n_slc))
    return jnp.asarray(ovl_t, BF16)


def kernel(x, mem, rel_bias, a_w_in, a_cmp_pe_k, a_cmp_w1_k, a_cmp_w2_k, a_cmp_pe_v, a_cmp_w1_v, a_cmp_w2_v,
           a_w_mem_kv, a_w_out, shared_w_kv, b_w_in, b_w_mem_kv, b_w_out, ln1_g, ln1_b, ln2_g, ln2_b,
           ffn_w_in, ffn_conv_w, ffn_conv_b, ffn_w_out):
    bsz, seq, d = x.shape
    n_mem = mem.shape[1]
    m = bsz * seq
    assert (seq, d) == (2048, D_MODEL) and seq % TQ == 0
    n_qt = seq // TQ

    bb, bc, far = _bias_tables(rel_bias, n_qt)
    ovl_t = _overlap_matrix(seq)
    memf = mem.reshape(bsz * n_mem, d)
    xf = x.reshape(m, d)

    def ffn(xcur, layer):
        return _ffn_ln(xcur.reshape(bsz, seq, d), ffn_w_in[layer].astype(BF16), ffn_conv_w[layer],
                       ffn_conv_b[layer][None, :], ffn_w_out[layer].astype(BF16),
                       ln2_g[layer][None, :], ln2_b[layer][None, :]).reshape(m, d)

    def out_ln(o_main, o_mem, w_out, xcur, layer):
        w = w_out.astype(BF16)
        return _mm_ln([o_main.reshape(m, MAIN_W), o_mem.reshape(m, MEM_W)], [w[:MAIN_W], w[MAIN_W:]], xcur,
                      ln1_g[layer][None, :], ln1_b[layer][None, :], 512, "outproj_ln")

    w = a_w_in[0]
    c0 = MAIN_W
    cols = [w[:, c0 + kk * HEAD_DIM:c0 + (kk + 1) * HEAD_DIM] for kk in range(6)]
    c_g = c0 + 6 * HEAD_DIM
    w_g = jnp.pad(w[:, c_g:c_g + 3 * N_HEADS], ((0, 0), (0, LANES - 3 * N_HEADS)))
    w_qm = w[:, c_g + 3 * N_HEADS:]
    weights = [w[:, :MAIN_W], jnp.concatenate(cols[0:2], axis=1), _dup(cols[2]), _dup(cols[3]),
               _dup(cols[4]), _dup(cols[5]), w_g, w_qm]
    weights = [wi.astype(BF16) for wi in weights]
    q, kvc_tok, ks2, vs2, kw2, vw2, g, qm = _proj(
        xf, weights, [BF16, BF16, BF16, BF16, BF16, BF16, F32, BF16], 512, "proj_a",
        scales=[Q_SCALE, None, None, None, None, None, None, Q_SCALE])

    pe, w1, w2 = _cmp_weights(a_cmp_pe_k[0], a_cmp_w1_k[0], a_cmp_w2_k[0],
                              a_cmp_pe_v[0], a_cmp_w1_v[0], a_cmp_w2_v[0])
    kvc = _compress(kvc_tok.reshape(bsz, seq // (CMP_LEN // 2), (CMP_LEN // 2) * LANES), pe, w1, w2)

    r3 = lambda t: t.reshape(bsz, seq, t.shape[-1])
    o_main = _nsa_attention(r3(q), r3(g), kvc, r3(ks2), r3(vs2), r3(kw2), r3(vw2), bb, far, bc, ovl_t)
    (mkv,) = _proj(memf, [a_w_mem_kv[0].astype(BF16)], [BF16], 512, "proj_mem_a")
    o_mem = _mem_attention(r3(qm), mkv.reshape(bsz, n_mem, 2 * MEM_W))
    x1 = out_ln(o_main, o_mem, a_w_out[0], xf, 0)
    x1 = ffn(x1, 0)

    wb = b_w_in[0].astype(BF16)
    wkv = shared_w_kv.astype(BF16)
    q, qm, k, v, kmean = _proj_b(x1, wb[:, :MAIN_W], wb[:, MAIN_W:], wkv[:, :MAIN_W], wkv[:, MAIN_W:])
    n_blk = seq // MOBA_BLOCK
    kmean = jnp.pad(kmean.reshape(bsz, n_blk, MAIN_W), ((0, 0), (0, LANES - n_blk), (0, 0)))
    o_main = _moba_attention(r3(q), r3(k), r3(v), kmean, bb, far)
    (mkv,) = _proj(memf, [b_w_mem_kv[0].astype(BF16)], [BF16], 512, "proj_mem_b")
    o_mem = _mem_attention(r3(qm), mkv.reshape(bsz, n_mem, 2 * MEM_W))
    x2 = out_ln(o_main, o_mem, b_w_out[0], x1, 1)
    x2 = ffn(x2, 1)
    return x2.reshape(bsz, seq, d)
```

```python
import functools
import math

import numpy as np
import jax
import jax.numpy as jnp
from jax import lax
from jax.experimental import pallas as pl
from jax.experimental.pallas import tpu as pltpu

F32 = jnp.float32
BF16 = jnp.bfloat16

D_MODEL = 1024
HEAD_DIM = 64
N_HEADS = 12
N_PAIRS = N_HEADS // 2
MAIN_W = N_HEADS * HEAD_DIM
MEM_HEADS = 4
MEM_W = MEM_HEADS * HEAD_DIM
CMP_LEN = 32
CMP_STRIDE = 16
CMP_HIDDEN = 256
SLC_BLOCK = 64
SLC_TOPK = 16
WINDOW = 512
FORCE_SCORE = 1.0e4
MOBA_BLOCK = 256
MOBA_TOPK = 3
REL_BUCKETS = 32
REL_MAX_DIST = 128
D_FF = 2816
CONV_W = 3
DEPTH = 2
ALPHA = (2.0 * DEPTH) ** 0.25
LN_EPS = 1e-5
NEG_INF = -1e30
TINY = 1e-30
LOG2E = math.log2(math.e)
Q_SCALE = HEAD_DIM ** -0.5 * LOG2E

LANES = 128
TQ = 256
VMEM_LIMIT = 60 * 1024 * 1024
V_ROWS = HEAD_DIM + 8
QK_LOOKAHEAD = 6


def _dot(a, b):
    return jnp.dot(a, b, preferred_element_type=F32)


def _dot_nt(a, b):
    return lax.dot_general(a, b, (((1,), (1,)), ((), ())), preferred_element_type=F32)


def _params(sem, vmem=None):
    return pltpu.CompilerParams(dimension_semantics=sem, vmem_limit_bytes=vmem)


def _proj_kernel(x_ref, w_ref, *o_refs, scales):
    y = _dot(x_ref[...].astype(BF16), w_ref[...])
    col = 0
    for o_ref, scale in zip(o_refs, scales):
        part = y[:, col:col + o_ref.shape[1]]
        o_ref[...] = (part if scale is None else part * scale).astype(o_ref.dtype)
        col += o_ref.shape[1]


def _proj(x, weights, dtypes, tm, name, scales=None):
    m, k = x.shape
    scales = tuple(scales) if scales is not None else (None,) * len(weights)
    assert all(w.shape[1] % LANES == 0 for w in weights)
    w_all = jnp.concatenate(weights, axis=1) if len(weights) > 1 else weights[0]
    out_specs = [pl.BlockSpec((tm, w.shape[1]), lambda i: (i, 0)) for w in weights]
    out_shape = [jax.ShapeDtypeStruct((m, w.shape[1]), dt) for w, dt in zip(weights, dtypes)]
    return pl.pallas_call(
        functools.partial(_proj_kernel, scales=scales), grid=(m // tm,),
        in_specs=[pl.BlockSpec((tm, k), lambda i: (i, 0)), pl.BlockSpec(w_all.shape, lambda i: (0, 0))],
        out_specs=out_specs, out_shape=out_shape,
        compiler_params=_params(("arbitrary",), VMEM_LIMIT), name=name)(x, w_all)


def _projb_kernel(x_ref, wq_ref, wm_ref, wk_ref, wv_ref, q_ref, qm_ref, k_ref, v_ref, km_ref):
    xb = x_ref[...].astype(BF16)
    q_ref[...] = (_dot(xb, wq_ref[...]) * Q_SCALE).astype(BF16)
    qm_ref[...] = (_dot(xb, wm_ref[...]) * Q_SCALE).astype(BF16)
    k = _dot(xb, wk_ref[...])
    k_ref[...] = k.astype(BF16)
    km_ref[...] = jnp.mean(k, axis=0, keepdims=True)
    v_ref[...] = _dot(xb, wv_ref[...]).astype(BF16)


def _proj_b(x, wq, wm, wk, wv):
    m, k = x.shape
    nblk = m // MOBA_BLOCK
    row = lambda i: (i, 0)
    full = lambda i: (0, 0)
    return pl.pallas_call(
        _projb_kernel, grid=(nblk,),
        in_specs=[pl.BlockSpec((MOBA_BLOCK, k), row)] + [pl.BlockSpec(w.shape, full) for w in (wq, wm, wk, wv)],
        out_specs=[pl.BlockSpec((MOBA_BLOCK, MAIN_W), row), pl.BlockSpec((MOBA_BLOCK, MEM_W), row),
                   pl.BlockSpec((MOBA_BLOCK, MAIN_W), row), pl.BlockSpec((MOBA_BLOCK, MAIN_W), row),
                   pl.BlockSpec((None, 1, MAIN_W), lambda i: (i, 0, 0))],
        out_shape=[jax.ShapeDtypeStruct((m, MAIN_W), BF16), jax.ShapeDtypeStruct((m, MEM_W), BF16),
                   jax.ShapeDtypeStruct((m, MAIN_W), BF16), jax.ShapeDtypeStruct((m, MAIN_W), BF16),
                   jax.ShapeDtypeStruct((nblk, 1, MAIN_W), F32)],
        compiler_params=_params(("arbitrary",), VMEM_LIMIT), name="proj_b")(x, wq, wm, wk, wv)


def _bias_kernel(tbl_ref, bb_ref, bc_ref, far_ref):
    h = pl.program_id(0)

    def bias_of(dist):
        n = jnp.maximum(dist, 0)
        max_exact = REL_BUCKETS // 2
        nf = jnp.maximum(n, 1).astype(F32)
        large = max_exact + (jnp.log(nf / max_exact) / math.log(REL_MAX_DIST / max_exact)
                             * (REL_BUCKETS - max_exact)).astype(jnp.int32)
        large = jnp.minimum(large, REL_BUCKETS - 1)
        bucket = jnp.where(n < max_exact, n, large)
        out = jnp.zeros(dist.shape, F32)
        for kk in range(REL_BUCKETS):
            out = jnp.where(bucket == kk, tbl_ref[h * REL_BUCKETS + kk], out)
        return out * LOG2E

    key = lax.broadcasted_iota(jnp.int32, (TQ, TQ), 0)
    qry = lax.broadcasted_iota(jnp.int32, (TQ, TQ), 1)
    d0 = qry - key
    bb_ref[0] = jnp.where(d0 >= 0, bias_of(d0), NEG_INF)
    bb_ref[1] = bias_of(d0 + TQ)
    far = bias_of(d0 + 2 * TQ)
    bb_ref[2] = far
    bb_ref[3] = jnp.where(d0 + 2 * TQ < WINDOW, far, NEG_INF)
    far_ref[...] = far[0:1, :]

    nc = lax.broadcasted_iota(jnp.int32, (LANES, TQ), 0)
    qc = lax.broadcasted_iota(jnp.int32, (LANES, TQ), 1)
    n_cmp = (2048 - CMP_LEN) // CMP_STRIDE + 1
    for i in range(bc_ref.shape[0]):
        dc = i * TQ + qc - (nc * CMP_STRIDE + CMP_LEN - 1)
        bc_ref[i] = jnp.where((dc >= 0) & (nc < n_cmp), bias_of(dc), NEG_INF)


def _bias_tables(rel_bias, n_qt):
    tbl = rel_bias.T.reshape(-1)
    return pl.pallas_call(
        _bias_kernel, grid=(N_HEADS,),
        in_specs=[pl.BlockSpec(memory_space=pltpu.SMEM)],
        out_specs=[pl.BlockSpec((None, 4, TQ, TQ), lambda h: (h, 0, 0, 0)),
                   pl.BlockSpec((n_qt, None, LANES, TQ), lambda h: (0, h, 0, 0)),
                   pl.BlockSpec((None, 1, TQ), lambda h: (h, 0, 0))],
        out_shape=[jax.ShapeDtypeStruct((N_HEADS, 4, TQ, TQ), F32),
                   jax.ShapeDtypeStruct((n_qt, N_HEADS, LANES, TQ), F32),
                   jax.ShapeDtypeStruct((N_HEADS, 1, TQ), F32)],
        compiler_params=_params(("arbitrary",)), name="bias_tables")(tbl)


def _cmp_kernel(kv_ref, pe_ref, w1_ref, w2_ref, o_ref):
    x = kv_ref[...].astype(F32)
    lo = _dot((x + pe_ref[0:1, :]).astype(BF16), w1_ref[0])
    hi = _dot((x + pe_ref[1:2, :]).astype(BF16), w1_ref[1])
    nrow = x.shape[0]
    hid = lo + pltpu.roll(hi, nrow - 1, 0)
    o_ref[...] = _dot(jax.nn.gelu(hid).astype(BF16), w2_ref[...])


def _compress(kvr, pe, w1, w2):
    b, nrow, width = kvr.shape
    return pl.pallas_call(
        _cmp_kernel, grid=(b,),
        in_specs=[pl.BlockSpec((None, nrow, width), lambda i: (i, 0, 0)),
                  pl.BlockSpec(pe.shape, lambda i: (0, 0)),
                  pl.BlockSpec(w1.shape, lambda i: (0, 0, 0)),
                  pl.BlockSpec(w2.shape, lambda i: (0, 0))],
        out_specs=pl.BlockSpec((None, nrow, w2.shape[1]), lambda i: (i, 0, 0)),
        out_shape=jax.ShapeDtypeStruct((b, nrow, w2.shape[1]), F32),
        compiler_params=_params(("arbitrary",), VMEM_LIMIT), name="nsa_compress")(kvr, pe, w1, w2)


def _softmax_tile(state, c, s_t, v_t, vis=None, cbias=None):
    m_ref, acc_ref = state
    m = m_ref[c]
    n_blk = 1 if vis is None else vis.shape[0]
    rows = s_t.shape[0] // n_blk
    parts = [s_t[b * rows:(b + 1) * rows] for b in range(n_blk)]
    tile_max = None
    for b in range(n_blk):
        mb = jnp.max(parts[b], axis=0, keepdims=True)
        if vis is not None:
            mb = jnp.where(vis[b:b + 1] > 0.5, mb, NEG_INF)
        tile_max = mb if tile_max is None else jnp.maximum(tile_max, mb)
    if cbias is not None:
        tile_max = tile_max + cbias
    m_new = jnp.maximum(m, tile_max)
    shift = m_new if cbias is None else m_new - cbias
    p_parts = []
    for b in range(n_blk):
        off = shift if vis is None else jnp.where(vis[b:b + 1] > 0.5, shift, -NEG_INF)
        p_parts.append(jnp.exp2(parts[b] - off))
    p_t = p_parts[0] if n_blk == 1 else jnp.concatenate(p_parts, axis=0)
    alpha = jnp.exp2(m - m_new)
    acc_ref[c] = alpha * acc_ref[c] + _dot(v_t, p_t.astype(BF16))
    m_ref[c] = m_new


def _run_chains(chains, logits_fn, update_fn):
    pending = {}
    for idx, c in enumerate(chains[:QK_LOOKAHEAD]):
        pending[c] = logits_fn(c, idx)
    for idx, c in enumerate(chains):
        update_fn(c, pending.pop(c), idx % QK_LOOKAHEAD)
        if idx + QK_LOOKAHEAD < len(chains):
            nxt = chains[idx + QK_LOOKAHEAD]
            pending[nxt] = logits_fn(nxt, idx % QK_LOOKAHEAD)


def _pair_loop(lo, hi, tiles_fn):
    odd = (hi - lo) % 2

    def single(_, carry):
        tiles_fn([lo])
        return carry

    def double(r, carry):
        jj = lo + odd + 2 * r
        tiles_fn([jj, jj + 1])
        return carry

    lax.fori_loop(0, odd, single, 0)
    lax.fori_loop(0, (hi - lo) // 2, double, 0)


def _stage_logits(s_ref, slot, s_t):
    s_ref[slot] = s_t


def _init_state(state):
    m_ref, acc_ref = state
    m_ref[...] = jnp.full(m_ref.shape, NEG_INF, F32)
    acc_ref[...] = jnp.zeros(acc_ref.shape, F32)


def _chain_out(state, c):
    _, acc_ref = state
    return acc_ref[c, 0:HEAD_DIM, :] / jnp.maximum(acc_ref[c, HEAD_DIM:HEAD_DIM + 1, :], TINY)


def _values_with_ones(tile, row0):
    v_t = tile.astype(F32).T[row0:row0 + HEAD_DIM]
    pad = lax.broadcasted_iota(jnp.int32, (V_ROWS - HEAD_DIM, tile.shape[0]), 0)
    return jnp.concatenate([v_t, jnp.where(pad == 0, 1.0, 0.0)], axis=0).astype(BF16)


def _rows(ref, kt, cols=slice(None)):
    return ref[pl.ds(pl.multiple_of(kt * TQ, TQ), TQ), cols]


def _store_head_queries(q_ref, qh_ref, n_pairs):
    row = lax.broadcasted_iota(jnp.int32, (LANES, TQ), 0)
    halves = (row < HEAD_DIM, row >= HEAD_DIM)
    for pp in range(n_pairs):
        q2t = q_ref[:, pp * LANES:(pp + 1) * LANES].astype(F32).T
        for j in range(2):
            qh_ref[2 * pp + j] = jnp.where(halves[j], q2t, 0.0).astype(BF16)


def _nsa_kernel(q_ref, g_ref, kvc_ref, ks_ref, vs_ref, kw_ref, vw_ref, bb_ref, far_ref, bc_ref, ovl_ref,
                o_ref, vst_ref, vwt_ref, qh_ref, gs_ref, oc_ref, sel_ref, m_ref, acc_ref, s_ref):
    i = pl.program_id(1)
    n_kt = vst_ref.shape[0]
    state = (m_ref, acc_ref)

    @pl.when(i == 0)
    def _values():
        for kt in range(n_kt):
            rows = slice(kt * TQ, (kt + 1) * TQ)
            vst_ref[kt] = _values_with_ones(vs_ref[rows, :], 0)
            vwt_ref[kt] = _values_with_ones(vw_ref[rows, :], 0)

    _store_head_queries(q_ref, qh_ref, N_PAIRS)
    gs_ref[...] = jax.nn.sigmoid(g_ref[...]).T

    kc2 = kvc_ref[:, 0:LANES].astype(BF16)
    vc_t = kvc_ref[:, LANES:2 * LANES].T[0:HEAD_DIM].astype(BF16)
    psums = []

    def cmp_logits(h, slot):
        s_ref[slot, 0:LANES, :] = _dot(kc2, qh_ref[h])

    def cmp_update(h, _s, slot):
        bias = bc_ref[h]
        s = s_ref[slot, 0:LANES, :] + bias
        m = jnp.maximum(jnp.max(s, axis=0, keepdims=True), 0.5 * NEG_INF)
        e = jnp.exp2(s - m)
        pc = e / jnp.maximum(jnp.sum(e, axis=0, keepdims=True), TINY)
        psums[:] = [pc if not psums else psums[0] + pc]
        oc_ref[h] = gs_ref[3 * h:3 * h + 1, :] * _dot(vc_t, pc.astype(BF16))

    _run_chains(list(range(N_HEADS)), cmp_logits, cmp_update)
    psum = psums[0]
    p_hi = psum.astype(BF16)
    p_lo = (psum - p_hi.astype(F32)).astype(BF16)
    imp = _dot(ovl_ref[...], p_hi) + _dot(ovl_ref[...], p_lo)
    per_tile = TQ // SLC_BLOCK
    n_slc = n_kt * per_tile
    imp = imp[0:n_slc, :]
    sidx = lax.broadcasted_iota(jnp.int32, (n_slc, TQ), 0)
    t = i * TQ + lax.broadcasted_iota(jnp.int32, (n_slc, TQ), 1)
    cur = lax.shift_right_logical(t, 6)
    eligible = sidx <= cur
    forced = (sidx == 0) | (sidx == cur) | (sidx == cur - 1)
    score = jnp.where(eligible, jnp.where(forced, FORCE_SCORE, 0.0), NEG_INF) + imp
    cnt = jnp.zeros((n_slc, TQ), jnp.int32)
    for sp in range(n_slc):
        row = score[sp:sp + 1, :]
        better = (row > score) | ((row == score) & (sp < sidx))
        cnt = cnt + better.astype(jnp.int32)
    chosen = jnp.where(cnt < min(SLC_TOPK, n_slc), 1.0, 0.0)
    for kt in range(n_kt):
        sel_ref[kt, 0:per_tile, :] = chosen[kt * per_tile:(kt + 1) * per_tile, :]

    _init_state(state)
    slc = [("s", h) for h in range(N_HEADS)]
    win = [("w", h) for h in range(N_HEADS)]
    n_near = jnp.minimum(i, WINDOW // TQ) + 1

    def near_body(jj, _):
        kt = i - jj
        k_s, k_w, v_s, v_w = _rows(ks_ref, kt), _rows(kw_ref, kt), vst_ref[kt], vwt_ref[kt]
        vis = sel_ref[kt, 0:per_tile, :]
        b_s = jnp.minimum(jj, 2)
        b_w = jnp.where(jj == 2, 3, jj)

        def logits(c, slot):
            _stage_logits(s_ref, slot, _dot(k_s if c[0] == "s" else k_w, qh_ref[c[1]]))

        def update(c, _s, slot):
            kind, h = c
            if kind == "s":
                _softmax_tile(state, h, s_ref[slot] + bb_ref[h, b_s], v_s, vis=vis)
            else:
                _softmax_tile(state, N_HEADS + h, s_ref[slot] + bb_ref[h, b_w], v_w)

        _run_chains([c for pair in zip(slc, win) for c in pair], logits, update)
        return 0

    lax.fori_loop(0, n_near, near_body, 0)

    def far_tiles(jjs):
        kts = [i - jj for jj in jjs]
        _run_chains([(n, h) for n in range(len(jjs)) for h in range(N_HEADS)],
                    lambda c, slot: _dot(_rows(ks_ref, kts[c[0]]), qh_ref[c[1]]),
                    lambda c, s, slot: _softmax_tile(state, c[1], s, vst_ref[kts[c[0]]],
                                                     vis=sel_ref[kts[c[0]], 0:per_tile, :], cbias=far_ref[c[1]]))

    _pair_loop(n_near, i + 1, far_tiles)

    for pp in range(N_PAIRS):
        hs = (2 * pp, 2 * pp + 1)
        outs = []
        for h in hs:
            g1 = gs_ref[3 * h + 1:3 * h + 2, :]
            g2 = gs_ref[3 * h + 2:3 * h + 3, :]
            outs.append(oc_ref[h] + g1 * _chain_out(state, h) + g2 * _chain_out(state, N_HEADS + h))
        o_ref[:, pp * LANES:(pp + 1) * LANES] = jnp.concatenate(outs, axis=0).T.astype(o_ref.dtype)


def _nsa_attention(q, g, kvc, ks2, vs2, kw2, vw2, bb, far, bc, ovl_t):
    b, s, _ = q.shape
    n_qt = s // TQ
    per_b = lambda bi, i: (bi, 0, 0)
    return pl.pallas_call(
        _nsa_kernel, grid=(b, n_qt),
        in_specs=[pl.BlockSpec((None, TQ, MAIN_W), lambda bi, i: (bi, i, 0)),
                  pl.BlockSpec((None, TQ, LANES), lambda bi, i: (bi, i, 0)),
                  pl.BlockSpec((None,) + kvc.shape[1:], per_b),
                  pl.BlockSpec((None, s, LANES), per_b), pl.BlockSpec((None, s, LANES), per_b),
                  pl.BlockSpec((None, s, LANES), per_b), pl.BlockSpec((None, s, LANES), per_b),
                  pl.BlockSpec(bb.shape, lambda bi, i: (0, 0, 0, 0)),
                  pl.BlockSpec(far.shape, lambda bi, i: (0, 0, 0)),
                  pl.BlockSpec((None,) + bc.shape[1:], lambda bi, i: (i, 0, 0, 0)),
                  pl.BlockSpec(ovl_t.shape, lambda bi, i: (0, 0))],
        out_specs=pl.BlockSpec((None, TQ, MAIN_W), lambda bi, i: (bi, i, 0)),
        out_shape=jax.ShapeDtypeStruct((b, s, MAIN_W), BF16),
        scratch_shapes=[pltpu.VMEM((n_qt, V_ROWS, TQ), BF16), pltpu.VMEM((n_qt, V_ROWS, TQ), BF16),
                        pltpu.VMEM((N_HEADS, LANES, TQ), BF16), pltpu.VMEM((LANES, TQ), F32),
                        pltpu.VMEM((N_HEADS, HEAD_DIM, TQ), F32), pltpu.VMEM((n_qt, 8, TQ), F32),
                        pltpu.VMEM((2 * N_HEADS, 1, TQ), F32),
                        pltpu.VMEM((2 * N_HEADS, V_ROWS, TQ), F32), pltpu.VMEM((QK_LOOKAHEAD, TQ, TQ), F32)],
        compiler_params=_params(("arbitrary", "arbitrary"), VMEM_LIMIT),
        name="nsa_attention")(q, g, kvc, ks2, vs2, kw2, vw2, bb, far, bc, ovl_t)


def _moba_kernel(q_ref, k_ref, v_ref, km_ref, bb_ref, far_ref, o_ref, vt_ref, qh_ref, sel_ref,
                 m_ref, acc_ref, s_ref):
    i = pl.program_id(1)
    n_blk = vt_ref.shape[1]
    state = (m_ref, acc_ref)
    heads = list(range(N_HEADS))

    @pl.when(i == 0)
    def _values():
        for pp in range(N_PAIRS):
            for kt in range(n_blk):
                tile = v_ref[kt * TQ:(kt + 1) * TQ, pp * LANES:(pp + 1) * LANES]
                for j in range(2):
                    vt_ref[2 * pp + j, kt] = _values_with_ones(tile, j * HEAD_DIM)

    _store_head_queries(q_ref, qh_ref, N_PAIRS)
    blk = lax.broadcasted_iota(jnp.int32, (n_blk, TQ), 0)
    eligible = blk < i
    for pp in range(N_PAIRS):
        km = km_ref[:, pp * LANES:(pp + 1) * LANES].astype(BF16)
        for j in range(2):
            gate = _dot(km, qh_ref[2 * pp + j])[0:n_blk, :]
            gate = jnp.where(eligible, gate, NEG_INF)
            cnt = jnp.zeros((n_blk, TQ), jnp.int32)
            for n in range(n_blk):
                row = gate[n:n + 1, :]
                better = (row > gate) | ((row == gate) & (n < blk))
                cnt = cnt + better.astype(jnp.int32)
            chosen = (cnt < min(MOBA_TOPK, n_blk - 1)) & eligible
            vis = jnp.where(chosen | (blk == i), 1.0, 0.0)
            for n in range(n_blk):
                sel_ref[2 * pp + j, n] = vis[n:n + 1, :]
    _init_state(state)

    def tile_inputs(kt, h):
        pp = h // 2
        k = _rows(k_ref, kt, slice(pp * LANES, (pp + 1) * LANES))
        return k, vt_ref[h, kt], sel_ref[h, kt]

    def near_tile(jj):
        kt = i - jj
        def update(h, _s, slot):
            _, v_t, vis = tile_inputs(kt, h)
            _softmax_tile(state, h, s_ref[slot] + bb_ref[h, jj], v_t, vis=vis)

        def logits(h, slot):
            _stage_logits(s_ref, slot, _dot(tile_inputs(kt, h)[0], qh_ref[h]))

        _run_chains(heads, logits, update)

    near_tile(0)
    n_near = jnp.minimum(i, 1) + 1

    @pl.when(i >= 1)
    def _():
        near_tile(1)

    def far_tiles(jjs):
        kts = [i - jj for jj in jjs]

        def update(c, s, slot):
            _, v_t, vis = tile_inputs(kts[c[0]], c[1])
            _softmax_tile(state, c[1], s, v_t, vis=vis, cbias=far_ref[c[1]])

        _run_chains([(n, h) for n in range(len(jjs)) for h in heads],
                    lambda c, slot: _dot(tile_inputs(kts[c[0]], c[1])[0], qh_ref[c[1]]), update)

    _pair_loop(n_near, i + 1, far_tiles)

    for pp in range(N_PAIRS):
        out_t = jnp.concatenate([_chain_out(state, 2 * pp), _chain_out(state, 2 * pp + 1)], axis=0)
        o_ref[:, pp * LANES:(pp + 1) * LANES] = out_t.T.astype(o_ref.dtype)


def _moba_attention(q, k, v, kmean, bb, far):
    b, s, _ = q.shape
    n_qt = s // TQ
    per_b = lambda bi, i: (bi, 0, 0)
    return pl.pallas_call(
        _moba_kernel, grid=(b, n_qt),
        in_specs=[pl.BlockSpec((None, TQ, MAIN_W), lambda bi, i: (bi, i, 0)),
                  pl.BlockSpec((None, s, MAIN_W), per_b), pl.BlockSpec((None, s, MAIN_W), per_b),
                  pl.BlockSpec((None, LANES, MAIN_W), per_b),
                  pl.BlockSpec((N_HEADS, 2, TQ, TQ), lambda bi, i: (0, 0, 0, 0)),
                  pl.BlockSpec(far.shape, lambda bi, i: (0, 0, 0))],
        out_specs=pl.BlockSpec((None, TQ, MAIN_W), lambda bi, i: (bi, i, 0)),
        out_shape=jax.ShapeDtypeStruct((b, s, MAIN_W), BF16),
        scratch_shapes=[pltpu.VMEM((N_HEADS, n_qt, V_ROWS, TQ), BF16), pltpu.VMEM((N_HEADS, LANES, TQ), BF16),
                        pltpu.VMEM((N_HEADS, n_qt, 1, TQ), F32),
                        pltpu.VMEM((N_HEADS, 1, TQ), F32),
                        pltpu.VMEM((N_HEADS, V_ROWS, TQ), F32), pltpu.VMEM((QK_LOOKAHEAD, TQ, TQ), F32)],
        compiler_params=_params(("arbitrary", "arbitrary"), VMEM_LIMIT),
        name="moba_attention")(q, k, v, kmean, bb, far)


def _mem_kernel(q_ref, kv_ref, o_ref, qh_ref):
    n_sub = q_ref.shape[0] // TQ
    for sub in range(n_sub):
        _store_head_queries(q_ref.at[sub * TQ:(sub + 1) * TQ, :], qh_ref.at[sub], MEM_HEADS // 2)
    outs = {}

    def logits(c, slot):
        sub, h = c
        return _dot(kv_ref[:, (h // 2) * LANES:(h // 2 + 1) * LANES], qh_ref[sub, h])

    def update(c, s, slot):
        h = c[1]
        lanes = slice(MEM_W + (h // 2) * LANES, MEM_W + (h // 2 + 1) * LANES)
        v_t = kv_ref[:, lanes].astype(F32).T[(h % 2) * HEAD_DIM:(h % 2 + 1) * HEAD_DIM].astype(BF16)
        e = jnp.exp2(s - jnp.max(s, axis=0, keepdims=True))
        pr = e / jnp.sum(e, axis=0, keepdims=True)
        outs[c] = _dot(v_t, pr.astype(BF16))

    _run_chains([(sub, h) for sub in range(n_sub) for h in range(MEM_HEADS)], logits, update)
    for sub in range(n_sub):
        for pp in range(MEM_HEADS // 2):
            pair = jnp.concatenate([outs[(sub, 2 * pp)], outs[(sub, 2 * pp + 1)]], axis=0)
            o_ref[sub * TQ:(sub + 1) * TQ, pp * LANES:(pp + 1) * LANES] = pair.T.astype(o_ref.dtype)


MEM_TILES = 2


def _mem_attention(qm, mem_kv):
    b, s, _ = qm.shape
    n_mem = mem_kv.shape[1]
    rows = MEM_TILES * TQ
    return pl.pallas_call(
        _mem_kernel, grid=(b, s // rows),
        in_specs=[pl.BlockSpec((None, rows, MEM_W), lambda bi, i: (bi, i, 0)),
                  pl.BlockSpec((None, n_mem, 2 * MEM_W), lambda bi, i: (bi, 0, 0))],
        out_specs=pl.BlockSpec((None, rows, MEM_W), lambda bi, i: (bi, i, 0)),
        out_shape=jax.ShapeDtypeStruct((b, s, MEM_W), BF16),
        scratch_shapes=[pltpu.VMEM((MEM_TILES, MEM_HEADS, LANES, TQ), BF16)],
        compiler_params=_params(("arbitrary", "arbitrary")), name="mem_attention")(qm, mem_kv)


def _layer_norm(y, g, b):
    mu = jnp.mean(y, axis=-1, keepdims=True)
    var = jnp.mean(jnp.square(y - mu), axis=-1, keepdims=True)
    return (y - mu) * lax.rsqrt(var + LN_EPS) * g + b


def _mm_ln_kernel(*refs, n_in):
    a_refs, w_refs = refs[:n_in], refs[n_in:2 * n_in]
    x_ref, g_ref, b_ref, o_ref, y_ref = refs[2 * n_in:]

    @pl.when(pl.program_id(0) == 0)
    def _():
        y_ref[...] = jnp.zeros(y_ref.shape, F32)

    o_ref[...] = _layer_norm(ALPHA * x_ref[...] + y_ref[...], g_ref[...], b_ref[...])
    acc = _dot(a_refs[0][...], w_refs[0][...])
    for a_ref, w_ref in zip(a_refs[1:], w_refs[1:]):
        acc = acc + _dot(a_ref[...], w_ref[...])
    y_ref[...] = acc


def _mm_ln(acts, weights, x, g, b, tm, name):
    m, d = x.shape
    n_t = m // tm
    cur = lambda t: (jnp.minimum(t, n_t - 1), 0)
    prev = lambda t: (jnp.maximum(t - 1, 0), 0)
    full = lambda t: (0, 0)
    return pl.pallas_call(
        functools.partial(_mm_ln_kernel, n_in=len(acts)), grid=(n_t + 1,),
        in_specs=[pl.BlockSpec((tm, a.shape[1]), cur) for a in acts]
                 + [pl.BlockSpec(w.shape, full) for w in weights]
                 + [pl.BlockSpec((tm, d), prev), pl.BlockSpec((1, d), full), pl.BlockSpec((1, d), full)],
        out_specs=pl.BlockSpec((tm, d), prev),
        out_shape=jax.ShapeDtypeStruct((m, d), F32),
        scratch_shapes=[pltpu.VMEM((tm, d), F32)],
        compiler_params=_params(("arbitrary",), VMEM_LIMIT), name=name)(*acts, *weights, x, g, b)


FF_CHUNK = 256
FF_ROWS = 512
SUBLANES = 8


def _ffn_kernel(x_ref, wa_ref, wb_ref, cw_ref, cb_ref, wo_ref, wol_ref, g_ref, b_ref, o_ref, xb_ref, hid_ref, *,
                n_c):
    c = pl.program_id(1)

    @pl.when(c == 0)
    def _():
        xb_ref[...] = x_ref[...].astype(BF16)
        hid_ref[1] = jnp.zeros(hid_ref.shape[1:], BF16)
        o_ref[...] = jnp.zeros(o_ref.shape, F32)

    xb = xb_ref[...]
    a = _dot(xb, wa_ref[...])
    o_ref[...] += _dot(hid_ref[(c + 1) % 2], wo_ref[...])
    gate = _dot(xb, wb_ref[...])
    ext = jnp.concatenate([jnp.zeros((SUBLANES, a.shape[1]), F32), a], axis=0)
    a1 = pltpu.roll(ext, 1, 0)[SUBLANES:]
    a2 = pltpu.roll(ext, 2, 0)[SUBLANES:]
    conv = cw_ref[0:1, :] * a2 + cw_ref[1:2, :] * a1 + cw_ref[2:3, :] * a + cb_ref[...]
    hid_ref[c % 2] = (jax.nn.gelu(conv) * gate).astype(BF16)

    @pl.when(c == n_c - 1)
    def _():
        for r0 in range(0, o_ref.shape[0], FF_ROWS):
            rows = slice(r0, r0 + FF_ROWS)
            y = o_ref[rows, :] + _dot(hid_ref[(n_c - 1) % 2, rows, :], wol_ref[...])
            o_ref[rows, :] = _layer_norm(ALPHA * x_ref[rows, :] + y, g_ref[...], b_ref[...])


def _ffn_ln(x, w_in, conv_w, conv_b, w_out, g, b):
    bsz, s, d = x.shape
    n_c = D_FF // FF_CHUNK
    prev = lambda c: jnp.maximum(c - 1, 0)
    return pl.pallas_call(
        functools.partial(_ffn_kernel, n_c=n_c), grid=(bsz, n_c),
        in_specs=[pl.BlockSpec((None, s, d), lambda bi, c: (bi, 0, 0)),
                  pl.BlockSpec((d, FF_CHUNK), lambda bi, c: (0, c)),
                  pl.BlockSpec((d, FF_CHUNK), lambda bi, c: (0, n_c + c)),
                  pl.BlockSpec((CONV_W, FF_CHUNK), lambda bi, c: (0, c)),
                  pl.BlockSpec((1, FF_CHUNK), lambda bi, c: (0, c)),
                  pl.BlockSpec((FF_CHUNK, d), lambda bi, c: (prev(c), 0)),
                  pl.BlockSpec((FF_CHUNK, d), lambda bi, c: (n_c - 1, 0)),
                  pl.BlockSpec((1, d), lambda bi, c: (0, 0)), pl.BlockSpec((1, d), lambda bi, c: (0, 0))],
        out_specs=pl.BlockSpec((None, s, d), lambda bi, c: (bi, 0, 0)),
        out_shape=jax.ShapeDtypeStruct((bsz, s, d), F32),
        scratch_shapes=[pltpu.VMEM((s, d), BF16), pltpu.VMEM((2, s, FF_CHUNK), BF16)],
        compiler_params=_params(("arbitrary", "arbitrary"), VMEM_LIMIT),
        name="ffn_ln")(x, w_in, w_in, conv_w, conv_b, w_out, w_out, g, b)


def _dup(w):
    return jnp.concatenate([w, w], axis=1)


def _cmp_weights(pe_k, w1_k, w2_k, pe_v, w1_v, w2_v):
    half = CMP_LEN // 2
    zk = jnp.zeros((half, HEAD_DIM, CMP_HIDDEN), F32)

    def w1_half(sl):
        wk = w1_k.reshape(CMP_LEN, HEAD_DIM, CMP_HIDDEN)[sl]
        wv = w1_v.reshape(CMP_LEN, HEAD_DIM, CMP_HIDDEN)[sl]
        top = jnp.concatenate([wk, zk], axis=2)
        bot = jnp.concatenate([zk, wv], axis=2)
        return jnp.concatenate([top, bot], axis=1).reshape(half * 2 * HEAD_DIM, 2 * CMP_HIDDEN)

    w1 = jnp.stack([w1_half(slice(0, half)), w1_half(slice(half, CMP_LEN))]).astype(BF16)
    pe = jnp.concatenate([pe_k, pe_v], axis=1)
    pe = jnp.stack([pe[:half].reshape(-1), pe[half:].reshape(-1)])
    zo = jnp.zeros((CMP_HIDDEN, 2 * HEAD_DIM), F32)
    w2 = jnp.concatenate([jnp.concatenate([_dup(w2_k), zo], axis=1),
                          jnp.concatenate([zo, _dup(w2_v)], axis=1)], axis=0).astype(BF16)
    return pe, w1, w2


def _overlap_matrix(seq):
    n_cmp = (seq - CMP_LEN) // CMP_STRIDE + 1
    n_slc = seq // SLC_BLOCK
    start = np.arange(LANES) * CMP_STRIDE
    bs = np.arange(LANES) * SLC_BLOCK
    ovl_t = ((start[None, :] < bs[:, None] + SLC_BLOCK) & (start[None, :] + CMP_LEN > bs[:, None])
             & (np.arange(LANES)[None, :] < n_cmp) & (np.arange(LANES)[:, None] < n_slc))
    return jnp.asarray(ovl_t, BF16)


def kernel(x, mem, rel_bias, a_w_in, a_cmp_pe_k, a_cmp_w1_k, a_cmp_w2_k, a_cmp_pe_v, a_cmp_w1_v, a_cmp_w2_v,
           a_w_mem_kv, a_w_out, shared_w_kv, b_w_in, b_w_mem_kv, b_w_out, ln1_g, ln1_b, ln2_g, ln2_b,
           ffn_w_in, ffn_conv_w, ffn_conv_b, ffn_w_out):
    bsz, seq, d = x.shape
    n_mem = mem.shape[1]
    m = bsz * seq
    assert (seq, d) == (2048, D_MODEL) and seq % TQ == 0
    n_qt = seq // TQ

    bb, bc, far = _bias_tables(rel_bias, n_qt)
    ovl_t = _overlap_matrix(seq)
    memf = mem.reshape(bsz * n_mem, d)
    xf = x.reshape(m, d)

    def ffn(xcur, layer):
        return _ffn_ln(xcur.reshape(bsz, seq, d), ffn_w_in[layer].astype(BF16), ffn_conv_w[layer],
                       ffn_conv_b[layer][None, :], ffn_w_out[layer].astype(BF16),
                       ln2_g[layer][None, :], ln2_b[layer][None, :]).reshape(m, d)

    def out_ln(o_main, o_mem, w_out, xcur, layer):
        w = w_out.astype(BF16)
        return _mm_ln([o_main.reshape(m, MAIN_W), o_mem.reshape(m, MEM_W)], [w[:MAIN_W], w[MAIN_W:]], xcur,
                      ln1_g[layer][None, :], ln1_b[layer][None, :], 512, "outproj_ln")

    w = a_w_in[0]
    c0 = MAIN_W
    cols = [w[:, c0 + kk * HEAD_DIM:c0 + (kk + 1) * HEAD_DIM] for kk in range(6)]
    c_g = c0 + 6 * HEAD_DIM
    w_g = jnp.pad(w[:, c_g:c_g + 3 * N_HEADS], ((0, 0), (0, LANES - 3 * N_HEADS)))
    w_qm = w[:, c_g + 3 * N_HEADS:]
    weights = [w[:, :MAIN_W], jnp.concatenate(cols[0:2], axis=1), _dup(cols[2]), _dup(cols[3]),
               _dup(cols[4]), _dup(cols[5]), w_g, w_qm]
    weights = [wi.astype(BF16) for wi in weights]
    q, kvc_tok, ks2, vs2, kw2, vw2, g, qm = _proj(
        xf, weights, [BF16, BF16, BF16, BF16, BF16, BF16, F32, BF16], 512, "proj_a",
        scales=[Q_SCALE, None, None, None, None, None, None, Q_SCALE])

    pe, w1, w2 = _cmp_weights(a_cmp_pe_k[0], a_cmp_w1_k[0], a_cmp_w2_k[0],
                              a_cmp_pe_v[0], a_cmp_w1_v[0], a_cmp_w2_v[0])
    kvc = _compress(kvc_tok.reshape(bsz, seq // (CMP_LEN // 2), (CMP_LEN // 2) * LANES), pe, w1, w2)

    r3 = lambda t: t.reshape(bsz, seq, t.shape[-1])
    o_main = _nsa_attention(r3(q), r3(g), kvc, r3(ks2), r3(vs2), r3(kw2), r3(vw2), bb, far, bc, ovl_t)
    (mkv,) = _proj(memf, [a_w_mem_kv[0].astype(BF16)], [BF16], 512, "proj_mem_a")
    o_mem = _mem_attention(r3(qm), mkv.reshape(bsz, n_mem, 2 * MEM_W))
    x1 = out_ln(o_main, o_mem, a_w_out[0], xf, 0)
    x1 = ffn(x1, 0)

    wb = b_w_in[0].astype(BF16)
    wkv = shared_w_kv.astype(BF16)
    q, qm, k, v, kmean = _proj_b(x1, wb[:, :MAIN_W], wb[:, MAIN_W:], wkv[:, :MAIN_W], wkv[:, MAIN_W:])
    n_blk = seq // MOBA_BLOCK
    kmean = jnp.pad(kmean.reshape(bsz, n_blk, MAIN_W), ((0, 0), (0, LANES - n_blk), (0, 0)))
    o_main = _moba_attention(r3(q), r3(k), r3(v), kmean, bb, far)
    (mkv,) = _proj(memf, [b_w_mem_kv[0].astype(BF16)], [BF16], 512, "proj_mem_b")
    o_mem = _mem_attention(r3(qm), mkv.reshape(bsz, n_mem, 2 * MEM_W))
    x2 = out_ln(o_main, o_mem, b_w_out[0], x1, 1)
    x2 = ffn(x2, 1)
    return x2.reshape(bsz, seq, d)
```

```python
import functools
import math

import numpy as np
import jax
import jax.numpy as jnp
from jax import lax
from jax.experimental import pallas as pl
from jax.experimental.pallas import tpu as pltpu

F32 = jnp.float32
BF16 = jnp.bfloat16

D_MODEL = 1024
HEAD_DIM = 64
N_HEADS = 12
N_PAIRS = N_HEADS // 2
MAIN_W = N_HEADS * HEAD_DIM
MEM_HEADS = 4
MEM_W = MEM_HEADS * HEAD_DIM
CMP_LEN = 32
CMP_STRIDE = 16
CMP_HIDDEN = 256
SLC_BLOCK = 64
SLC_TOPK = 16
WINDOW = 512
FORCE_SCORE = 1.0e4
MOBA_BLOCK = 256
MOBA_TOPK = 3
REL_BUCKETS = 32
REL_MAX_DIST = 128
D_FF = 2816
CONV_W = 3
DEPTH = 2
ALPHA = (2.0 * DEPTH) ** 0.25
LN_EPS = 1e-5
NEG_INF = -1e30
TINY = 1e-30
LOG2E = math.log2(math.e)
Q_SCALE = HEAD_DIM ** -0.5 * LOG2E

LANES = 128
TQ = 256
VMEM_LIMIT = 60 * 1024 * 1024
V_ROWS = HEAD_DIM + 8
QK_LOOKAHEAD = 6


def _dot(a, b):
    return jnp.dot(a, b, preferred_element_type=F32)


def _dot_nt(a, b):
    return lax.dot_general(a, b, (((1,), (1,)), ((), ())), preferred_element_type=F32)


def _params(sem, vmem=None):
    return pltpu.CompilerParams(dimension_semantics=sem, vmem_limit_bytes=vmem)


def _proj_kernel(x_ref, w_ref, *o_refs, scales):
    y = _dot(x_ref[...].astype(BF16), w_ref[...])
    col = 0
    for o_ref, scale in zip(o_refs, scales):
        part = y[:, col:col + o_ref.shape[1]]
        o_ref[...] = (part if scale is None else part * scale).astype(o_ref.dtype)
        col += o_ref.shape[1]


def _proj(x, weights, dtypes, tm, name, scales=None):
    m, k = x.shape
    scales = tuple(scales) if scales is not None else (None,) * len(weights)
    assert all(w.shape[1] % LANES == 0 for w in weights)
    w_all = jnp.concatenate(weights, axis=1) if len(weights) > 1 else weights[0]
    out_specs = [pl.BlockSpec((tm, w.shape[1]), lambda i: (i, 0)) for w in weights]
    out_shape = [jax.ShapeDtypeStruct((m, w.shape[1]), dt) for w, dt in zip(weights, dtypes)]
    return pl.pallas_call(
        functools.partial(_proj_kernel, scales=scales), grid=(m // tm,),
        in_specs=[pl.BlockSpec((tm, k), lambda i: (i, 0)), pl.BlockSpec(w_all.shape, lambda i: (0, 0))],
        out_specs=out_specs, out_shape=out_shape,
        compiler_params=_params(("arbitrary",), VMEM_LIMIT), name=name)(x, w_all)


def _projb_kernel(x_ref, wq_ref, wm_ref, wk_ref, wv_ref, q_ref, qm_ref, k_ref, v_ref, km_ref):
    xb = x_ref[...].astype(BF16)
    q_ref[...] = (_dot(xb, wq_ref[...]) * Q_SCALE).astype(BF16)
    qm_ref[...] = (_dot(xb, wm_ref[...]) * Q_SCALE).astype(BF16)
    k = _dot(xb, wk_ref[...])
    k_ref[...] = k.astype(BF16)
    km_ref[...] = jnp.mean(k, axis=0, keepdims=True)
    v_ref[...] = _dot(xb, wv_ref[...]).astype(BF16)


def _proj_b(x, wq, wm, wk, wv):
    m, k = x.shape
    nblk = m // MOBA_BLOCK
    row = lambda i: (i, 0)
    full = lambda i: (0, 0)
    return pl.pallas_call(
        _projb_kernel, grid=(nblk,),
        in_specs=[pl.BlockSpec((MOBA_BLOCK, k), row)] + [pl.BlockSpec(w.shape, full) for w in (wq, wm, wk, wv)],
        out_specs=[pl.BlockSpec((MOBA_BLOCK, MAIN_W), row), pl.BlockSpec((MOBA_BLOCK, MEM_W), row),
                   pl.BlockSpec((MOBA_BLOCK, MAIN_W), row), pl.BlockSpec((MOBA_BLOCK, MAIN_W), row),
                   pl.BlockSpec((None, 1, MAIN_W), lambda i: (i, 0, 0))],
        out_shape=[jax.ShapeDtypeStruct((m, MAIN_W), BF16), jax.ShapeDtypeStruct((m, MEM_W), BF16),
                   jax.ShapeDtypeStruct((m, MAIN_W), BF16), jax.ShapeDtypeStruct((m, MAIN_W), BF16),
                   jax.ShapeDtypeStruct((nblk, 1, MAIN_W), F32)],
        compiler_params=_params(("arbitrary",), VMEM_LIMIT), name="proj_b")(x, wq, wm, wk, wv)


def _bias_kernel(tbl_ref, bb_ref, bc_ref, far_ref):
    h = pl.program_id(0)

    def bias_of(dist):
        n = jnp.maximum(dist, 0)
        max_exact = REL_BUCKETS // 2
        nf = jnp.maximum(n, 1).astype(F32)
        large = max_exact + (jnp.log(nf / max_exact) / math.log(REL_MAX_DIST / max_exact)
                             * (REL_BUCKETS - max_exact)).astype(jnp.int32)
        large = jnp.minimum(large, REL_BUCKETS - 1)
        bucket = jnp.where(n < max_exact, n, large)
        out = jnp.zeros(dist.shape, F32)
        for kk in range(REL_BUCKETS):
            out = jnp.where(bucket == kk, tbl_ref[h * REL_BUCKETS + kk], out)
        return out * LOG2E

    key = lax.broadcasted_iota(jnp.int32, (TQ, TQ), 0)
    qry = lax.broadcasted_iota(jnp.int32, (TQ, TQ), 1)
    d0 = qry - key
    bb_ref[0] = jnp.where(d0 >= 0, bias_of(d0), NEG_INF)
    bb_ref[1] = bias_of(d0 + TQ)
    far = bias_of(d0 + 2 * TQ)
    bb_ref[2] = far
    bb_ref[3] = jnp.where(d0 + 2 * TQ < WINDOW, far, NEG_INF)
    far_ref[...] = far[0:1, :]

    nc = lax.broadcasted_iota(jnp.int32, (LANES, TQ), 0)
    qc = lax.broadcasted_iota(jnp.int32, (LANES, TQ), 1)
    n_cmp = (2048 - CMP_LEN) // CMP_STRIDE + 1
    for i in range(bc_ref.shape[0]):
        dc = i * TQ + qc - (nc * CMP_STRIDE + CMP_LEN - 1)
        bc_ref[i] = jnp.where((dc >= 0) & (nc < n_cmp), bias_of(dc), NEG_INF)


def _bias_tables(rel_bias, n_qt):
    tbl = rel_bias.T.reshape(-1)
    return pl.pallas_call(
        _bias_kernel, grid=(N_HEADS,),
        in_specs=[pl.BlockSpec(memory_space=pltpu.SMEM)],
        out_specs=[pl.BlockSpec((None, 4, TQ, TQ), lambda h: (h, 0, 0, 0)),
                   pl.BlockSpec((n_qt, None, LANES, TQ), lambda h: (0, h, 0, 0)),
                   pl.BlockSpec((None, 1, TQ), lambda h: (h, 0, 0))],
        out_shape=[jax.ShapeDtypeStruct((N_HEADS, 4, TQ, TQ), F32),
                   jax.ShapeDtypeStruct((n_qt, N_HEADS, LANES, TQ), F32),
                   jax.ShapeDtypeStruct((N_HEADS, 1, TQ), F32)],
        compiler_params=_params(("arbitrary",)), name="bias_tables")(tbl)


def _cmp_kernel(kv_ref, pe_ref, w1_ref, w2_ref, o_ref):
    x = kv_ref[...].astype(F32)
    lo = _dot((x + pe_ref[0:1, :]).astype(BF16), w1_ref[0])
    hi = _dot((x + pe_ref[1:2, :]).astype(BF16), w1_ref[1])
    nrow = x.shape[0]
    hid = lo + pltpu.roll(hi, nrow - 1, 0)
    o_ref[...] = _dot(jax.nn.gelu(hid).astype(BF16), w2_ref[...])


def _compress(kvr, pe, w1, w2):
    b, nrow, width = kvr.shape
    return pl.pallas_call(
        _cmp_kernel, grid=(b,),
        in_specs=[pl.BlockSpec((None, nrow, width), lambda i: (i, 0, 0)),
                  pl.BlockSpec(pe.shape, lambda i: (0, 0)),
                  pl.BlockSpec(w1.shape, lambda i: (0, 0, 0)),
                  pl.BlockSpec(w2.shape, lambda i: (0, 0))],
        out_specs=pl.BlockSpec((None, nrow, w2.shape[1]), lambda i: (i, 0, 0)),
        out_shape=jax.ShapeDtypeStruct((b, nrow, w2.shape[1]), F32),
        compiler_params=_params(("arbitrary",), VMEM_LIMIT), name="nsa_compress")(kvr, pe, w1, w2)


def _softmax_tile(state, c, s_t, v_t, vis=None, cbias=None):
    m_ref, acc_ref = state
    m = m_ref[c]
    n_blk = 1 if vis is None else vis.shape[0]
    rows = s_t.shape[0] // n_blk
    parts = [s_t[b * rows:(b + 1) * rows] for b in range(n_blk)]
    tile_max = None
    for b in range(n_blk):
        mb = jnp.max(parts[b], axis=0, keepdims=True)
        if vis is not None:
            mb = jnp.where(vis[b:b + 1] > 0.5, mb, NEG_INF)
        tile_max = mb if tile_max is None else jnp.maximum(tile_max, mb)
    if cbias is not None:
        tile_max = tile_max + cbias
    m_new = jnp.maximum(m, tile_max)
    shift = m_new if cbias is None else m_new - cbias
    p_parts = []
    for b in range(n_blk):
        off = shift if vis is None else jnp.where(vis[b:b + 1] > 0.5, shift, -NEG_INF)
        p_parts.append(jnp.exp2(parts[b] - off))
    p_t = p_parts[0] if n_blk == 1 else jnp.concatenate(p_parts, axis=0)
    alpha = jnp.exp2(m - m_new)
    acc_ref[c] = alpha * acc_ref[c] + _dot(v_t, p_t.astype(BF16))
    m_ref[c] = m_new


def _run_chains(chains, logits_fn, update_fn):
    pending = {}
    for idx, c in enumerate(chains[:QK_LOOKAHEAD]):
        pending[c] = logits_fn(c, idx)
    for idx, c in enumerate(chains):
        update_fn(c, pending.pop(c), idx % QK_LOOKAHEAD)
        if idx + QK_LOOKAHEAD < len(chains):
            nxt = chains[idx + QK_LOOKAHEAD]
            pending[nxt] = logits_fn(nxt, idx % QK_LOOKAHEAD)


def _pair_loop(lo, hi, tiles_fn):
    odd = (hi - lo) % 2

    def single(_, carry):
        tiles_fn([lo])
        return carry

    def double(r, carry):
        jj = lo + odd + 2 * r
        tiles_fn([jj, jj + 1])
        return carry

    lax.fori_loop(0, odd, single, 0)
    lax.fori_loop(0, (hi - lo) // 2, double, 0)


def _stage_logits(s_ref, slot, s_t):
    s_ref[slot] = s_t


def _init_state(state):
    m_ref, acc_ref = state
    m_ref[...] = jnp.full(m_ref.shape, NEG_INF, F32)
    acc_ref[...] = jnp.zeros(acc_ref.shape, F32)


def _chain_out(state, c):
    _, acc_ref = state
    return acc_ref[c, 0:HEAD_DIM, :] / jnp.maximum(acc_ref[c, HEAD_DIM:HEAD_DIM + 1, :], TINY)


def _values_with_ones(tile, row0):
    v_t = tile.astype(F32).T[row0:row0 + HEAD_DIM]
    pad = lax.broadcasted_iota(jnp.int32, (V_ROWS - HEAD_DIM, tile.shape[0]), 0)
    return jnp.concatenate([v_t, jnp.where(pad == 0, 1.0, 0.0)], axis=0).astype(BF16)


def _rows(ref, kt, cols=slice(None)):
    return ref[pl.ds(pl.multiple_of(kt * TQ, TQ), TQ), cols]


def _store_head_queries(q_ref, qh_ref, n_pairs):
    row = lax.broadcasted_iota(jnp.int32, (LANES, TQ), 0)
    halves = (row < HEAD_DIM, row >= HEAD_DIM)
    for pp in range(n_pairs):
        q2t = q_ref[:, pp * LANES:(pp + 1) * LANES].astype(F32).T
        for j in range(2):
            qh_ref[2 * pp + j] = jnp.where(halves[j], q2t, 0.0).astype(BF16)


def _nsa_kernel(q_ref, g_ref, kvc_ref, ks_ref, vs_ref, kw_ref, vw_ref, bb_ref, far_ref, bc_ref, ovl_ref,
                o_ref, vst_ref, vwt_ref, qh_ref, gs_ref, oc_ref, sel_ref, m_ref, acc_ref, s_ref):
    i = pl.program_id(1)
    n_kt = vst_ref.shape[0]
    state = (m_ref, acc_ref)

    @pl.when(i == 0)
    def _values():
        for kt in range(n_kt):
            rows = slice(kt * TQ, (kt + 1) * TQ)
            vst_ref[kt] = _values_with_ones(vs_ref[rows, :], 0)
            vwt_ref[kt] = _values_with_ones(vw_ref[rows, :], 0)

    _store_head_queries(q_ref, qh_ref, N_PAIRS)
    gs_ref[...] = jax.nn.sigmoid(g_ref[...]).T

    kc2 = kvc_ref[:, 0:LANES].astype(BF16)
    vc_t = kvc_ref[:, LANES:2 * LANES].T[0:HEAD_DIM].astype(BF16)
    psums = []

    def cmp_logits(h, slot):
        s_ref[slot, 0:LANES, :] = _dot(kc2, qh_ref[h])

    def cmp_update(h, _s, slot):
        bias = bc_ref[h]
        s = s_ref[slot, 0:LANES, :] + bias
        m = jnp.maximum(jnp.max(s, axis=0, keepdims=True), 0.5 * NEG_INF)
        e = jnp.exp2(s - m)
        pc = e / jnp.maximum(jnp.sum(e, axis=0, keepdims=True), TINY)
        psums[:] = [pc if not psums else psums[0] + pc]
        oc_ref[h] = gs_ref[3 * h:3 * h + 1, :] * _dot(vc_t, pc.astype(BF16))

    _run_chains(list(range(N_HEADS)), cmp_logits, cmp_update)
    psum = psums[0]
    p_hi = psum.astype(BF16)
    p_lo = (psum - p_hi.astype(F32)).astype(BF16)
    imp = _dot(ovl_ref[...], p_hi) + _dot(ovl_ref[...], p_lo)
    per_tile = TQ // SLC_BLOCK
    n_slc = n_kt * per_tile
    imp = imp[0:n_slc, :]
    sidx = lax.broadcasted_iota(jnp.int32, (n_slc, TQ), 0)
    t = i * TQ + lax.broadcasted_iota(jnp.int32, (n_slc, TQ), 1)
    cur = lax.shift_right_logical(t, 6)
    eligible = sidx <= cur
    forced = (sidx == 0) | (sidx == cur) | (sidx == cur - 1)
    score = jnp.where(eligible, jnp.where(forced, FORCE_SCORE, 0.0), NEG_INF) + imp
    cnt = jnp.zeros((n_slc, TQ), jnp.int32)
    for sp in range(n_slc):
        row = score[sp:sp + 1, :]
        better = (row > score) | ((row == score) & (sp < sidx))
        cnt = cnt + better.astype(jnp.int32)
    chosen = jnp.where(cnt < min(SLC_TOPK, n_slc), 1.0, 0.0)
    for kt in range(n_kt):
        sel_ref[kt, 0:per_tile, :] = chosen[kt * per_tile:(kt + 1) * per_tile, :]

    _init_state(state)
    win_tiles = WINDOW // TQ
    n_near = jnp.minimum(i, win_tiles) + 1

    def near_tiles(n_tiles):
        def logits(c, slot):
            kind, jj, h = c
            _stage_logits(s_ref, slot, _dot(_rows(ks_ref if kind == "s" else kw_ref, i - jj), qh_ref[h]))

        def update(c, _s, slot):
            kind, jj, h = c
            kt = i - jj
            if kind == "s":
                _softmax_tile(state, h, s_ref[slot] + bb_ref[h, min(jj, 2)], vst_ref[kt],
                              vis=sel_ref[kt, 0:per_tile, :])
            else:
                _softmax_tile(state, N_HEADS + h, s_ref[slot] + bb_ref[h, 3 if jj == win_tiles else jj], vwt_ref[kt])

        _run_chains([(kind, jj, h) for jj in range(n_tiles) for h in range(N_HEADS) for kind in ("s", "w")],
                    logits, update)

    for n_tiles in range(1, win_tiles + 2):
        last = n_tiles == win_tiles + 1
        pl.when((i >= n_tiles - 1) if last else (i == n_tiles - 1))(functools.partial(near_tiles, n_tiles))

    def far_tiles(jjs):
        kts = [i - jj for jj in jjs]
        _run_chains([(n, h) for n in range(len(jjs)) for h in range(N_HEADS)],
                    lambda c, slot: _dot(_rows(ks_ref, kts[c[0]]), qh_ref[c[1]]),
                    lambda c, s, slot: _softmax_tile(state, c[1], s, vst_ref[kts[c[0]]],
                                                     vis=sel_ref[kts[c[0]], 0:per_tile, :], cbias=far_ref[c[1]]))

    _pair_loop(n_near, i + 1, far_tiles)

    for pp in range(N_PAIRS):
        hs = (2 * pp, 2 * pp + 1)
        outs = []
        for h in hs:
            g1 = gs_ref[3 * h + 1:3 * h + 2, :]
            g2 = gs_ref[3 * h + 2:3 * h + 3, :]
            outs.append(oc_ref[h] + g1 * _chain_out(state, h) + g2 * _chain_out(state, N_HEADS + h))
        o_ref[:, pp * LANES:(pp + 1) * LANES] = jnp.concatenate(outs, axis=0).T.astype(o_ref.dtype)


def _nsa_attention(q, g, kvc, ks2, vs2, kw2, vw2, bb, far, bc, ovl_t):
    b, s, _ = q.shape
    n_qt = s // TQ
    per_b = lambda bi, i: (bi, 0, 0)
    return pl.pallas_call(
        _nsa_kernel, grid=(b, n_qt),
        in_specs=[pl.BlockSpec((None, TQ, MAIN_W), lambda bi, i: (bi, i, 0)),
                  pl.BlockSpec((None, TQ, LANES), lambda bi, i: (bi, i, 0)),
                  pl.BlockSpec((None,) + kvc.shape[1:], per_b),
                  pl.BlockSpec((None, s, LANES), per_b), pl.BlockSpec((None, s, LANES), per_b),
                  pl.BlockSpec((None, s, LANES), per_b), pl.BlockSpec((None, s, LANES), per_b),
                  pl.BlockSpec(bb.shape, lambda bi, i: (0, 0, 0, 0)),
                  pl.BlockSpec(far.shape, lambda bi, i: (0, 0, 0)),
                  pl.BlockSpec((None,) + bc.shape[1:], lambda bi, i: (i, 0, 0, 0)),
                  pl.BlockSpec(ovl_t.shape, lambda bi, i: (0, 0))],
        out_specs=pl.BlockSpec((None, TQ, MAIN_W), lambda bi, i: (bi, i, 0)),
        out_shape=jax.ShapeDtypeStruct((b, s, MAIN_W), BF16),
        scratch_shapes=[pltpu.VMEM((n_qt, V_ROWS, TQ), BF16), pltpu.VMEM((n_qt, V_ROWS, TQ), BF16),
                        pltpu.VMEM((N_HEADS, LANES, TQ), BF16), pltpu.VMEM((LANES, TQ), F32),
                        pltpu.VMEM((N_HEADS, HEAD_DIM, TQ), F32), pltpu.VMEM((n_qt, 8, TQ), F32),
                        pltpu.VMEM((2 * N_HEADS, 1, TQ), F32),
                        pltpu.VMEM((2 * N_HEADS, V_ROWS, TQ), F32), pltpu.VMEM((QK_LOOKAHEAD, TQ, TQ), F32)],
        compiler_params=_params(("arbitrary", "arbitrary"), VMEM_LIMIT),
        name="nsa_attention")(q, g, kvc, ks2, vs2, kw2, vw2, bb, far, bc, ovl_t)


def _moba_kernel(q_ref, k_ref, v_ref, km_ref, bb_ref, far_ref, o_ref, vt_ref, qh_ref, sel_ref,
                 m_ref, acc_ref, s_ref):
    i = pl.program_id(1)
    n_blk = vt_ref.shape[1]
    state = (m_ref, acc_ref)
    heads = list(range(N_HEADS))

    @pl.when(i == 0)
    def _values():
        for pp in range(N_PAIRS):
            for kt in range(n_blk):
                tile = v_ref[kt * TQ:(kt + 1) * TQ, pp * LANES:(pp + 1) * LANES]
                for j in range(2):
                    vt_ref[2 * pp + j, kt] = _values_with_ones(tile, j * HEAD_DIM)

    _store_head_queries(q_ref, qh_ref, N_PAIRS)
    blk = lax.broadcasted_iota(jnp.int32, (n_blk, TQ), 0)
    eligible = blk < i
    for pp in range(N_PAIRS):
        km = km_ref[:, pp * LANES:(pp + 1) * LANES].astype(BF16)
        for j in range(2):
            gate = _dot(km, qh_ref[2 * pp + j])[0:n_blk, :]
            gate = jnp.where(eligible, gate, NEG_INF)
            cnt = jnp.zeros((n_blk, TQ), jnp.int32)
            for n in range(n_blk):
                row = gate[n:n + 1, :]
                better = (row > gate) | ((row == gate) & (n < blk))
                cnt = cnt + better.astype(jnp.int32)
            chosen = (cnt < min(MOBA_TOPK, n_blk - 1)) & eligible
            vis = jnp.where(chosen | (blk == i), 1.0, 0.0)
            for n in range(n_blk):
                sel_ref[2 * pp + j, n] = vis[n:n + 1, :]
    _init_state(state)

    def tile_inputs(kt, h):
        pp = h // 2
        k = _rows(k_ref, kt, slice(pp * LANES, (pp + 1) * LANES))
        return k, vt_ref[h, kt], sel_ref[h, kt]

    def near_tile(jj):
        kt = i - jj
        def update(h, _s, slot):
            _, v_t, vis = tile_inputs(kt, h)
            _softmax_tile(state, h, s_ref[slot] + bb_ref[h, jj], v_t, vis=vis)

        def logits(h, slot):
            _stage_logits(s_ref, slot, _dot(tile_inputs(kt, h)[0], qh_ref[h]))

        _run_chains(heads, logits, update)

    near_tile(0)
    n_near = jnp.minimum(i, 1) + 1

    @pl.when(i >= 1)
    def _():
        near_tile(1)

    def far_tiles(jjs):
        kts = [i - jj for jj in jjs]

        def update(c, s, slot):
            _, v_t, vis = tile_inputs(kts[c[0]], c[1])
            _softmax_tile(state, c[1], s, v_t, vis=vis, cbias=far_ref[c[1]])

        _run_chains([(n, h) for n in range(len(jjs)) for h in heads],
                    lambda c, slot: _dot(tile_inputs(kts[c[0]], c[1])[0], qh_ref[c[1]]), update)

    _pair_loop(n_near, i + 1, far_tiles)

    for pp in range(N_PAIRS):
        out_t = jnp.concatenate([_chain_out(state, 2 * pp), _chain_out(state, 2 * pp + 1)], axis=0)
        o_ref[:, pp * LANES:(pp + 1) * LANES] = out_t.T.astype(o_ref.dtype)


def _moba_attention(q, k, v, kmean, bb, far):
    b, s, _ = q.shape
    n_qt = s // TQ
    per_b = lambda bi, i: (bi, 0, 0)
    return pl.pallas_call(
        _moba_kernel, grid=(b, n_qt),
        in_specs=[pl.BlockSpec((None, TQ, MAIN_W), lambda bi, i: (bi, i, 0)),
                  pl.BlockSpec((None, s, MAIN_W), per_b), pl.BlockSpec((None, s, MAIN_W), per_b),
                  pl.BlockSpec((None, LANES, MAIN_W), per_b),
                  pl.BlockSpec((N_HEADS, 2, TQ, TQ), lambda bi, i: (0, 0, 0, 0)),
                  pl.BlockSpec(far.shape, lambda bi, i: (0, 0, 0))],
        out_specs=pl.BlockSpec((None, TQ, MAIN_W), lambda bi, i: (bi, i, 0)),
        out_shape=jax.ShapeDtypeStruct((b, s, MAIN_W), BF16),
        scratch_shapes=[pltpu.VMEM((N_HEADS, n_qt, V_ROWS, TQ), BF16), pltpu.VMEM((N_HEADS, LANES, TQ), BF16),
                        pltpu.VMEM((N_HEADS, n_qt, 1, TQ), F32),
                        pltpu.VMEM((N_HEADS, 1, TQ), F32),
                        pltpu.VMEM((N_HEADS, V_ROWS, TQ), F32), pltpu.VMEM((QK_LOOKAHEAD, TQ, TQ), F32)],
        compiler_params=_params(("arbitrary", "arbitrary"), VMEM_LIMIT),
        name="moba_attention")(q, k, v, kmean, bb, far)


def _mem_kernel(q_ref, kv_ref, o_ref, qh_ref):
    n_sub = q_ref.shape[0] // TQ
    for sub in range(n_sub):
        _store_head_queries(q_ref.at[sub * TQ:(sub + 1) * TQ, :], qh_ref.at[sub], MEM_HEADS // 2)
    outs = {}

    def logits(c, slot):
        sub, h = c
        return _dot(kv_ref[:, (h // 2) * LANES:(h // 2 + 1) * LANES], qh_ref[sub, h])

    def update(c, s, slot):
        h = c[1]
        lanes = slice(MEM_W + (h // 2) * LANES, MEM_W + (h // 2 + 1) * LANES)
        v_t = kv_ref[:, lanes].astype(F32).T[(h % 2) * HEAD_DIM:(h % 2 + 1) * HEAD_DIM].astype(BF16)
        e = jnp.exp2(s - jnp.max(s, axis=0, keepdims=True))
        pr = e / jnp.sum(e, axis=0, keepdims=True)
        outs[c] = _dot(v_t, pr.astype(BF16))

    _run_chains([(sub, h) for sub in range(n_sub) for h in range(MEM_HEADS)], logits, update)
    for sub in range(n_sub):
        for pp in range(MEM_HEADS // 2):
            pair = jnp.concatenate([outs[(sub, 2 * pp)], outs[(sub, 2 * pp + 1)]], axis=0)
            o_ref[sub * TQ:(sub + 1) * TQ, pp * LANES:(pp + 1) * LANES] = pair.T.astype(o_ref.dtype)


MEM_TILES = 2


def _mem_attention(qm, mem_kv):
    b, s, _ = qm.shape
    n_mem = mem_kv.shape[1]
    rows = MEM_TILES * TQ
    return pl.pallas_call(
        _mem_kernel, grid=(b, s // rows),
        in_specs=[pl.BlockSpec((None, rows, MEM_W), lambda bi, i: (bi, i, 0)),
                  pl.BlockSpec((None, n_mem, 2 * MEM_W), lambda bi, i: (bi, 0, 0))],
        out_specs=pl.BlockSpec((None, rows, MEM_W), lambda bi, i: (bi, i, 0)),
        out_shape=jax.ShapeDtypeStruct((b, s, MEM_W), BF16),
        scratch_shapes=[pltpu.VMEM((MEM_TILES, MEM_HEADS, LANES, TQ), BF16)],
        compiler_params=_params(("arbitrary", "arbitrary")), name="mem_attention")(qm, mem_kv)


def _layer_norm(y, g, b):
    mu = jnp.mean(y, axis=-1, keepdims=True)
    var = jnp.mean(jnp.square(y - mu), axis=-1, keepdims=True)
    return (y - mu) * lax.rsqrt(var + LN_EPS) * g + b


def _mm_ln_kernel(*refs, n_in):
    a_refs, w_refs = refs[:n_in], refs[n_in:2 * n_in]
    x_ref, g_ref, b_ref, o_ref, y_ref = refs[2 * n_in:]

    @pl.when(pl.program_id(0) == 0)
    def _():
        y_ref[...] = jnp.zeros(y_ref.shape, F32)

    o_ref[...] = _layer_norm(ALPHA * x_ref[...] + y_ref[...], g_ref[...], b_ref[...])
    acc = _dot(a_refs[0][...], w_refs[0][...])
    for a_ref, w_ref in zip(a_refs[1:], w_refs[1:]):
        acc = acc + _dot(a_ref[...], w_ref[...])
    y_ref[...] = acc


def _mm_ln(acts, weights, x, g, b, tm, name):
    m, d = x.shape
    n_t = m // tm
    cur = lambda t: (jnp.minimum(t, n_t - 1), 0)
    prev = lambda t: (jnp.maximum(t - 1, 0), 0)
    full = lambda t: (0, 0)
    return pl.pallas_call(
        functools.partial(_mm_ln_kernel, n_in=len(acts)), grid=(n_t + 1,),
        in_specs=[pl.BlockSpec((tm, a.shape[1]), cur) for a in acts]
                 + [pl.BlockSpec(w.shape, full) for w in weights]
                 + [pl.BlockSpec((tm, d), prev), pl.BlockSpec((1, d), full), pl.BlockSpec((1, d), full)],
        out_specs=pl.BlockSpec((tm, d), prev),
        out_shape=jax.ShapeDtypeStruct((m, d), F32),
        scratch_shapes=[pltpu.VMEM((tm, d), F32)],
        compiler_params=_params(("arbitrary",), VMEM_LIMIT), name=name)(*acts, *weights, x, g, b)


FF_CHUNK = 256
FF_ROWS = 512
SUBLANES = 8


def _ffn_kernel(x_ref, wa_ref, wb_ref, cw_ref, cb_ref, wo_ref, wol_ref, g_ref, b_ref, o_ref, xb_ref, hid_ref, *,
                n_c):
    c = pl.program_id(1)

    @pl.when(c == 0)
    def _():
        xb_ref[...] = x_ref[...].astype(BF16)
        hid_ref[1] = jnp.zeros(hid_ref.shape[1:], BF16)
        o_ref[...] = jnp.zeros(o_ref.shape, F32)

    xb = xb_ref[...]
    a = _dot(xb, wa_ref[...])
    o_ref[...] += _dot(hid_ref[(c + 1) % 2], wo_ref[...])
    gate = _dot(xb, wb_ref[...])
    ext = jnp.concatenate([jnp.zeros((SUBLANES, a.shape[1]), F32), a], axis=0)
    a1 = pltpu.roll(ext, 1, 0)[SUBLANES:]
    a2 = pltpu.roll(ext, 2, 0)[SUBLANES:]
    conv = cw_ref[0:1, :] * a2 + cw_ref[1:2, :] * a1 + cw_ref[2:3, :] * a + cb_ref[...]
    hid_ref[c % 2] = (jax.nn.gelu(conv) * gate).astype(BF16)

    @pl.when(c == n_c - 1)
    def _():
        for r0 in range(0, o_ref.shape[0], FF_ROWS):
            rows = slice(r0, r0 + FF_ROWS)
            y = o_ref[rows, :] + _dot(hid_ref[(n_c - 1) % 2, rows, :], wol_ref[...])
            o_ref[rows, :] = _layer_norm(ALPHA * x_ref[rows, :] + y, g_ref[...], b_ref[...])


def _ffn_ln(x, w_in, conv_w, conv_b, w_out, g, b):
    bsz, s, d = x.shape
    n_c = D_FF // FF_CHUNK
    prev = lambda c: jnp.maximum(c - 1, 0)
    return pl.pallas_call(
        functools.partial(_ffn_kernel, n_c=n_c), grid=(bsz, n_c),
        in_specs=[pl.BlockSpec((None, s, d), lambda bi, c: (bi, 0, 0)),
                  pl.BlockSpec((d, FF_CHUNK), lambda bi, c: (0, c)),
                  pl.BlockSpec((d, FF_CHUNK), lambda bi, c: (0, n_c + c)),
                  pl.BlockSpec((CONV_W, FF_CHUNK), lambda bi, c: (0, c)),
                  pl.BlockSpec((1, FF_CHUNK), lambda bi, c: (0, c)),
                  pl.BlockSpec((FF_CHUNK, d), lambda bi, c: (prev(c), 0)),
                  pl.BlockSpec((FF_CHUNK, d), lambda bi, c: (n_c - 1, 0)),
                  pl.BlockSpec((1, d), lambda bi, c: (0, 0)), pl.BlockSpec((1, d), lambda bi, c: (0, 0))],
        out_specs=pl.BlockSpec((None, s, d), lambda bi, c: (bi, 0, 0)),
        out_shape=jax.ShapeDtypeStruct((bsz, s, d), F32),
        scratch_shapes=[pltpu.VMEM((s, d), BF16), pltpu.VMEM((2, s, FF_CHUNK), BF16)],
        compiler_params=_params(("arbitrary", "arbitrary"), VMEM_LIMIT),
        name="ffn_ln")(x, w_in, w_in, conv_w, conv_b, w_out, w_out, g, b)


def _dup(w):
    return jnp.concatenate([w, w], axis=1)


def _cmp_weights(pe_k, w1_k, w2_k, pe_v, w1_v, w2_v):
    half = CMP_LEN // 2
    zk = jnp.zeros((half, HEAD_DIM, CMP_HIDDEN), F32)

    def w1_half(sl):
        wk = w1_k.reshape(CMP_LEN, HEAD_DIM, CMP_HIDDEN)[sl]
        wv = w1_v.reshape(CMP_LEN, HEAD_DIM, CMP_HIDDEN)[sl]
        top = jnp.concatenate([wk, zk], axis=2)
        bot = jnp.concatenate([zk, wv], axis=2)
        return jnp.concatenate([top, bot], axis=1).reshape(half * 2 * HEAD_DIM, 2 * CMP_HIDDEN)

    w1 = jnp.stack([w1_half(slice(0, half)), w1_half(slice(half, CMP_LEN))]).astype(BF16)
    pe = jnp.concatenate([pe_k, pe_v], axis=1)
    pe = jnp.stack([pe[:half].reshape(-1), pe[half:].reshape(-1)])
    zo = jnp.zeros((CMP_HIDDEN, 2 * HEAD_DIM), F32)
    w2 = jnp.concatenate([jnp.concatenate([_dup(w2_k), zo], axis=1),
                          jnp.concatenate([zo, _dup(w2_v)], axis=1)], axis=0).astype(BF16)
    return pe, w1, w2


def _overlap_matrix(seq):
    n_cmp = (seq - CMP_LEN) // CMP_STRIDE + 1
    n_slc = seq // SLC_BLOCK
    start = np.arange(LANES) * CMP_STRIDE
    bs = np.arange(LANES) * SLC_BLOCK
    ovl_t = ((start[None, :] < bs[:, None] + SLC_BLOCK) & (start[None, :] + CMP_LEN > bs[:, None])
             & (np.arange(LANES)[None, :] < n_cmp) & (np.arange(LANES)[:, None] < n_slc))
    return jnp.asarray(ovl_t, BF16)


def kernel(x, mem, rel_bias, a_w_in, a_cmp_pe_k, a_cmp_w1_k, a_cmp_w2_k, a_cmp_pe_v, a_cmp_w1_v, a_cmp_w2_v,
           a_w_mem_kv, a_w_out, shared_w_kv, b_w_in, b_w_mem_kv, b_w_out, ln1_g, ln1_b, ln2_g, ln2_b,
           ffn_w_in, ffn_conv_w, ffn_conv_b, ffn_w_out):
    bsz, seq, d = x.shape
    n_mem = mem.shape[1]
    m = bsz * seq
    assert (seq, d) == (2048, D_MODEL) and seq % TQ == 0
    n_qt = seq // TQ

    bb, bc, far = _bias_tables(rel_bias, n_qt)
    ovl_t = _overlap_matrix(seq)
    memf = mem.reshape(bsz * n_mem, d)
    xf = x.reshape(m, d)

    def ffn(xcur, layer):
        return _ffn_ln(xcur.reshape(bsz, seq, d), ffn_w_in[layer].astype(BF16), ffn_conv_w[layer],
                       ffn_conv_b[layer][None, :], ffn_w_out[layer].astype(BF16),
                       ln2_g[layer][None, :], ln2_b[layer][None, :]).reshape(m, d)

    def out_ln(o_main, o_mem, w_out, xcur, layer):
        w = w_out.astype(BF16)
        return _mm_ln([o_main.reshape(m, MAIN_W), o_mem.reshape(m, MEM_W)], [w[:MAIN_W], w[MAIN_W:]], xcur,
                      ln1_g[layer][None, :], ln1_b[layer][None, :], 512, "outproj_ln")

    w = a_w_in[0]
    c0 = MAIN_W
    cols = [w[:, c0 + kk * HEAD_DIM:c0 + (kk + 1) * HEAD_DIM] for kk in range(6)]
    c_g = c0 + 6 * HEAD_DIM
    w_g = jnp.pad(w[:, c_g:c_g + 3 * N_HEADS], ((0, 0), (0, LANES - 3 * N_HEADS)))
    w_qm = w[:, c_g + 3 * N_HEADS:]
    weights = [w[:, :MAIN_W], jnp.concatenate(cols[0:2], axis=1), _dup(cols[2]), _dup(cols[3]),
               _dup(cols[4]), _dup(cols[5]), w_g, w_qm]
    weights = [wi.astype(BF16) for wi in weights]
    q, kvc_tok, ks2, vs2, kw2, vw2, g, qm = _proj(
        xf, weights, [BF16, BF16, BF16, BF16, BF16, BF16, F32, BF16], 512, "proj_a",
        scales=[Q_SCALE, None, None, None, None, None, None, Q_SCALE])

    pe, w1, w2 = _cmp_weights(a_cmp_pe_k[0], a_cmp_w1_k[0], a_cmp_w2_k[0],
                              a_cmp_pe_v[0], a_cmp_w1_v[0], a_cmp_w2_v[0])
    kvc = _compress(kvc_tok.reshape(bsz, seq // (CMP_LEN // 2), (CMP_LEN // 2) * LANES), pe, w1, w2)

    r3 = lambda t: t.reshape(bsz, seq, t.shape[-1])
    o_main = _nsa_attention(r3(q), r3(g), kvc, r3(ks2), r3(vs2), r3(kw2), r3(vw2), bb, far, bc, ovl_t)
    (mkv,) = _proj(memf, [a_w_mem_kv[0].astype(BF16)], [BF16], 512, "proj_mem_a")
    o_mem = _mem_attention(r3(qm), mkv.reshape(bsz, n_mem, 2 * MEM_W))
    x1 = out_ln(o_main, o_mem, a_w_out[0], xf, 0)
    x1 = ffn(x1, 0)

    wb = b_w_in[0].astype(BF16)
    wkv = shared_w_kv.astype(BF16)
    q, qm, k, v, kmean = _proj_b(x1, wb[:, :MAIN_W], wb[:, MAIN_W:], wkv[:, :MAIN_W], wkv[:, MAIN_W:])
    n_blk = seq // MOBA_BLOCK
    kmean = jnp.pad(kmean.reshape(bsz, n_blk, MAIN_W), ((0, 0), (0, LANES - n_blk), (0, 0)))
    o_main = _moba_attention(r3(q), r3(k), r3(v), kmean, bb, far)
    (mkv,) = _proj(memf, [b_w_mem_kv[0].astype(BF16)], [BF16], 512, "proj_mem_b")
    o_mem = _mem_attention(r3(qm), mkv.reshape(bsz, n_mem, 2 * MEM_W))
    x2 = out_ln(o_main, o_mem, b_w_out[0], x1, 1)
    x2 = ffn(x2, 1)
    return x2.reshape(bsz, seq, d)
```

```python
import functools
import math

import numpy as np
import jax
import jax.numpy as jnp
from jax import lax
from jax.experimental import pallas as pl
from jax.experimental.pallas import tpu as pltpu

F32 = jnp.float32
BF16 = jnp.bfloat16

D_MODEL = 1024
HEAD_DIM = 64
N_HEADS = 12
N_PAIRS = N_HEADS // 2
MAIN_W = N_HEADS * HEAD_DIM
MEM_HEADS = 4
MEM_W = MEM_HEADS * HEAD_DIM
CMP_LEN = 32
CMP_STRIDE = 16
CMP_HIDDEN = 256
SLC_BLOCK = 64
SLC_TOPK = 16
WINDOW = 512
FORCE_SCORE = 1.0e4
MOBA_BLOCK = 256
MOBA_TOPK = 3
REL_BUCKETS = 32
REL_MAX_DIST = 128
D_FF = 2816
CONV_W = 3
DEPTH = 2
ALPHA = (2.0 * DEPTH) ** 0.25
LN_EPS = 1e-5
NEG_INF = -1e30
TINY = 1e-30
LOG2E = math.log2(math.e)
Q_SCALE = HEAD_DIM ** -0.5 * LOG2E

LANES = 128
TQ = 256
VMEM_LIMIT = 60 * 1024 * 1024
V_ROWS = HEAD_DIM + 8
QK_LOOKAHEAD = 6


def _dot(a, b):
    return jnp.dot(a, b, preferred_element_type=F32)


def _dot_nt(a, b):
    return lax.dot_general(a, b, (((1,), (1,)), ((), ())), preferred_element_type=F32)


def _params(sem, vmem=None):
    return pltpu.CompilerParams(dimension_semantics=sem, vmem_limit_bytes=vmem)


def _proj_kernel(x_ref, w_ref, *o_refs, scales):
    y = _dot(x_ref[...].astype(BF16), w_ref[...])
    col = 0
    for o_ref, scale in zip(o_refs, scales):
        part = y[:, col:col + o_ref.shape[1]]
        o_ref[...] = (part if scale is None else part * scale).astype(o_ref.dtype)
        col += o_ref.shape[1]


def _proj(x, weights, dtypes, tm, name, scales=None):
    m, k = x.shape
    scales = tuple(scales) if scales is not None else (None,) * len(weights)
    assert all(w.shape[1] % LANES == 0 for w in weights)
    w_all = jnp.concatenate(weights, axis=1) if len(weights) > 1 else weights[0]
    out_specs = [pl.BlockSpec((tm, w.shape[1]), lambda i: (i, 0)) for w in weights]
    out_shape = [jax.ShapeDtypeStruct((m, w.shape[1]), dt) for w, dt in zip(weights, dtypes)]
    return pl.pallas_call(
        functools.partial(_proj_kernel, scales=scales), grid=(m // tm,),
        in_specs=[pl.BlockSpec((tm, k), lambda i: (i, 0)), pl.BlockSpec(w_all.shape, lambda i: (0, 0))],
        out_specs=out_specs, out_shape=out_shape,
        compiler_params=_params(("arbitrary",), VMEM_LIMIT), name=name)(x, w_all)


def _projb_kernel(x_ref, wq_ref, wm_ref, wk_ref, wv_ref, q_ref, qm_ref, k_ref, v_ref, km_ref):
    xb = x_ref[...].astype(BF16)
    q_ref[...] = (_dot(xb, wq_ref[...]) * Q_SCALE).astype(BF16)
    qm_ref[...] = (_dot(xb, wm_ref[...]) * Q_SCALE).astype(BF16)
    k = _dot(xb, wk_ref[...])
    k_ref[...] = k.astype(BF16)
    km_ref[...] = jnp.mean(k, axis=0, keepdims=True)
    v_ref[...] = _dot(xb, wv_ref[...]).astype(BF16)


def _proj_b(x, wq, wm, wk, wv):
    m, k = x.shape
    nblk = m // MOBA_BLOCK
    row = lambda i: (i, 0)
    full = lambda i: (0, 0)
    return pl.pallas_call(
        _projb_kernel, grid=(nblk,),
        in_specs=[pl.BlockSpec((MOBA_BLOCK, k), row)] + [pl.BlockSpec(w.shape, full) for w in (wq, wm, wk, wv)],
        out_specs=[pl.BlockSpec((MOBA_BLOCK, MAIN_W), row), pl.BlockSpec((MOBA_BLOCK, MEM_W), row),
                   pl.BlockSpec((MOBA_BLOCK, MAIN_W), row), pl.BlockSpec((MOBA_BLOCK, MAIN_W), row),
                   pl.BlockSpec((None, 1, MAIN_W), lambda i: (i, 0, 0))],
        out_shape=[jax.ShapeDtypeStruct((m, MAIN_W), BF16), jax.ShapeDtypeStruct((m, MEM_W), BF16),
                   jax.ShapeDtypeStruct((m, MAIN_W), BF16), jax.ShapeDtypeStruct((m, MAIN_W), BF16),
                   jax.ShapeDtypeStruct((nblk, 1, MAIN_W), F32)],
        compiler_params=_params(("arbitrary",), VMEM_LIMIT), name="proj_b")(x, wq, wm, wk, wv)


def _bias_kernel(tbl_ref, bb_ref, bc_ref, far_ref):
    h = pl.program_id(0)

    def bias_of(dist):
        n = jnp.maximum(dist, 0)
        max_exact = REL_BUCKETS // 2
        nf = jnp.maximum(n, 1).astype(F32)
        large = max_exact + (jnp.log(nf / max_exact) / math.log(REL_MAX_DIST / max_exact)
                             * (REL_BUCKETS - max_exact)).astype(jnp.int32)
        large = jnp.minimum(large, REL_BUCKETS - 1)
        bucket = jnp.where(n < max_exact, n, large)
        out = jnp.zeros(dist.shape, F32)
        for kk in range(REL_BUCKETS):
            out = jnp.where(bucket == kk, tbl_ref[h * REL_BUCKETS + kk], out)
        return out * LOG2E

    key = lax.broadcasted_iota(jnp.int32, (TQ, TQ), 0)
    qry = lax.broadcasted_iota(jnp.int32, (TQ, TQ), 1)
    d0 = qry - key
    bb_ref[0] = jnp.where(d0 >= 0, bias_of(d0), NEG_INF)
    bb_ref[1] = bias_of(d0 + TQ)
    far = bias_of(d0 + 2 * TQ)
    bb_ref[2] = far
    bb_ref[3] = jnp.where(d0 + 2 * TQ < WINDOW, far, NEG_INF)
    far_ref[...] = far[0:1, :]

    nc = lax.broadcasted_iota(jnp.int32, (LANES, TQ), 0)
    qc = lax.broadcasted_iota(jnp.int32, (LANES, TQ), 1)
    n_cmp = (2048 - CMP_LEN) // CMP_STRIDE + 1
    for i in range(bc_ref.shape[0]):
        dc = i * TQ + qc - (nc * CMP_STRIDE + CMP_LEN - 1)
        bc_ref[i] = jnp.where((dc >= 0) & (nc < n_cmp), bias_of(dc), NEG_INF)


def _bias_tables(rel_bias, n_qt):
    tbl = rel_bias.T.reshape(-1)
    return pl.pallas_call(
        _bias_kernel, grid=(N_HEADS,),
        in_specs=[pl.BlockSpec(memory_space=pltpu.SMEM)],
        out_specs=[pl.BlockSpec((None, 4, TQ, TQ), lambda h: (h, 0, 0, 0)),
                   pl.BlockSpec((n_qt, None, LANES, TQ), lambda h: (0, h, 0, 0)),
                   pl.BlockSpec((None, 1, TQ), lambda h: (h, 0, 0))],
        out_shape=[jax.ShapeDtypeStruct((N_HEADS, 4, TQ, TQ), F32),
                   jax.ShapeDtypeStruct((n_qt, N_HEADS, LANES, TQ), F32),
                   jax.ShapeDtypeStruct((N_HEADS, 1, TQ), F32)],
        compiler_params=_params(("arbitrary",)), name="bias_tables")(tbl)


def _cmp_kernel(kv_ref, pe_ref, w1_ref, w2_ref, o_ref):
    x = kv_ref[...].astype(F32)
    lo = _dot((x + pe_ref[0:1, :]).astype(BF16), w1_ref[0])
    hi = _dot((x + pe_ref[1:2, :]).astype(BF16), w1_ref[1])
    nrow = x.shape[0]
    hid = lo + pltpu.roll(hi, nrow - 1, 0)
    o_ref[...] = _dot(jax.nn.gelu(hid).astype(BF16), w2_ref[...])


def _compress(kvr, pe, w1, w2):
    b, nrow, width = kvr.shape
    return pl.pallas_call(
        _cmp_kernel, grid=(b,),
        in_specs=[pl.BlockSpec((None, nrow, width), lambda i: (i, 0, 0)),
                  pl.BlockSpec(pe.shape, lambda i: (0, 0)),
                  pl.BlockSpec(w1.shape, lambda i: (0, 0, 0)),
                  pl.BlockSpec(w2.shape, lambda i: (0, 0))],
        out_specs=pl.BlockSpec((None, nrow, w2.shape[1]), lambda i: (i, 0, 0)),
        out_shape=jax.ShapeDtypeStruct((b, nrow, w2.shape[1]), F32),
        compiler_params=_params(("arbitrary",), VMEM_LIMIT), name="nsa_compress")(kvr, pe, w1, w2)


def _softmax_tile(state, c, s_t, v_t, vis=None, cbias=None):
    m_ref, acc_ref = state
    m = m_ref[c]
    n_blk = 1 if vis is None else vis.shape[0]
    rows = s_t.shape[0] // n_blk
    parts = [s_t[b * rows:(b + 1) * rows] for b in range(n_blk)]
    tile_max = None
    for b in range(n_blk):
        mb = jnp.max(parts[b], axis=0, keepdims=True)
        if vis is not None:
            mb = jnp.where(vis[b:b + 1] > 0.5, mb, NEG_INF)
        tile_max = mb if tile_max is None else jnp.maximum(tile_max, mb)
    if cbias is not None:
        tile_max = tile_max + cbias
    m_new = jnp.maximum(m, tile_max)
    shift = m_new if cbias is None else m_new - cbias
    p_parts = []
    for b in range(n_blk):
        off = shift if vis is None else jnp.where(vis[b:b + 1] > 0.5, shift, -NEG_INF)
        p_parts.append(jnp.exp2(parts[b] - off))
    p_t = p_parts[0] if n_blk == 1 else jnp.concatenate(p_parts, axis=0)
    alpha = jnp.exp2(m - m_new)
    acc_ref[c] = alpha * acc_ref[c] + _dot(v_t, p_t.astype(BF16))
    m_ref[c] = m_new


def _run_chains(chains, logits_fn, update_fn):
    pending = {}
    for idx, c in enumerate(chains[:QK_LOOKAHEAD]):
        pending[c] = logits_fn(c, idx)
    for idx, c in enumerate(chains):
        update_fn(c, pending.pop(c), idx % QK_LOOKAHEAD)
        if idx + QK_LOOKAHEAD < len(chains):
            nxt = chains[idx + QK_LOOKAHEAD]
            pending[nxt] = logits_fn(nxt, idx % QK_LOOKAHEAD)


def _pair_loop(lo, hi, tiles_fn):
    odd = (hi - lo) % 2

    def single(_, carry):
        tiles_fn([lo])
        return carry

    def double(r, carry):
        jj = lo + odd + 2 * r
        tiles_fn([jj, jj + 1])
        return carry

    lax.fori_loop(0, odd, single, 0)
    lax.fori_loop(0, (hi - lo) // 2, double, 0)


def _stage_logits(s_ref, slot, s_t):
    s_ref[slot] = s_t


def _init_state(state):
    m_ref, acc_ref = state
    m_ref[...] = jnp.full(m_ref.shape, NEG_INF, F32)
    acc_ref[...] = jnp.zeros(acc_ref.shape, F32)


def _chain_out(state, c):
    _, acc_ref = state
    return acc_ref[c, 0:HEAD_DIM, :] / jnp.maximum(acc_ref[c, HEAD_DIM:HEAD_DIM + 1, :], TINY)


def _values_with_ones(tile, row0):
    v_t = tile.astype(F32).T[row0:row0 + HEAD_DIM]
    pad = lax.broadcasted_iota(jnp.int32, (V_ROWS - HEAD_DIM, tile.shape[0]), 0)
    return jnp.concatenate([v_t, jnp.where(pad == 0, 1.0, 0.0)], axis=0).astype(BF16)


def _rows(ref, kt, cols=slice(None)):
    return ref[pl.ds(pl.multiple_of(kt * TQ, TQ), TQ), cols]


def _store_head_queries(q_ref, qh_ref, n_pairs):
    row = lax.broadcasted_iota(jnp.int32, (LANES, TQ), 0)
    halves = (row < HEAD_DIM, row >= HEAD_DIM)
    for pp in range(n_pairs):
        q2t = q_ref[:, pp * LANES:(pp + 1) * LANES].astype(F32).T
        for j in range(2):
            qh_ref[2 * pp + j] = jnp.where(halves[j], q2t, 0.0).astype(BF16)


def _nsa_kernel(q_ref, g_ref, kvc_ref, ks_ref, vs_ref, kw_ref, vw_ref, bb_ref, far_ref, bc_ref, ovl_ref,
                o_ref, vst_ref, vwt_ref, qh_ref, gs_ref, oc_ref, sel_ref, m_ref, acc_ref, s_ref):
    i = pl.program_id(1)
    n_kt = vst_ref.shape[0]
    state = (m_ref, acc_ref)

    @pl.when(i == 0)
    def _values():
        for kt in range(n_kt):
            rows = slice(kt * TQ, (kt + 1) * TQ)
            vst_ref[kt] = _values_with_ones(vs_ref[rows, :], 0)
            vwt_ref[kt] = _values_with_ones(vw_ref[rows, :], 0)

    per_tile = TQ // SLC_BLOCK
    win_tiles = WINDOW // TQ
    n_near = jnp.minimum(i, win_tiles) + 1

    def select_blocks():
        _store_head_queries(q_ref, qh_ref, N_PAIRS)
        gs_ref[...] = jax.nn.sigmoid(g_ref[...]).T
        kc2 = kvc_ref[:, 0:LANES].astype(BF16)
        vc_t = kvc_ref[:, LANES:2 * LANES].T[0:HEAD_DIM].astype(BF16)
        psums = []

        def cmp_logits(h, slot):
            s_ref[slot, 0:LANES, :] = _dot(kc2, qh_ref[h])

        def cmp_update(h, _s, slot):
            bias = bc_ref[h]
            s = s_ref[slot, 0:LANES, :] + bias
            m = jnp.maximum(jnp.max(s, axis=0, keepdims=True), 0.5 * NEG_INF)
            e = jnp.exp2(s - m)
            pc = e / jnp.maximum(jnp.sum(e, axis=0, keepdims=True), TINY)
            psums[:] = [pc if not psums else psums[0] + pc]
            oc_ref[h] = gs_ref[3 * h:3 * h + 1, :] * _dot(vc_t, pc.astype(BF16))

        _run_chains(list(range(N_HEADS)), cmp_logits, cmp_update)
        psum = psums[0]
        p_hi = psum.astype(BF16)
        p_lo = (psum - p_hi.astype(F32)).astype(BF16)
        imp = _dot(ovl_ref[...], p_hi) + _dot(ovl_ref[...], p_lo)
        n_slc = n_kt * per_tile
        imp = imp[0:n_slc, :]
        sidx = lax.broadcasted_iota(jnp.int32, (n_slc, TQ), 0)
        t = i * TQ + lax.broadcasted_iota(jnp.int32, (n_slc, TQ), 1)
        cur = lax.shift_right_logical(t, 6)
        eligible = sidx <= cur
        forced = (sidx == 0) | (sidx == cur) | (sidx == cur - 1)
        score = jnp.where(eligible, jnp.where(forced, FORCE_SCORE, 0.0), NEG_INF) + imp
        cnt = jnp.zeros((n_slc, TQ), jnp.int32)
        for sp in range(n_slc):
            row = score[sp:sp + 1, :]
            better = (row > score) | ((row == score) & (sp < sidx))
            cnt = cnt + better.astype(jnp.int32)
        chosen = jnp.where(cnt < min(SLC_TOPK, n_slc), 1.0, 0.0)
        for kt in range(n_kt):
            sel_ref[kt, 0:per_tile, :] = chosen[kt * per_tile:(kt + 1) * per_tile, :]
        _init_state(state)

    def near_tiles(n_tiles):
        def logits(c, slot):
            kind, jj, h = c
            _stage_logits(s_ref, slot, _dot(_rows(ks_ref if kind == "s" else kw_ref, i - jj), qh_ref[h]))

        def update(c, _s, slot):
            kind, jj, h = c
            kt = i - jj
            if kind == "s":
                _softmax_tile(state, h, s_ref[slot] + bb_ref[h, min(jj, 2)], vst_ref[kt],
                              vis=sel_ref[kt, 0:per_tile, :])
            else:
                _softmax_tile(state, N_HEADS + h, s_ref[slot] + bb_ref[h, 3 if jj == win_tiles else jj], vwt_ref[kt])

        _run_chains([(kind, jj, h) for jj in range(n_tiles) for h in range(N_HEADS) for kind in ("s", "w")],
                    logits, update)

    def front(n_tiles):
        select_blocks()
        near_tiles(n_tiles)

    for n_tiles in range(1, win_tiles + 2):
        last = n_tiles == win_tiles + 1
        pl.when((i >= n_tiles - 1) if last else (i == n_tiles - 1))(functools.partial(front, n_tiles))

    def far_tiles(jjs):
        kts = [i - jj for jj in jjs]
        _run_chains([(n, h) for n in range(len(jjs)) for h in range(N_HEADS)],
                    lambda c, slot: _dot(_rows(ks_ref, kts[c[0]]), qh_ref[c[1]]),
                    lambda c, s, slot: _softmax_tile(state, c[1], s, vst_ref[kts[c[0]]],
                                                     vis=sel_ref[kts[c[0]], 0:per_tile, :], cbias=far_ref[c[1]]))

    _pair_loop(n_near, i + 1, far_tiles)

    for pp in range(N_PAIRS):
        hs = (2 * pp, 2 * pp + 1)
        outs = []
        for h in hs:
            g1 = gs_ref[3 * h + 1:3 * h + 2, :]
            g2 = gs_ref[3 * h + 2:3 * h + 3, :]
            outs.append(oc_ref[h] + g1 * _chain_out(state, h) + g2 * _chain_out(state, N_HEADS + h))
        o_ref[:, pp * LANES:(pp + 1) * LANES] = jnp.concatenate(outs, axis=0).T.astype(o_ref.dtype)


def _nsa_attention(q, g, kvc, ks2, vs2, kw2, vw2, bb, far, bc, ovl_t):
    b, s, _ = q.shape
    n_qt = s // TQ
    per_b = lambda bi, i: (bi, 0, 0)
    return pl.pallas_call(
        _nsa_kernel, grid=(b, n_qt),
        in_specs=[pl.BlockSpec((None, TQ, MAIN_W), lambda bi, i: (bi, i, 0)),
                  pl.BlockSpec((None, TQ, LANES), lambda bi, i: (bi, i, 0)),
                  pl.BlockSpec((None,) + kvc.shape[1:], per_b),
                  pl.BlockSpec((None, s, LANES), per_b), pl.BlockSpec((None, s, LANES), per_b),
                  pl.BlockSpec((None, s, LANES), per_b), pl.BlockSpec((None, s, LANES), per_b),
                  pl.BlockSpec(bb.shape, lambda bi, i: (0, 0, 0, 0)),
                  pl.BlockSpec(far.shape, lambda bi, i: (0, 0, 0)),
                  pl.BlockSpec((None,) + bc.shape[1:], lambda bi, i: (i, 0, 0, 0)),
                  pl.BlockSpec(ovl_t.shape, lambda bi, i: (0, 0))],
        out_specs=pl.BlockSpec((None, TQ, MAIN_W), lambda bi, i: (bi, i, 0)),
        out_shape=jax.ShapeDtypeStruct((b, s, MAIN_W), BF16),
        scratch_shapes=[pltpu.VMEM((n_qt, V_ROWS, TQ), BF16), pltpu.VMEM((n_qt, V_ROWS, TQ), BF16),
                        pltpu.VMEM((N_HEADS, LANES, TQ), BF16), pltpu.VMEM((LANES, TQ), F32),
                        pltpu.VMEM((N_HEADS, HEAD_DIM, TQ), F32), pltpu.VMEM((n_qt, 8, TQ), F32),
                        pltpu.VMEM((2 * N_HEADS, 1, TQ), F32),
                        pltpu.VMEM((2 * N_HEADS, V_ROWS, TQ), F32), pltpu.VMEM((QK_LOOKAHEAD, TQ, TQ), F32)],
        compiler_params=_params(("arbitrary", "arbitrary"), VMEM_LIMIT),
        name="nsa_attention")(q, g, kvc, ks2, vs2, kw2, vw2, bb, far, bc, ovl_t)


def _moba_kernel(q_ref, k_ref, v_ref, km_ref, bb_ref, far_ref, o_ref, vt_ref, qh_ref, sel_ref,
                 m_ref, acc_ref, s_ref):
    i = pl.program_id(1)
    n_blk = vt_ref.shape[1]
    state = (m_ref, acc_ref)
    heads = list(range(N_HEADS))

    @pl.when(i == 0)
    def _values():
        for pp in range(N_PAIRS):
            for kt in range(n_blk):
                tile = v_ref[kt * TQ:(kt + 1) * TQ, pp * LANES:(pp + 1) * LANES]
                for j in range(2):
                    vt_ref[2 * pp + j, kt] = _values_with_ones(tile, j * HEAD_DIM)

    def gate_blocks():
        _store_head_queries(q_ref, qh_ref, N_PAIRS)
        blk = lax.broadcasted_iota(jnp.int32, (n_blk, TQ), 0)
        eligible = blk < i
        for pp in range(N_PAIRS):
            km = km_ref[:, pp * LANES:(pp + 1) * LANES].astype(BF16)
            for j in range(2):
                gate = _dot(km, qh_ref[2 * pp + j])[0:n_blk, :]
                gate = jnp.where(eligible, gate, NEG_INF)
                cnt = jnp.zeros((n_blk, TQ), jnp.int32)
                for n in range(n_blk):
                    row = gate[n:n + 1, :]
                    better = (row > gate) | ((row == gate) & (n < blk))
                    cnt = cnt + better.astype(jnp.int32)
                chosen = (cnt < min(MOBA_TOPK, n_blk - 1)) & eligible
                vis = jnp.where(chosen | (blk == i), 1.0, 0.0)
                for n in range(n_blk):
                    sel_ref[2 * pp + j, n] = vis[n:n + 1, :]
        _init_state(state)

    def tile_inputs(kt, h):
        pp = h // 2
        k = _rows(k_ref, kt, slice(pp * LANES, (pp + 1) * LANES))
        return k, vt_ref[h, kt], sel_ref[h, kt]

    def front(n_tiles):
        gate_blocks()

        def update(c, _s, slot):
            jj, h = c
            _, v_t, vis = tile_inputs(i - jj, h)
            _softmax_tile(state, h, s_ref[slot] + bb_ref[h, jj], v_t, vis=vis)

        def logits(c, slot):
            jj, h = c
            _stage_logits(s_ref, slot, _dot(tile_inputs(i - jj, h)[0], qh_ref[h]))

        _run_chains([(jj, h) for jj in range(n_tiles) for h in heads], logits, update)

    n_near = jnp.minimum(i, 1) + 1
    pl.when(i == 0)(functools.partial(front, 1))
    pl.when(i >= 1)(functools.partial(front, 2))

    def far_tiles(jjs):
        kts = [i - jj for jj in jjs]

        def update(c, s, slot):
            _, v_t, vis = tile_inputs(kts[c[0]], c[1])
            _softmax_tile(state, c[1], s, v_t, vis=vis, cbias=far_ref[c[1]])

        _run_chains([(n, h) for n in range(len(jjs)) for h in heads],
                    lambda c, slot: _dot(tile_inputs(kts[c[0]], c[1])[0], qh_ref[c[1]]), update)

    _pair_loop(n_near, i + 1, far_tiles)

    for pp in range(N_PAIRS):
        out_t = jnp.concatenate([_chain_out(state, 2 * pp), _chain_out(state, 2 * pp + 1)], axis=0)
        o_ref[:, pp * LANES:(pp + 1) * LANES] = out_t.T.astype(o_ref.dtype)


def _moba_attention(q, k, v, kmean, bb, far):
    b, s, _ = q.shape
    n_qt = s // TQ
    per_b = lambda bi, i: (bi, 0, 0)
    return pl.pallas_call(
        _moba_kernel, grid=(b, n_qt),
        in_specs=[pl.BlockSpec((None, TQ, MAIN_W), lambda bi, i: (bi, i, 0)),
                  pl.BlockSpec((None, s, MAIN_W), per_b), pl.BlockSpec((None, s, MAIN_W), per_b),
                  pl.BlockSpec((None, LANES, MAIN_W), per_b),
                  pl.BlockSpec((N_HEADS, 2, TQ, TQ), lambda bi, i: (0, 0, 0, 0)),
                  pl.BlockSpec(far.shape, lambda bi, i: (0, 0, 0))],
        out_specs=pl.BlockSpec((None, TQ, MAIN_W), lambda bi, i: (bi, i, 0)),
        out_shape=jax.ShapeDtypeStruct((b, s, MAIN_W), BF16),
        scratch_shapes=[pltpu.VMEM((N_HEADS, n_qt, V_ROWS, TQ), BF16), pltpu.VMEM((N_HEADS, LANES, TQ), BF16),
                        pltpu.VMEM((N_HEADS, n_qt, 1, TQ), F32),
                        pltpu.VMEM((N_HEADS, 1, TQ), F32),
                        pltpu.VMEM((N_HEADS, V_ROWS, TQ), F32), pltpu.VMEM((QK_LOOKAHEAD, TQ, TQ), F32)],
        compiler_params=_params(("arbitrary", "arbitrary"), VMEM_LIMIT),
        name="moba_attention")(q, k, v, kmean, bb, far)


def _mem_kernel(q_ref, kv_ref, o_ref, qh_ref):
    n_sub = q_ref.shape[0] // TQ
    for sub in range(n_sub):
        _store_head_queries(q_ref.at[sub * TQ:(sub + 1) * TQ, :], qh_ref.at[sub], MEM_HEADS // 2)
    outs = {}

    def logits(c, slot):
        sub, h = c
        return _dot(kv_ref[:, (h // 2) * LANES:(h // 2 + 1) * LANES], qh_ref[sub, h])

    def update(c, s, slot):
        h = c[1]
        lanes = slice(MEM_W + (h // 2) * LANES, MEM_W + (h // 2 + 1) * LANES)
        v_t = kv_ref[:, lanes].astype(F32).T[(h % 2) * HEAD_DIM:(h % 2 + 1) * HEAD_DIM].astype(BF16)
        e = jnp.exp2(s - jnp.max(s, axis=0, keepdims=True))
        pr = e / jnp.sum(e, axis=0, keepdims=True)
        outs[c] = _dot(v_t, pr.astype(BF16))

    _run_chains([(sub, h) for sub in range(n_sub) for h in range(MEM_HEADS)], logits, update)
    for sub in range(n_sub):
        for pp in range(MEM_HEADS // 2):
            pair = jnp.concatenate([outs[(sub, 2 * pp)], outs[(sub, 2 * pp + 1)]], axis=0)
            o_ref[sub * TQ:(sub + 1) * TQ, pp * LANES:(pp + 1) * LANES] = pair.T.astype(o_ref.dtype)


MEM_TILES = 2


def _mem_attention(qm, mem_kv):
    b, s, _ = qm.shape
    n_mem = mem_kv.shape[1]
    rows = MEM_TILES * TQ
    return pl.pallas_call(
        _mem_kernel, grid=(b, s // rows),
        in_specs=[pl.BlockSpec((None, rows, MEM_W), lambda bi, i: (bi, i, 0)),
                  pl.BlockSpec((None, n_mem, 2 * MEM_W), lambda bi, i: (bi, 0, 0))],
        out_specs=pl.BlockSpec((None, rows, MEM_W), lambda bi, i: (bi, i, 0)),
        out_shape=jax.ShapeDtypeStruct((b, s, MEM_W), BF16),
        scratch_shapes=[pltpu.VMEM((MEM_TILES, MEM_HEADS, LANES, TQ), BF16)],
        compiler_params=_params(("arbitrary", "arbitrary")), name="mem_attention")(qm, mem_kv)


def _layer_norm(y, g, b):
    mu = jnp.mean(y, axis=-1, keepdims=True)
    var = jnp.mean(jnp.square(y - mu), axis=-1, keepdims=True)
    return (y - mu) * lax.rsqrt(var + LN_EPS) * g + b


def _mm_ln_kernel(*refs, n_in):
    a_refs, w_refs = refs[:n_in], refs[n_in:2 * n_in]
    x_ref, g_ref, b_ref, o_ref, y_ref = refs[2 * n_in:]

    @pl.when(pl.program_id(0) == 0)
    def _():
        y_ref[...] = jnp.zeros(y_ref.shape, F32)

    o_ref[...] = _layer_norm(ALPHA * x_ref[...] + y_ref[...], g_ref[...], b_ref[...])
    acc = _dot(a_refs[0][...], w_refs[0][...])
    for a_ref, w_ref in zip(a_refs[1:], w_refs[1:]):
        acc = acc + _dot(a_ref[...], w_ref[...])
    y_ref[...] = acc


def _mm_ln(acts, weights, x, g, b, tm, name):
    m, d = x.shape
    n_t = m // tm
    cur = lambda t: (jnp.minimum(t, n_t - 1), 0)
    prev = lambda t: (jnp.maximum(t - 1, 0), 0)
    full = lambda t: (0, 0)
    return pl.pallas_call(
        functools.partial(_mm_ln_kernel, n_in=len(acts)), grid=(n_t + 1,),
        in_specs=[pl.BlockSpec((tm, a.shape[1]), cur) for a in acts]
                 + [pl.BlockSpec(w.shape, full) for w in weights]
                 + [pl.BlockSpec((tm, d), prev), pl.BlockSpec((1, d), full), pl.BlockSpec((1, d), full)],
        out_specs=pl.BlockSpec((tm, d), prev),
        out_shape=jax.ShapeDtypeStruct((m, d), F32),
        scratch_shapes=[pltpu.VMEM((tm, d), F32)],
        compiler_params=_params(("arbitrary",), VMEM_LIMIT), name=name)(*acts, *weights, x, g, b)


FF_CHUNK = 256
FF_ROWS = 512
SUBLANES = 8


def _ffn_kernel(x_ref, wa_ref, wb_ref, cw_ref, cb_ref, wo_ref, wol_ref, g_ref, b_ref, o_ref, xb_ref, hid_ref, *,
                n_c):
    c = pl.program_id(1)

    @pl.when(c == 0)
    def _():
        xb_ref[...] = x_ref[...].astype(BF16)
        hid_ref[1] = jnp.zeros(hid_ref.shape[1:], BF16)
        o_ref[...] = jnp.zeros(o_ref.shape, F32)

    xb = xb_ref[...]
    a = _dot(xb, wa_ref[...])
    o_ref[...] += _dot(hid_ref[(c + 1) % 2], wo_ref[...])
    gate = _dot(xb, wb_ref[...])
    ext = jnp.concatenate([jnp.zeros((SUBLANES, a.shape[1]), F32), a], axis=0)
    a1 = pltpu.roll(ext, 1, 0)[SUBLANES:]
    a2 = pltpu.roll(ext, 2, 0)[SUBLANES:]
    conv = cw_ref[0:1, :] * a2 + cw_ref[1:2, :] * a1 + cw_ref[2:3, :] * a + cb_ref[...]
    hid_ref[c % 2] = (jax.nn.gelu(conv) * gate).astype(BF16)

    @pl.when(c == n_c - 1)
    def _():
        for r0 in range(0, o_ref.shape[0], FF_ROWS):
            rows = slice(r0, r0 + FF_ROWS)
            y = o_ref[rows, :] + _dot(hid_ref[(n_c - 1) % 2, rows, :], wol_ref[...])
            o_ref[rows, :] = _layer_norm(ALPHA * x_ref[rows, :] + y, g_ref[...], b_ref[...])


def _ffn_ln(x, w_in, conv_w, conv_b, w_out, g, b):
    bsz, s, d = x.shape
    n_c = D_FF // FF_CHUNK
    prev = lambda c: jnp.maximum(c - 1, 0)
    return pl.pallas_call(
        functools.partial(_ffn_kernel, n_c=n_c), grid=(bsz, n_c),
        in_specs=[pl.BlockSpec((None, s, d), lambda bi, c: (bi, 0, 0)),
                  pl.BlockSpec((d, FF_CHUNK), lambda bi, c: (0, c)),
                  pl.BlockSpec((d, FF_CHUNK), lambda bi, c: (0, n_c + c)),
                  pl.BlockSpec((CONV_W, FF_CHUNK), lambda bi, c: (0, c)),
                  pl.BlockSpec((1, FF_CHUNK), lambda bi, c: (0, c)),
                  pl.BlockSpec((FF_CHUNK, d), lambda bi, c: (prev(c), 0)),
                  pl.BlockSpec((FF_CHUNK, d), lambda bi, c: (n_c - 1, 0)),
                  pl.BlockSpec((1, d), lambda bi, c: (0, 0)), pl.BlockSpec((1, d), lambda bi, c: (0, 0))],
        out_specs=pl.BlockSpec((None, s, d), lambda bi, c: (bi, 0, 0)),
        out_shape=jax.ShapeDtypeStruct((bsz, s, d), F32),
        scratch_shapes=[pltpu.VMEM((s, d), BF16), pltpu.VMEM((2, s, FF_CHUNK), BF16)],
        compiler_params=_params(("arbitrary", "arbitrary"), VMEM_LIMIT),
        name="ffn_ln")(x, w_in, w_in, conv_w, conv_b, w_out, w_out, g, b)


def _dup(w):
    return jnp.concatenate([w, w], axis=1)


def _cmp_weights(pe_k, w1_k, w2_k, pe_v, w1_v, w2_v):
    half = CMP_LEN // 2
    zk = jnp.zeros((half, HEAD_DIM, CMP_HIDDEN), F32)

    def w1_half(sl):
        wk = w1_k.reshape(CMP_LEN, HEAD_DIM, CMP_HIDDEN)[sl]
        wv = w1_v.reshape(CMP_LEN, HEAD_DIM, CMP_HIDDEN)[sl]
        top = jnp.concatenate([wk, zk], axis=2)
        bot = jnp.concatenate([zk, wv], axis=2)
        return jnp.concatenate([top, bot], axis=1).reshape(half * 2 * HEAD_DIM, 2 * CMP_HIDDEN)

    w1 = jnp.stack([w1_half(slice(0, half)), w1_half(slice(half, CMP_LEN))]).astype(BF16)
    pe = jnp.concatenate([pe_k, pe_v], axis=1)
    pe = jnp.stack([pe[:half].reshape(-1), pe[half:].reshape(-1)])
    zo = jnp.zeros((CMP_HIDDEN, 2 * HEAD_DIM), F32)
    w2 = jnp.concatenate([jnp.concatenate([_dup(w2_k), zo], axis=1),
                          jnp.concatenate([zo, _dup(w2_v)], axis=1)], axis=0).astype(BF16)
    return pe, w1, w2


def _overlap_matrix(seq):
    n_cmp = (seq - CMP_LEN) // CMP_STRIDE + 1
    n_slc = seq // SLC_BLOCK
    start = np.arange(LANES) * CMP_STRIDE
    bs = np.arange(LANES) * SLC_BLOCK
    ovl_t = ((start[None, :] < bs[:, None] + SLC_BLOCK) & (start[None, :] + CMP_LEN > bs[:, None])
             & (np.arange(LANES)[None, :] < n_cmp) & (np.arange(LANES)[:, None] < n_slc))
    return jnp.asarray(ovl_t, BF16)


def kernel(x, mem, rel_bias, a_w_in, a_cmp_pe_k, a_cmp_w1_k, a_cmp_w2_k, a_cmp_pe_v, a_cmp_w1_v, a_cmp_w2_v,
           a_w_mem_kv, a_w_out, shared_w_kv, b_w_in, b_w_mem_kv, b_w_out, ln1_g, ln1_b, ln2_g, ln2_b,
           ffn_w_in, ffn_conv_w, ffn_conv_b, ffn_w_out):
    bsz, seq, d = x.shape
    n_mem = mem.shape[1]
    m = bsz * seq
    assert (seq, d) == (2048, D_MODEL) and seq % TQ == 0
    n_qt = seq // TQ

    bb, bc, far = _bias_tables(rel_bias, n_qt)
    ovl_t = _overlap_matrix(seq)
    memf = mem.reshape(bsz * n_mem, d)
    xf = x.reshape(m, d)

    def ffn(xcur, layer):
        return _ffn_ln(xcur.reshape(bsz, seq, d), ffn_w_in[layer].astype(BF16), ffn_conv_w[layer],
                       ffn_conv_b[layer][None, :], ffn_w_out[layer].astype(BF16),
                       ln2_g[layer][None, :], ln2_b[layer][None, :]).reshape(m, d)

    def out_ln(o_main, o_mem, w_out, xcur, layer):
        w = w_out.astype(BF16)
        return _mm_ln([o_main.reshape(m, MAIN_W), o_mem.reshape(m, MEM_W)], [w[:MAIN_W], w[MAIN_W:]], xcur,
                      ln1_g[layer][None, :], ln1_b[layer][None, :], 512, "outproj_ln")

    w = a_w_in[0]
    c0 = MAIN_W
    cols = [w[:, c0 + kk * HEAD_DIM:c0 + (kk + 1) * HEAD_DIM] for kk in range(6)]
    c_g = c0 + 6 * HEAD_DIM
    w_g = jnp.pad(w[:, c_g:c_g + 3 * N_HEADS], ((0, 0), (0, LANES - 3 * N_HEADS)))
    w_qm = w[:, c_g + 3 * N_HEADS:]
    weights = [w[:, :MAIN_W], jnp.concatenate(cols[0:2], axis=1), _dup(cols[2]), _dup(cols[3]),
               _dup(cols[4]), _dup(cols[5]), w_g, w_qm]
    weights = [wi.astype(BF16) for wi in weights]
    q, kvc_tok, ks2, vs2, kw2, vw2, g, qm = _proj(
        xf, weights, [BF16, BF16, BF16, BF16, BF16, BF16, F32, BF16], 512, "proj_a",
        scales=[Q_SCALE, None, None, None, None, None, None, Q_SCALE])

    pe, w1, w2 = _cmp_weights(a_cmp_pe_k[0], a_cmp_w1_k[0], a_cmp_w2_k[0],
                              a_cmp_pe_v[0], a_cmp_w1_v[0], a_cmp_w2_v[0])
    kvc = _compress(kvc_tok.reshape(bsz, seq // (CMP_LEN // 2), (CMP_LEN // 2) * LANES), pe, w1, w2)

    r3 = lambda t: t.reshape(bsz, seq, t.shape[-1])
    o_main = _nsa_attention(r3(q), r3(g), kvc, r3(ks2), r3(vs2), r3(kw2), r3(vw2), bb, far, bc, ovl_t)
    (mkv,) = _proj(memf, [a_w_mem_kv[0].astype(BF16)], [BF16], 512, "proj_mem_a")
    o_mem = _mem_attention(r3(qm), mkv.reshape(bsz, n_mem, 2 * MEM_W))
    x1 = out_ln(o_main, o_mem, a_w_out[0], xf, 0)
    x1 = ffn(x1, 0)

    wb = b_w_in[0].astype(BF16)
    wkv = shared_w_kv.astype(BF16)
    q, qm, k, v, kmean = _proj_b(x1, wb[:, :MAIN_W], wb[:, MAIN_W:], wkv[:, :MAIN_W], wkv[:, MAIN_W:])
    n_blk = seq // MOBA_BLOCK
    kmean = jnp.pad(kmean.reshape(bsz, n_blk, MAIN_W), ((0, 0), (0, LANES - n_blk), (0, 0)))
    o_main = _moba_attention(r3(q), r3(k), r3(v), kmean, bb, far)
    (mkv,) = _proj(memf, [b_w_mem_kv[0].astype(BF16)], [BF16], 512, "proj_mem_b")
    o_mem = _mem_attention(r3(qm), mkv.reshape(bsz, n_mem, 2 * MEM_W))
    x2 = out_ln(o_main, o_mem, b_w_out[0], x1, 1)
    x2 = ffn(x2, 1)
    return x2.reshape(bsz, seq, d)
```

```python
import functools
import math

import numpy as np
import jax
import jax.numpy as jnp
from jax import lax
from jax.experimental import pallas as pl
from jax.experimental.pallas import tpu as pltpu

F32 = jnp.float32
BF16 = jnp.bfloat16

D_MODEL = 1024
HEAD_DIM = 64
N_HEADS = 12
N_PAIRS = N_HEADS // 2
MAIN_W = N_HEADS * HEAD_DIM
MEM_HEADS = 4
MEM_W = MEM_HEADS * HEAD_DIM
CMP_LEN = 32
CMP_STRIDE = 16
CMP_HIDDEN = 256
SLC_BLOCK = 64
SLC_TOPK = 16
WINDOW = 512
FORCE_SCORE = 1.0e4
MOBA_BLOCK = 256
MOBA_TOPK = 3
REL_BUCKETS = 32
REL_MAX_DIST = 128
D_FF = 2816
CONV_W = 3
DEPTH = 2
ALPHA = (2.0 * DEPTH) ** 0.25
LN_EPS = 1e-5
NEG_INF = -1e30
TINY = 1e-30
LOG2E = math.log2(math.e)
Q_SCALE = HEAD_DIM ** -0.5 * LOG2E

LANES = 128
TQ = 256
VMEM_LIMIT = 60 * 1024 * 1024
V_ROWS = HEAD_DIM + 8
QK_LOOKAHEAD = 6


def _dot(a, b):
    return jnp.dot(a, b, preferred_element_type=F32)


def _dot_nt(a, b):
    return lax.dot_general(a, b, (((1,), (1,)), ((), ())), preferred_element_type=F32)


def _params(sem, vmem=None):
    return pltpu.CompilerParams(dimension_semantics=sem, vmem_limit_bytes=vmem)


def _proj_kernel(x_ref, w_ref, *o_refs, scales):
    y = _dot(x_ref[...].astype(BF16), w_ref[...])
    col = 0
    for o_ref, scale in zip(o_refs, scales):
        part = y[:, col:col + o_ref.shape[1]]
        o_ref[...] = (part if scale is None else part * scale).astype(o_ref.dtype)
        col += o_ref.shape[1]


def _proj(x, weights, dtypes, tm, name, scales=None):
    m, k = x.shape
    scales = tuple(scales) if scales is not None else (None,) * len(weights)
    assert all(w.shape[1] % LANES == 0 for w in weights)
    w_all = jnp.concatenate(weights, axis=1) if len(weights) > 1 else weights[0]
    out_specs = [pl.BlockSpec((tm, w.shape[1]), lambda i: (i, 0)) for w in weights]
    out_shape = [jax.ShapeDtypeStruct((m, w.shape[1]), dt) for w, dt in zip(weights, dtypes)]
    return pl.pallas_call(
        functools.partial(_proj_kernel, scales=scales), grid=(m // tm,),
        in_specs=[pl.BlockSpec((tm, k), lambda i: (i, 0)), pl.BlockSpec(w_all.shape, lambda i: (0, 0))],
        out_specs=out_specs, out_shape=out_shape,
        compiler_params=_params(("arbitrary",), VMEM_LIMIT), name=name)(x, w_all)


def _projb_kernel(x_ref, wq_ref, wm_ref, wk_ref, wv_ref, q_ref, qm_ref, k_ref, v_ref, km_ref):
    xb = x_ref[...].astype(BF16)
    q_ref[...] = (_dot(xb, wq_ref[...]) * Q_SCALE).astype(BF16)
    qm_ref[...] = (_dot(xb, wm_ref[...]) * Q_SCALE).astype(BF16)
    k = _dot(xb, wk_ref[...])
    k_ref[...] = k.astype(BF16)
    km_ref[...] = jnp.mean(k, axis=0, keepdims=True)
    v_ref[...] = _dot(xb, wv_ref[...]).astype(BF16)


def _proj_b(x, wq, wm, wk, wv):
    m, k = x.shape
    nblk = m // MOBA_BLOCK
    row = lambda i: (i, 0)
    full = lambda i: (0, 0)
    return pl.pallas_call(
        _projb_kernel, grid=(nblk,),
        in_specs=[pl.BlockSpec((MOBA_BLOCK, k), row)] + [pl.BlockSpec(w.shape, full) for w in (wq, wm, wk, wv)],
        out_specs=[pl.BlockSpec((MOBA_BLOCK, MAIN_W), row), pl.BlockSpec((MOBA_BLOCK, MEM_W), row),
                   pl.BlockSpec((MOBA_BLOCK, MAIN_W), row), pl.BlockSpec((MOBA_BLOCK, MAIN_W), row),
                   pl.BlockSpec((None, 1, MAIN_W), lambda i: (i, 0, 0))],
        out_shape=[jax.ShapeDtypeStruct((m, MAIN_W), BF16), jax.ShapeDtypeStruct((m, MEM_W), BF16),
                   jax.ShapeDtypeStruct((m, MAIN_W), BF16), jax.ShapeDtypeStruct((m, MAIN_W), BF16),
                   jax.ShapeDtypeStruct((nblk, 1, MAIN_W), F32)],
        compiler_params=_params(("arbitrary",), VMEM_LIMIT), name="proj_b")(x, wq, wm, wk, wv)


def _bias_kernel(tbl_ref, bb_ref, bc_ref, far_ref):
    h = pl.program_id(0)

    def bias_of(dist):
        n = jnp.maximum(dist, 0)
        max_exact = REL_BUCKETS // 2
        nf = jnp.maximum(n, 1).astype(F32)
        large = max_exact + (jnp.log(nf / max_exact) / math.log(REL_MAX_DIST / max_exact)
                             * (REL_BUCKETS - max_exact)).astype(jnp.int32)
        large = jnp.minimum(large, REL_BUCKETS - 1)
        bucket = jnp.where(n < max_exact, n, large)
        out = jnp.zeros(dist.shape, F32)
        for kk in range(REL_BUCKETS):
            out = jnp.where(bucket == kk, tbl_ref[h * REL_BUCKETS + kk], out)
        return out * LOG2E

    key = lax.broadcasted_iota(jnp.int32, (TQ, TQ), 0)
    qry = lax.broadcasted_iota(jnp.int32, (TQ, TQ), 1)
    d0 = qry - key
    bb_ref[0] = jnp.where(d0 >= 0, bias_of(d0), NEG_INF)
    bb_ref[1] = bias_of(d0 + TQ)
    far = bias_of(d0 + 2 * TQ)
    bb_ref[2] = far
    bb_ref[3] = jnp.where(d0 + 2 * TQ < WINDOW, far, NEG_INF)
    far_ref[...] = far[0:1, :]

    nc = lax.broadcasted_iota(jnp.int32, (LANES, TQ), 0)
    qc = lax.broadcasted_iota(jnp.int32, (LANES, TQ), 1)
    n_cmp = (2048 - CMP_LEN) // CMP_STRIDE + 1
    for i in range(bc_ref.shape[0]):
        dc = i * TQ + qc - (nc * CMP_STRIDE + CMP_LEN - 1)
        bc_ref[i] = jnp.where((dc >= 0) & (nc < n_cmp), bias_of(dc), NEG_INF)


def _bias_tables(rel_bias, n_qt):
    tbl = rel_bias.T.reshape(-1)
    return pl.pallas_call(
        _bias_kernel, grid=(N_HEADS,),
        in_specs=[pl.BlockSpec(memory_space=pltpu.SMEM)],
        out_specs=[pl.BlockSpec((None, 4, TQ, TQ), lambda h: (h, 0, 0, 0)),
                   pl.BlockSpec((n_qt, None, LANES, TQ), lambda h: (0, h, 0, 0)),
                   pl.BlockSpec((None, 1, TQ), lambda h: (h, 0, 0))],
        out_shape=[jax.ShapeDtypeStruct((N_HEADS, 4, TQ, TQ), F32),
                   jax.ShapeDtypeStruct((n_qt, N_HEADS, LANES, TQ), F32),
                   jax.ShapeDtypeStruct((N_HEADS, 1, TQ), F32)],
        compiler_params=_params(("arbitrary",)), name="bias_tables")(tbl)


def _cmp_kernel(kv_ref, pe_ref, w1_ref, w2_ref, o_ref):
    x = kv_ref[...].astype(F32)
    lo = _dot((x + pe_ref[0:1, :]).astype(BF16), w1_ref[0])
    hi = _dot((x + pe_ref[1:2, :]).astype(BF16), w1_ref[1])
    nrow = x.shape[0]
    hid = lo + pltpu.roll(hi, nrow - 1, 0)
    o_ref[...] = _dot(jax.nn.gelu(hid).astype(BF16), w2_ref[...])


def _compress(kvr, pe, w1, w2):
    b, nrow, width = kvr.shape
    return pl.pallas_call(
        _cmp_kernel, grid=(b,),
        in_specs=[pl.BlockSpec((None, nrow, width), lambda i: (i, 0, 0)),
                  pl.BlockSpec(pe.shape, lambda i: (0, 0)),
                  pl.BlockSpec(w1.shape, lambda i: (0, 0, 0)),
                  pl.BlockSpec(w2.shape, lambda i: (0, 0))],
        out_specs=pl.BlockSpec((None, nrow, w2.shape[1]), lambda i: (i, 0, 0)),
        out_shape=jax.ShapeDtypeStruct((b, nrow, w2.shape[1]), F32),
        compiler_params=_params(("arbitrary",), VMEM_LIMIT), name="nsa_compress")(kvr, pe, w1, w2)


def _softmax_tile(state, c, s_t, v_t, vis=None, cbias=None):
    m_ref, acc_ref = state
    m = m_ref[c]
    n_blk = 1 if vis is None else vis.shape[0]
    rows = s_t.shape[0] // n_blk
    parts = [s_t[b * rows:(b + 1) * rows] for b in range(n_blk)]
    tile_max = None
    for b in range(n_blk):
        mb = jnp.max(parts[b], axis=0, keepdims=True)
        if vis is not None:
            mb = jnp.where(vis[b:b + 1] > 0.5, mb, NEG_INF)
        tile_max = mb if tile_max is None else jnp.maximum(tile_max, mb)
    if cbias is not None:
        tile_max = tile_max + cbias
    m_new = jnp.maximum(m, tile_max)
    shift = m_new if cbias is None else m_new - cbias
    p_parts = []
    for b in range(n_blk):
        off = shift if vis is None else jnp.where(vis[b:b + 1] > 0.5, shift, -NEG_INF)
        p_parts.append(jnp.exp2(parts[b] - off))
    p_t = p_parts[0] if n_blk == 1 else jnp.concatenate(p_parts, axis=0)
    alpha = jnp.exp2(m - m_new)
    acc_ref[c] = alpha * acc_ref[c] + _dot(v_t, p_t.astype(BF16))
    m_ref[c] = m_new


def _run_chains(chains, logits_fn, update_fn):
    pending = {}
    for idx, c in enumerate(chains[:QK_LOOKAHEAD]):
        pending[c] = logits_fn(c, idx)
    for idx, c in enumerate(chains):
        update_fn(c, pending.pop(c), idx % QK_LOOKAHEAD)
        if idx + QK_LOOKAHEAD < len(chains):
            nxt = chains[idx + QK_LOOKAHEAD]
            pending[nxt] = logits_fn(nxt, idx % QK_LOOKAHEAD)


def _stage_logits(s_ref, slot, s_t):
    s_ref[slot] = s_t


def _init_state(state):
    m_ref, acc_ref = state
    m_ref[...] = jnp.full(m_ref.shape, NEG_INF, F32)
    acc_ref[...] = jnp.zeros(acc_ref.shape, F32)


def _chain_out(state, c):
    _, acc_ref = state
    return acc_ref[c, 0:HEAD_DIM, :] / jnp.maximum(acc_ref[c, HEAD_DIM:HEAD_DIM + 1, :], TINY)


def _values_with_ones(tile, row0):
    v_t = tile.astype(F32).T[row0:row0 + HEAD_DIM]
    pad = lax.broadcasted_iota(jnp.int32, (V_ROWS - HEAD_DIM, tile.shape[0]), 0)
    return jnp.concatenate([v_t, jnp.where(pad == 0, 1.0, 0.0)], axis=0).astype(BF16)


def _rows(ref, kt, cols=slice(None)):
    return ref[kt * TQ:(kt + 1) * TQ, cols]


def _store_head_queries(q_ref, qh_ref, n_pairs):
    row = lax.broadcasted_iota(jnp.int32, (LANES, TQ), 0)
    halves = (row < HEAD_DIM, row >= HEAD_DIM)
    for pp in range(n_pairs):
        q2t = q_ref[:, pp * LANES:(pp + 1) * LANES].astype(F32).T
        for j in range(2):
            qh_ref[2 * pp + j] = jnp.where(halves[j], q2t, 0.0).astype(BF16)


def _per_query_tile(step_fn, n_tiles):
    def kernel_fn(*refs):
        for i in range(n_tiles):
            pl.when(pl.program_id(1) == i)(functools.partial(step_fn, *refs, i=i))
    return kernel_fn


def _nsa_step(q_ref, g_ref, kvc_ref, ks_ref, vs_ref, kw_ref, vw_ref, bb_ref, far_ref, bc_ref, ovl_ref,
              o_ref, vst_ref, vwt_ref, qh_ref, gs_ref, oc_ref, sel_ref, m_ref, acc_ref, s_ref, *, i):
    n_kt = vst_ref.shape[0]
    state = (m_ref, acc_ref)

    if i == 0:
        for kt in range(n_kt):
            rows = slice(kt * TQ, (kt + 1) * TQ)
            vst_ref[kt] = _values_with_ones(vs_ref[rows, :], 0)
            vwt_ref[kt] = _values_with_ones(vw_ref[rows, :], 0)

    per_tile = TQ // SLC_BLOCK
    win_tiles = WINDOW // TQ
    n_near = min(i, win_tiles) + 1

    def select_blocks():
        _store_head_queries(q_ref, qh_ref, N_PAIRS)
        gs_ref[...] = jax.nn.sigmoid(g_ref[...]).T
        kc2 = kvc_ref[:, 0:LANES].astype(BF16)
        vc_t = kvc_ref[:, LANES:2 * LANES].T[0:HEAD_DIM].astype(BF16)
        psums = []

        def cmp_logits(h, slot):
            s_ref[slot, 0:LANES, :] = _dot(kc2, qh_ref[h])

        def cmp_update(h, _s, slot):
            bias = bc_ref[h]
            s = s_ref[slot, 0:LANES, :] + bias
            m = jnp.maximum(jnp.max(s, axis=0, keepdims=True), 0.5 * NEG_INF)
            e = jnp.exp2(s - m)
            pc = e / jnp.maximum(jnp.sum(e, axis=0, keepdims=True), TINY)
            psums[:] = [pc if not psums else psums[0] + pc]
            oc_ref[h] = gs_ref[3 * h:3 * h + 1, :] * _dot(vc_t, pc.astype(BF16))

        _run_chains(list(range(N_HEADS)), cmp_logits, cmp_update)
        psum = psums[0]
        p_hi = psum.astype(BF16)
        p_lo = (psum - p_hi.astype(F32)).astype(BF16)
        imp = _dot(ovl_ref[...], p_hi) + _dot(ovl_ref[...], p_lo)
        n_slc = n_kt * per_tile
        imp = imp[0:n_slc, :]
        sidx = lax.broadcasted_iota(jnp.int32, (n_slc, TQ), 0)
        t = i * TQ + lax.broadcasted_iota(jnp.int32, (n_slc, TQ), 1)
        cur = lax.shift_right_logical(t, 6)
        eligible = sidx <= cur
        forced = (sidx == 0) | (sidx == cur) | (sidx == cur - 1)
        score = jnp.where(eligible, jnp.where(forced, FORCE_SCORE, 0.0), NEG_INF) + imp
        cnt = jnp.zeros((n_slc, TQ), jnp.int32)
        for sp in range(n_slc):
            row = score[sp:sp + 1, :]
            better = (row > score) | ((row == score) & (sp < sidx))
            cnt = cnt + better.astype(jnp.int32)
        chosen = jnp.where(cnt < min(SLC_TOPK, n_slc), 1.0, 0.0)
        for kt in range(n_kt):
            sel_ref[kt, 0:per_tile, :] = chosen[kt * per_tile:(kt + 1) * per_tile, :]
        _init_state(state)

    def near_tiles(n_tiles):
        def logits(c, slot):
            kind, jj, h = c
            _stage_logits(s_ref, slot, _dot(_rows(ks_ref if kind == "s" else kw_ref, i - jj), qh_ref[h]))

        def update(c, _s, slot):
            kind, jj, h = c
            kt = i - jj
            if kind == "s":
                _softmax_tile(state, h, s_ref[slot] + bb_ref[h, min(jj, 2)], vst_ref[kt],
                              vis=sel_ref[kt, 0:per_tile, :])
            else:
                _softmax_tile(state, N_HEADS + h, s_ref[slot] + bb_ref[h, 3 if jj == win_tiles else jj], vwt_ref[kt])

        _run_chains([(kind, jj, h) for jj in range(n_tiles) for h in range(N_HEADS) for kind in ("s", "w")],
                    logits, update)

    def far_tiles(jjs):
        kts = [i - jj for jj in jjs]
        _run_chains([(n, h) for n in range(len(jjs)) for h in range(N_HEADS)],
                    lambda c, slot: _dot(_rows(ks_ref, kts[c[0]]), qh_ref[c[1]]),
                    lambda c, s, slot: _softmax_tile(state, c[1], s, vst_ref[kts[c[0]]],
                                                     vis=sel_ref[kts[c[0]], 0:per_tile, :], cbias=far_ref[c[1]]))

    select_blocks()
    near_tiles(n_near)
    if i + 1 > n_near:
        far_tiles(list(range(n_near, i + 1)))

    for pp in range(N_PAIRS):
        hs = (2 * pp, 2 * pp + 1)
        outs = []
        for h in hs:
            g1 = gs_ref[3 * h + 1:3 * h + 2, :]
            g2 = gs_ref[3 * h + 2:3 * h + 3, :]
            outs.append(oc_ref[h] + g1 * _chain_out(state, h) + g2 * _chain_out(state, N_HEADS + h))
        o_ref[:, pp * LANES:(pp + 1) * LANES] = jnp.concatenate(outs, axis=0).T.astype(o_ref.dtype)


def _nsa_attention(q, g, kvc, ks2, vs2, kw2, vw2, bb, far, bc, ovl_t):
    b, s, _ = q.shape
    n_qt = s // TQ
    per_b = lambda bi, i: (bi, 0, 0)
    return pl.pallas_call(
        _per_query_tile(_nsa_step, n_qt), grid=(b, n_qt),
        in_specs=[pl.BlockSpec((None, TQ, MAIN_W), lambda bi, i: (bi, i, 0)),
                  pl.BlockSpec((None, TQ, LANES), lambda bi, i: (bi, i, 0)),
                  pl.BlockSpec((None,) + kvc.shape[1:], per_b),
                  pl.BlockSpec((None, s, LANES), per_b), pl.BlockSpec((None, s, LANES), per_b),
                  pl.BlockSpec((None, s, LANES), per_b), pl.BlockSpec((None, s, LANES), per_b),
                  pl.BlockSpec(bb.shape, lambda bi, i: (0, 0, 0, 0)),
                  pl.BlockSpec(far.shape, lambda bi, i: (0, 0, 0)),
                  pl.BlockSpec((None,) + bc.shape[1:], lambda bi, i: (i, 0, 0, 0)),
                  pl.BlockSpec(ovl_t.shape, lambda bi, i: (0, 0))],
        out_specs=pl.BlockSpec((None, TQ, MAIN_W), lambda bi, i: (bi, i, 0)),
        out_shape=jax.ShapeDtypeStruct((b, s, MAIN_W), BF16),
        scratch_shapes=[pltpu.VMEM((n_qt, V_ROWS, TQ), BF16), pltpu.VMEM((n_qt, V_ROWS, TQ), BF16),
                        pltpu.VMEM((N_HEADS, LANES, TQ), BF16), pltpu.VMEM((LANES, TQ), F32),
                        pltpu.VMEM((N_HEADS, HEAD_DIM, TQ), F32), pltpu.VMEM((n_qt, 8, TQ), F32),
                        pltpu.VMEM((2 * N_HEADS, 1, TQ), F32),
                        pltpu.VMEM((2 * N_HEADS, V_ROWS, TQ), F32), pltpu.VMEM((QK_LOOKAHEAD, TQ, TQ), F32)],
        compiler_params=_params(("arbitrary", "arbitrary"), VMEM_LIMIT),
        name="nsa_attention")(q, g, kvc, ks2, vs2, kw2, vw2, bb, far, bc, ovl_t)


def _moba_step(q_ref, k_ref, v_ref, km_ref, bb_ref, far_ref, o_ref, vt_ref, qh_ref, sel_ref,
               m_ref, acc_ref, s_ref, *, i):
    n_blk = vt_ref.shape[1]
    state = (m_ref, acc_ref)
    heads = list(range(N_HEADS))

    if i == 0:
        for pp in range(N_PAIRS):
            for kt in range(n_blk):
                tile = v_ref[kt * TQ:(kt + 1) * TQ, pp * LANES:(pp + 1) * LANES]
                for j in range(2):
                    vt_ref[2 * pp + j, kt] = _values_with_ones(tile, j * HEAD_DIM)

    def gate_blocks():
        _store_head_queries(q_ref, qh_ref, N_PAIRS)
        blk = lax.broadcasted_iota(jnp.int32, (n_blk, TQ), 0)
        eligible = blk < i
        for pp in range(N_PAIRS):
            km = km_ref[:, pp * LANES:(pp + 1) * LANES].astype(BF16)
            for j in range(2):
                gate = _dot(km, qh_ref[2 * pp + j])[0:n_blk, :]
                gate = jnp.where(eligible, gate, NEG_INF)
                cnt = jnp.zeros((n_blk, TQ), jnp.int32)
                for n in range(n_blk):
                    row = gate[n:n + 1, :]
                    better = (row > gate) | ((row == gate) & (n < blk))
                    cnt = cnt + better.astype(jnp.int32)
                chosen = (cnt < min(MOBA_TOPK, n_blk - 1)) & eligible
                vis = jnp.where(chosen | (blk == i), 1.0, 0.0)
                for n in range(n_blk):
                    sel_ref[2 * pp + j, n] = vis[n:n + 1, :]
        _init_state(state)

    def tile_inputs(kt, h):
        pp = h // 2
        k = _rows(k_ref, kt, slice(pp * LANES, (pp + 1) * LANES))
        return k, vt_ref[h, kt], sel_ref[h, kt]

    def front(n_tiles):
        gate_blocks()

        def update(c, _s, slot):
            jj, h = c
            _, v_t, vis = tile_inputs(i - jj, h)
            _softmax_tile(state, h, s_ref[slot] + bb_ref[h, jj], v_t, vis=vis)

        def logits(c, slot):
            jj, h = c
            _stage_logits(s_ref, slot, _dot(tile_inputs(i - jj, h)[0], qh_ref[h]))

        _run_chains([(jj, h) for jj in range(n_tiles) for h in heads], logits, update)

    n_near = min(i, 1) + 1

    def far_tiles(jjs):
        kts = [i - jj for jj in jjs]

        def update(c, s, slot):
            _, v_t, vis = tile_inputs(kts[c[0]], c[1])
            _softmax_tile(state, c[1], s, v_t, vis=vis, cbias=far_ref[c[1]])

        _run_chains([(n, h) for n in range(len(jjs)) for h in heads],
                    lambda c, slot: _dot(tile_inputs(kts[c[0]], c[1])[0], qh_ref[c[1]]), update)

    front(n_near)
    if i + 1 > n_near:
        far_tiles(list(range(n_near, i + 1)))

    for pp in range(N_PAIRS):
        out_t = jnp.concatenate([_chain_out(state, 2 * pp), _chain_out(state, 2 * pp + 1)], axis=0)
        o_ref[:, pp * LANES:(pp + 1) * LANES] = out_t.T.astype(o_ref.dtype)


def _moba_attention(q, k, v, kmean, bb, far):
    b, s, _ = q.shape
    n_qt = s // TQ
    per_b = lambda bi, i: (bi, 0, 0)
    return pl.pallas_call(
        _per_query_tile(_moba_step, n_qt), grid=(b, n_qt),
        in_specs=[pl.BlockSpec((None, TQ, MAIN_W), lambda bi, i: (bi, i, 0)),
                  pl.BlockSpec((None, s, MAIN_W), per_b), pl.BlockSpec((None, s, MAIN_W), per_b),
                  pl.BlockSpec((None, LANES, MAIN_W), per_b),
                  pl.BlockSpec((N_HEADS, 2, TQ, TQ), lambda bi, i: (0, 0, 0, 0)),
                  pl.BlockSpec(far.shape, lambda bi, i: (0, 0, 0))],
        out_specs=pl.BlockSpec((None, TQ, MAIN_W), lambda bi, i: (bi, i, 0)),
        out_shape=jax.ShapeDtypeStruct((b, s, MAIN_W), BF16),
        scratch_shapes=[pltpu.VMEM((N_HEADS, n_qt, V_ROWS, TQ), BF16), pltpu.VMEM((N_HEADS, LANES, TQ), BF16),
                        pltpu.VMEM((N_HEADS, n_qt, 1, TQ), F32),
                        pltpu.VMEM((N_HEADS, 1, TQ), F32),
                        pltpu.VMEM((N_HEADS, V_ROWS, TQ), F32), pltpu.VMEM((QK_LOOKAHEAD, TQ, TQ), F32)],
        compiler_params=_params(("arbitrary", "arbitrary"), VMEM_LIMIT),
        name="moba_attention")(q, k, v, kmean, bb, far)


def _mem_kernel(q_ref, kv_ref, o_ref, qh_ref):
    n_sub = q_ref.shape[0] // TQ
    for sub in range(n_sub):
        _store_head_queries(q_ref.at[sub * TQ:(sub + 1) * TQ, :], qh_ref.at[sub], MEM_HEADS // 2)
    outs = {}

    def logits(c, slot):
        sub, h = c
        return _dot(kv_ref[:, (h // 2) * LANES:(h // 2 + 1) * LANES], qh_ref[sub, h])

    def update(c, s, slot):
        h = c[1]
        lanes = slice(MEM_W + (h // 2) * LANES, MEM_W + (h // 2 + 1) * LANES)
        v_t = kv_ref[:, lanes].astype(F32).T[(h % 2) * HEAD_DIM:(h % 2 + 1) * HEAD_DIM].astype(BF16)
        e = jnp.exp2(s - jnp.max(s, axis=0, keepdims=True))
        pr = e / jnp.sum(e, axis=0, keepdims=True)
        outs[c] = _dot(v_t, pr.astype(BF16))

    _run_chains([(sub, h) for sub in range(n_sub) for h in range(MEM_HEADS)], logits, update)
    for sub in range(n_sub):
        for pp in range(MEM_HEADS // 2):
            pair = jnp.concatenate([outs[(sub, 2 * pp)], outs[(sub, 2 * pp + 1)]], axis=0)
            o_ref[sub * TQ:(sub + 1) * TQ, pp * LANES:(pp + 1) * LANES] = pair.T.astype(o_ref.dtype)


MEM_TILES = 2


def _mem_attention(qm, mem_kv):
    b, s, _ = qm.shape
    n_mem = mem_kv.shape[1]
    rows = MEM_TILES * TQ
    return pl.pallas_call(
        _mem_kernel, grid=(b, s // rows),
        in_specs=[pl.BlockSpec((None, rows, MEM_W), lambda bi, i: (bi, i, 0)),
                  pl.BlockSpec((None, n_mem, 2 * MEM_W), lambda bi, i: (bi, 0, 0))],
        out_specs=pl.BlockSpec((None, rows, MEM_W), lambda bi, i: (bi, i, 0)),
        out_shape=jax.ShapeDtypeStruct((b, s, MEM_W), BF16),
        scratch_shapes=[pltpu.VMEM((MEM_TILES, MEM_HEADS, LANES, TQ), BF16)],
        compiler_params=_params(("arbitrary", "arbitrary")), name="mem_attention")(qm, mem_kv)


def _layer_norm(y, g, b):
    mu = jnp.mean(y, axis=-1, keepdims=True)
    var = jnp.mean(jnp.square(y - mu), axis=-1, keepdims=True)
    return (y - mu) * lax.rsqrt(var + LN_EPS) * g + b


def _mm_ln_kernel(*refs, n_in):
    a_refs, w_refs = refs[:n_in], refs[n_in:2 * n_in]
    x_ref, g_ref, b_ref, o_ref, y_ref = refs[2 * n_in:]

    @pl.when(pl.program_id(0) == 0)
    def _():
        y_ref[...] = jnp.zeros(y_ref.shape, F32)

    o_ref[...] = _layer_norm(ALPHA * x_ref[...] + y_ref[...], g_ref[...], b_ref[...])
    acc = _dot(a_refs[0][...], w_refs[0][...])
    for a_ref, w_ref in zip(a_refs[1:], w_refs[1:]):
        acc = acc + _dot(a_ref[...], w_ref[...])
    y_ref[...] = acc


def _mm_ln(acts, weights, x, g, b, tm, name):
    m, d = x.shape
    n_t = m // tm
    cur = lambda t: (jnp.minimum(t, n_t - 1), 0)
    prev = lambda t: (jnp.maximum(t - 1, 0), 0)
    full = lambda t: (0, 0)
    return pl.pallas_call(
        functools.partial(_mm_ln_kernel, n_in=len(acts)), grid=(n_t + 1,),
        in_specs=[pl.BlockSpec((tm, a.shape[1]), cur) for a in acts]
                 + [pl.BlockSpec(w.shape, full) for w in weights]
                 + [pl.BlockSpec((tm, d), prev), pl.BlockSpec((1, d), full), pl.BlockSpec((1, d), full)],
        out_specs=pl.BlockSpec((tm, d), prev),
        out_shape=jax.ShapeDtypeStruct((m, d), F32),
        scratch_shapes=[pltpu.VMEM((tm, d), F32)],
        compiler_params=_params(("arbitrary",), VMEM_LIMIT), name=name)(*acts, *weights, x, g, b)


FF_CHUNK = 256
FF_ROWS = 512
SUBLANES = 8


def _ffn_kernel(x_ref, wa_ref, wb_ref, cw_ref, cb_ref, wo_ref, wol_ref, g_ref, b_ref, o_ref, xb_ref, hid_ref, *,
                n_c):
    c = pl.program_id(1)

    @pl.when(c == 0)
    def _():
        xb_ref[...] = x_ref[...].astype(BF16)
        hid_ref[1] = jnp.zeros(hid_ref.shape[1:], BF16)
        o_ref[...] = jnp.zeros(o_ref.shape, F32)

    xb = xb_ref[...]
    a = _dot(xb, wa_ref[...])
    o_ref[...] += _dot(hid_ref[(c + 1) % 2], wo_ref[...])
    gate = _dot(xb, wb_ref[...])
    ext = jnp.concatenate([jnp.zeros((SUBLANES, a.shape[1]), F32), a], axis=0)
    a1 = pltpu.roll(ext, 1, 0)[SUBLANES:]
    a2 = pltpu.roll(ext, 2, 0)[SUBLANES:]
    conv = cw_ref[0:1, :] * a2 + cw_ref[1:2, :] * a1 + cw_ref[2:3, :] * a + cb_ref[...]
    hid_ref[c % 2] = (jax.nn.gelu(conv) * gate).astype(BF16)

    @pl.when(c == n_c - 1)
    def _():
        for r0 in range(0, o_ref.shape[0], FF_ROWS):
            rows = slice(r0, r0 + FF_ROWS)
            y = o_ref[rows, :] + _dot(hid_ref[(n_c - 1) % 2, rows, :], wol_ref[...])
            o_ref[rows, :] = _layer_norm(ALPHA * x_ref[rows, :] + y, g_ref[...], b_ref[...])


def _ffn_ln(x, w_in, conv_w, conv_b, w_out, g, b):
    bsz, s, d = x.shape
    n_c = D_FF // FF_CHUNK
    prev = lambda c: jnp.maximum(c - 1, 0)
    return pl.pallas_call(
        functools.partial(_ffn_kernel, n_c=n_c), grid=(bsz, n_c),
        in_specs=[pl.BlockSpec((None, s, d), lambda bi, c: (bi, 0, 0)),
                  pl.BlockSpec((d, FF_CHUNK), lambda bi, c: (0, c)),
                  pl.BlockSpec((d, FF_CHUNK), lambda bi, c: (0, n_c + c)),
                  pl.BlockSpec((CONV_W, FF_CHUNK), lambda bi, c: (0, c)),
                  pl.BlockSpec((1, FF_CHUNK), lambda bi, c: (0, c)),
                  pl.BlockSpec((FF_CHUNK, d), lambda bi, c: (prev(c), 0)),
                  pl.BlockSpec((FF_CHUNK, d), lambda bi, c: (n_c - 1, 0)),
                  pl.BlockSpec((1, d), lambda bi, c: (0, 0)), pl.BlockSpec((1, d), lambda bi, c: (0, 0))],
        out_specs=pl.BlockSpec((None, s, d), lambda bi, c: (bi, 0, 0)),
        out_shape=jax.ShapeDtypeStruct((bsz, s, d), F32),
        scratch_shapes=[pltpu.VMEM((s, d), BF16), pltpu.VMEM((2, s, FF_CHUNK), BF16)],
        compiler_params=_params(("arbitrary", "arbitrary"), VMEM_LIMIT),
        name="ffn_ln")(x, w_in, w_in, conv_w, conv_b, w_out, w_out, g, b)


def _dup(w):
    return jnp.concatenate([w, w], axis=1)


def _cmp_weights(pe_k, w1_k, w2_k, pe_v, w1_v, w2_v):
    half = CMP_LEN // 2
    zk = jnp.zeros((half, HEAD_DIM, CMP_HIDDEN), F32)

    def w1_half(sl):
        wk = w1_k.reshape(CMP_LEN, HEAD_DIM, CMP_HIDDEN)[sl]
        wv = w1_v.reshape(CMP_LEN, HEAD_DIM, CMP_HIDDEN)[sl]
        top = jnp.concatenate([wk, zk], axis=2)
        bot = jnp.concatenate([zk, wv], axis=2)
        return jnp.concatenate([top, bot], axis=1).reshape(half * 2 * HEAD_DIM, 2 * CMP_HIDDEN)

    w1 = jnp.stack([w1_half(slice(0, half)), w1_half(slice(half, CMP_LEN))]).astype(BF16)
    pe = jnp.concatenate([pe_k, pe_v], axis=1)
    pe = jnp.stack([pe[:half].reshape(-1), pe[half:].reshape(-1)])
    zo = jnp.zeros((CMP_HIDDEN, 2 * HEAD_DIM), F32)
    w2 = jnp.concatenate([jnp.concatenate([_dup(w2_k), zo], axis=1),
                          jnp.concatenate([zo, _dup(w2_v)], axis=1)], axis=0).astype(BF16)
    return pe, w1, w2


def _overlap_matrix(seq):
    n_cmp = (seq - CMP_LEN) // CMP_STRIDE + 1
    n_slc = seq // SLC_BLOCK
    start = np.arange(LANES) * CMP_STRIDE
    bs = np.arange(LANES) * SLC_BLOCK
    ovl_t = ((start[None, :] < bs[:, None] + SLC_BLOCK) & (start[None, :] + CMP_LEN > bs[:, None])
             & (np.arange(LANES)[None, :] < n_cmp) & (np.arange(LANES)[:, None] < n_slc))
    return jnp.asarray(ovl_t, BF16)


def kernel(x, mem, rel_bias, a_w_in, a_cmp_pe_k, a_cmp_w1_k, a_cmp_w2_k, a_cmp_pe_v, a_cmp_w1_v, a_cmp_w2_v,
           a_w_mem_kv, a_w_out, shared_w_kv, b_w_in, b_w_mem_kv, b_w_out, ln1_g, ln1_b, ln2_g, ln2_b,
           ffn_w_in, ffn_conv_w, ffn_conv_b, ffn_w_out):
    bsz, seq, d = x.shape
    n_mem = mem.shape[1]
    m = bsz * seq
    assert (seq, d) == (2048, D_MODEL) and seq % TQ == 0
    n_qt = seq // TQ

    bb, bc, far = _bias_tables(rel_bias, n_qt)
    ovl_t = _overlap_matrix(seq)
    memf = mem.reshape(bsz * n_mem, d)
    xf = x.reshape(m, d)

    def ffn(xcur, layer):
        return _ffn_ln(xcur.reshape(bsz, seq, d), ffn_w_in[layer].astype(BF16), ffn_conv_w[layer],
                       ffn_conv_b[layer][None, :], ffn_w_out[layer].astype(BF16),
                       ln2_g[layer][None, :], ln2_b[layer][None, :]).reshape(m, d)

    def out_ln(o_main, o_mem, w_out, xcur, layer):
        w = w_out.astype(BF16)
        return _mm_ln([o_main.reshape(m, MAIN_W), o_mem.reshape(m, MEM_W)], [w[:MAIN_W], w[MAIN_W:]], xcur,
                      ln1_g[layer][None, :], ln1_b[layer][None, :], 512, "outproj_ln")

    w = a_w_in[0]
    c0 = MAIN_W
    cols = [w[:, c0 + kk * HEAD_DIM:c0 + (kk + 1) * HEAD_DIM] for kk in range(6)]
    c_g = c0 + 6 * HEAD_DIM
    w_g = jnp.pad(w[:, c_g:c_g + 3 * N_HEADS], ((0, 0), (0, LANES - 3 * N_HEADS)))
    w_qm = w[:, c_g + 3 * N_HEADS:]
    weights = [w[:, :MAIN_W], jnp.concatenate(cols[0:2], axis=1), _dup(cols[2]), _dup(cols[3]),
               _dup(cols[4]), _dup(cols[5]), w_g, w_qm]
    weights = [wi.astype(BF16) for wi in weights]
    q, kvc_tok, ks2, vs2, kw2, vw2, g, qm = _proj(
        xf, weights, [BF16, BF16, BF16, BF16, BF16, BF16, F32, BF16], 512, "proj_a",
        scales=[Q_SCALE, None, None, None, None, None, None, Q_SCALE])

    pe, w1, w2 = _cmp_weights(a_cmp_pe_k[0], a_cmp_w1_k[0], a_cmp_w2_k[0],
                              a_cmp_pe_v[0], a_cmp_w1_v[0], a_cmp_w2_v[0])
    kvc = _compress(kvc_tok.reshape(bsz, seq // (CMP_LEN // 2), (CMP_LEN // 2) * LANES), pe, w1, w2)

    r3 = lambda t: t.reshape(bsz, seq, t.shape[-1])
    o_main = _nsa_attention(r3(q), r3(g), kvc, r3(ks2), r3(vs2), r3(kw2), r3(vw2), bb, far, bc, ovl_t)
    (mkv,) = _proj(memf, [a_w_mem_kv[0].astype(BF16)], [BF16], 512, "proj_mem_a")
    o_mem = _mem_attention(r3(qm), mkv.reshape(bsz, n_mem, 2 * MEM_W))
    x1 = out_ln(o_main, o_mem, a_w_out[0], xf, 0)
    x1 = ffn(x1, 0)

    wb = b_w_in[0].astype(BF16)
    wkv = shared_w_kv.astype(BF16)
    q, qm, k, v, kmean = _proj_b(x1, wb[:, :MAIN_W], wb[:, MAIN_W:], wkv[:, :MAIN_W], wkv[:, MAIN_W:])
    n_blk = seq // MOBA_BLOCK
    kmean = jnp.pad(kmean.reshape(bsz, n_blk, MAIN_W), ((0, 0), (0, LANES - n_blk), (0, 0)))
    o_main = _moba_attention(r3(q), r3(k), r3(v), kmean, bb, far)
    (mkv,) = _proj(memf, [b_w_mem_kv[0].astype(BF16)], [BF16], 512, "proj_mem_b")
    o_mem = _mem_attention(r3(qm), mkv.reshape(bsz, n_mem, 2 * MEM_W))
    x2 = out_ln(o_main, o_mem, b_w_out[0], x1, 1)
    x2 = ffn(x2, 1)
    return x2.reshape(bsz, seq, d)
```

```python
import functools
import math

import numpy as np
import jax
import jax.numpy as jnp
from jax import lax
from jax.experimental import pallas as pl
from jax.experimental.pallas import tpu as pltpu

F32 = jnp.float32
BF16 = jnp.bfloat16

D_MODEL = 1024
HEAD_DIM = 64
N_HEADS = 12
N_PAIRS = N_HEADS // 2
MAIN_W = N_HEADS * HEAD_DIM
MEM_HEADS = 4
MEM_W = MEM_HEADS * HEAD_DIM
CMP_LEN = 32
CMP_STRIDE = 16
CMP_HIDDEN = 256
SLC_BLOCK = 64
SLC_TOPK = 16
WINDOW = 512
FORCE_SCORE = 1.0e4
MOBA_BLOCK = 256
MOBA_TOPK = 3
REL_BUCKETS = 32
REL_MAX_DIST = 128
D_FF = 2816
CONV_W = 3
DEPTH = 2
ALPHA = (2.0 * DEPTH) ** 0.25
LN_EPS = 1e-5
NEG_INF = -1e30
TINY = 1e-30
LOG2E = math.log2(math.e)
Q_SCALE = HEAD_DIM ** -0.5 * LOG2E

LANES = 128
TQ = 256
VMEM_LIMIT = 60 * 1024 * 1024
V_ROWS = HEAD_DIM + 8
QK_LOOKAHEAD = 6


def _dot(a, b):
    return jnp.dot(a, b, preferred_element_type=F32)


def _dot_nt(a, b):
    return lax.dot_general(a, b, (((1,), (1,)), ((), ())), preferred_element_type=F32)


def _params(sem, vmem=None):
    return pltpu.CompilerParams(dimension_semantics=sem, vmem_limit_bytes=vmem)


def _proj_kernel(x_ref, w_ref, *o_refs, scales):
    y = _dot(x_ref[...].astype(BF16), w_ref[...])
    col = 0
    for o_ref, scale in zip(o_refs, scales):
        part = y[:, col:col + o_ref.shape[1]]
        o_ref[...] = (part if scale is None else part * scale).astype(o_ref.dtype)
        col += o_ref.shape[1]


def _proj(x, weights, dtypes, tm, name, scales=None):
    m, k = x.shape
    scales = tuple(scales) if scales is not None else (None,) * len(weights)
    assert all(w.shape[1] % LANES == 0 for w in weights)
    w_all = jnp.concatenate(weights, axis=1) if len(weights) > 1 else weights[0]
    out_specs = [pl.BlockSpec((tm, w.shape[1]), lambda i: (i, 0)) for w in weights]
    out_shape = [jax.ShapeDtypeStruct((m, w.shape[1]), dt) for w, dt in zip(weights, dtypes)]
    return pl.pallas_call(
        functools.partial(_proj_kernel, scales=scales), grid=(m // tm,),
        in_specs=[pl.BlockSpec((tm, k), lambda i: (i, 0)), pl.BlockSpec(w_all.shape, lambda i: (0, 0))],
        out_specs=out_specs, out_shape=out_shape,
        compiler_params=_params(("arbitrary",), VMEM_LIMIT), name=name)(x, w_all)


def _projb_kernel(x_ref, wq_ref, wm_ref, wk_ref, wv_ref, q_ref, qm_ref, k_ref, v_ref, km_ref):
    xb = x_ref[...].astype(BF16)
    q_ref[...] = (_dot(xb, wq_ref[...]) * Q_SCALE).astype(BF16)
    qm_ref[...] = (_dot(xb, wm_ref[...]) * Q_SCALE).astype(BF16)
    k = _dot(xb, wk_ref[...])
    k_ref[...] = k.astype(BF16)
    km_ref[...] = jnp.mean(k, axis=0, keepdims=True)
    v_ref[...] = _dot(xb, wv_ref[...]).astype(BF16)


def _proj_b(x, wq, wm, wk, wv):
    m, k = x.shape
    nblk = m // MOBA_BLOCK
    row = lambda i: (i, 0)
    full = lambda i: (0, 0)
    return pl.pallas_call(
        _projb_kernel, grid=(nblk,),
        in_specs=[pl.BlockSpec((MOBA_BLOCK, k), row)] + [pl.BlockSpec(w.shape, full) for w in (wq, wm, wk, wv)],
        out_specs=[pl.BlockSpec((MOBA_BLOCK, MAIN_W), row), pl.BlockSpec((MOBA_BLOCK, MEM_W), row),
                   pl.BlockSpec((MOBA_BLOCK, MAIN_W), row), pl.BlockSpec((MOBA_BLOCK, MAIN_W), row),
                   pl.BlockSpec((None, 1, MAIN_W), lambda i: (i, 0, 0))],
        out_shape=[jax.ShapeDtypeStruct((m, MAIN_W), BF16), jax.ShapeDtypeStruct((m, MEM_W), BF16),
                   jax.ShapeDtypeStruct((m, MAIN_W), BF16), jax.ShapeDtypeStruct((m, MAIN_W), BF16),
                   jax.ShapeDtypeStruct((nblk, 1, MAIN_W), F32)],
        compiler_params=_params(("arbitrary",), VMEM_LIMIT), name="proj_b")(x, wq, wm, wk, wv)


def _bias_kernel(tbl_ref, bb_ref, bc_ref, far_ref):
    h = pl.program_id(0)

    def bias_of(dist):
        n = jnp.maximum(dist, 0)
        max_exact = REL_BUCKETS // 2
        nf = jnp.maximum(n, 1).astype(F32)
        large = max_exact + (jnp.log(nf / max_exact) / math.log(REL_MAX_DIST / max_exact)
                             * (REL_BUCKETS - max_exact)).astype(jnp.int32)
        large = jnp.minimum(large, REL_BUCKETS - 1)
        bucket = jnp.where(n < max_exact, n, large)
        out = jnp.zeros(dist.shape, F32)
        for kk in range(REL_BUCKETS):
            out = jnp.where(bucket == kk, tbl_ref[h * REL_BUCKETS + kk], out)
        return out * LOG2E

    key = lax.broadcasted_iota(jnp.int32, (TQ, TQ), 0)
    qry = lax.broadcasted_iota(jnp.int32, (TQ, TQ), 1)
    d0 = qry - key
    bb_ref[0] = jnp.where(d0 >= 0, bias_of(d0), NEG_INF)
    bb_ref[1] = bias_of(d0 + TQ)
    far = bias_of(d0 + 2 * TQ)
    bb_ref[2] = far
    bb_ref[3] = jnp.where(d0 + 2 * TQ < WINDOW, far, NEG_INF)
    far_ref[...] = far[0:1, :]

    nc = lax.broadcasted_iota(jnp.int32, (LANES, TQ), 0)
    qc = lax.broadcasted_iota(jnp.int32, (LANES, TQ), 1)
    n_cmp = (2048 - CMP_LEN) // CMP_STRIDE + 1
    for i in range(bc_ref.shape[0]):
        dc = i * TQ + qc - (nc * CMP_STRIDE + CMP_LEN - 1)
        bc_ref[i] = jnp.where((dc >= 0) & (nc < n_cmp), bias_of(dc), NEG_INF)


def _bias_tables(rel_bias, n_qt):
    tbl = rel_bias.T.reshape(-1)
    return pl.pallas_call(
        _bias_kernel, grid=(N_HEADS,),
        in_specs=[pl.BlockSpec(memory_space=pltpu.SMEM)],
        out_specs=[pl.BlockSpec((None, 4, TQ, TQ), lambda h: (h, 0, 0, 0)),
                   pl.BlockSpec((n_qt, None, LANES, TQ), lambda h: (0, h, 0, 0)),
                   pl.BlockSpec((None, 1, TQ), lambda h: (h, 0, 0))],
        out_shape=[jax.ShapeDtypeStruct((N_HEADS, 4, TQ, TQ), F32),
                   jax.ShapeDtypeStruct((n_qt, N_HEADS, LANES, TQ), F32),
                   jax.ShapeDtypeStruct((N_HEADS, 1, TQ), F32)],
        compiler_params=_params(("arbitrary",)), name="bias_tables")(tbl)


def _cmp_kernel(kv_ref, pe_ref, w1_ref, w2_ref, o_ref):
    x = kv_ref[...].astype(F32)
    lo = _dot((x + pe_ref[0:1, :]).astype(BF16), w1_ref[0])
    hi = _dot((x + pe_ref[1:2, :]).astype(BF16), w1_ref[1])
    nrow = x.shape[0]
    hid = lo + pltpu.roll(hi, nrow - 1, 0)
    o_ref[...] = _dot(jax.nn.gelu(hid).astype(BF16), w2_ref[...])


def _compress(kvr, pe, w1, w2):
    b, nrow, width = kvr.shape
    return pl.pallas_call(
        _cmp_kernel, grid=(b,),
        in_specs=[pl.BlockSpec((None, nrow, width), lambda i: (i, 0, 0)),
                  pl.BlockSpec(pe.shape, lambda i: (0, 0)),
                  pl.BlockSpec(w1.shape, lambda i: (0, 0, 0)),
                  pl.BlockSpec(w2.shape, lambda i: (0, 0))],
        out_specs=pl.BlockSpec((None, nrow, w2.shape[1]), lambda i: (i, 0, 0)),
        out_shape=jax.ShapeDtypeStruct((b, nrow, w2.shape[1]), F32),
        compiler_params=_params(("arbitrary",), VMEM_LIMIT), name="nsa_compress")(kvr, pe, w1, w2)


def _softmax_tile(state, c, s_t, v_t, vis=None, cbias=None):
    m_ref, acc_ref = state
    m = m_ref[c]
    n_blk = 1 if vis is None else vis.shape[0]
    rows = s_t.shape[0] // n_blk
    parts = [s_t[b * rows:(b + 1) * rows] for b in range(n_blk)]
    tile_max = None
    for b in range(n_blk):
        mb = jnp.max(parts[b], axis=0, keepdims=True)
        if vis is not None:
            mb = jnp.where(vis[b:b + 1] > 0.5, mb, NEG_INF)
        tile_max = mb if tile_max is None else jnp.maximum(tile_max, mb)
    if cbias is not None:
        tile_max = tile_max + cbias
    m_new = jnp.maximum(m, tile_max)
    shift = m_new if cbias is None else m_new - cbias
    p_parts = []
    for b in range(n_blk):
        off = shift if vis is None else jnp.where(vis[b:b + 1] > 0.5, shift, -NEG_INF)
        p_parts.append(jnp.exp2(parts[b] - off))
    p_t = p_parts[0] if n_blk == 1 else jnp.concatenate(p_parts, axis=0)
    alpha = jnp.exp2(m - m_new)
    acc_ref[c] = alpha * acc_ref[c] + _dot(v_t, p_t.astype(BF16))
    m_ref[c] = m_new


def _run_chains(chains, logits_fn, update_fn):
    pending = {}
    for idx, c in enumerate(chains[:QK_LOOKAHEAD]):
        pending[c] = logits_fn(c, idx)
    for idx, c in enumerate(chains):
        update_fn(c, pending.pop(c), idx % QK_LOOKAHEAD)
        if idx + QK_LOOKAHEAD < len(chains):
            nxt = chains[idx + QK_LOOKAHEAD]
            pending[nxt] = logits_fn(nxt, idx % QK_LOOKAHEAD)


def _pair_loop(lo, hi, tiles_fn):
    odd = (hi - lo) % 2

    def single(_, carry):
        tiles_fn([lo])
        return carry

    def double(r, carry):
        jj = lo + odd + 2 * r
        tiles_fn([jj, jj + 1])
        return carry

    lax.fori_loop(0, odd, single, 0)
    lax.fori_loop(0, (hi - lo) // 2, double, 0)


def _stage_logits(s_ref, slot, s_t):
    s_ref[slot] = s_t


def _init_state(state):
    m_ref, acc_ref = state
    m_ref[...] = jnp.full(m_ref.shape, NEG_INF, F32)
    acc_ref[...] = jnp.zeros(acc_ref.shape, F32)


def _chain_out(state, c):
    _, acc_ref = state
    return acc_ref[c, 0:HEAD_DIM, :] / jnp.maximum(acc_ref[c, HEAD_DIM:HEAD_DIM + 1, :], TINY)


def _values_with_ones(tile, row0):
    v_t = tile.astype(F32).T[row0:row0 + HEAD_DIM]
    pad = lax.broadcasted_iota(jnp.int32, (V_ROWS - HEAD_DIM, tile.shape[0]), 0)
    return jnp.concatenate([v_t, jnp.where(pad == 0, 1.0, 0.0)], axis=0).astype(BF16)


def _rows(ref, kt, cols=slice(None)):
    start = kt * TQ if isinstance(kt, int) else pl.multiple_of(kt * TQ, TQ)
    return ref[pl.ds(start, TQ), cols]


def _store_head_queries(q_ref, qh_ref, n_pairs):
    row = lax.broadcasted_iota(jnp.int32, (LANES, TQ), 0)
    halves = (row < HEAD_DIM, row >= HEAD_DIM)
    for pp in range(n_pairs):
        q2t = q_ref[:, pp * LANES:(pp + 1) * LANES].astype(F32).T
        for j in range(2):
            qh_ref[2 * pp + j] = jnp.where(halves[j], q2t, 0.0).astype(BF16)


def _per_query_tile(step_fn, n_tiles):
    def kernel_fn(*refs):
        for i in range(n_tiles):
            pl.when(pl.program_id(1) == i)(functools.partial(step_fn, *refs, i=i))
    return kernel_fn


def _nsa_kernel(q_ref, g_ref, kvc_ref, ks_ref, vs_ref, kw_ref, vw_ref, bb_ref, far_ref, bc_ref, ovl_ref,
                o_ref, vst_ref, vwt_ref, qh_ref, gs_ref, oc_ref, sel_ref, m_ref, acc_ref, s_ref):
    i = pl.program_id(1)
    n_kt = vst_ref.shape[0]
    state = (m_ref, acc_ref)

    @pl.when(i == 0)
    def _values():
        for kt in range(n_kt):
            rows = slice(kt * TQ, (kt + 1) * TQ)
            vst_ref[kt] = _values_with_ones(vs_ref[rows, :], 0)
            vwt_ref[kt] = _values_with_ones(vw_ref[rows, :], 0)

    per_tile = TQ // SLC_BLOCK
    win_tiles = WINDOW // TQ
    n_near = jnp.minimum(i, win_tiles) + 1

    def select_blocks():
        _store_head_queries(q_ref, qh_ref, N_PAIRS)
        gs_ref[...] = jax.nn.sigmoid(g_ref[...]).T
        kc2 = kvc_ref[:, 0:LANES].astype(BF16)
        vc_t = kvc_ref[:, LANES:2 * LANES].T[0:HEAD_DIM].astype(BF16)
        psums = []

        def cmp_logits(h, slot):
            s_ref[slot, 0:LANES, :] = _dot(kc2, qh_ref[h])

        def cmp_update(h, _s, slot):
            bias = bc_ref[h]
            s = s_ref[slot, 0:LANES, :] + bias
            m = jnp.maximum(jnp.max(s, axis=0, keepdims=True), 0.5 * NEG_INF)
            e = jnp.exp2(s - m)
            pc = e / jnp.maximum(jnp.sum(e, axis=0, keepdims=True), TINY)
            psums[:] = [pc if not psums else psums[0] + pc]
            oc_ref[h] = gs_ref[3 * h:3 * h + 1, :] * _dot(vc_t, pc.astype(BF16))

        _run_chains(list(range(N_HEADS)), cmp_logits, cmp_update)
        psum = psums[0]
        p_hi = psum.astype(BF16)
        p_lo = (psum - p_hi.astype(F32)).astype(BF16)
        imp = _dot(ovl_ref[...], p_hi) + _dot(ovl_ref[...], p_lo)
        n_slc = n_kt * per_tile
        imp = imp[0:n_slc, :]
        sidx = lax.broadcasted_iota(jnp.int32, (n_slc, TQ), 0)
        t = i * TQ + lax.broadcasted_iota(jnp.int32, (n_slc, TQ), 1)
        cur = lax.shift_right_logical(t, 6)
        eligible = sidx <= cur
        forced = (sidx == 0) | (sidx == cur) | (sidx == cur - 1)
        score = jnp.where(eligible, jnp.where(forced, FORCE_SCORE, 0.0), NEG_INF) + imp
        cnt = jnp.zeros((n_slc, TQ), jnp.int32)
        for sp in range(n_slc):
            row = score[sp:sp + 1, :]
            better = (row > score) | ((row == score) & (sp < sidx))
            cnt = cnt + better.astype(jnp.int32)
        chosen = jnp.where(cnt < min(SLC_TOPK, n_slc), 1.0, 0.0)
        for kt in range(n_kt):
            sel_ref[kt, 0:per_tile, :] = chosen[kt * per_tile:(kt + 1) * per_tile, :]
        _init_state(state)

    def near_tiles(n_tiles):
        def logits(c, slot):
            kind, jj, h = c
            _stage_logits(s_ref, slot, _dot(_rows(ks_ref if kind == "s" else kw_ref, i - jj), qh_ref[h]))

        def update(c, _s, slot):
            kind, jj, h = c
            kt = i - jj
            if kind == "s":
                _softmax_tile(state, h, s_ref[slot] + bb_ref[h, min(jj, 2)], vst_ref[kt],
                              vis=sel_ref[kt, 0:per_tile, :])
            else:
                _softmax_tile(state, N_HEADS + h, s_ref[slot] + bb_ref[h, 3 if jj == win_tiles else jj], vwt_ref[kt])

        _run_chains([(kind, jj, h) for jj in range(n_tiles) for h in range(N_HEADS) for kind in ("s", "w")],
                    logits, update)

    def far_tiles(jjs):
        kts = [i - jj for jj in jjs]
        _run_chains([(n, h) for n in range(len(jjs)) for h in range(N_HEADS)],
                    lambda c, slot: _dot(_rows(ks_ref, kts[c[0]]), qh_ref[c[1]]),
                    lambda c, s, slot: _softmax_tile(state, c[1], s, vst_ref[kts[c[0]]],
                                                     vis=sel_ref[kts[c[0]], 0:per_tile, :], cbias=far_ref[c[1]]))

    def front(n_tiles):
        select_blocks()
        near_tiles(n_tiles)

    for n_tiles in range(1, win_tiles + 2):
        last = n_tiles == win_tiles + 1
        pl.when((i >= n_tiles - 1) if last else (i == n_tiles - 1))(functools.partial(front, n_tiles))
    _pair_loop(n_near, i + 1, far_tiles)

    for pp in range(N_PAIRS):
        hs = (2 * pp, 2 * pp + 1)
        outs = []
        for h in hs:
            g1 = gs_ref[3 * h + 1:3 * h + 2, :]
            g2 = gs_ref[3 * h + 2:3 * h + 3, :]
            outs.append(oc_ref[h] + g1 * _chain_out(state, h) + g2 * _chain_out(state, N_HEADS + h))
        o_ref[:, pp * LANES:(pp + 1) * LANES] = jnp.concatenate(outs, axis=0).T.astype(o_ref.dtype)


def _nsa_attention(q, g, kvc, ks2, vs2, kw2, vw2, bb, far, bc, ovl_t):
    b, s, _ = q.shape
    n_qt = s // TQ
    per_b = lambda bi, i: (bi, 0, 0)
    return pl.pallas_call(
        _nsa_kernel, grid=(b, n_qt),
        in_specs=[pl.BlockSpec((None, TQ, MAIN_W), lambda bi, i: (bi, i, 0)),
                  pl.BlockSpec((None, TQ, LANES), lambda bi, i: (bi, i, 0)),
                  pl.BlockSpec((None,) + kvc.shape[1:], per_b),
                  pl.BlockSpec((None, s, LANES), per_b), pl.BlockSpec((None, s, LANES), per_b),
                  pl.BlockSpec((None, s, LANES), per_b), pl.BlockSpec((None, s, LANES), per_b),
                  pl.BlockSpec(bb.shape, lambda bi, i: (0, 0, 0, 0)),
                  pl.BlockSpec(far.shape, lambda bi, i: (0, 0, 0)),
                  pl.BlockSpec((None,) + bc.shape[1:], lambda bi, i: (i, 0, 0, 0)),
                  pl.BlockSpec(ovl_t.shape, lambda bi, i: (0, 0))],
        out_specs=pl.BlockSpec((None, TQ, MAIN_W), lambda bi, i: (bi, i, 0)),
        out_shape=jax.ShapeDtypeStruct((b, s, MAIN_W), BF16),
        scratch_shapes=[pltpu.VMEM((n_qt, V_ROWS, TQ), BF16), pltpu.VMEM((n_qt, V_ROWS, TQ), BF16),
                        pltpu.VMEM((N_HEADS, LANES, TQ), BF16), pltpu.VMEM((LANES, TQ), F32),
                        pltpu.VMEM((N_HEADS, HEAD_DIM, TQ), F32), pltpu.VMEM((n_qt, 8, TQ), F32),
                        pltpu.VMEM((2 * N_HEADS, 1, TQ), F32),
                        pltpu.VMEM((2 * N_HEADS, V_ROWS, TQ), F32), pltpu.VMEM((QK_LOOKAHEAD, TQ, TQ), F32)],
        compiler_params=_params(("arbitrary", "arbitrary"), VMEM_LIMIT),
        name="nsa_attention")(q, g, kvc, ks2, vs2, kw2, vw2, bb, far, bc, ovl_t)


def _moba_step(q_ref, k_ref, v_ref, km_ref, bb_ref, far_ref, o_ref, vt_ref, qh_ref, sel_ref,
               m_ref, acc_ref, s_ref, *, i):
    n_blk = vt_ref.shape[1]
    state = (m_ref, acc_ref)
    heads = list(range(N_HEADS))

    if i == 0:
        for pp in range(N_PAIRS):
            for kt in range(n_blk):
                tile = v_ref[kt * TQ:(kt + 1) * TQ, pp * LANES:(pp + 1) * LANES]
                for j in range(2):
                    vt_ref[2 * pp + j, kt] = _values_with_ones(tile, j * HEAD_DIM)

    def gate_blocks():
        _store_head_queries(q_ref, qh_ref, N_PAIRS)
        blk = lax.broadcasted_iota(jnp.int32, (n_blk, TQ), 0)
        eligible = blk < i
        for pp in range(N_PAIRS):
            km = km_ref[:, pp * LANES:(pp + 1) * LANES].astype(BF16)
            for j in range(2):
                gate = _dot(km, qh_ref[2 * pp + j])[0:n_blk, :]
                gate = jnp.where(eligible, gate, NEG_INF)
                cnt = jnp.zeros((n_blk, TQ), jnp.int32)
                for n in range(n_blk):
                    row = gate[n:n + 1, :]
                    better = (row > gate) | ((row == gate) & (n < blk))
                    cnt = cnt + better.astype(jnp.int32)
                chosen = (cnt < min(MOBA_TOPK, n_blk - 1)) & eligible
                vis = jnp.where(chosen | (blk == i), 1.0, 0.0)
                for n in range(n_blk):
                    sel_ref[2 * pp + j, n] = vis[n:n + 1, :]
        _init_state(state)

    def tile_inputs(kt, h):
        pp = h // 2
        k = _rows(k_ref, kt, slice(pp * LANES, (pp + 1) * LANES))
        return k, vt_ref[h, kt], sel_ref[h, kt]

    def front(n_tiles):
        gate_blocks()

        def update(c, _s, slot):
            jj, h = c
            _, v_t, vis = tile_inputs(i - jj, h)
            _softmax_tile(state, h, s_ref[slot] + bb_ref[h, jj], v_t, vis=vis)

        def logits(c, slot):
            jj, h = c
            _stage_logits(s_ref, slot, _dot(tile_inputs(i - jj, h)[0], qh_ref[h]))

        _run_chains([(jj, h) for jj in range(n_tiles) for h in heads], logits, update)

    n_near = min(i, 1) + 1

    def far_tiles(jjs):
        kts = [i - jj for jj in jjs]

        def update(c, s, slot):
            _, v_t, vis = tile_inputs(kts[c[0]], c[1])
            _softmax_tile(state, c[1], s, v_t, vis=vis, cbias=far_ref[c[1]])

        _run_chains([(n, h) for n in range(len(jjs)) for h in heads],
                    lambda c, slot: _dot(tile_inputs(kts[c[0]], c[1])[0], qh_ref[c[1]]), update)

    front(n_near)
    if i + 1 > n_near:
        far_tiles(list(range(n_near, i + 1)))

    for pp in range(N_PAIRS):
        out_t = jnp.concatenate([_chain_out(state, 2 * pp), _chain_out(state, 2 * pp + 1)], axis=0)
        o_ref[:, pp * LANES:(pp + 1) * LANES] = out_t.T.astype(o_ref.dtype)


def _moba_attention(q, k, v, kmean, bb, far):
    b, s, _ = q.shape
    n_qt = s // TQ
    per_b = lambda bi, i: (bi, 0, 0)
    return pl.pallas_call(
        _per_query_tile(_moba_step, n_qt), grid=(b, n_qt),
        in_specs=[pl.BlockSpec((None, TQ, MAIN_W), lambda bi, i: (bi, i, 0)),
                  pl.BlockSpec((None, s, MAIN_W), per_b), pl.BlockSpec((None, s, MAIN_W), per_b),
                  pl.BlockSpec((None, LANES, MAIN_W), per_b),
                  pl.BlockSpec((N_HEADS, 2, TQ, TQ), lambda bi, i: (0, 0, 0, 0)),
                  pl.BlockSpec(far.shape, lambda bi, i: (0, 0, 0))],
        out_specs=pl.BlockSpec((None, TQ, MAIN_W), lambda bi, i: (bi, i, 0)),
        out_shape=jax.ShapeDtypeStruct((b, s, MAIN_W), BF16),
        scratch_shapes=[pltpu.VMEM((N_HEADS, n_qt, V_ROWS, TQ), BF16), pltpu.VMEM((N_HEADS, LANES, TQ), BF16),
                        pltpu.VMEM((N_HEADS, n_qt, 1, TQ), F32),
                        pltpu.VMEM((N_HEADS, 1, TQ), F32),
                        pltpu.VMEM((N_HEADS, V_ROWS, TQ), F32), pltpu.VMEM((QK_LOOKAHEAD, TQ, TQ), F32)],
        compiler_params=_params(("arbitrary", "arbitrary"), VMEM_LIMIT),
        name="moba_attention")(q, k, v, kmean, bb, far)


def _mem_kernel(q_ref, kv_ref, o_ref, qh_ref):
    n_sub = q_ref.shape[0] // TQ
    for sub in range(n_sub):
        _store_head_queries(q_ref.at[sub * TQ:(sub + 1) * TQ, :], qh_ref.at[sub], MEM_HEADS // 2)
    outs = {}

    def logits(c, slot):
        sub, h = c
        return _dot(kv_ref[:, (h // 2) * LANES:(h // 2 + 1) * LANES], qh_ref[sub, h])

    def update(c, s, slot):
        h = c[1]
        lanes = slice(MEM_W + (h // 2) * LANES, MEM_W + (h // 2 + 1) * LANES)
        v_t = kv_ref[:, lanes].astype(F32).T[(h % 2) * HEAD_DIM:(h % 2 + 1) * HEAD_DIM].astype(BF16)
        e = jnp.exp2(s - jnp.max(s, axis=0, keepdims=True))
        pr = e / jnp.sum(e, axis=0, keepdims=True)
        outs[c] = _dot(v_t, pr.astype(BF16))

    _run_chains([(sub, h) for sub in range(n_sub) for h in range(MEM_HEADS)], logits, update)
    for sub in range(n_sub):
        for pp in range(MEM_HEADS // 2):
            pair = jnp.concatenate([outs[(sub, 2 * pp)], outs[(sub, 2 * pp + 1)]], axis=0)
            o_ref[sub * TQ:(sub + 1) * TQ, pp * LANES:(pp + 1) * LANES] = pair.T.astype(o_ref.dtype)


MEM_TILES = 2


def _mem_attention(qm, mem_kv):
    b, s, _ = qm.shape
    n_mem = mem_kv.shape[1]
    rows = MEM_TILES * TQ
    return pl.pallas_call(
        _mem_kernel, grid=(b, s // rows),
        in_specs=[pl.BlockSpec((None, rows, MEM_W), lambda bi, i: (bi, i, 0)),
                  pl.BlockSpec((None, n_mem, 2 * MEM_W), lambda bi, i: (bi, 0, 0))],
        out_specs=pl.BlockSpec((None, rows, MEM_W), lambda bi, i: (bi, i, 0)),
        out_shape=jax.ShapeDtypeStruct((b, s, MEM_W), BF16),
        scratch_shapes=[pltpu.VMEM((MEM_TILES, MEM_HEADS, LANES, TQ), BF16)],
        compiler_params=_params(("arbitrary", "arbitrary")), name="mem_attention")(qm, mem_kv)


def _layer_norm(y, g, b):
    mu = jnp.mean(y, axis=-1, keepdims=True)
    var = jnp.mean(jnp.square(y - mu), axis=-1, keepdims=True)
    return (y - mu) * lax.rsqrt(var + LN_EPS) * g + b


def _mm_ln_kernel(*refs, n_in):
    a_refs, w_refs = refs[:n_in], refs[n_in:2 * n_in]
    x_ref, g_ref, b_ref, o_ref, y_ref = refs[2 * n_in:]

    @pl.when(pl.program_id(0) == 0)
    def _():
        y_ref[...] = jnp.zeros(y_ref.shape, F32)

    o_ref[...] = _layer_norm(ALPHA * x_ref[...] + y_ref[...], g_ref[...], b_ref[...])
    acc = _dot(a_refs[0][...], w_refs[0][...])
    for a_ref, w_ref in zip(a_refs[1:], w_refs[1:]):
        acc = acc + _dot(a_ref[...], w_ref[...])
    y_ref[...] = acc


def _mm_ln(acts, weights, x, g, b, tm, name):
    m, d = x.shape
    n_t = m // tm
    cur = lambda t: (jnp.minimum(t, n_t - 1), 0)
    prev = lambda t: (jnp.maximum(t - 1, 0), 0)
    full = lambda t: (0, 0)
    return pl.pallas_call(
        functools.partial(_mm_ln_kernel, n_in=len(acts)), grid=(n_t + 1,),
        in_specs=[pl.BlockSpec((tm, a.shape[1]), cur) for a in acts]
                 + [pl.BlockSpec(w.shape, full) for w in weights]
                 + [pl.BlockSpec((tm, d), prev), pl.BlockSpec((1, d), full), pl.BlockSpec((1, d), full)],
        out_specs=pl.BlockSpec((tm, d), prev),
        out_shape=jax.ShapeDtypeStruct((m, d), F32),
        scratch_shapes=[pltpu.VMEM((tm, d), F32)],
        compiler_params=_params(("arbitrary",), VMEM_LIMIT), name=name)(*acts, *weights, x, g, b)


FF_CHUNK = 256
FF_ROWS = 512
SUBLANES = 8


def _ffn_kernel(x_ref, wa_ref, wb_ref, cw_ref, cb_ref, wo_ref, wol_ref, g_ref, b_ref, o_ref, xb_ref, hid_ref, *,
                n_c):
    c = pl.program_id(1)

    @pl.when(c == 0)
    def _():
        xb_ref[...] = x_ref[...].astype(BF16)
        hid_ref[1] = jnp.zeros(hid_ref.shape[1:], BF16)
        o_ref[...] = jnp.zeros(o_ref.shape, F32)

    xb = xb_ref[...]
    a = _dot(xb, wa_ref[...])
    o_ref[...] += _dot(hid_ref[(c + 1) % 2], wo_ref[...])
    gate = _dot(xb, wb_ref[...])
    ext = jnp.concatenate([jnp.zeros((SUBLANES, a.shape[1]), F32), a], axis=0)
    a1 = pltpu.roll(ext, 1, 0)[SUBLANES:]
    a2 = pltpu.roll(ext, 2, 0)[SUBLANES:]
    conv = cw_ref[0:1, :] * a2 + cw_ref[1:2, :] * a1 + cw_ref[2:3, :] * a + cb_ref[...]
    hid_ref[c % 2] = (jax.nn.gelu(conv) * gate).astype(BF16)

    @pl.when(c == n_c - 1)
    def _():
        for r0 in range(0, o_ref.shape[0], FF_ROWS):
            rows = slice(r0, r0 + FF_ROWS)
            y = o_ref[rows, :] + _dot(hid_ref[(n_c - 1) % 2, rows, :], wol_ref[...])
            o_ref[rows, :] = _layer_norm(ALPHA * x_ref[rows, :] + y, g_ref[...], b_ref[...])


def _ffn_ln(x, w_in, conv_w, conv_b, w_out, g, b):
    bsz, s, d = x.shape
    n_c = D_FF // FF_CHUNK
    prev = lambda c: jnp.maximum(c - 1, 0)
    return pl.pallas_call(
        functools.partial(_ffn_kernel, n_c=n_c), grid=(bsz, n_c),
        in_specs=[pl.BlockSpec((None, s, d), lambda bi, c: (bi, 0, 0)),
                  pl.BlockSpec((d, FF_CHUNK), lambda bi, c: (0, c)),
                  pl.BlockSpec((d, FF_CHUNK), lambda bi, c: (0, n_c + c)),
                  pl.BlockSpec((CONV_W, FF_CHUNK), lambda bi, c: (0, c)),
                  pl.BlockSpec((1, FF_CHUNK), lambda bi, c: (0, c)),
                  pl.BlockSpec((FF_CHUNK, d), lambda bi, c: (prev(c), 0)),
                  pl.BlockSpec((FF_CHUNK, d), lambda bi, c: (n_c - 1, 0)),
                  pl.BlockSpec((1, d), lambda bi, c: (0, 0)), pl.BlockSpec((1, d), lambda bi, c: (0, 0))],
        out_specs=pl.BlockSpec((None, s, d), lambda bi, c: (bi, 0, 0)),
        out_shape=jax.ShapeDtypeStruct((bsz, s, d), F32),
        scratch_shapes=[pltpu.VMEM((s, d), BF16), pltpu.VMEM((2, s, FF_CHUNK), BF16)],
        compiler_params=_params(("arbitrary", "arbitrary"), VMEM_LIMIT),
        name="ffn_ln")(x, w_in, w_in, conv_w, conv_b, w_out, w_out, g, b)


def _dup(w):
    return jnp.concatenate([w, w], axis=1)


def _cmp_weights(pe_k, w1_k, w2_k, pe_v, w1_v, w2_v):
    half = CMP_LEN // 2
    zk = jnp.zeros((half, HEAD_DIM, CMP_HIDDEN), F32)

    def w1_half(sl):
        wk = w1_k.reshape(CMP_LEN, HEAD_DIM, CMP_HIDDEN)[sl]
        wv = w1_v.reshape(CMP_LEN, HEAD_DIM, CMP_HIDDEN)[sl]
        top = jnp.concatenate([wk, zk], axis=2)
        bot = jnp.concatenate([zk, wv], axis=2)
        return jnp.concatenate([top, bot], axis=1).reshape(half * 2 * HEAD_DIM, 2 * CMP_HIDDEN)

    w1 = jnp.stack([w1_half(slice(0, half)), w1_half(slice(half, CMP_LEN))]).astype(BF16)
    pe = jnp.concatenate([pe_k, pe_v], axis=1)
    pe = jnp.stack([pe[:half].reshape(-1), pe[half:].reshape(-1)])
    zo = jnp.zeros((CMP_HIDDEN, 2 * HEAD_DIM), F32)
    w2 = jnp.concatenate([jnp.concatenate([_dup(w2_k), zo], axis=1),
                          jnp.concatenate([zo, _dup(w2_v)], axis=1)], axis=0).astype(BF16)
    return pe, w1, w2


def _overlap_matrix(seq):
    n_cmp = (seq - CMP_LEN) // CMP_STRIDE + 1
    n_slc = seq // SLC_BLOCK
    start = np.arange(LANES) * CMP_STRIDE
    bs = np.arange(LANES) * SLC_BLOCK
    ovl_t = ((start[None, :] < bs[:, None] + SLC_BLOCK) & (start[None, :] + CMP_LEN > bs[:, None])
             & (np.arange(LANES)[None, :] < n_cmp) & (np.arange(LANES)[:, None] < n_slc))
    return jnp.asarray(ovl_t, BF16)


def kernel(x, mem, rel_bias, a_w_in, a_cmp_pe_k, a_cmp_w1_k, a_cmp_w2_k, a_cmp_pe_v, a_cmp_w1_v, a_cmp_w2_v,
           a_w_mem_kv, a_w_out, shared_w_kv, b_w_in, b_w_mem_kv, b_w_out, ln1_g, ln1_b, ln2_g, ln2_b,
           ffn_w_in, ffn_conv_w, ffn_conv_b, ffn_w_out):
    bsz, seq, d = x.shape
    n_mem = mem.shape[1]
    m = bsz * seq
    assert (seq, d) == (2048, D_MODEL) and seq % TQ == 0
    n_qt = seq // TQ

    bb, bc, far = _bias_tables(rel_bias, n_qt)
    ovl_t = _overlap_matrix(seq)
    memf = mem.reshape(bsz * n_mem, d)
    xf = x.reshape(m, d)

    def ffn(xcur, layer):
        return _ffn_ln(xcur.reshape(bsz, seq, d), ffn_w_in[layer].astype(BF16), ffn_conv_w[layer],
                       ffn_conv_b[layer][None, :], ffn_w_out[layer].astype(BF16),
                       ln2_g[layer][None, :], ln2_b[layer][None, :]).reshape(m, d)

    def out_ln(o_main, o_mem, w_out, xcur, layer):
        w = w_out.astype(BF16)
        return _mm_ln([o_main.reshape(m, MAIN_W), o_mem.reshape(m, MEM_W)], [w[:MAIN_W], w[MAIN_W:]], xcur,
                      ln1_g[layer][None, :], ln1_b[layer][None, :], 512, "outproj_ln")

    w = a_w_in[0]
    c0 = MAIN_W
    cols = [w[:, c0 + kk * HEAD_DIM:c0 + (kk + 1) * HEAD_DIM] for kk in range(6)]
    c_g = c0 + 6 * HEAD_DIM
    w_g = jnp.pad(w[:, c_g:c_g + 3 * N_HEADS], ((0, 0), (0, LANES - 3 * N_HEADS)))
    w_qm = w[:, c_g + 3 * N_HEADS:]
    weights = [w[:, :MAIN_W], jnp.concatenate(cols[0:2], axis=1), _dup(cols[2]), _dup(cols[3]),
               _dup(cols[4]), _dup(cols[5]), w_g, w_qm]
    weights = [wi.astype(BF16) for wi in weights]
    q, kvc_tok, ks2, vs2, kw2, vw2, g, qm = _proj(
        xf, weights, [BF16, BF16, BF16, BF16, BF16, BF16, F32, BF16], 512, "proj_a",
        scales=[Q_SCALE, None, None, None, None, None, None, Q_SCALE])

    pe, w1, w2 = _cmp_weights(a_cmp_pe_k[0], a_cmp_w1_k[0], a_cmp_w2_k[0],
                              a_cmp_pe_v[0], a_cmp_w1_v[0], a_cmp_w2_v[0])
    kvc = _compress(kvc_tok.reshape(bsz, seq // (CMP_LEN // 2), (CMP_LEN // 2) * LANES), pe, w1, w2)

    r3 = lambda t: t.reshape(bsz, seq, t.shape[-1])
    o_main = _nsa_attention(r3(q), r3(g), kvc, r3(ks2), r3(vs2), r3(kw2), r3(vw2), bb, far, bc, ovl_t)
    (mkv,) = _proj(memf, [a_w_mem_kv[0].astype(BF16)], [BF16], 512, "proj_mem_a")
    o_mem = _mem_attention(r3(qm), mkv.reshape(bsz, n_mem, 2 * MEM_W))
    x1 = out_ln(o_main, o_mem, a_w_out[0], xf, 0)
    x1 = ffn(x1, 0)

    wb = b_w_in[0].astype(BF16)
    wkv = shared_w_kv.astype(BF16)
    q, qm, k, v, kmean = _proj_b(x1, wb[:, :MAIN_W], wb[:, MAIN_W:], wkv[:, :MAIN_W], wkv[:, MAIN_W:])
    n_blk = seq // MOBA_BLOCK
    kmean = jnp.pad(kmean.reshape(bsz, n_blk, MAIN_W), ((0, 0), (0, LANES - n_blk), (0, 0)))
    o_main = _moba_attention(r3(q), r3(k), r3(v), kmean, bb, far)
    (mkv,) = _proj(memf, [b_w_mem_kv[0].astype(BF16)], [BF16], 512, "proj_mem_b")
    o_mem = _mem_attention(r3(qm), mkv.reshape(bsz, n_mem, 2 * MEM_W))
    x2 = out_ln(o_main, o_mem, b_w_out[0], x1, 1)
    x2 = ffn(x2, 1)
    return x2.reshape(bsz, seq, d)
```

```python
import functools
import math

import numpy as np
import jax
import jax.numpy as jnp
from jax import lax
from jax.experimental import pallas as pl
from jax.experimental.pallas import tpu as pltpu

F32 = jnp.float32
BF16 = jnp.bfloat16

D_MODEL = 1024
HEAD_DIM = 64
N_HEADS = 12
N_PAIRS = N_HEADS // 2
MAIN_W = N_HEADS * HEAD_DIM
MEM_HEADS = 4
MEM_W = MEM_HEADS * HEAD_DIM
CMP_LEN = 32
CMP_STRIDE = 16
CMP_HIDDEN = 256
SLC_BLOCK = 64
SLC_TOPK = 16
WINDOW = 512
FORCE_SCORE = 1.0e4
MOBA_BLOCK = 256
MOBA_TOPK = 3
REL_BUCKETS = 32
REL_MAX_DIST = 128
D_FF = 2816
CONV_W = 3
DEPTH = 2
ALPHA = (2.0 * DEPTH) ** 0.25
LN_EPS = 1e-5
NEG_INF = -1e30
TINY = 1e-30
LOG2E = math.log2(math.e)
Q_SCALE = HEAD_DIM ** -0.5 * LOG2E

LANES = 128
TQ = 256
VMEM_LIMIT = 60 * 1024 * 1024
V_ROWS = HEAD_DIM + 8
NSA_FRONT_FAR = 2
QK_LOOKAHEAD = 6


def _dot(a, b):
    return jnp.dot(a, b, preferred_element_type=F32)


def _dot_nt(a, b):
    return lax.dot_general(a, b, (((1,), (1,)), ((), ())), preferred_element_type=F32)


def _params(sem, vmem=None):
    return pltpu.CompilerParams(dimension_semantics=sem, vmem_limit_bytes=vmem)


def _proj_kernel(x_ref, w_ref, *o_refs, scales):
    y = _dot(x_ref[...].astype(BF16), w_ref[...])
    col = 0
    for o_ref, scale in zip(o_refs, scales):
        part = y[:, col:col + o_ref.shape[1]]
        o_ref[...] = (part if scale is None else part * scale).astype(o_ref.dtype)
        col += o_ref.shape[1]


def _proj(x, weights, dtypes, tm, name, scales=None):
    m, k = x.shape
    scales = tuple(scales) if scales is not None else (None,) * len(weights)
    assert all(w.shape[1] % LANES == 0 for w in weights)
    w_all = jnp.concatenate(weights, axis=1) if len(weights) > 1 else weights[0]
    out_specs = [pl.BlockSpec((tm, w.shape[1]), lambda i: (i, 0)) for w in weights]
    out_shape = [jax.ShapeDtypeStruct((m, w.shape[1]), dt) for w, dt in zip(weights, dtypes)]
    return pl.pallas_call(
        functools.partial(_proj_kernel, scales=scales), grid=(m // tm,),
        in_specs=[pl.BlockSpec((tm, k), lambda i: (i, 0)), pl.BlockSpec(w_all.shape, lambda i: (0, 0))],
        out_specs=out_specs, out_shape=out_shape,
        compiler_params=_params(("arbitrary",), VMEM_LIMIT), name=name)(x, w_all)


def _projb_kernel(x_ref, wq_ref, wm_ref, wk_ref, wv_ref, q_ref, qm_ref, k_ref, v_ref, km_ref):
    xb = x_ref[...].astype(BF16)
    q_ref[...] = (_dot(xb, wq_ref[...]) * Q_SCALE).astype(BF16)
    qm_ref[...] = (_dot(xb, wm_ref[...]) * Q_SCALE).astype(BF16)
    k = _dot(xb, wk_ref[...])
    k_ref[...] = k.astype(BF16)
    km_ref[...] = jnp.mean(k, axis=0, keepdims=True)
    v_ref[...] = _dot(xb, wv_ref[...]).astype(BF16)


def _proj_b(x, wq, wm, wk, wv):
    m, k = x.shape
    nblk = m // MOBA_BLOCK
    row = lambda i: (i, 0)
    full = lambda i: (0, 0)
    return pl.pallas_call(
        _projb_kernel, grid=(nblk,),
        in_specs=[pl.BlockSpec((MOBA_BLOCK, k), row)] + [pl.BlockSpec(w.shape, full) for w in (wq, wm, wk, wv)],
        out_specs=[pl.BlockSpec((MOBA_BLOCK, MAIN_W), row), pl.BlockSpec((MOBA_BLOCK, MEM_W), row),
                   pl.BlockSpec((MOBA_BLOCK, MAIN_W), row), pl.BlockSpec((MOBA_BLOCK, MAIN_W), row),
                   pl.BlockSpec((None, 1, MAIN_W), lambda i: (i, 0, 0))],
        out_shape=[jax.ShapeDtypeStruct((m, MAIN_W), BF16), jax.ShapeDtypeStruct((m, MEM_W), BF16),
                   jax.ShapeDtypeStruct((m, MAIN_W), BF16), jax.ShapeDtypeStruct((m, MAIN_W), BF16),
                   jax.ShapeDtypeStruct((nblk, 1, MAIN_W), F32)],
        compiler_params=_params(("arbitrary",), VMEM_LIMIT), name="proj_b")(x, wq, wm, wk, wv)


def _bias_kernel(tbl_ref, bb_ref, bc_ref, far_ref):
    h = pl.program_id(0)

    def bias_of(dist):
        n = jnp.maximum(dist, 0)
        max_exact = REL_BUCKETS // 2
        nf = jnp.maximum(n, 1).astype(F32)
        large = max_exact + (jnp.log(nf / max_exact) / math.log(REL_MAX_DIST / max_exact)
                             * (REL_BUCKETS - max_exact)).astype(jnp.int32)
        large = jnp.minimum(large, REL_BUCKETS - 1)
        bucket = jnp.where(n < max_exact, n, large)
        out = jnp.zeros(dist.shape, F32)
        for kk in range(REL_BUCKETS):
            out = jnp.where(bucket == kk, tbl_ref[h * REL_BUCKETS + kk], out)
        return out * LOG2E

    key = lax.broadcasted_iota(jnp.int32, (TQ, TQ), 0)
    qry = lax.broadcasted_iota(jnp.int32, (TQ, TQ), 1)
    d0 = qry - key
    bb_ref[0] = jnp.where(d0 >= 0, bias_of(d0), NEG_INF)
    bb_ref[1] = bias_of(d0 + TQ)
    far = bias_of(d0 + 2 * TQ)
    bb_ref[2] = far
    bb_ref[3] = jnp.where(d0 + 2 * TQ < WINDOW, far, NEG_INF)
    far_ref[...] = far[0:1, :]

    nc = lax.broadcasted_iota(jnp.int32, (LANES, TQ), 0)
    qc = lax.broadcasted_iota(jnp.int32, (LANES, TQ), 1)
    n_cmp = (2048 - CMP_LEN) // CMP_STRIDE + 1
    for i in range(bc_ref.shape[0]):
        dc = i * TQ + qc - (nc * CMP_STRIDE + CMP_LEN - 1)
        bc_ref[i] = jnp.where((dc >= 0) & (nc < n_cmp), bias_of(dc), NEG_INF)


def _bias_tables(rel_bias, n_qt):
    tbl = rel_bias.T.reshape(-1)
    return pl.pallas_call(
        _bias_kernel, grid=(N_HEADS,),
        in_specs=[pl.BlockSpec(memory_space=pltpu.SMEM)],
        out_specs=[pl.BlockSpec((None, 4, TQ, TQ), lambda h: (h, 0, 0, 0)),
                   pl.BlockSpec((n_qt, None, LANES, TQ), lambda h: (0, h, 0, 0)),
                   pl.BlockSpec((None, 1, TQ), lambda h: (h, 0, 0))],
        out_shape=[jax.ShapeDtypeStruct((N_HEADS, 4, TQ, TQ), F32),
                   jax.ShapeDtypeStruct((n_qt, N_HEADS, LANES, TQ), F32),
                   jax.ShapeDtypeStruct((N_HEADS, 1, TQ), F32)],
        compiler_params=_params(("arbitrary",)), name="bias_tables")(tbl)


def _cmp_kernel(kv_ref, pe_ref, w1_ref, w2_ref, o_ref):
    x = kv_ref[...].astype(F32)
    lo = _dot((x + pe_ref[0:1, :]).astype(BF16), w1_ref[0])
    hi = _dot((x + pe_ref[1:2, :]).astype(BF16), w1_ref[1])
    nrow = x.shape[0]
    hid = lo + pltpu.roll(hi, nrow - 1, 0)
    o_ref[...] = _dot(jax.nn.gelu(hid).astype(BF16), w2_ref[...])


def _compress(kvr, pe, w1, w2):
    b, nrow, width = kvr.shape
    return pl.pallas_call(
        _cmp_kernel, grid=(b,),
        in_specs=[pl.BlockSpec((None, nrow, width), lambda i: (i, 0, 0)),
                  pl.BlockSpec(pe.shape, lambda i: (0, 0)),
                  pl.BlockSpec(w1.shape, lambda i: (0, 0, 0)),
                  pl.BlockSpec(w2.shape, lambda i: (0, 0))],
        out_specs=pl.BlockSpec((None, nrow, w2.shape[1]), lambda i: (i, 0, 0)),
        out_shape=jax.ShapeDtypeStruct((b, nrow, w2.shape[1]), F32),
        compiler_params=_params(("arbitrary",), VMEM_LIMIT), name="nsa_compress")(kvr, pe, w1, w2)


def _softmax_tile(state, c, s_t, v_t, vis=None, cbias=None):
    m_ref, acc_ref = state
    m = m_ref[c]
    n_blk = 1 if vis is None else vis.shape[0]
    rows = s_t.shape[0] // n_blk
    parts = [s_t[b * rows:(b + 1) * rows] for b in range(n_blk)]
    tile_max = None
    for b in range(n_blk):
        mb = jnp.max(parts[b], axis=0, keepdims=True)
        if vis is not None:
            mb = jnp.where(vis[b:b + 1] > 0.5, mb, NEG_INF)
        tile_max = mb if tile_max is None else jnp.maximum(tile_max, mb)
    if cbias is not None:
        tile_max = tile_max + cbias
    m_new = jnp.maximum(m, tile_max)
    shift = m_new if cbias is None else m_new - cbias
    p_parts = []
    for b in range(n_blk):
        off = shift if vis is None else jnp.where(vis[b:b + 1] > 0.5, shift, -NEG_INF)
        p_parts.append(jnp.exp2(parts[b] - off))
    p_t = p_parts[0] if n_blk == 1 else jnp.concatenate(p_parts, axis=0)
    alpha = jnp.exp2(m - m_new)
    acc_ref[c] = alpha * acc_ref[c] + _dot(v_t, p_t.astype(BF16))
    m_ref[c] = m_new


def _run_chains(chains, logits_fn, update_fn):
    pending = {}
    for idx, c in enumerate(chains[:QK_LOOKAHEAD]):
        pending[c] = logits_fn(c, idx)
    for idx, c in enumerate(chains):
        update_fn(c, pending.pop(c), idx % QK_LOOKAHEAD)
        if idx + QK_LOOKAHEAD < len(chains):
            nxt = chains[idx + QK_LOOKAHEAD]
            pending[nxt] = logits_fn(nxt, idx % QK_LOOKAHEAD)


def _pair_loop(lo, hi, tiles_fn):
    odd = (hi - lo) % 2

    def single(_, carry):
        tiles_fn([lo])
        return carry

    def double(r, carry):
        jj = lo + odd + 2 * r
        tiles_fn([jj, jj + 1])
        return carry

    lax.fori_loop(0, odd, single, 0)
    lax.fori_loop(0, (hi - lo) // 2, double, 0)


def _stage_logits(s_ref, slot, s_t):
    s_ref[slot] = s_t


def _init_state(state):
    m_ref, acc_ref = state
    m_ref[...] = jnp.full(m_ref.shape, NEG_INF, F32)
    acc_ref[...] = jnp.zeros(acc_ref.shape, F32)


def _chain_out(state, c):
    _, acc_ref = state
    return acc_ref[c, 0:HEAD_DIM, :] / jnp.maximum(acc_ref[c, HEAD_DIM:HEAD_DIM + 1, :], TINY)


def _values_with_ones(tile, row0):
    v_t = tile.astype(F32).T[row0:row0 + HEAD_DIM]
    pad = lax.broadcasted_iota(jnp.int32, (V_ROWS - HEAD_DIM, tile.shape[0]), 0)
    return jnp.concatenate([v_t, jnp.where(pad == 0, 1.0, 0.0)], axis=0).astype(BF16)


def _rows(ref, kt, cols=slice(None)):
    start = kt * TQ if isinstance(kt, int) else pl.multiple_of(kt * TQ, TQ)
    return ref[pl.ds(start, TQ), cols]


def _store_head_queries(q_ref, qh_ref, n_pairs):
    row = lax.broadcasted_iota(jnp.int32, (LANES, TQ), 0)
    halves = (row < HEAD_DIM, row >= HEAD_DIM)
    for pp in range(n_pairs):
        q2t = q_ref[:, pp * LANES:(pp + 1) * LANES].astype(F32).T
        for j in range(2):
            qh_ref[2 * pp + j] = jnp.where(halves[j], q2t, 0.0).astype(BF16)


def _per_query_tile(step_fn, n_tiles):
    def kernel_fn(*refs):
        for i in range(n_tiles):
            pl.when(pl.program_id(1) == i)(functools.partial(step_fn, *refs, i=i))
    return kernel_fn


def _nsa_kernel(q_ref, g_ref, kvc_ref, ks_ref, vs_ref, kw_ref, vw_ref, bb_ref, far_ref, bc_ref, ovl_ref,
                o_ref, vst_ref, vwt_ref, qh_ref, gs_ref, oc_ref, sel_ref, m_ref, acc_ref, s_ref):
    i = pl.program_id(1)
    n_kt = vst_ref.shape[0]
    state = (m_ref, acc_ref)

    @pl.when(i == 0)
    def _values():
        for kt in range(n_kt):
            rows = slice(kt * TQ, (kt + 1) * TQ)
            vst_ref[kt] = _values_with_ones(vs_ref[rows, :], 0)
            vwt_ref[kt] = _values_with_ones(vw_ref[rows, :], 0)

    per_tile = TQ // SLC_BLOCK
    win_tiles = WINDOW // TQ

    def select_blocks():
        _store_head_queries(q_ref, qh_ref, N_PAIRS)
        gs_ref[...] = jax.nn.sigmoid(g_ref[...]).T
        kc2 = kvc_ref[:, 0:LANES].astype(BF16)
        vc_t = kvc_ref[:, LANES:2 * LANES].T[0:HEAD_DIM].astype(BF16)
        psums = []

        def cmp_logits(h, slot):
            s_ref[slot, 0:LANES, :] = _dot(kc2, qh_ref[h])

        def cmp_update(h, _s, slot):
            bias = bc_ref[h]
            s = s_ref[slot, 0:LANES, :] + bias
            m = jnp.maximum(jnp.max(s, axis=0, keepdims=True), 0.5 * NEG_INF)
            e = jnp.exp2(s - m)
            pc = e / jnp.maximum(jnp.sum(e, axis=0, keepdims=True), TINY)
            psums[:] = [pc if not psums else psums[0] + pc]
            oc_ref[h] = gs_ref[3 * h:3 * h + 1, :] * _dot(vc_t, pc.astype(BF16))

        _run_chains(list(range(N_HEADS)), cmp_logits, cmp_update)
        psum = psums[0]
        p_hi = psum.astype(BF16)
        p_lo = (psum - p_hi.astype(F32)).astype(BF16)
        imp = _dot(ovl_ref[...], p_hi) + _dot(ovl_ref[...], p_lo)
        n_slc = n_kt * per_tile
        imp = imp[0:n_slc, :]
        sidx = lax.broadcasted_iota(jnp.int32, (n_slc, TQ), 0)
        t = i * TQ + lax.broadcasted_iota(jnp.int32, (n_slc, TQ), 1)
        cur = lax.shift_right_logical(t, 6)
        eligible = sidx <= cur
        forced = (sidx == 0) | (sidx == cur) | (sidx == cur - 1)
        score = jnp.where(eligible, jnp.where(forced, FORCE_SCORE, 0.0), NEG_INF) + imp
        cnt = jnp.zeros((n_slc, TQ), jnp.int32)
        for sp in range(n_slc):
            row = score[sp:sp + 1, :]
            better = (row > score) | ((row == score) & (sp < sidx))
            cnt = cnt + better.astype(jnp.int32)
        chosen = jnp.where(cnt < min(SLC_TOPK, n_slc), 1.0, 0.0)
        for kt in range(n_kt):
            sel_ref[kt, 0:per_tile, :] = chosen[kt * per_tile:(kt + 1) * per_tile, :]
        _init_state(state)

    def near_tiles(n_tiles):
        def logits(c, slot):
            kind, jj, h = c
            _stage_logits(s_ref, slot, _dot(_rows(ks_ref if kind == "s" else kw_ref, i - jj), qh_ref[h]))

        def update(c, _s, slot):
            kind, jj, h = c
            kt = i - jj
            if kind == "s":
                _softmax_tile(state, h, s_ref[slot] + bb_ref[h, min(jj, 2)], vst_ref[kt],
                              vis=sel_ref[kt, 0:per_tile, :])
            else:
                _softmax_tile(state, N_HEADS + h, s_ref[slot] + bb_ref[h, 3 if jj == win_tiles else jj], vwt_ref[kt])

        _run_chains([(kind, jj, h) for jj in range(n_tiles) for h in range(N_HEADS) for kind in ("s", "w")],
                    logits, update)

    def far_tiles(jjs):
        kts = [i - jj for jj in jjs]
        _run_chains([(n, h) for n in range(len(jjs)) for h in range(N_HEADS)],
                    lambda c, slot: _dot(_rows(ks_ref, kts[c[0]]), qh_ref[c[1]]),
                    lambda c, s, slot: _softmax_tile(state, c[1], s, vst_ref[kts[c[0]]],
                                                     vis=sel_ref[kts[c[0]], 0:per_tile, :], cbias=far_ref[c[1]]))

    def front(n_tiles):
        select_blocks()
        near_tiles(min(n_tiles, win_tiles + 1))
        if n_tiles > win_tiles + 1:
            far_tiles(list(range(win_tiles + 1, n_tiles)))

    max_front = win_tiles + 1 + NSA_FRONT_FAR
    for n_tiles in range(1, max_front + 1):
        cond = (i >= n_tiles - 1) if n_tiles == max_front else (i == n_tiles - 1)
        pl.when(cond)(functools.partial(front, n_tiles))
    _pair_loop(jnp.minimum(i + 1, max_front), i + 1, far_tiles)

    for pp in range(N_PAIRS):
        hs = (2 * pp, 2 * pp + 1)
        outs = []
        for h in hs:
            g1 = gs_ref[3 * h + 1:3 * h + 2, :]
            g2 = gs_ref[3 * h + 2:3 * h + 3, :]
            outs.append(oc_ref[h] + g1 * _chain_out(state, h) + g2 * _chain_out(state, N_HEADS + h))
        o_ref[:, pp * LANES:(pp + 1) * LANES] = jnp.concatenate(outs, axis=0).T.astype(o_ref.dtype)


def _nsa_attention(q, g, kvc, ks2, vs2, kw2, vw2, bb, far, bc, ovl_t):
    b, s, _ = q.shape
    n_qt = s // TQ
    per_b = lambda bi, i: (bi, 0, 0)
    return pl.pallas_call(
        _nsa_kernel, grid=(b, n_qt),
        in_specs=[pl.BlockSpec((None, TQ, MAIN_W), lambda bi, i: (bi, i, 0)),
                  pl.BlockSpec((None, TQ, LANES), lambda bi, i: (bi, i, 0)),
                  pl.BlockSpec((None,) + kvc.shape[1:], per_b),
                  pl.BlockSpec((None, s, LANES), per_b), pl.BlockSpec((None, s, LANES), per_b),
                  pl.BlockSpec((None, s, LANES), per_b), pl.BlockSpec((None, s, LANES), per_b),
                  pl.BlockSpec(bb.shape, lambda bi, i: (0, 0, 0, 0)),
                  pl.BlockSpec(far.shape, lambda bi, i: (0, 0, 0)),
                  pl.BlockSpec((None,) + bc.shape[1:], lambda bi, i: (i, 0, 0, 0)),
                  pl.BlockSpec(ovl_t.shape, lambda bi, i: (0, 0))],
        out_specs=pl.BlockSpec((None, TQ, MAIN_W), lambda bi, i: (bi, i, 0)),
        out_shape=jax.ShapeDtypeStruct((b, s, MAIN_W), BF16),
        scratch_shapes=[pltpu.VMEM((n_qt, V_ROWS, TQ), BF16), pltpu.VMEM((n_qt, V_ROWS, TQ), BF16),
                        pltpu.VMEM((N_HEADS, LANES, TQ), BF16), pltpu.VMEM((LANES, TQ), F32),
                        pltpu.VMEM((N_HEADS, HEAD_DIM, TQ), F32), pltpu.VMEM((n_qt, 8, TQ), F32),
                        pltpu.VMEM((2 * N_HEADS, 1, TQ), F32),
                        pltpu.VMEM((2 * N_HEADS, V_ROWS, TQ), F32), pltpu.VMEM((QK_LOOKAHEAD, TQ, TQ), F32)],
        compiler_params=_params(("arbitrary", "arbitrary"), VMEM_LIMIT),
        name="nsa_attention")(q, g, kvc, ks2, vs2, kw2, vw2, bb, far, bc, ovl_t)


def _moba_step(q_ref, k_ref, v_ref, km_ref, bb_ref, far_ref, o_ref, vt_ref, qh_ref, sel_ref,
               m_ref, acc_ref, s_ref, *, i):
    n_blk = vt_ref.shape[1]
    state = (m_ref, acc_ref)
    heads = list(range(N_HEADS))

    if i == 0:
        for pp in range(N_PAIRS):
            for kt in range(n_blk):
                tile = v_ref[kt * TQ:(kt + 1) * TQ, pp * LANES:(pp + 1) * LANES]
                for j in range(2):
                    vt_ref[2 * pp + j, kt] = _values_with_ones(tile, j * HEAD_DIM)

    def gate_blocks():
        _store_head_queries(q_ref, qh_ref, N_PAIRS)
        blk = lax.broadcasted_iota(jnp.int32, (n_blk, TQ), 0)
        eligible = blk < i
        for pp in range(N_PAIRS):
            km = km_ref[:, pp * LANES:(pp + 1) * LANES].astype(BF16)
            for j in range(2):
                gate = _dot(km, qh_ref[2 * pp + j])[0:n_blk, :]
                gate = jnp.where(eligible, gate, NEG_INF)
                cnt = jnp.zeros((n_blk, TQ), jnp.int32)
                for n in range(n_blk):
                    row = gate[n:n + 1, :]
                    better = (row > gate) | ((row == gate) & (n < blk))
                    cnt = cnt + better.astype(jnp.int32)
                chosen = (cnt < min(MOBA_TOPK, n_blk - 1)) & eligible
                vis = jnp.where(chosen | (blk == i), 1.0, 0.0)
                for n in range(n_blk):
                    sel_ref[2 * pp + j, n] = vis[n:n + 1, :]
        _init_state(state)

    def tile_inputs(kt, h):
        pp = h // 2
        k = _rows(k_ref, kt, slice(pp * LANES, (pp + 1) * LANES))
        return k, vt_ref[h, kt], sel_ref[h, kt]

    def front(n_tiles):
        gate_blocks()

        def update(c, _s, slot):
            jj, h = c
            _, v_t, vis = tile_inputs(i - jj, h)
            _softmax_tile(state, h, s_ref[slot] + bb_ref[h, jj], v_t, vis=vis)

        def logits(c, slot):
            jj, h = c
            _stage_logits(s_ref, slot, _dot(tile_inputs(i - jj, h)[0], qh_ref[h]))

        _run_chains([(jj, h) for jj in range(n_tiles) for h in heads], logits, update)

    n_near = min(i, 1) + 1

    def far_tiles(jjs):
        kts = [i - jj for jj in jjs]

        def update(c, s, slot):
            _, v_t, vis = tile_inputs(kts[c[0]], c[1])
            _softmax_tile(state, c[1], s, v_t, vis=vis, cbias=far_ref[c[1]])

        _run_chains([(n, h) for n in range(len(jjs)) for h in heads],
                    lambda c, slot: _dot(tile_inputs(kts[c[0]], c[1])[0], qh_ref[c[1]]), update)

    front(n_near)
    if i + 1 > n_near:
        far_tiles(list(range(n_near, i + 1)))

    for pp in range(N_PAIRS):
        out_t = jnp.concatenate([_chain_out(state, 2 * pp), _chain_out(state, 2 * pp + 1)], axis=0)
        o_ref[:, pp * LANES:(pp + 1) * LANES] = out_t.T.astype(o_ref.dtype)


def _moba_attention(q, k, v, kmean, bb, far):
    b, s, _ = q.shape
    n_qt = s // TQ
    per_b = lambda bi, i: (bi, 0, 0)
    return pl.pallas_call(
        _per_query_tile(_moba_step, n_qt), grid=(b, n_qt),
        in_specs=[pl.BlockSpec((None, TQ, MAIN_W), lambda bi, i: (bi, i, 0)),
                  pl.BlockSpec((None, s, MAIN_W), per_b), pl.BlockSpec((None, s, MAIN_W), per_b),
                  pl.BlockSpec((None, LANES, MAIN_W), per_b),
                  pl.BlockSpec((N_HEADS, 2, TQ, TQ), lambda bi, i: (0, 0, 0, 0)),
                  pl.BlockSpec(far.shape, lambda bi, i: (0, 0, 0))],
        out_specs=pl.BlockSpec((None, TQ, MAIN_W), lambda bi, i: (bi, i, 0)),
        out_shape=jax.ShapeDtypeStruct((b, s, MAIN_W), BF16),
        scratch_shapes=[pltpu.VMEM((N_HEADS, n_qt, V_ROWS, TQ), BF16), pltpu.VMEM((N_HEADS, LANES, TQ), BF16),
                        pltpu.VMEM((N_HEADS, n_qt, 1, TQ), F32),
                        pltpu.VMEM((N_HEADS, 1, TQ), F32),
                        pltpu.VMEM((N_HEADS, V_ROWS, TQ), F32), pltpu.VMEM((QK_LOOKAHEAD, TQ, TQ), F32)],
        compiler_params=_params(("arbitrary", "arbitrary"), VMEM_LIMIT),
        name="moba_attention")(q, k, v, kmean, bb, far)


def _mem_kernel(q_ref, kv_ref, o_ref, qh_ref):
    n_sub = q_ref.shape[0] // TQ
    for sub in range(n_sub):
        _store_head_queries(q_ref.at[sub * TQ:(sub + 1) * TQ, :], qh_ref.at[sub], MEM_HEADS // 2)
    outs = {}

    def logits(c, slot):
        sub, h = c
        return _dot(kv_ref[:, (h // 2) * LANES:(h // 2 + 1) * LANES], qh_ref[sub, h])

    def update(c, s, slot):
        h = c[1]
        lanes = slice(MEM_W + (h // 2) * LANES, MEM_W + (h // 2 + 1) * LANES)
        v_t = kv_ref[:, lanes].astype(F32).T[(h % 2) * HEAD_DIM:(h % 2 + 1) * HEAD_DIM].astype(BF16)
        e = jnp.exp2(s - jnp.max(s, axis=0, keepdims=True))
        pr = e / jnp.sum(e, axis=0, keepdims=True)
        outs[c] = _dot(v_t, pr.astype(BF16))

    _run_chains([(sub, h) for sub in range(n_sub) for h in range(MEM_HEADS)], logits, update)
    for sub in range(n_sub):
        for pp in range(MEM_HEADS // 2):
            pair = jnp.concatenate([outs[(sub, 2 * pp)], outs[(sub, 2 * pp + 1)]], axis=0)
            o_ref[sub * TQ:(sub + 1) * TQ, pp * LANES:(pp + 1) * LANES] = pair.T.astype(o_ref.dtype)


MEM_TILES = 4


def _mem_attention(qm, mem_kv):
    b, s, _ = qm.shape
    n_mem = mem_kv.shape[1]
    rows = MEM_TILES * TQ
    return pl.pallas_call(
        _mem_kernel, grid=(b, s // rows),
        in_specs=[pl.BlockSpec((None, rows, MEM_W), lambda bi, i: (bi, i, 0)),
                  pl.BlockSpec((None, n_mem, 2 * MEM_W), lambda bi, i: (bi, 0, 0))],
        out_specs=pl.BlockSpec((None, rows, MEM_W), lambda bi, i: (bi, i, 0)),
        out_shape=jax.ShapeDtypeStruct((b, s, MEM_W), BF16),
        scratch_shapes=[pltpu.VMEM((MEM_TILES, MEM_HEADS, LANES, TQ), BF16)],
        compiler_params=_params(("arbitrary", "arbitrary")), name="mem_attention")(qm, mem_kv)


def _layer_norm(y, g, b):
    mu = jnp.mean(y, axis=-1, keepdims=True)
    var = jnp.mean(jnp.square(y - mu), axis=-1, keepdims=True)
    return (y - mu) * lax.rsqrt(var + LN_EPS) * g + b


def _mm_ln_kernel(*refs, n_in):
    a_refs, w_refs = refs[:n_in], refs[n_in:2 * n_in]
    x_ref, g_ref, b_ref, o_ref, y_ref = refs[2 * n_in:]

    @pl.when(pl.program_id(0) == 0)
    def _():
        y_ref[...] = jnp.zeros(y_ref.shape, F32)

    o_ref[...] = _layer_norm(ALPHA * x_ref[...] + y_ref[...], g_ref[...], b_ref[...])
    acc = _dot(a_refs[0][...], w_refs[0][...])
    for a_ref, w_ref in zip(a_refs[1:], w_refs[1:]):
        acc = acc + _dot(a_ref[...], w_ref[...])
    y_ref[...] = acc


def _mm_ln(acts, weights, x, g, b, tm, name):
    m, d = x.shape
    n_t = m // tm
    cur = lambda t: (jnp.minimum(t, n_t - 1), 0)
    prev = lambda t: (jnp.maximum(t - 1, 0), 0)
    full = lambda t: (0, 0)
    return pl.pallas_call(
        functools.partial(_mm_ln_kernel, n_in=len(acts)), grid=(n_t + 1,),
        in_specs=[pl.BlockSpec((tm, a.shape[1]), cur) for a in acts]
                 + [pl.BlockSpec(w.shape, full) for w in weights]
                 + [pl.BlockSpec((tm, d), prev), pl.BlockSpec((1, d), full), pl.BlockSpec((1, d), full)],
        out_specs=pl.BlockSpec((tm, d), prev),
        out_shape=jax.ShapeDtypeStruct((m, d), F32),
        scratch_shapes=[pltpu.VMEM((tm, d), F32)],
        compiler_params=_params(("arbitrary",), VMEM_LIMIT), name=name)(*acts, *weights, x, g, b)


FF_CHUNK = 256
FF_ROWS = 512
SUBLANES = 8


def _ffn_kernel(x_ref, wa_ref, wb_ref, cw_ref, cb_ref, wo_ref, wol_ref, g_ref, b_ref, o_ref, xb_ref, hid_ref, *,
                n_c):
    c = pl.program_id(1)

    @pl.when(c == 0)
    def _():
        xb_ref[...] = x_ref[...].astype(BF16)
        hid_ref[1] = jnp.zeros(hid_ref.shape[1:], BF16)
        o_ref[...] = jnp.zeros(o_ref.shape, F32)

    xb = xb_ref[...]
    a = _dot(xb, wa_ref[...])
    o_ref[...] += _dot(hid_ref[(c + 1) % 2], wo_ref[...])
    gate = _dot(xb, wb_ref[...])
    ext = jnp.concatenate([jnp.zeros((SUBLANES, a.shape[1]), F32), a], axis=0)
    a1 = pltpu.roll(ext, 1, 0)[SUBLANES:]
    a2 = pltpu.roll(ext, 2, 0)[SUBLANES:]
    conv = cw_ref[0:1, :] * a2 + cw_ref[1:2, :] * a1 + cw_ref[2:3, :] * a + cb_ref[...]
    hid_ref[c % 2] = (jax.nn.gelu(conv) * gate).astype(BF16)

    @pl.when(c == n_c - 1)
    def _():
        for r0 in range(0, o_ref.shape[0], FF_ROWS):
            rows = slice(r0, r0 + FF_ROWS)
            y = o_ref[rows, :] + _dot(hid_ref[(n_c - 1) % 2, rows, :], wol_ref[...])
            o_ref[rows, :] = _layer_norm(ALPHA * x_ref[rows, :] + y, g_ref[...], b_ref[...])


def _ffn_ln(x, w_in, conv_w, conv_b, w_out, g, b):
    bsz, s, d = x.shape
    n_c = D_FF // FF_CHUNK
    prev = lambda c: jnp.maximum(c - 1, 0)
    return pl.pallas_call(
        functools.partial(_ffn_kernel, n_c=n_c), grid=(bsz, n_c),
        in_specs=[pl.BlockSpec((None, s, d), lambda bi, c: (bi, 0, 0)),
                  pl.BlockSpec((d, FF_CHUNK), lambda bi, c: (0, c)),
                  pl.BlockSpec((d, FF_CHUNK), lambda bi, c: (0, n_c + c)),
                  pl.BlockSpec((CONV_W, FF_CHUNK), lambda bi, c: (0, c)),
                  pl.BlockSpec((1, FF_CHUNK), lambda bi, c: (0, c)),
                  pl.BlockSpec((FF_CHUNK, d), lambda bi, c: (prev(c), 0)),
                  pl.BlockSpec((FF_CHUNK, d), lambda bi, c: (n_c - 1, 0)),
                  pl.BlockSpec((1, d), lambda bi, c: (0, 0)), pl.BlockSpec((1, d), lambda bi, c: (0, 0))],
        out_specs=pl.BlockSpec((None, s, d), lambda bi, c: (bi, 0, 0)),
        out_shape=jax.ShapeDtypeStruct((bsz, s, d), F32),
        scratch_shapes=[pltpu.VMEM((s, d), BF16), pltpu.VMEM((2, s, FF_CHUNK), BF16)],
        compiler_params=_params(("arbitrary", "arbitrary"), VMEM_LIMIT),
        name="ffn_ln")(x, w_in, w_in, conv_w, conv_b, w_out, w_out, g, b)


def _dup(w):
    return jnp.concatenate([w, w], axis=1)


def _cmp_weights(pe_k, w1_k, w2_k, pe_v, w1_v, w2_v):
    half = CMP_LEN // 2
    zk = jnp.zeros((half, HEAD_DIM, CMP_HIDDEN), F32)

    def w1_half(sl):
        wk = w1_k.reshape(CMP_LEN, HEAD_DIM, CMP_HIDDEN)[sl]
        wv = w1_v.reshape(CMP_LEN, HEAD_DIM, CMP_HIDDEN)[sl]
        top = jnp.concatenate([wk, zk], axis=2)
        bot = jnp.concatenate([zk, wv], axis=2)
        return jnp.concatenate([top, bot], axis=1).reshape(half * 2 * HEAD_DIM, 2 * CMP_HIDDEN)

    w1 = jnp.stack([w1_half(slice(0, half)), w1_half(slice(half, CMP_LEN))]).astype(BF16)
    pe = jnp.concatenate([pe_k, pe_v], axis=1)
    pe = jnp.stack([pe[:half].reshape(-1), pe[half:].reshape(-1)])
    zo = jnp.zeros((CMP_HIDDEN, 2 * HEAD_DIM), F32)
    w2 = jnp.concatenate([jnp.concatenate([_dup(w2_k), zo], axis=1),
                          jnp.concatenate([zo, _dup(w2_v)], axis=1)], axis=0).astype(BF16)
    return pe, w1, w2


def _overlap_matrix(seq):
    n_cmp = (seq - CMP_LEN) // CMP_STRIDE + 1
    n_slc = seq // SLC_BLOCK
    start = np.arange(LANES) * CMP_STRIDE
    bs = np.arange(LANES) * SLC_BLOCK
    ovl_t = ((start[None, :] < bs[:, None] + SLC_BLOCK) & (start[None, :] + CMP_LEN > bs[:, None])
             & (np.arange(LANES)[None, :] < n_cmp) & (np.arange(LANES)[:, None] < n_slc))
    return jnp.asarray(ovl_t, BF16)


def kernel(x, mem, rel_bias, a_w_in, a_cmp_pe_k, a_cmp_w1_k, a_cmp_w2_k, a_cmp_pe_v, a_cmp_w1_v, a_cmp_w2_v,
           a_w_mem_kv, a_w_out, shared_w_kv, b_w_in, b_w_mem_kv, b_w_out, ln1_g, ln1_b, ln2_g, ln2_b,
           ffn_w_in, ffn_conv_w, ffn_conv_b, ffn_w_out):
    bsz, seq, d = x.shape
    n_mem = mem.shape[1]
    m = bsz * seq
    assert (seq, d) == (2048, D_MODEL) and seq % TQ == 0
    n_qt = seq // TQ

    bb, bc, far = _bias_tables(rel_bias, n_qt)
    ovl_t = _overlap_matrix(seq)
    memf = mem.reshape(bsz * n_mem, d)
    xf = x.reshape(m, d)

    def ffn(xcur, layer):
        return _ffn_ln(xcur.reshape(bsz, seq, d), ffn_w_in[layer].astype(BF16), ffn_conv_w[layer],
                       ffn_conv_b[layer][None, :], ffn_w_out[layer].astype(BF16),
                       ln2_g[layer][None, :], ln2_b[layer][None, :]).reshape(m, d)

    def out_ln(o_main, o_mem, w_out, xcur, layer):
        w = w_out.astype(BF16)
        return _mm_ln([o_main.reshape(m, MAIN_W), o_mem.reshape(m, MEM_W)], [w[:MAIN_W], w[MAIN_W:]], xcur,
                      ln1_g[layer][None, :], ln1_b[layer][None, :], 512, "outproj_ln")

    w = a_w_in[0]
    c0 = MAIN_W
    cols = [w[:, c0 + kk * HEAD_DIM:c0 + (kk + 1) * HEAD_DIM] for kk in range(6)]
    c_g = c0 + 6 * HEAD_DIM
    w_g = jnp.pad(w[:, c_g:c_g + 3 * N_HEADS], ((0, 0), (0, LANES - 3 * N_HEADS)))
    w_qm = w[:, c_g + 3 * N_HEADS:]
    weights = [w[:, :MAIN_W], jnp.concatenate(cols[0:2], axis=1), _dup(cols[2]), _dup(cols[3]),
               _dup(cols[4]), _dup(cols[5]), w_g, w_qm]
    weights = [wi.astype(BF16) for wi in weights]
    q, kvc_tok, ks2, vs2, kw2, vw2, g, qm = _proj(
        xf, weights, [BF16, BF16, BF16, BF16, BF16, BF16, F32, BF16], 512, "proj_a",
        scales=[Q_SCALE, None, None, None, None, None, None, Q_SCALE])

    pe, w1, w2 = _cmp_weights(a_cmp_pe_k[0], a_cmp_w1_k[0], a_cmp_w2_k[0],
                              a_cmp_pe_v[0], a_cmp_w1_v[0], a_cmp_w2_v[0])
    kvc = _compress(kvc_tok.reshape(bsz, seq // (CMP_LEN // 2), (CMP_LEN // 2) * LANES), pe, w1, w2)

    r3 = lambda t: t.reshape(bsz, seq, t.shape[-1])
    o_main = _nsa_attention(r3(q), r3(g), kvc, r3(ks2), r3(vs2), r3(kw2), r3(vw2), bb, far, bc, ovl_t)
    (mkv,) = _proj(memf, [a_w_mem_kv[0].astype(BF16)], [BF16], 512, "proj_mem_a")
    o_mem = _mem_attention(r3(qm), mkv.reshape(bsz, n_mem, 2 * MEM_W))
    x1 = out_ln(o_main, o_mem, a_w_out[0], xf, 0)
    x1 = ffn(x1, 0)

    wb = b_w_in[0].astype(BF16)
    wkv = shared_w_kv.astype(BF16)
    q, qm, k, v, kmean = _proj_b(x1, wb[:, :MAIN_W], wb[:, MAIN_W:], wkv[:, :MAIN_W], wkv[:, MAIN_W:])
    n_blk = seq // MOBA_BLOCK
    kmean = jnp.pad(kmean.reshape(bsz, n_blk, MAIN_W), ((0, 0), (0, LANES - n_blk), (0, 0)))
    o_main = _moba_attention(r3(q), r3(k), r3(v), kmean, bb, far)
    (mkv,) = _proj(memf, [b_w_mem_kv[0].astype(BF16)], [BF16], 512, "proj_mem_b")
    o_mem = _mem_attention(r3(qm), mkv.reshape(bsz, n_mem, 2 * MEM_W))
    x2 = out_ln(o_main, o_mem, b_w_out[0], x1, 1)
    x2 = ffn(x2, 1)
    return x2.reshape(bsz, seq, d)
```

```python
import functools
import math

import numpy as np
import jax
import jax.numpy as jnp
from jax import lax
from jax.experimental import pallas as pl
from jax.experimental.pallas import tpu as pltpu

F32 = jnp.float32
BF16 = jnp.bfloat16

D_MODEL = 1024
HEAD_DIM = 64
N_HEADS = 12
N_PAIRS = N_HEADS // 2
MAIN_W = N_HEADS * HEAD_DIM
MEM_HEADS = 4
MEM_W = MEM_HEADS * HEAD_DIM
CMP_LEN = 32
CMP_STRIDE = 16
CMP_HIDDEN = 256
SLC_BLOCK = 64
SLC_TOPK = 16
WINDOW = 512
FORCE_SCORE = 1.0e4
MOBA_BLOCK = 256
MOBA_TOPK = 3
REL_BUCKETS = 32
REL_MAX_DIST = 128
D_FF = 2816
CONV_W = 3
DEPTH = 2
ALPHA = (2.0 * DEPTH) ** 0.25
LN_EPS = 1e-5
NEG_INF = -1e30
TINY = 1e-30
LOG2E = math.log2(math.e)
Q_SCALE = HEAD_DIM ** -0.5 * LOG2E

LANES = 128
TQ = 256
VMEM_LIMIT = 60 * 1024 * 1024
V_ROWS = HEAD_DIM + 8
ROW_TILE = 1024
NSA_FRONT_FAR = 2
QK_LOOKAHEAD = 6


def _dot(a, b):
    return jnp.dot(a, b, preferred_element_type=F32)


def _dot_nt(a, b):
    return lax.dot_general(a, b, (((1,), (1,)), ((), ())), preferred_element_type=F32)


def _params(sem, vmem=None):
    return pltpu.CompilerParams(dimension_semantics=sem, vmem_limit_bytes=vmem)


def _proj_kernel(x_ref, w_ref, *o_refs, scales):
    y = _dot(x_ref[...].astype(BF16), w_ref[...])
    col = 0
    for o_ref, scale in zip(o_refs, scales):
        part = y[:, col:col + o_ref.shape[1]]
        o_ref[...] = (part if scale is None else part * scale).astype(o_ref.dtype)
        col += o_ref.shape[1]


def _proj(x, weights, dtypes, tm, name, scales=None):
    m, k = x.shape
    scales = tuple(scales) if scales is not None else (None,) * len(weights)
    assert all(w.shape[1] % LANES == 0 for w in weights)
    w_all = jnp.concatenate(weights, axis=1) if len(weights) > 1 else weights[0]
    out_specs = [pl.BlockSpec((tm, w.shape[1]), lambda i: (i, 0)) for w in weights]
    out_shape = [jax.ShapeDtypeStruct((m, w.shape[1]), dt) for w, dt in zip(weights, dtypes)]
    return pl.pallas_call(
        functools.partial(_proj_kernel, scales=scales), grid=(m // tm,),
        in_specs=[pl.BlockSpec((tm, k), lambda i: (i, 0)), pl.BlockSpec(w_all.shape, lambda i: (0, 0))],
        out_specs=out_specs, out_shape=out_shape,
        compiler_params=_params(("arbitrary",), VMEM_LIMIT), name=name)(x, w_all)


def _projb_kernel(x_ref, wq_ref, wm_ref, wk_ref, wv_ref, q_ref, qm_ref, k_ref, v_ref, km_ref):
    xb = x_ref[...].astype(BF16)
    q_ref[...] = (_dot(xb, wq_ref[...]) * Q_SCALE).astype(BF16)
    qm_ref[...] = (_dot(xb, wm_ref[...]) * Q_SCALE).astype(BF16)
    k = _dot(xb, wk_ref[...])
    k_ref[...] = k.astype(BF16)
    for blk in range(km_ref.shape[0]):
        km_ref[blk] = jnp.mean(k[blk * MOBA_BLOCK:(blk + 1) * MOBA_BLOCK], axis=0, keepdims=True)
    v_ref[...] = _dot(xb, wv_ref[...]).astype(BF16)


PROJ_B_BLOCKS = 2


def _proj_b(x, wq, wm, wk, wv):
    m, k = x.shape
    nblk = m // MOBA_BLOCK
    tm = PROJ_B_BLOCKS * MOBA_BLOCK
    row = lambda i: (i, 0)
    full = lambda i: (0, 0)
    return pl.pallas_call(
        _projb_kernel, grid=(m // tm,),
        in_specs=[pl.BlockSpec((tm, k), row)] + [pl.BlockSpec(w.shape, full) for w in (wq, wm, wk, wv)],
        out_specs=[pl.BlockSpec((tm, MAIN_W), row), pl.BlockSpec((tm, MEM_W), row),
                   pl.BlockSpec((tm, MAIN_W), row), pl.BlockSpec((tm, MAIN_W), row),
                   pl.BlockSpec((PROJ_B_BLOCKS, 1, MAIN_W), lambda i: (i, 0, 0))],
        out_shape=[jax.ShapeDtypeStruct((m, MAIN_W), BF16), jax.ShapeDtypeStruct((m, MEM_W), BF16),
                   jax.ShapeDtypeStruct((m, MAIN_W), BF16), jax.ShapeDtypeStruct((m, MAIN_W), BF16),
                   jax.ShapeDtypeStruct((nblk, 1, MAIN_W), F32)],
        compiler_params=_params(("arbitrary",), VMEM_LIMIT), name="proj_b")(x, wq, wm, wk, wv)


def _bias_kernel(tbl_ref, bb_ref, bc_ref, far_ref):
    h = pl.program_id(0)

    def bias_of(dist):
        n = jnp.maximum(dist, 0)
        max_exact = REL_BUCKETS // 2
        nf = jnp.maximum(n, 1).astype(F32)
        large = max_exact + (jnp.log(nf / max_exact) / math.log(REL_MAX_DIST / max_exact)
                             * (REL_BUCKETS - max_exact)).astype(jnp.int32)
        large = jnp.minimum(large, REL_BUCKETS - 1)
        bucket = jnp.where(n < max_exact, n, large)
        out = jnp.zeros(dist.shape, F32)
        for kk in range(REL_BUCKETS):
            out = jnp.where(bucket == kk, tbl_ref[h * REL_BUCKETS + kk], out)
        return out * LOG2E

    key = lax.broadcasted_iota(jnp.int32, (TQ, TQ), 0)
    qry = lax.broadcasted_iota(jnp.int32, (TQ, TQ), 1)
    d0 = qry - key
    bb_ref[0] = jnp.where(d0 >= 0, bias_of(d0), NEG_INF)
    bb_ref[1] = bias_of(d0 + TQ)
    far = bias_of(d0 + 2 * TQ)
    bb_ref[2] = far
    bb_ref[3] = jnp.where(d0 + 2 * TQ < WINDOW, far, NEG_INF)
    far_ref[...] = far[0:1, :]

    nc = lax.broadcasted_iota(jnp.int32, (LANES, TQ), 0)
    qc = lax.broadcasted_iota(jnp.int32, (LANES, TQ), 1)
    n_cmp = (2048 - CMP_LEN) // CMP_STRIDE + 1
    for i in range(bc_ref.shape[0]):
        dc = i * TQ + qc - (nc * CMP_STRIDE + CMP_LEN - 1)
        bc_ref[i] = jnp.where((dc >= 0) & (nc < n_cmp), bias_of(dc), NEG_INF)


def _bias_tables(rel_bias, n_qt):
    tbl = rel_bias.T.reshape(-1)
    return pl.pallas_call(
        _bias_kernel, grid=(N_HEADS,),
        in_specs=[pl.BlockSpec(memory_space=pltpu.SMEM)],
        out_specs=[pl.BlockSpec((None, 4, TQ, TQ), lambda h: (h, 0, 0, 0)),
                   pl.BlockSpec((n_qt, None, LANES, TQ), lambda h: (0, h, 0, 0)),
                   pl.BlockSpec((None, 1, TQ), lambda h: (h, 0, 0))],
        out_shape=[jax.ShapeDtypeStruct((N_HEADS, 4, TQ, TQ), F32),
                   jax.ShapeDtypeStruct((n_qt, N_HEADS, LANES, TQ), F32),
                   jax.ShapeDtypeStruct((N_HEADS, 1, TQ), F32)],
        compiler_params=_params(("arbitrary",)), name="bias_tables")(tbl)


def _cmp_kernel(kv_ref, pe_ref, w1_ref, w2_ref, o_ref):
    x = kv_ref[...].astype(F32)
    lo = _dot((x + pe_ref[0:1, :]).astype(BF16), w1_ref[0])
    hi = _dot((x + pe_ref[1:2, :]).astype(BF16), w1_ref[1])
    nrow = x.shape[0]
    hid = lo + pltpu.roll(hi, nrow - 1, 0)
    o_ref[...] = _dot(jax.nn.gelu(hid).astype(BF16), w2_ref[...])


def _compress(kvr, pe, w1, w2):
    b, nrow, width = kvr.shape
    return pl.pallas_call(
        _cmp_kernel, grid=(b,),
        in_specs=[pl.BlockSpec((None, nrow, width), lambda i: (i, 0, 0)),
                  pl.BlockSpec(pe.shape, lambda i: (0, 0)),
                  pl.BlockSpec(w1.shape, lambda i: (0, 0, 0)),
                  pl.BlockSpec(w2.shape, lambda i: (0, 0))],
        out_specs=pl.BlockSpec((None, nrow, w2.shape[1]), lambda i: (i, 0, 0)),
        out_shape=jax.ShapeDtypeStruct((b, nrow, w2.shape[1]), F32),
        compiler_params=_params(("arbitrary",), VMEM_LIMIT), name="nsa_compress")(kvr, pe, w1, w2)


def _softmax_tile(state, c, s_t, v_t, vis=None, cbias=None):
    m_ref, acc_ref = state
    m = m_ref[c]
    n_blk = 1 if vis is None else vis.shape[0]
    rows = s_t.shape[0] // n_blk
    parts = [s_t[b * rows:(b + 1) * rows] for b in range(n_blk)]
    tile_max = None
    for b in range(n_blk):
        mb = jnp.max(parts[b], axis=0, keepdims=True)
        if vis is not None:
            mb = jnp.where(vis[b:b + 1] > 0.5, mb, NEG_INF)
        tile_max = mb if tile_max is None else jnp.maximum(tile_max, mb)
    if cbias is not None:
        tile_max = tile_max + cbias
    m_new = jnp.maximum(m, tile_max)
    shift = m_new if cbias is None else m_new - cbias
    p_parts = []
    for b in range(n_blk):
        off = shift if vis is None else jnp.where(vis[b:b + 1] > 0.5, shift, -NEG_INF)
        p_parts.append(jnp.exp2(parts[b] - off))
    p_t = p_parts[0] if n_blk == 1 else jnp.concatenate(p_parts, axis=0)
    alpha = jnp.exp2(m - m_new)
    acc_ref[c] = alpha * acc_ref[c] + _dot(v_t, p_t.astype(BF16))
    m_ref[c] = m_new


def _run_chains(chains, logits_fn, update_fn):
    pending = {}
    for idx, c in enumerate(chains[:QK_LOOKAHEAD]):
        pending[c] = logits_fn(c, idx)
    for idx, c in enumerate(chains):
        update_fn(c, pending.pop(c), idx % QK_LOOKAHEAD)
        if idx + QK_LOOKAHEAD < len(chains):
            nxt = chains[idx + QK_LOOKAHEAD]
            pending[nxt] = logits_fn(nxt, idx % QK_LOOKAHEAD)


def _pair_loop(lo, hi, tiles_fn):
    odd = (hi - lo) % 2

    def single(_, carry):
        tiles_fn([lo])
        return carry

    def double(r, carry):
        jj = lo + odd + 2 * r
        tiles_fn([jj, jj + 1])
        return carry

    lax.fori_loop(0, odd, single, 0)
    lax.fori_loop(0, (hi - lo) // 2, double, 0)


def _stage_logits(s_ref, slot, s_t):
    s_ref[slot] = s_t


def _init_state(state):
    m_ref, acc_ref = state
    m_ref[...] = jnp.full(m_ref.shape, NEG_INF, F32)
    acc_ref[...] = jnp.zeros(acc_ref.shape, F32)


def _chain_out(state, c):
    _, acc_ref = state
    return acc_ref[c, 0:HEAD_DIM, :] / jnp.maximum(acc_ref[c, HEAD_DIM:HEAD_DIM + 1, :], TINY)


def _values_with_ones(tile, row0):
    v_t = tile.astype(F32).T[row0:row0 + HEAD_DIM]
    pad = lax.broadcasted_iota(jnp.int32, (V_ROWS - HEAD_DIM, tile.shape[0]), 0)
    return jnp.concatenate([v_t, jnp.where(pad == 0, 1.0, 0.0)], axis=0).astype(BF16)


def _rows(ref, kt, cols=slice(None)):
    start = kt * TQ if isinstance(kt, int) else pl.multiple_of(kt * TQ, TQ)
    return ref[pl.ds(start, TQ), cols]


def _store_head_queries(q_ref, qh_ref, n_pairs):
    row = lax.broadcasted_iota(jnp.int32, (LANES, TQ), 0)
    halves = (row < HEAD_DIM, row >= HEAD_DIM)
    for pp in range(n_pairs):
        q2t = q_ref[:, pp * LANES:(pp + 1) * LANES].astype(F32).T
        for j in range(2):
            qh_ref[2 * pp + j] = jnp.where(halves[j], q2t, 0.0).astype(BF16)


def _per_query_tile(step_fn, n_tiles):
    def kernel_fn(*refs):
        for i in range(n_tiles):
            pl.when(pl.program_id(1) == i)(functools.partial(step_fn, *refs, i=i))
    return kernel_fn


def _nsa_kernel(q_ref, g_ref, kvc_ref, ks_ref, vs_ref, kw_ref, vw_ref, bb_ref, far_ref, bc_ref, ovl_ref,
                o_ref, vst_ref, vwt_ref, qh_ref, gs_ref, oc_ref, sel_ref, m_ref, acc_ref, s_ref):
    i = pl.program_id(1)
    n_kt = vst_ref.shape[0]
    state = (m_ref, acc_ref)

    @pl.when(i == 0)
    def _values():
        for kt in range(n_kt):
            rows = slice(kt * TQ, (kt + 1) * TQ)
            vst_ref[kt] = _values_with_ones(vs_ref[rows, :], 0)
            vwt_ref[kt] = _values_with_ones(vw_ref[rows, :], 0)

    per_tile = TQ // SLC_BLOCK
    win_tiles = WINDOW // TQ

    def select_blocks():
        _store_head_queries(q_ref, qh_ref, N_PAIRS)
        gs_ref[...] = jax.nn.sigmoid(g_ref[...]).T
        kc2 = kvc_ref[:, 0:LANES].astype(BF16)
        vc_t = kvc_ref[:, LANES:2 * LANES].T[0:HEAD_DIM].astype(BF16)
        psums = []

        def cmp_logits(h, slot):
            s_ref[slot, 0:LANES, :] = _dot(kc2, qh_ref[h])

        def cmp_update(h, _s, slot):
            bias = bc_ref[h]
            s = s_ref[slot, 0:LANES, :] + bias
            m = jnp.maximum(jnp.max(s, axis=0, keepdims=True), 0.5 * NEG_INF)
            e = jnp.exp2(s - m)
            pc = e / jnp.maximum(jnp.sum(e, axis=0, keepdims=True), TINY)
            psums[:] = [pc if not psums else psums[0] + pc]
            oc_ref[h] = gs_ref[3 * h:3 * h + 1, :] * _dot(vc_t, pc.astype(BF16))

        _run_chains(list(range(N_HEADS)), cmp_logits, cmp_update)
        psum = psums[0]
        p_hi = psum.astype(BF16)
        p_lo = (psum - p_hi.astype(F32)).astype(BF16)
        imp = _dot(ovl_ref[...], p_hi) + _dot(ovl_ref[...], p_lo)
        n_slc = n_kt * per_tile
        imp = imp[0:n_slc, :]
        sidx = lax.broadcasted_iota(jnp.int32, (n_slc, TQ), 0)
        t = i * TQ + lax.broadcasted_iota(jnp.int32, (n_slc, TQ), 1)
        cur = lax.shift_right_logical(t, 6)
        eligible = sidx <= cur
        forced = (sidx == 0) | (sidx == cur) | (sidx == cur - 1)
        score = jnp.where(eligible, jnp.where(forced, FORCE_SCORE, 0.0), NEG_INF) + imp
        cnt = jnp.zeros((n_slc, TQ), jnp.int32)
        for sp in range(n_slc):
            row = score[sp:sp + 1, :]
            better = (row > score) | ((row == score) & (sp < sidx))
            cnt = cnt + better.astype(jnp.int32)
        chosen = jnp.where(cnt < min(SLC_TOPK, n_slc), 1.0, 0.0)
        for kt in range(n_kt):
            sel_ref[kt, 0:per_tile, :] = chosen[kt * per_tile:(kt + 1) * per_tile, :]
        _init_state(state)

    def near_tiles(n_tiles):
        def logits(c, slot):
            kind, jj, h = c
            _stage_logits(s_ref, slot, _dot(_rows(ks_ref if kind == "s" else kw_ref, i - jj), qh_ref[h]))

        def update(c, _s, slot):
            kind, jj, h = c
            kt = i - jj
            if kind == "s":
                _softmax_tile(state, h, s_ref[slot] + bb_ref[h, min(jj, 2)], vst_ref[kt],
                              vis=sel_ref[kt, 0:per_tile, :])
            else:
                _softmax_tile(state, N_HEADS + h, s_ref[slot] + bb_ref[h, 3 if jj == win_tiles else jj], vwt_ref[kt])

        _run_chains([(kind, jj, h) for jj in range(n_tiles) for h in range(N_HEADS) for kind in ("s", "w")],
                    logits, update)

    def far_tiles(jjs):
        kts = [i - jj for jj in jjs]
        _run_chains([(n, h) for n in range(len(jjs)) for h in range(N_HEADS)],
                    lambda c, slot: _dot(_rows(ks_ref, kts[c[0]]), qh_ref[c[1]]),
                    lambda c, s, slot: _softmax_tile(state, c[1], s, vst_ref[kts[c[0]]],
                                                     vis=sel_ref[kts[c[0]], 0:per_tile, :], cbias=far_ref[c[1]]))

    def front(n_tiles):
        select_blocks()
        near_tiles(min(n_tiles, win_tiles + 1))
        if n_tiles > win_tiles + 1:
            far_tiles(list(range(win_tiles + 1, n_tiles)))

    max_front = win_tiles + 1 + NSA_FRONT_FAR
    for n_tiles in range(1, max_front + 1):
        cond = (i >= n_tiles - 1) if n_tiles == max_front else (i == n_tiles - 1)
        pl.when(cond)(functools.partial(front, n_tiles))
    _pair_loop(jnp.minimum(i + 1, max_front), i + 1, far_tiles)

    for pp in range(N_PAIRS):
        hs = (2 * pp, 2 * pp + 1)
        outs = []
        for h in hs:
            g1 = gs_ref[3 * h + 1:3 * h + 2, :]
            g2 = gs_ref[3 * h + 2:3 * h + 3, :]
            outs.append(oc_ref[h] + g1 * _chain_out(state, h) + g2 * _chain_out(state, N_HEADS + h))
        o_ref[:, pp * LANES:(pp + 1) * LANES] = jnp.concatenate(outs, axis=0).T.astype(o_ref.dtype)


def _nsa_attention(q, g, kvc, ks2, vs2, kw2, vw2, bb, far, bc, ovl_t):
    b, s, _ = q.shape
    n_qt = s // TQ
    per_b = lambda bi, i: (bi, 0, 0)
    return pl.pallas_call(
        _nsa_kernel, grid=(b, n_qt),
        in_specs=[pl.BlockSpec((None, TQ, MAIN_W), lambda bi, i: (bi, i, 0)),
                  pl.BlockSpec((None, TQ, LANES), lambda bi, i: (bi, i, 0)),
                  pl.BlockSpec((None,) + kvc.shape[1:], per_b),
                  pl.BlockSpec((None, s, LANES), per_b), pl.BlockSpec((None, s, LANES), per_b),
                  pl.BlockSpec((None, s, LANES), per_b), pl.BlockSpec((None, s, LANES), per_b),
                  pl.BlockSpec(bb.shape, lambda bi, i: (0, 0, 0, 0)),
                  pl.BlockSpec(far.shape, lambda bi, i: (0, 0, 0)),
                  pl.BlockSpec((None,) + bc.shape[1:], lambda bi, i: (i, 0, 0, 0)),
                  pl.BlockSpec(ovl_t.shape, lambda bi, i: (0, 0))],
        out_specs=pl.BlockSpec((None, TQ, MAIN_W), lambda bi, i: (bi, i, 0)),
        out_shape=jax.ShapeDtypeStruct((b, s, MAIN_W), BF16),
        scratch_shapes=[pltpu.VMEM((n_qt, V_ROWS, TQ), BF16), pltpu.VMEM((n_qt, V_ROWS, TQ), BF16),
                        pltpu.VMEM((N_HEADS, LANES, TQ), BF16), pltpu.VMEM((LANES, TQ), F32),
                        pltpu.VMEM((N_HEADS, HEAD_DIM, TQ), F32), pltpu.VMEM((n_qt, 8, TQ), F32),
                        pltpu.VMEM((2 * N_HEADS, 1, TQ), F32),
                        pltpu.VMEM((2 * N_HEADS, V_ROWS, TQ), F32), pltpu.VMEM((QK_LOOKAHEAD, TQ, TQ), F32)],
        compiler_params=_params(("arbitrary", "arbitrary"), VMEM_LIMIT),
        name="nsa_attention")(q, g, kvc, ks2, vs2, kw2, vw2, bb, far, bc, ovl_t)


def _moba_step(q_ref, k_ref, v_ref, km_ref, bb_ref, far_ref, o_ref, vt_ref, qh_ref, sel_ref,
               m_ref, acc_ref, s_ref, *, i):
    n_blk = vt_ref.shape[1]
    state = (m_ref, acc_ref)
    heads = list(range(N_HEADS))

    if i == 0:
        for pp in range(N_PAIRS):
            for kt in range(n_blk):
                tile = v_ref[kt * TQ:(kt + 1) * TQ, pp * LANES:(pp + 1) * LANES]
                for j in range(2):
                    vt_ref[2 * pp + j, kt] = _values_with_ones(tile, j * HEAD_DIM)

    def gate_blocks():
        _store_head_queries(q_ref, qh_ref, N_PAIRS)
        blk = lax.broadcasted_iota(jnp.int32, (n_blk, TQ), 0)
        eligible = blk < i
        for pp in range(N_PAIRS):
            km = km_ref[:, pp * LANES:(pp + 1) * LANES].astype(BF16)
            for j in range(2):
                gate = _dot(km, qh_ref[2 * pp + j])[0:n_blk, :]
                gate = jnp.where(eligible, gate, NEG_INF)
                cnt = jnp.zeros((n_blk, TQ), jnp.int32)
                for n in range(n_blk):
                    row = gate[n:n + 1, :]
                    better = (row > gate) | ((row == gate) & (n < blk))
                    cnt = cnt + better.astype(jnp.int32)
                chosen = (cnt < min(MOBA_TOPK, n_blk - 1)) & eligible
                vis = jnp.where(chosen | (blk == i), 1.0, 0.0)
                for n in range(n_blk):
                    sel_ref[2 * pp + j, n] = vis[n:n + 1, :]
        _init_state(state)

    def tile_inputs(kt, h):
        pp = h // 2
        k = _rows(k_ref, kt, slice(pp * LANES, (pp + 1) * LANES))
        return k, vt_ref[h, kt], sel_ref[h, kt]

    def front(n_tiles):
        gate_blocks()

        def update(c, _s, slot):
            jj, h = c
            _, v_t, vis = tile_inputs(i - jj, h)
            _softmax_tile(state, h, s_ref[slot] + bb_ref[h, jj], v_t, vis=vis)

        def logits(c, slot):
            jj, h = c
            _stage_logits(s_ref, slot, _dot(tile_inputs(i - jj, h)[0], qh_ref[h]))

        _run_chains([(jj, h) for jj in range(n_tiles) for h in heads], logits, update)

    n_near = min(i, 1) + 1

    def far_tiles(jjs):
        kts = [i - jj for jj in jjs]

        def update(c, s, slot):
            _, v_t, vis = tile_inputs(kts[c[0]], c[1])
            _softmax_tile(state, c[1], s, v_t, vis=vis, cbias=far_ref[c[1]])

        _run_chains([(n, h) for n in range(len(jjs)) for h in heads],
                    lambda c, slot: _dot(tile_inputs(kts[c[0]], c[1])[0], qh_ref[c[1]]), update)

    front(n_near)
    if i + 1 > n_near:
        far_tiles(list(range(n_near, i + 1)))

    for pp in range(N_PAIRS):
        out_t = jnp.concatenate([_chain_out(state, 2 * pp), _chain_out(state, 2 * pp + 1)], axis=0)
        o_ref[:, pp * LANES:(pp + 1) * LANES] = out_t.T.astype(o_ref.dtype)


def _moba_attention(q, k, v, kmean, bb, far):
    b, s, _ = q.shape
    n_qt = s // TQ
    per_b = lambda bi, i: (bi, 0, 0)
    return pl.pallas_call(
        _per_query_tile(_moba_step, n_qt), grid=(b, n_qt),
        in_specs=[pl.BlockSpec((None, TQ, MAIN_W), lambda bi, i: (bi, i, 0)),
                  pl.BlockSpec((None, s, MAIN_W), per_b), pl.BlockSpec((None, s, MAIN_W), per_b),
                  pl.BlockSpec((None, LANES, MAIN_W), per_b),
                  pl.BlockSpec((N_HEADS, 2, TQ, TQ), lambda bi, i: (0, 0, 0, 0)),
                  pl.BlockSpec(far.shape, lambda bi, i: (0, 0, 0))],
        out_specs=pl.BlockSpec((None, TQ, MAIN_W), lambda bi, i: (bi, i, 0)),
        out_shape=jax.ShapeDtypeStruct((b, s, MAIN_W), BF16),
        scratch_shapes=[pltpu.VMEM((N_HEADS, n_qt, V_ROWS, TQ), BF16), pltpu.VMEM((N_HEADS, LANES, TQ), BF16),
                        pltpu.VMEM((N_HEADS, n_qt, 1, TQ), F32),
                        pltpu.VMEM((N_HEADS, 1, TQ), F32),
                        pltpu.VMEM((N_HEADS, V_ROWS, TQ), F32), pltpu.VMEM((QK_LOOKAHEAD, TQ, TQ), F32)],
        compiler_params=_params(("arbitrary", "arbitrary"), VMEM_LIMIT),
        name="moba_attention")(q, k, v, kmean, bb, far)


def _mem_kernel(q_ref, kv_ref, o_ref, qh_ref):
    n_sub = q_ref.shape[0] // TQ
    for sub in range(n_sub):
        _store_head_queries(q_ref.at[sub * TQ:(sub + 1) * TQ, :], qh_ref.at[sub], MEM_HEADS // 2)
    outs = {}

    def logits(c, slot):
        sub, h = c
        return _dot(kv_ref[:, (h // 2) * LANES:(h // 2 + 1) * LANES], qh_ref[sub, h])

    def update(c, s, slot):
        h = c[1]
        lanes = slice(MEM_W + (h // 2) * LANES, MEM_W + (h // 2 + 1) * LANES)
        v_t = kv_ref[:, lanes].astype(F32).T[(h % 2) * HEAD_DIM:(h % 2 + 1) * HEAD_DIM].astype(BF16)
        e = jnp.exp2(s - jnp.max(s, axis=0, keepdims=True))
        pr = e / jnp.sum(e, axis=0, keepdims=True)
        outs[c] = _dot(v_t, pr.astype(BF16))

    _run_chains([(sub, h) for sub in range(n_sub) for h in range(MEM_HEADS)], logits, update)
    for sub in range(n_sub):
        for pp in range(MEM_HEADS // 2):
            pair = jnp.concatenate([outs[(sub, 2 * pp)], outs[(sub, 2 * pp + 1)]], axis=0)
            o_ref[sub * TQ:(sub + 1) * TQ, pp * LANES:(pp + 1) * LANES] = pair.T.astype(o_ref.dtype)


MEM_TILES = 8


def _mem_attention(qm, mem_kv):
    b, s, _ = qm.shape
    n_mem = mem_kv.shape[1]
    rows = MEM_TILES * TQ
    return pl.pallas_call(
        _mem_kernel, grid=(b, s // rows),
        in_specs=[pl.BlockSpec((None, rows, MEM_W), lambda bi, i: (bi, i, 0)),
                  pl.BlockSpec((None, n_mem, 2 * MEM_W), lambda bi, i: (bi, 0, 0))],
        out_specs=pl.BlockSpec((None, rows, MEM_W), lambda bi, i: (bi, i, 0)),
        out_shape=jax.ShapeDtypeStruct((b, s, MEM_W), BF16),
        scratch_shapes=[pltpu.VMEM((MEM_TILES, MEM_HEADS, LANES, TQ), BF16)],
        compiler_params=_params(("arbitrary", "arbitrary")), name="mem_attention")(qm, mem_kv)


def _layer_norm(y, g, b):
    mu = jnp.mean(y, axis=-1, keepdims=True)
    var = jnp.mean(jnp.square(y - mu), axis=-1, keepdims=True)
    return (y - mu) * lax.rsqrt(var + LN_EPS) * g + b


def _mm_ln_kernel(*refs, n_in):
    a_refs, w_refs = refs[:n_in], refs[n_in:2 * n_in]
    x_ref, g_ref, b_ref, o_ref, y_ref = refs[2 * n_in:]

    @pl.when(pl.program_id(0) == 0)
    def _():
        y_ref[...] = jnp.zeros(y_ref.shape, F32)

    o_ref[...] = _layer_norm(ALPHA * x_ref[...] + y_ref[...], g_ref[...], b_ref[...])
    acc = _dot(a_refs[0][...], w_refs[0][...])
    for a_ref, w_ref in zip(a_refs[1:], w_refs[1:]):
        acc = acc + _dot(a_ref[...], w_ref[...])
    y_ref[...] = acc


def _mm_ln(acts, weights, x, g, b, tm, name):
    m, d = x.shape
    n_t = m // tm
    cur = lambda t: (jnp.minimum(t, n_t - 1), 0)
    prev = lambda t: (jnp.maximum(t - 1, 0), 0)
    full = lambda t: (0, 0)
    return pl.pallas_call(
        functools.partial(_mm_ln_kernel, n_in=len(acts)), grid=(n_t + 1,),
        in_specs=[pl.BlockSpec((tm, a.shape[1]), cur) for a in acts]
                 + [pl.BlockSpec(w.shape, full) for w in weights]
                 + [pl.BlockSpec((tm, d), prev), pl.BlockSpec((1, d), full), pl.BlockSpec((1, d), full)],
        out_specs=pl.BlockSpec((tm, d), prev),
        out_shape=jax.ShapeDtypeStruct((m, d), F32),
        scratch_shapes=[pltpu.VMEM((tm, d), F32)],
        compiler_params=_params(("arbitrary",), VMEM_LIMIT), name=name)(*acts, *weights, x, g, b)


FF_CHUNK = 256
FF_ROWS = 512
SUBLANES = 8


def _ffn_kernel(x_ref, wa_ref, wb_ref, cw_ref, cb_ref, wo_ref, wol_ref, g_ref, b_ref, o_ref, xb_ref, hid_ref, *,
                n_c):
    c = pl.program_id(1)

    @pl.when(c == 0)
    def _():
        xb_ref[...] = x_ref[...].astype(BF16)
        hid_ref[1] = jnp.zeros(hid_ref.shape[1:], BF16)
        o_ref[...] = jnp.zeros(o_ref.shape, F32)

    xb = xb_ref[...]
    a = _dot(xb, wa_ref[...])
    o_ref[...] += _dot(hid_ref[(c + 1) % 2], wo_ref[...])
    gate = _dot(xb, wb_ref[...])
    ext = jnp.concatenate([jnp.zeros((SUBLANES, a.shape[1]), F32), a], axis=0)
    a1 = pltpu.roll(ext, 1, 0)[SUBLANES:]
    a2 = pltpu.roll(ext, 2, 0)[SUBLANES:]
    conv = cw_ref[0:1, :] * a2 + cw_ref[1:2, :] * a1 + cw_ref[2:3, :] * a + cb_ref[...]
    hid_ref[c % 2] = (jax.nn.gelu(conv) * gate).astype(BF16)

    @pl.when(c == n_c - 1)
    def _():
        for r0 in range(0, o_ref.shape[0], FF_ROWS):
            rows = slice(r0, r0 + FF_ROWS)
            y = o_ref[rows, :] + _dot(hid_ref[(n_c - 1) % 2, rows, :], wol_ref[...])
            o_ref[rows, :] = _layer_norm(ALPHA * x_ref[rows, :] + y, g_ref[...], b_ref[...])


def _ffn_ln(x, w_in, conv_w, conv_b, w_out, g, b):
    bsz, s, d = x.shape
    n_c = D_FF // FF_CHUNK
    prev = lambda c: jnp.maximum(c - 1, 0)
    return pl.pallas_call(
        functools.partial(_ffn_kernel, n_c=n_c), grid=(bsz, n_c),
        in_specs=[pl.BlockSpec((None, s, d), lambda bi, c: (bi, 0, 0)),
                  pl.BlockSpec((d, FF_CHUNK), lambda bi, c: (0, c)),
                  pl.BlockSpec((d, FF_CHUNK), lambda bi, c: (0, n_c + c)),
                  pl.BlockSpec((CONV_W, FF_CHUNK), lambda bi, c: (0, c)),
                  pl.BlockSpec((1, FF_CHUNK), lambda bi, c: (0, c)),
                  pl.BlockSpec((FF_CHUNK, d), lambda bi, c: (prev(c), 0)),
                  pl.BlockSpec((FF_CHUNK, d), lambda bi, c: (n_c - 1, 0)),
                  pl.BlockSpec((1, d), lambda bi, c: (0, 0)), pl.BlockSpec((1, d), lambda bi, c: (0, 0))],
        out_specs=pl.BlockSpec((None, s, d), lambda bi, c: (bi, 0, 0)),
        out_shape=jax.ShapeDtypeStruct((bsz, s, d), F32),
        scratch_shapes=[pltpu.VMEM((s, d), BF16), pltpu.VMEM((2, s, FF_CHUNK), BF16)],
        compiler_params=_params(("arbitrary", "arbitrary"), VMEM_LIMIT),
        name="ffn_ln")(x, w_in, w_in, conv_w, conv_b, w_out, w_out, g, b)


def _dup(w):
    return jnp.concatenate([w, w], axis=1)


def _cmp_weights(pe_k, w1_k, w2_k, pe_v, w1_v, w2_v):
    half = CMP_LEN // 2
    zk = jnp.zeros((half, HEAD_DIM, CMP_HIDDEN), F32)

    def w1_half(sl):
        wk = w1_k.reshape(CMP_LEN, HEAD_DIM, CMP_HIDDEN)[sl]
        wv = w1_v.reshape(CMP_LEN, HEAD_DIM, CMP_HIDDEN)[sl]
        top = jnp.concatenate([wk, zk], axis=2)
        bot = jnp.concatenate([zk, wv], axis=2)
        return jnp.concatenate([top, bot], axis=1).reshape(half * 2 * HEAD_DIM, 2 * CMP_HIDDEN)

    w1 = jnp.stack([w1_half(slice(0, half)), w1_half(slice(half, CMP_LEN))]).astype(BF16)
    pe = jnp.concatenate([pe_k, pe_v], axis=1)
    pe = jnp.stack([pe[:half].reshape(-1), pe[half:].reshape(-1)])
    zo = jnp.zeros((CMP_HIDDEN, 2 * HEAD_DIM), F32)
    w2 = jnp.concatenate([jnp.concatenate([_dup(w2_k), zo], axis=1),
                          jnp.concatenate([zo, _dup(w2_v)], axis=1)], axis=0).astype(BF16)
    return pe, w1, w2


def _overlap_matrix(seq):
    n_cmp = (seq - CMP_LEN) // CMP_STRIDE + 1
    n_slc = seq // SLC_BLOCK
    start = np.arange(LANES) * CMP_STRIDE
    bs = np.arange(LANES) * SLC_BLOCK
    ovl_t = ((start[None, :] < bs[:, None] + SLC_BLOCK) & (start[None, :] + CMP_LEN > bs[:, None])
             & (np.arange(LANES)[None, :] < n_cmp) & (np.arange(LANES)[:, None] < n_slc))
    return jnp.asarray(ovl_t, BF16)


def kernel(x, mem, rel_bias, a_w_in, a_cmp_pe_k, a_cmp_w1_k, a_cmp_w2_k, a_cmp_pe_v, a_cmp_w1_v, a_cmp_w2_v,
           a_w_mem_kv, a_w_out, shared_w_kv, b_w_in, b_w_mem_kv, b_w_out, ln1_g, ln1_b, ln2_g, ln2_b,
           ffn_w_in, ffn_conv_w, ffn_conv_b, ffn_w_out):
    bsz, seq, d = x.shape
    n_mem = mem.shape[1]
    m = bsz * seq
    assert (seq, d) == (2048, D_MODEL) and seq % TQ == 0
    n_qt = seq // TQ

    bb, bc, far = _bias_tables(rel_bias, n_qt)
    ovl_t = _overlap_matrix(seq)
    memf = mem.reshape(bsz * n_mem, d)
    xf = x.reshape(m, d)

    def ffn(xcur, layer):
        return _ffn_ln(xcur.reshape(bsz, seq, d), ffn_w_in[layer].astype(BF16), ffn_conv_w[layer],
                       ffn_conv_b[layer][None, :], ffn_w_out[layer].astype(BF16),
                       ln2_g[layer][None, :], ln2_b[layer][None, :]).reshape(m, d)

    def out_ln(o_main, o_mem, w_out, xcur, layer):
        w = w_out.astype(BF16)
        return _mm_ln([o_main.reshape(m, MAIN_W), o_mem.reshape(m, MEM_W)], [w[:MAIN_W], w[MAIN_W:]], xcur,
                      ln1_g[layer][None, :], ln1_b[layer][None, :], ROW_TILE, "outproj_ln")

    w = a_w_in[0]
    c0 = MAIN_W
    cols = [w[:, c0 + kk * HEAD_DIM:c0 + (kk + 1) * HEAD_DIM] for kk in range(6)]
    c_g = c0 + 6 * HEAD_DIM
    w_g = jnp.pad(w[:, c_g:c_g + 3 * N_HEADS], ((0, 0), (0, LANES - 3 * N_HEADS)))
    w_qm = w[:, c_g + 3 * N_HEADS:]
    weights = [w[:, :MAIN_W], jnp.concatenate(cols[0:2], axis=1), _dup(cols[2]), _dup(cols[3]),
               _dup(cols[4]), _dup(cols[5]), w_g, w_qm]
    weights = [wi.astype(BF16) for wi in weights]
    q, kvc_tok, ks2, vs2, kw2, vw2, g, qm = _proj(
        xf, weights, [BF16, BF16, BF16, BF16, BF16, BF16, F32, BF16], ROW_TILE, "proj_a",
        scales=[Q_SCALE, None, None, None, None, None, None, Q_SCALE])

    pe, w1, w2 = _cmp_weights(a_cmp_pe_k[0], a_cmp_w1_k[0], a_cmp_w2_k[0],
                              a_cmp_pe_v[0], a_cmp_w1_v[0], a_cmp_w2_v[0])
    kvc = _compress(kvc_tok.reshape(bsz, seq // (CMP_LEN // 2), (CMP_LEN // 2) * LANES), pe, w1, w2)

    r3 = lambda t: t.reshape(bsz, seq, t.shape[-1])
    o_main = _nsa_attention(r3(q), r3(g), kvc, r3(ks2), r3(vs2), r3(kw2), r3(vw2), bb, far, bc, ovl_t)
    (mkv,) = _proj(memf, [a_w_mem_kv[0].astype(BF16)], [BF16], 512, "proj_mem_a")
    o_mem = _mem_attention(r3(qm), mkv.reshape(bsz, n_mem, 2 * MEM_W))
    x1 = out_ln(o_main, o_mem, a_w_out[0], xf, 0)
    x1 = ffn(x1, 0)

    wb = b_w_in[0].astype(BF16)
    wkv = shared_w_kv.astype(BF16)
    q, qm, k, v, kmean = _proj_b(x1, wb[:, :MAIN_W], wb[:, MAIN_W:], wkv[:, :MAIN_W], wkv[:, MAIN_W:])
    n_blk = seq // MOBA_BLOCK
    kmean = jnp.pad(kmean.reshape(bsz, n_blk, MAIN_W), ((0, 0), (0, LANES - n_blk), (0, 0)))
    o_main = _moba_attention(r3(q), r3(k), r3(v), kmean, bb, far)
    (mkv,) = _proj(memf, [b_w_mem_kv[0].astype(BF16)], [BF16], 512, "proj_mem_b")
    o_mem = _mem_attention(r3(qm), mkv.reshape(bsz, n_mem, 2 * MEM_W))
    x2 = out_ln(o_main, o_mem, b_w_out[0], x1, 1)
    x2 = ffn(x2, 1)
    return x2.reshape(bsz, seq, d)
```

```python
import functools
import math

import numpy as np
import jax
import jax.numpy as jnp
from jax import lax
from jax.experimental import pallas as pl
from jax.experimental.pallas import tpu as pltpu

F32 = jnp.float32
BF16 = jnp.bfloat16

D_MODEL = 1024
HEAD_DIM = 64
N_HEADS = 12
N_PAIRS = N_HEADS // 2
MAIN_W = N_HEADS * HEAD_DIM
MEM_HEADS = 4
MEM_W = MEM_HEADS * HEAD_DIM
CMP_LEN = 32
CMP_STRIDE = 16
CMP_HIDDEN = 256
SLC_BLOCK = 64
SLC_TOPK = 16
WINDOW = 512
FORCE_SCORE = 1.0e4
MOBA_BLOCK = 256
MOBA_TOPK = 3
REL_BUCKETS = 32
REL_MAX_DIST = 128
D_FF = 2816
CONV_W = 3
DEPTH = 2
ALPHA = (2.0 * DEPTH) ** 0.25
LN_EPS = 1e-5
NEG_INF = -1e30
TINY = 1e-30
LOG2E = math.log2(math.e)
Q_SCALE = HEAD_DIM ** -0.5 * LOG2E

LANES = 128
TQ = 256
VMEM_LIMIT = 60 * 1024 * 1024
V_ROWS = HEAD_DIM + 8
ROW_TILE = 1024
NSA_FRONT_FAR = 2
QK_LOOKAHEAD = 6


def _dot(a, b):
    return jnp.dot(a, b, preferred_element_type=F32)


def _dot_nt(a, b):
    return lax.dot_general(a, b, (((1,), (1,)), ((), ())), preferred_element_type=F32)


def _params(sem, vmem=None):
    return pltpu.CompilerParams(dimension_semantics=sem, vmem_limit_bytes=vmem)


def _proj_kernel(x_ref, w_ref, *o_refs, scales):
    y = _dot(x_ref[...].astype(BF16), w_ref[...])
    col = 0
    for o_ref, scale in zip(o_refs, scales):
        part = y[:, col:col + o_ref.shape[1]]
        o_ref[...] = (part if scale is None else part * scale).astype(o_ref.dtype)
        col += o_ref.shape[1]


def _proj(x, weights, dtypes, tm, name, scales=None):
    m, k = x.shape
    scales = tuple(scales) if scales is not None else (None,) * len(weights)
    assert all(w.shape[1] % LANES == 0 for w in weights)
    w_all = jnp.concatenate(weights, axis=1) if len(weights) > 1 else weights[0]
    out_specs = [pl.BlockSpec((tm, w.shape[1]), lambda i: (i, 0)) for w in weights]
    out_shape = [jax.ShapeDtypeStruct((m, w.shape[1]), dt) for w, dt in zip(weights, dtypes)]
    return pl.pallas_call(
        functools.partial(_proj_kernel, scales=scales), grid=(m // tm,),
        in_specs=[pl.BlockSpec((tm, k), lambda i: (i, 0)), pl.BlockSpec(w_all.shape, lambda i: (0, 0))],
        out_specs=out_specs, out_shape=out_shape,
        compiler_params=_params(("arbitrary",), VMEM_LIMIT), name=name)(x, w_all)


def _projb_kernel(x_ref, wq_ref, wm_ref, wk_ref, wv_ref, q_ref, qm_ref, k_ref, v_ref, km_ref):
    xb = x_ref[...].astype(BF16)
    q_ref[...] = (_dot(xb, wq_ref[...]) * Q_SCALE).astype(BF16)
    qm_ref[...] = (_dot(xb, wm_ref[...]) * Q_SCALE).astype(BF16)
    k = _dot(xb, wk_ref[...])
    k_ref[...] = k.astype(BF16)
    for blk in range(km_ref.shape[0]):
        km_ref[blk] = jnp.mean(k[blk * MOBA_BLOCK:(blk + 1) * MOBA_BLOCK], axis=0, keepdims=True)
    v_ref[...] = _dot(xb, wv_ref[...]).astype(BF16)


PROJ_B_BLOCKS = 4


def _proj_b(x, wq, wm, wk, wv):
    m, k = x.shape
    nblk = m // MOBA_BLOCK
    tm = PROJ_B_BLOCKS * MOBA_BLOCK
    row = lambda i: (i, 0)
    full = lambda i: (0, 0)
    return pl.pallas_call(
        _projb_kernel, grid=(m // tm,),
        in_specs=[pl.BlockSpec((tm, k), row)] + [pl.BlockSpec(w.shape, full) for w in (wq, wm, wk, wv)],
        out_specs=[pl.BlockSpec((tm, MAIN_W), row), pl.BlockSpec((tm, MEM_W), row),
                   pl.BlockSpec((tm, MAIN_W), row), pl.BlockSpec((tm, MAIN_W), row),
                   pl.BlockSpec((PROJ_B_BLOCKS, 1, MAIN_W), lambda i: (i, 0, 0))],
        out_shape=[jax.ShapeDtypeStruct((m, MAIN_W), BF16), jax.ShapeDtypeStruct((m, MEM_W), BF16),
                   jax.ShapeDtypeStruct((m, MAIN_W), BF16), jax.ShapeDtypeStruct((m, MAIN_W), BF16),
                   jax.ShapeDtypeStruct((nblk, 1, MAIN_W), F32)],
        compiler_params=_params(("arbitrary",), VMEM_LIMIT), name="proj_b")(x, wq, wm, wk, wv)


def _bias_kernel(tbl_ref, bb_ref, bc_ref, far_ref):
    h = pl.program_id(0)

    def bias_of(dist):
        n = jnp.maximum(dist, 0)
        max_exact = REL_BUCKETS // 2
        nf = jnp.maximum(n, 1).astype(F32)
        large = max_exact + (jnp.log(nf / max_exact) / math.log(REL_MAX_DIST / max_exact)
                             * (REL_BUCKETS - max_exact)).astype(jnp.int32)
        large = jnp.minimum(large, REL_BUCKETS - 1)
        bucket = jnp.where(n < max_exact, n, large)
        out = jnp.zeros(dist.shape, F32)
        for kk in range(REL_BUCKETS):
            out = jnp.where(bucket == kk, tbl_ref[h * REL_BUCKETS + kk], out)
        return out * LOG2E

    key = lax.broadcasted_iota(jnp.int32, (TQ, TQ), 0)
    qry = lax.broadcasted_iota(jnp.int32, (TQ, TQ), 1)
    d0 = qry - key
    bb_ref[0] = jnp.where(d0 >= 0, bias_of(d0), NEG_INF)
    bb_ref[1] = bias_of(d0 + TQ)
    far = bias_of(d0 + 2 * TQ)
    bb_ref[2] = far
    bb_ref[3] = jnp.where(d0 + 2 * TQ < WINDOW, far, NEG_INF)
    far_ref[...] = far[0:1, :]

    nc = lax.broadcasted_iota(jnp.int32, (LANES, TQ), 0)
    qc = lax.broadcasted_iota(jnp.int32, (LANES, TQ), 1)
    n_cmp = (2048 - CMP_LEN) // CMP_STRIDE + 1
    for i in range(bc_ref.shape[0]):
        dc = i * TQ + qc - (nc * CMP_STRIDE + CMP_LEN - 1)
        bc_ref[i] = jnp.where((dc >= 0) & (nc < n_cmp), bias_of(dc), NEG_INF)


def _bias_tables(rel_bias, n_qt):
    tbl = rel_bias.T.reshape(-1)
    return pl.pallas_call(
        _bias_kernel, grid=(N_HEADS,),
        in_specs=[pl.BlockSpec(memory_space=pltpu.SMEM)],
        out_specs=[pl.BlockSpec((None, 4, TQ, TQ), lambda h: (h, 0, 0, 0)),
                   pl.BlockSpec((n_qt, None, LANES, TQ), lambda h: (0, h, 0, 0)),
                   pl.BlockSpec((None, 1, TQ), lambda h: (h, 0, 0))],
        out_shape=[jax.ShapeDtypeStruct((N_HEADS, 4, TQ, TQ), F32),
                   jax.ShapeDtypeStruct((n_qt, N_HEADS, LANES, TQ), F32),
                   jax.ShapeDtypeStruct((N_HEADS, 1, TQ), F32)],
        compiler_params=_params(("arbitrary",)), name="bias_tables")(tbl)


def _cmp_kernel(kv_ref, pe_ref, w1_ref, w2_ref, o_ref):
    x = kv_ref[...].astype(F32)
    lo = _dot((x + pe_ref[0:1, :]).astype(BF16), w1_ref[0])
    hi = _dot((x + pe_ref[1:2, :]).astype(BF16), w1_ref[1])
    nrow = x.shape[0]
    hid = lo + pltpu.roll(hi, nrow - 1, 0)
    o_ref[...] = _dot(jax.nn.gelu(hid).astype(BF16), w2_ref[...])


def _compress(kvr, pe, w1, w2):
    b, nrow, width = kvr.shape
    return pl.pallas_call(
        _cmp_kernel, grid=(b,),
        in_specs=[pl.BlockSpec((None, nrow, width), lambda i: (i, 0, 0)),
                  pl.BlockSpec(pe.shape, lambda i: (0, 0)),
                  pl.BlockSpec(w1.shape, lambda i: (0, 0, 0)),
                  pl.BlockSpec(w2.shape, lambda i: (0, 0))],
        out_specs=pl.BlockSpec((None, nrow, w2.shape[1]), lambda i: (i, 0, 0)),
        out_shape=jax.ShapeDtypeStruct((b, nrow, w2.shape[1]), F32),
        compiler_params=_params(("arbitrary",), VMEM_LIMIT), name="nsa_compress")(kvr, pe, w1, w2)


def _softmax_tile(state, c, s_t, v_t, vis=None, cbias=None):
    m_ref, acc_ref = state
    m = m_ref[c]
    n_blk = 1 if vis is None else vis.shape[0]
    rows = s_t.shape[0] // n_blk
    parts = [s_t[b * rows:(b + 1) * rows] for b in range(n_blk)]
    tile_max = None
    for b in range(n_blk):
        mb = jnp.max(parts[b], axis=0, keepdims=True)
        if vis is not None:
            mb = jnp.where(vis[b:b + 1] > 0.5, mb, NEG_INF)
        tile_max = mb if tile_max is None else jnp.maximum(tile_max, mb)
    if cbias is not None:
        tile_max = tile_max + cbias
    m_new = jnp.maximum(m, tile_max)
    shift = m_new if cbias is None else m_new - cbias
    p_parts = []
    for b in range(n_blk):
        off = shift if vis is None else jnp.where(vis[b:b + 1] > 0.5, shift, -NEG_INF)
        p_parts.append(jnp.exp2(parts[b] - off))
    p_t = p_parts[0] if n_blk == 1 else jnp.concatenate(p_parts, axis=0)
    alpha = jnp.exp2(m - m_new)
    acc_ref[c] = alpha * acc_ref[c] + _dot(v_t, p_t.astype(BF16))
    m_ref[c] = m_new


def _run_chains(chains, logits_fn, update_fn):
    pending = {}
    for idx, c in enumerate(chains[:QK_LOOKAHEAD]):
        pending[c] = logits_fn(c, idx)
    for idx, c in enumerate(chains):
        update_fn(c, pending.pop(c), idx % QK_LOOKAHEAD)
        if idx + QK_LOOKAHEAD < len(chains):
            nxt = chains[idx + QK_LOOKAHEAD]
            pending[nxt] = logits_fn(nxt, idx % QK_LOOKAHEAD)


def _pair_loop(lo, hi, tiles_fn):
    odd = (hi - lo) % 2

    def single(_, carry):
        tiles_fn([lo])
        return carry

    def double(r, carry):
        jj = lo + odd + 2 * r
        tiles_fn([jj, jj + 1])
        return carry

    lax.fori_loop(0, odd, single, 0)
    lax.fori_loop(0, (hi - lo) // 2, double, 0)


def _stage_logits(s_ref, slot, s_t):
    s_ref[slot] = s_t


def _init_state(state):
    m_ref, acc_ref = state
    m_ref[...] = jnp.full(m_ref.shape, NEG_INF, F32)
    acc_ref[...] = jnp.zeros(acc_ref.shape, F32)


def _chain_out(state, c):
    _, acc_ref = state
    return acc_ref[c, 0:HEAD_DIM, :] / jnp.maximum(acc_ref[c, HEAD_DIM:HEAD_DIM + 1, :], TINY)


def _values_with_ones(tile, row0):
    v_t = tile.astype(F32).T[row0:row0 + HEAD_DIM]
    pad = lax.broadcasted_iota(jnp.int32, (V_ROWS - HEAD_DIM, tile.shape[0]), 0)
    return jnp.concatenate([v_t, jnp.where(pad == 0, 1.0, 0.0)], axis=0).astype(BF16)


def _rows(ref, kt, cols=slice(None)):
    start = kt * TQ if isinstance(kt, int) else pl.multiple_of(kt * TQ, TQ)
    return ref[pl.ds(start, TQ), cols]


def _store_head_queries(q_ref, qh_ref, n_pairs):
    row = lax.broadcasted_iota(jnp.int32, (LANES, TQ), 0)
    halves = (row < HEAD_DIM, row >= HEAD_DIM)
    for pp in range(n_pairs):
        q2t = q_ref[:, pp * LANES:(pp + 1) * LANES].astype(F32).T
        for j in range(2):
            qh_ref[2 * pp + j] = jnp.where(halves[j], q2t, 0.0).astype(BF16)


def _per_query_tile(step_fn, n_tiles):
    def kernel_fn(*refs):
        for i in range(n_tiles):
            pl.when(pl.program_id(1) == i)(functools.partial(step_fn, *refs, i=i))
    return kernel_fn


def _nsa_kernel(q_ref, g_ref, kvc_ref, ks_ref, vs_ref, kw_ref, vw_ref, bb_ref, far_ref, bc_ref, ovl_ref,
                o_ref, vst_ref, vwt_ref, qh_ref, gs_ref, oc_ref, sel_ref, m_ref, acc_ref, s_ref):
    i = pl.program_id(1)
    n_kt = vst_ref.shape[0]
    state = (m_ref, acc_ref)

    @pl.when(i == 0)
    def _values():
        for kt in range(n_kt):
            rows = slice(kt * TQ, (kt + 1) * TQ)
            vst_ref[kt] = _values_with_ones(vs_ref[rows, :], 0)
            vwt_ref[kt] = _values_with_ones(vw_ref[rows, :], 0)

    per_tile = TQ // SLC_BLOCK
    win_tiles = WINDOW // TQ

    def select_blocks():
        _store_head_queries(q_ref, qh_ref, N_PAIRS)
        gs_ref[...] = jax.nn.sigmoid(g_ref[...]).T
        kc2 = kvc_ref[:, 0:LANES].astype(BF16)
        vc_t = kvc_ref[:, LANES:2 * LANES].T[0:HEAD_DIM].astype(BF16)
        psums = []

        def cmp_logits(h, slot):
            s_ref[slot, 0:LANES, :] = _dot(kc2, qh_ref[h])

        def cmp_update(h, _s, slot):
            bias = bc_ref[h]
            s = s_ref[slot, 0:LANES, :] + bias
            m = jnp.maximum(jnp.max(s, axis=0, keepdims=True), 0.5 * NEG_INF)
            e = jnp.exp2(s - m)
            pc = e / jnp.maximum(jnp.sum(e, axis=0, keepdims=True), TINY)
            psums[:] = [pc if not psums else psums[0] + pc]
            oc_ref[h] = gs_ref[3 * h:3 * h + 1, :] * _dot(vc_t, pc.astype(BF16))

        _run_chains(list(range(N_HEADS)), cmp_logits, cmp_update)
        psum = psums[0]
        p_hi = psum.astype(BF16)
        p_lo = (psum - p_hi.astype(F32)).astype(BF16)
        imp = _dot(ovl_ref[...], p_hi) + _dot(ovl_ref[...], p_lo)
        n_slc = n_kt * per_tile
        imp = imp[0:n_slc, :]
        sidx = lax.broadcasted_iota(jnp.int32, (n_slc, TQ), 0)
        t = i * TQ + lax.broadcasted_iota(jnp.int32, (n_slc, TQ), 1)
        cur = lax.shift_right_logical(t, 6)
        eligible = sidx <= cur
        forced = (sidx == 0) | (sidx == cur) | (sidx == cur - 1)
        score = jnp.where(eligible, jnp.where(forced, FORCE_SCORE, 0.0), NEG_INF) + imp
        cnt = jnp.zeros((n_slc, TQ), jnp.int32)
        for sp in range(n_slc):
            row = score[sp:sp + 1, :]
            better = (row > score) | ((row == score) & (sp < sidx))
            cnt = cnt + better.astype(jnp.int32)
        chosen = jnp.where(cnt < min(SLC_TOPK, n_slc), 1.0, 0.0)
        for kt in range(n_kt):
            sel_ref[kt, 0:per_tile, :] = chosen[kt * per_tile:(kt + 1) * per_tile, :]
        _init_state(state)

    def near_tiles(n_tiles):
        def logits(c, slot):
            kind, jj, h = c
            _stage_logits(s_ref, slot, _dot(_rows(ks_ref if kind == "s" else kw_ref, i - jj), qh_ref[h]))

        def update(c, _s, slot):
            kind, jj, h = c
            kt = i - jj
            if kind == "s":
                _softmax_tile(state, h, s_ref[slot] + bb_ref[h, min(jj, 2)], vst_ref[kt],
                              vis=sel_ref[kt, 0:per_tile, :])
            else:
                _softmax_tile(state, N_HEADS + h, s_ref[slot] + bb_ref[h, 3 if jj == win_tiles else jj], vwt_ref[kt])

        _run_chains([(kind, jj, h) for jj in range(n_tiles) for h in range(N_HEADS) for kind in ("s", "w")],
                    logits, update)

    def far_tiles(jjs):
        kts = [i - jj for jj in jjs]
        _run_chains([(n, h) for n in range(len(jjs)) for h in range(N_HEADS)],
                    lambda c, slot: _dot(_rows(ks_ref, kts[c[0]]), qh_ref[c[1]]),
                    lambda c, s, slot: _softmax_tile(state, c[1], s, vst_ref[kts[c[0]]],
                                                     vis=sel_ref[kts[c[0]], 0:per_tile, :], cbias=far_ref[c[1]]))

    def front(n_tiles):
        select_blocks()
        near_tiles(min(n_tiles, win_tiles + 1))
        if n_tiles > win_tiles + 1:
            far_tiles(list(range(win_tiles + 1, n_tiles)))

    max_front = win_tiles + 1 + NSA_FRONT_FAR
    for n_tiles in range(1, max_front + 1):
        cond = (i >= n_tiles - 1) if n_tiles == max_front else (i == n_tiles - 1)
        pl.when(cond)(functools.partial(front, n_tiles))
    _pair_loop(jnp.minimum(i + 1, max_front), i + 1, far_tiles)

    for pp in range(N_PAIRS):
        hs = (2 * pp, 2 * pp + 1)
        outs = []
        for h in hs:
            g1 = gs_ref[3 * h + 1:3 * h + 2, :]
            g2 = gs_ref[3 * h + 2:3 * h + 3, :]
            outs.append(oc_ref[h] + g1 * _chain_out(state, h) + g2 * _chain_out(state, N_HEADS + h))
        o_ref[:, pp * LANES:(pp + 1) * LANES] = jnp.concatenate(outs, axis=0).T.astype(o_ref.dtype)


def _nsa_attention(q, g, kvc, ks2, vs2, kw2, vw2, bb, far, bc, ovl_t):
    b, s, _ = q.shape
    n_qt = s // TQ
    per_b = lambda bi, i: (bi, 0, 0)
    return pl.pallas_call(
        _nsa_kernel, grid=(b, n_qt),
        in_specs=[pl.BlockSpec((None, TQ, MAIN_W), lambda bi, i: (bi, i, 0)),
                  pl.BlockSpec((None, TQ, LANES), lambda bi, i: (bi, i, 0)),
                  pl.BlockSpec((None,) + kvc.shape[1:], per_b),
                  pl.BlockSpec((None, s, LANES), per_b), pl.BlockSpec((None, s, LANES), per_b),
                  pl.BlockSpec((None, s, LANES), per_b), pl.BlockSpec((None, s, LANES), per_b),
                  pl.BlockSpec(bb.shape, lambda bi, i: (0, 0, 0, 0)),
                  pl.BlockSpec(far.shape, lambda bi, i: (0, 0, 0)),
                  pl.BlockSpec((None,) + bc.shape[1:], lambda bi, i: (i, 0, 0, 0)),
                  pl.BlockSpec(ovl_t.shape, lambda bi, i: (0, 0))],
        out_specs=pl.BlockSpec((None, TQ, MAIN_W), lambda bi, i: (bi, i, 0)),
        out_shape=jax.ShapeDtypeStruct((b, s, MAIN_W), BF16),
        scratch_shapes=[pltpu.VMEM((n_qt, V_ROWS, TQ), BF16), pltpu.VMEM((n_qt, V_ROWS, TQ), BF16),
                        pltpu.VMEM((N_HEADS, LANES, TQ), BF16), pltpu.VMEM((LANES, TQ), F32),
                        pltpu.VMEM((N_HEADS, HEAD_DIM, TQ), F32), pltpu.VMEM((n_qt, 8, TQ), F32),
                        pltpu.VMEM((2 * N_HEADS, 1, TQ), F32),
                        pltpu.VMEM((2 * N_HEADS, V_ROWS, TQ), F32), pltpu.VMEM((QK_LOOKAHEAD, TQ, TQ), F32)],
        compiler_params=_params(("arbitrary", "arbitrary"), VMEM_LIMIT),
        name="nsa_attention")(q, g, kvc, ks2, vs2, kw2, vw2, bb, far, bc, ovl_t)


def _moba_step(q_ref, k_ref, v_ref, km_ref, bb_ref, far_ref, o_ref, vt_ref, qh_ref, sel_ref,
               m_ref, acc_ref, s_ref, *, i):
    n_blk = vt_ref.shape[1]
    state = (m_ref, acc_ref)
    heads = list(range(N_HEADS))

    if i == 0:
        for pp in range(N_PAIRS):
            for kt in range(n_blk):
                tile = v_ref[kt * TQ:(kt + 1) * TQ, pp * LANES:(pp + 1) * LANES]
                for j in range(2):
                    vt_ref[2 * pp + j, kt] = _values_with_ones(tile, j * HEAD_DIM)

    def gate_blocks():
        _store_head_queries(q_ref, qh_ref, N_PAIRS)
        blk = lax.broadcasted_iota(jnp.int32, (n_blk, TQ), 0)
        eligible = blk < i
        for pp in range(N_PAIRS):
            km = km_ref[:, pp * LANES:(pp + 1) * LANES].astype(BF16)
            for j in range(2):
                gate = _dot(km, qh_ref[2 * pp + j])[0:n_blk, :]
                gate = jnp.where(eligible, gate, NEG_INF)
                cnt = jnp.zeros((n_blk, TQ), jnp.int32)
                for n in range(n_blk):
                    row = gate[n:n + 1, :]
                    better = (row > gate) | ((row == gate) & (n < blk))
                    cnt = cnt + better.astype(jnp.int32)
                chosen = (cnt < min(MOBA_TOPK, n_blk - 1)) & eligible
                vis = jnp.where(chosen | (blk == i), 1.0, 0.0)
                for n in range(n_blk):
                    sel_ref[2 * pp + j, n] = vis[n:n + 1, :]
        _init_state(state)

    def tile_inputs(kt, h):
        pp = h // 2
        k = _rows(k_ref, kt, slice(pp * LANES, (pp + 1) * LANES))
        return k, vt_ref[h, kt], sel_ref[h, kt]

    def front(n_tiles):
        gate_blocks()

        def update(c, _s, slot):
            jj, h = c
            _, v_t, vis = tile_inputs(i - jj, h)
            _softmax_tile(state, h, s_ref[slot] + bb_ref[h, jj], v_t, vis=vis)

        def logits(c, slot):
            jj, h = c
            _stage_logits(s_ref, slot, _dot(tile_inputs(i - jj, h)[0], qh_ref[h]))

        _run_chains([(jj, h) for jj in range(n_tiles) for h in heads], logits, update)

    n_near = min(i, 1) + 1

    def far_tiles(jjs):
        kts = [i - jj for jj in jjs]

        def update(c, s, slot):
            _, v_t, vis = tile_inputs(kts[c[0]], c[1])
            _softmax_tile(state, c[1], s, v_t, vis=vis, cbias=far_ref[c[1]])

        _run_chains([(n, h) for n in range(len(jjs)) for h in heads],
                    lambda c, slot: _dot(tile_inputs(kts[c[0]], c[1])[0], qh_ref[c[1]]), update)

    front(n_near)
    if i + 1 > n_near:
        far_tiles(list(range(n_near, i + 1)))

    for pp in range(N_PAIRS):
        out_t = jnp.concatenate([_chain_out(state, 2 * pp), _chain_out(state, 2 * pp + 1)], axis=0)
        o_ref[:, pp * LANES:(pp + 1) * LANES] = out_t.T.astype(o_ref.dtype)


def _moba_attention(q, k, v, kmean, bb, far):
    b, s, _ = q.shape
    n_qt = s // TQ
    per_b = lambda bi, i: (bi, 0, 0)
    return pl.pallas_call(
        _per_query_tile(_moba_step, n_qt), grid=(b, n_qt),
        in_specs=[pl.BlockSpec((None, TQ, MAIN_W), lambda bi, i: (bi, i, 0)),
                  pl.BlockSpec((None, s, MAIN_W), per_b), pl.BlockSpec((None, s, MAIN_W), per_b),
                  pl.BlockSpec((None, LANES, MAIN_W), per_b),
                  pl.BlockSpec((N_HEADS, 2, TQ, TQ), lambda bi, i: (0, 0, 0, 0)),
                  pl.BlockSpec(far.shape, lambda bi, i: (0, 0, 0))],
        out_specs=pl.BlockSpec((None, TQ, MAIN_W), lambda bi, i: (bi, i, 0)),
        out_shape=jax.ShapeDtypeStruct((b, s, MAIN_W), BF16),
        scratch_shapes=[pltpu.VMEM((N_HEADS, n_qt, V_ROWS, TQ), BF16), pltpu.VMEM((N_HEADS, LANES, TQ), BF16),
                        pltpu.VMEM((N_HEADS, n_qt, 1, TQ), F32),
                        pltpu.VMEM((N_HEADS, 1, TQ), F32),
                        pltpu.VMEM((N_HEADS, V_ROWS, TQ), F32), pltpu.VMEM((QK_LOOKAHEAD, TQ, TQ), F32)],
        compiler_params=_params(("arbitrary", "arbitrary"), VMEM_LIMIT),
        name="moba_attention")(q, k, v, kmean, bb, far)


def _mem_kernel(q_ref, kv_ref, o_ref, qh_ref):
    n_sub = q_ref.shape[0] // TQ
    for sub in range(n_sub):
        _store_head_queries(q_ref.at[sub * TQ:(sub + 1) * TQ, :], qh_ref.at[sub], MEM_HEADS // 2)
    outs = {}

    def logits(c, slot):
        sub, h = c
        return _dot(kv_ref[:, (h // 2) * LANES:(h // 2 + 1) * LANES], qh_ref[sub, h])

    def update(c, s, slot):
        h = c[1]
        lanes = slice(MEM_W + (h // 2) * LANES, MEM_W + (h // 2 + 1) * LANES)
        v_t = kv_ref[:, lanes].astype(F32).T[(h % 2) * HEAD_DIM:(h % 2 + 1) * HEAD_DIM].astype(BF16)
        e = jnp.exp2(s - jnp.max(s, axis=0, keepdims=True))
        pr = e / jnp.sum(e, axis=0, keepdims=True)
        outs[c] = _dot(v_t, pr.astype(BF16))

    _run_chains([(sub, h) for sub in range(n_sub) for h in range(MEM_HEADS)], logits, update)
    for sub in range(n_sub):
        for pp in range(MEM_HEADS // 2):
            pair = jnp.concatenate([outs[(sub, 2 * pp)], outs[(sub, 2 * pp + 1)]], axis=0)
            o_ref[sub * TQ:(sub + 1) * TQ, pp * LANES:(pp + 1) * LANES] = pair.T.astype(o_ref.dtype)


MEM_TILES = 8


def _mem_attention(qm, mem_kv):
    b, s, _ = qm.shape
    n_mem = mem_kv.shape[1]
    rows = MEM_TILES * TQ
    return pl.pallas_call(
        _mem_kernel, grid=(b, s // rows),
        in_specs=[pl.BlockSpec((None, rows, MEM_W), lambda bi, i: (bi, i, 0)),
                  pl.BlockSpec((None, n_mem, 2 * MEM_W), lambda bi, i: (bi, 0, 0))],
        out_specs=pl.BlockSpec((None, rows, MEM_W), lambda bi, i: (bi, i, 0)),
        out_shape=jax.ShapeDtypeStruct((b, s, MEM_W), BF16),
        scratch_shapes=[pltpu.VMEM((MEM_TILES, MEM_HEADS, LANES, TQ), BF16)],
        compiler_params=_params(("arbitrary", "arbitrary")), name="mem_attention")(qm, mem_kv)


def _layer_norm(y, g, b):
    mu = jnp.mean(y, axis=-1, keepdims=True)
    var = jnp.mean(jnp.square(y - mu), axis=-1, keepdims=True)
    return (y - mu) * lax.rsqrt(var + LN_EPS) * g + b


def _mm_ln_kernel(*refs, n_in):
    a_refs, w_refs = refs[:n_in], refs[n_in:2 * n_in]
    x_ref, g_ref, b_ref, o_ref, y_ref = refs[2 * n_in:]

    @pl.when(pl.program_id(0) == 0)
    def _():
        y_ref[...] = jnp.zeros(y_ref.shape, F32)

    o_ref[...] = _layer_norm(ALPHA * x_ref[...] + y_ref[...], g_ref[...], b_ref[...])
    acc = _dot(a_refs[0][...], w_refs[0][...])
    for a_ref, w_ref in zip(a_refs[1:], w_refs[1:]):
        acc = acc + _dot(a_ref[...], w_ref[...])
    y_ref[...] = acc


def _mm_ln(acts, weights, x, g, b, tm, name):
    m, d = x.shape
    n_t = m // tm
    cur = lambda t: (jnp.minimum(t, n_t - 1), 0)
    prev = lambda t: (jnp.maximum(t - 1, 0), 0)
    full = lambda t: (0, 0)
    return pl.pallas_call(
        functools.partial(_mm_ln_kernel, n_in=len(acts)), grid=(n_t + 1,),
        in_specs=[pl.BlockSpec((tm, a.shape[1]), cur) for a in acts]
                 + [pl.BlockSpec(w.shape, full) for w in weights]
                 + [pl.BlockSpec((tm, d), prev), pl.BlockSpec((1, d), full), pl.BlockSpec((1, d), full)],
        out_specs=pl.BlockSpec((tm, d), prev),
        out_shape=jax.ShapeDtypeStruct((m, d), F32),
        scratch_shapes=[pltpu.VMEM((tm, d), F32)],
        compiler_params=_params(("arbitrary",), VMEM_LIMIT), name=name)(*acts, *weights, x, g, b)


FF_CHUNK = 256
FF_ROWS = 512
SUBLANES = 8


def _ffn_kernel(x_ref, wa_ref, wb_ref, cw_ref, cb_ref, wo_ref, wol_ref, g_ref, b_ref, o_ref, xb_ref, hid_ref, *,
                n_c):
    c = pl.program_id(1)

    @pl.when(c == 0)
    def _():
        xb_ref[...] = x_ref[...].astype(BF16)
        hid_ref[1] = jnp.zeros(hid_ref.shape[1:], BF16)
        o_ref[...] = jnp.zeros(o_ref.shape, F32)

    xb = xb_ref[...]
    a = _dot(xb, wa_ref[...])
    o_ref[...] += _dot(hid_ref[(c + 1) % 2], wo_ref[...])
    gate = _dot(xb, wb_ref[...])
    ext = jnp.concatenate([jnp.zeros((SUBLANES, a.shape[1]), F32), a], axis=0)
    a1 = pltpu.roll(ext, 1, 0)[SUBLANES:]
    a2 = pltpu.roll(ext, 2, 0)[SUBLANES:]
    conv = cw_ref[0:1, :] * a2 + cw_ref[1:2, :] * a1 + cw_ref[2:3, :] * a + cb_ref[...]
    hid_ref[c % 2] = (jax.nn.gelu(conv) * gate).astype(BF16)

    @pl.when(c == n_c - 1)
    def _():
        for r0 in range(0, o_ref.shape[0], FF_ROWS):
            rows = slice(r0, r0 + FF_ROWS)
            y = o_ref[rows, :] + _dot(hid_ref[(n_c - 1) % 2, rows, :], wol_ref[...])
            o_ref[rows, :] = _layer_norm(ALPHA * x_ref[rows, :] + y, g_ref[...], b_ref[...])


def _ffn_ln(x, w_in, conv_w, conv_b, w_out, g, b):
    bsz, s, d = x.shape
    n_c = D_FF // FF_CHUNK
    prev = lambda c: jnp.maximum(c - 1, 0)
    return pl.pallas_call(
        functools.partial(_ffn_kernel, n_c=n_c), grid=(bsz, n_c),
        in_specs=[pl.BlockSpec((None, s, d), lambda bi, c: (bi, 0, 0)),
                  pl.BlockSpec((d, FF_CHUNK), lambda bi, c: (0, c)),
                  pl.BlockSpec((d, FF_CHUNK), lambda bi, c: (0, n_c + c)),
                  pl.BlockSpec((CONV_W, FF_CHUNK), lambda bi, c: (0, c)),
                  pl.BlockSpec((1, FF_CHUNK), lambda bi, c: (0, c)),
                  pl.BlockSpec((FF_CHUNK, d), lambda bi, c: (prev(c), 0)),
                  pl.BlockSpec((FF_CHUNK, d), lambda bi, c: (n_c - 1, 0)),
                  pl.BlockSpec((1, d), lambda bi, c: (0, 0)), pl.BlockSpec((1, d), lambda bi, c: (0, 0))],
        out_specs=pl.BlockSpec((None, s, d), lambda bi, c: (bi, 0, 0)),
        out_shape=jax.ShapeDtypeStruct((bsz, s, d), F32),
        scratch_shapes=[pltpu.VMEM((s, d), BF16), pltpu.VMEM((2, s, FF_CHUNK), BF16)],
        compiler_params=_params(("arbitrary", "arbitrary"), VMEM_LIMIT),
        name="ffn_ln")(x, w_in, w_in, conv_w, conv_b, w_out, w_out, g, b)


def _dup(w):
    return jnp.concatenate([w, w], axis=1)


def _cmp_weights(pe_k, w1_k, w2_k, pe_v, w1_v, w2_v):
    half = CMP_LEN // 2
    zk = jnp.zeros((half, HEAD_DIM, CMP_HIDDEN), F32)

    def w1_half(sl):
        wk = w1_k.reshape(CMP_LEN, HEAD_DIM, CMP_HIDDEN)[sl]
        wv = w1_v.reshape(CMP_LEN, HEAD_DIM, CMP_HIDDEN)[sl]
        top = jnp.concatenate([wk, zk], axis=2)
        bot = jnp.concatenate([zk, wv], axis=2)
        return jnp.concatenate([top, bot], axis=1).reshape(half * 2 * HEAD_DIM, 2 * CMP_HIDDEN)

    w1 = jnp.stack([w1_half(slice(0, half)), w1_half(slice(half, CMP_LEN))]).astype(BF16)
    pe = jnp.concatenate([pe_k, pe_v], axis=1)
    pe = jnp.stack([pe[:half].reshape(-1), pe[half:].reshape(-1)])
    zo = jnp.zeros((CMP_HIDDEN, 2 * HEAD_DIM), F32)
    w2 = jnp.concatenate([jnp.concatenate([_dup(w2_k), zo], axis=1),
                          jnp.concatenate([zo, _dup(w2_v)], axis=1)], axis=0).astype(BF16)
    return pe, w1, w2


def _overlap_matrix(seq):
    n_cmp = (seq - CMP_LEN) // CMP_STRIDE + 1
    n_slc = seq // SLC_BLOCK
    start = np.arange(LANES) * CMP_STRIDE
    bs = np.arange(LANES) * SLC_BLOCK
    ovl_t = ((start[None, :] < bs[:, None] + SLC_BLOCK) & (start[None, :] + CMP_LEN > bs[:, None])
             & (np.arange(LANES)[None, :] < n_cmp) & (np.arange(LANES)[:, None] < n_slc))
    return jnp.asarray(ovl_t, BF16)


def kernel(x, mem, rel_bias, a_w_in, a_cmp_pe_k, a_cmp_w1_k, a_cmp_w2_k, a_cmp_pe_v, a_cmp_w1_v, a_cmp_w2_v,
           a_w_mem_kv, a_w_out, shared_w_kv, b_w_in, b_w_mem_kv, b_w_out, ln1_g, ln1_b, ln2_g, ln2_b,
           ffn_w_in, ffn_conv_w, ffn_conv_b, ffn_w_out):
    bsz, seq, d = x.shape
    n_mem = mem.shape[1]
    m = bsz * seq
    assert (seq, d) == (2048, D_MODEL) and seq % TQ == 0
    n_qt = seq // TQ

    bb, bc, far = _bias_tables(rel_bias, n_qt)
    ovl_t = _overlap_matrix(seq)
    memf = mem.reshape(bsz * n_mem, d)
    xf = x.reshape(m, d)

    def ffn(xcur, layer):
        return _ffn_ln(xcur.reshape(bsz, seq, d), ffn_w_in[layer].astype(BF16), ffn_conv_w[layer],
                       ffn_conv_b[layer][None, :], ffn_w_out[layer].astype(BF16),
                       ln2_g[layer][None, :], ln2_b[layer][None, :]).reshape(m, d)

    def out_ln(o_main, o_mem, w_out, xcur, layer):
        w = w_out.astype(BF16)
        return _mm_ln([o_main.reshape(m, MAIN_W), o_mem.reshape(m, MEM_W)], [w[:MAIN_W], w[MAIN_W:]], xcur,
                      ln1_g[layer][None, :], ln1_b[layer][None, :], ROW_TILE, "outproj_ln")

    w = a_w_in[0]
    c0 = MAIN_W
    cols = [w[:, c0 + kk * HEAD_DIM:c0 + (kk + 1) * HEAD_DIM] for kk in range(6)]
    c_g = c0 + 6 * HEAD_DIM
    w_g = jnp.pad(w[:, c_g:c_g + 3 * N_HEADS], ((0, 0), (0, LANES - 3 * N_HEADS)))
    w_qm = w[:, c_g + 3 * N_HEADS:]
    weights = [w[:, :MAIN_W], jnp.concatenate(cols[0:2], axis=1), _dup(cols[2]), _dup(cols[3]),
               _dup(cols[4]), _dup(cols[5]), w_g, w_qm]
    weights = [wi.astype(BF16) for wi in weights]
    q, kvc_tok, ks2, vs2, kw2, vw2, g, qm = _proj(
        xf, weights, [BF16, BF16, BF16, BF16, BF16, BF16, F32, BF16], 2 * ROW_TILE, "proj_a",
        scales=[Q_SCALE, None, None, None, None, None, None, Q_SCALE])

    pe, w1, w2 = _cmp_weights(a_cmp_pe_k[0], a_cmp_w1_k[0], a_cmp_w2_k[0],
                              a_cmp_pe_v[0], a_cmp_w1_v[0], a_cmp_w2_v[0])
    kvc = _compress(kvc_tok.reshape(bsz, seq // (CMP_LEN // 2), (CMP_LEN // 2) * LANES), pe, w1, w2)

    r3 = lambda t: t.reshape(bsz, seq, t.shape[-1])
    o_main = _nsa_attention(r3(q), r3(g), kvc, r3(ks2), r3(vs2), r3(kw2), r3(vw2), bb, far, bc, ovl_t)
    (mkv,) = _proj(memf, [a_w_mem_kv[0].astype(BF16)], [BF16], 512, "proj_mem_a")
    o_mem = _mem_attention(r3(qm), mkv.reshape(bsz, n_mem, 2 * MEM_W))
    x1 = out_ln(o_main, o_mem, a_w_out[0], xf, 0)
    x1 = ffn(x1, 0)

    wb = b_w_in[0].astype(BF16)
    wkv = shared_w_kv.astype(BF16)
    q, qm, k, v, kmean = _proj_b(x1, wb[:, :MAIN_W], wb[:, MAIN_W:], wkv[:, :MAIN_W], wkv[:, MAIN_W:])
    n_blk = seq // MOBA_BLOCK
    kmean = jnp.pad(kmean.reshape(bsz, n_blk, MAIN_W), ((0, 0), (0, LANES - n_blk), (0, 0)))
    o_main = _moba_attention(r3(q), r3(k), r3(v), kmean, bb, far)
    (mkv,) = _proj(memf, [b_w_mem_kv[0].astype(BF16)], [BF16], 512, "proj_mem_b")
    o_mem = _mem_attention(r3(qm), mkv.reshape(bsz, n_mem, 2 * MEM_W))
    x2 = out_ln(o_main, o_mem, b_w_out[0], x1, 1)
    x2 = ffn(x2, 1)
    return x2.reshape(bsz, seq, d)
```

```python
import functools
import math

import numpy as np
import jax
import jax.numpy as jnp
from jax import lax
from jax.experimental import pallas as pl
from jax.experimental.pallas import tpu as pltpu

F32 = jnp.float32
BF16 = jnp.bfloat16

D_MODEL = 1024
HEAD_DIM = 64
N_HEADS = 12
N_PAIRS = N_HEADS // 2
MAIN_W = N_HEADS * HEAD_DIM
MEM_HEADS = 4
MEM_W = MEM_HEADS * HEAD_DIM
CMP_LEN = 32
CMP_STRIDE = 16
CMP_HIDDEN = 256
SLC_BLOCK = 64
SLC_TOPK = 16
WINDOW = 512
FORCE_SCORE = 1.0e4
MOBA_BLOCK = 256
MOBA_TOPK = 3
REL_BUCKETS = 32
REL_MAX_DIST = 128
D_FF = 2816
CONV_W = 3
DEPTH = 2
ALPHA = (2.0 * DEPTH) ** 0.25
LN_EPS = 1e-5
NEG_INF = -1e30
TINY = 1e-30
LOG2E = math.log2(math.e)
Q_SCALE = HEAD_DIM ** -0.5 * LOG2E

LANES = 128
TQ = 256
VMEM_LIMIT = 60 * 1024 * 1024
V_ROWS = HEAD_DIM + 8
ROW_TILE = 1024
NSA_FRONT_FAR = 2
QK_LOOKAHEAD = 6


def _dot(a, b):
    return jnp.dot(a, b, preferred_element_type=F32)


def _dot_nt(a, b):
    return lax.dot_general(a, b, (((1,), (1,)), ((), ())), preferred_element_type=F32)


def _params(sem, vmem=None):
    return pltpu.CompilerParams(dimension_semantics=sem, vmem_limit_bytes=vmem)


def _proj_kernel(x_ref, w_ref, *o_refs, scales):
    y = _dot(x_ref[...].astype(BF16), w_ref[...])
    col = 0
    for o_ref, scale in zip(o_refs, scales):
        part = y[:, col:col + o_ref.shape[1]]
        o_ref[...] = (part if scale is None else part * scale).astype(o_ref.dtype)
        col += o_ref.shape[1]


def _proj(x, weights, dtypes, tm, name, scales=None):
    m, k = x.shape
    scales = tuple(scales) if scales is not None else (None,) * len(weights)
    assert all(w.shape[1] % LANES == 0 for w in weights)
    w_all = jnp.concatenate(weights, axis=1) if len(weights) > 1 else weights[0]
    out_specs = [pl.BlockSpec((tm, w.shape[1]), lambda i: (i, 0)) for w in weights]
    out_shape = [jax.ShapeDtypeStruct((m, w.shape[1]), dt) for w, dt in zip(weights, dtypes)]
    return pl.pallas_call(
        functools.partial(_proj_kernel, scales=scales), grid=(m // tm,),
        in_specs=[pl.BlockSpec((tm, k), lambda i: (i, 0)), pl.BlockSpec(w_all.shape, lambda i: (0, 0))],
        out_specs=out_specs, out_shape=out_shape,
        compiler_params=_params(("arbitrary",), VMEM_LIMIT), name=name)(x, w_all)


def _projb_kernel(x_ref, wq_ref, wm_ref, wk_ref, wv_ref, q_ref, qm_ref, k_ref, v_ref, km_ref):
    xb = x_ref[...].astype(BF16)
    q_ref[...] = (_dot(xb, wq_ref[...]) * Q_SCALE).astype(BF16)
    qm_ref[...] = (_dot(xb, wm_ref[...]) * Q_SCALE).astype(BF16)
    k = _dot(xb, wk_ref[...])
    k_ref[...] = k.astype(BF16)
    for blk in range(km_ref.shape[0]):
        km_ref[blk] = jnp.mean(k[blk * MOBA_BLOCK:(blk + 1) * MOBA_BLOCK], axis=0, keepdims=True)
    v_ref[...] = _dot(xb, wv_ref[...]).astype(BF16)


PROJ_B_BLOCKS = 4


def _proj_b(x, wq, wm, wk, wv):
    m, k = x.shape
    nblk = m // MOBA_BLOCK
    tm = PROJ_B_BLOCKS * MOBA_BLOCK
    row = lambda i: (i, 0)
    full = lambda i: (0, 0)
    return pl.pallas_call(
        _projb_kernel, grid=(m // tm,),
        in_specs=[pl.BlockSpec((tm, k), row)] + [pl.BlockSpec(w.shape, full) for w in (wq, wm, wk, wv)],
        out_specs=[pl.BlockSpec((tm, MAIN_W), row), pl.BlockSpec((tm, MEM_W), row),
                   pl.BlockSpec((tm, MAIN_W), row), pl.BlockSpec((tm, MAIN_W), row),
                   pl.BlockSpec((PROJ_B_BLOCKS, 1, MAIN_W), lambda i: (i, 0, 0))],
        out_shape=[jax.ShapeDtypeStruct((m, MAIN_W), BF16), jax.ShapeDtypeStruct((m, MEM_W), BF16),
                   jax.ShapeDtypeStruct((m, MAIN_W), BF16), jax.ShapeDtypeStruct((m, MAIN_W), BF16),
                   jax.ShapeDtypeStruct((nblk, 1, MAIN_W), F32)],
        compiler_params=_params(("arbitrary",), VMEM_LIMIT), name="proj_b")(x, wq, wm, wk, wv)


def _bias_kernel(tbl_ref, bb_ref, bc_ref, far_ref):
    h = pl.program_id(0)

    def bias_of(dist):
        n = jnp.maximum(dist, 0)
        max_exact = REL_BUCKETS // 2
        nf = jnp.maximum(n, 1).astype(F32)
        large = max_exact + (jnp.log(nf / max_exact) / math.log(REL_MAX_DIST / max_exact)
                             * (REL_BUCKETS - max_exact)).astype(jnp.int32)
        large = jnp.minimum(large, REL_BUCKETS - 1)
        bucket = jnp.where(n < max_exact, n, large)
        out = jnp.zeros(dist.shape, F32)
        for kk in range(REL_BUCKETS):
            out = jnp.where(bucket == kk, tbl_ref[h * REL_BUCKETS + kk], out)
        return out * LOG2E

    key = lax.broadcasted_iota(jnp.int32, (TQ, TQ), 0)
    qry = lax.broadcasted_iota(jnp.int32, (TQ, TQ), 1)
    d0 = qry - key
    bb_ref[0] = jnp.where(d0 >= 0, bias_of(d0), NEG_INF)
    bb_ref[1] = bias_of(d0 + TQ)
    far = bias_of(d0 + 2 * TQ)
    bb_ref[2] = far
    bb_ref[3] = jnp.where(d0 + 2 * TQ < WINDOW, far, NEG_INF)
    far_ref[...] = far[0:1, :]

    nc = lax.broadcasted_iota(jnp.int32, (LANES, TQ), 0)
    qc = lax.broadcasted_iota(jnp.int32, (LANES, TQ), 1)
    n_cmp = (2048 - CMP_LEN) // CMP_STRIDE + 1
    for i in range(bc_ref.shape[0]):
        dc = i * TQ + qc - (nc * CMP_STRIDE + CMP_LEN - 1)
        bc_ref[i] = jnp.where((dc >= 0) & (nc < n_cmp), bias_of(dc), NEG_INF)


def _bias_tables(rel_bias, n_qt):
    tbl = rel_bias.T.reshape(-1)
    return pl.pallas_call(
        _bias_kernel, grid=(N_HEADS,),
        in_specs=[pl.BlockSpec(memory_space=pltpu.SMEM)],
        out_specs=[pl.BlockSpec((None, 4, TQ, TQ), lambda h: (h, 0, 0, 0)),
                   pl.BlockSpec((n_qt, None, LANES, TQ), lambda h: (0, h, 0, 0)),
                   pl.BlockSpec((None, 1, TQ), lambda h: (h, 0, 0))],
        out_shape=[jax.ShapeDtypeStruct((N_HEADS, 4, TQ, TQ), F32),
                   jax.ShapeDtypeStruct((n_qt, N_HEADS, LANES, TQ), F32),
                   jax.ShapeDtypeStruct((N_HEADS, 1, TQ), F32)],
        compiler_params=_params(("arbitrary",)), name="bias_tables")(tbl)


def _cmp_kernel(kv_ref, pe_ref, w1_ref, w2_ref, o_ref):
    x = kv_ref[...].astype(F32)
    lo = _dot((x + pe_ref[0:1, :]).astype(BF16), w1_ref[0])
    hi = _dot((x + pe_ref[1:2, :]).astype(BF16), w1_ref[1])
    nrow = x.shape[0]
    hid = lo + pltpu.roll(hi, nrow - 1, 0)
    o_ref[...] = _dot(jax.nn.gelu(hid).astype(BF16), w2_ref[...])


def _compress(kvr, pe, w1, w2):
    b, nrow, width = kvr.shape
    return pl.pallas_call(
        _cmp_kernel, grid=(b,),
        in_specs=[pl.BlockSpec((None, nrow, width), lambda i: (i, 0, 0)),
                  pl.BlockSpec(pe.shape, lambda i: (0, 0)),
                  pl.BlockSpec(w1.shape, lambda i: (0, 0, 0)),
                  pl.BlockSpec(w2.shape, lambda i: (0, 0))],
        out_specs=pl.BlockSpec((None, nrow, w2.shape[1]), lambda i: (i, 0, 0)),
        out_shape=jax.ShapeDtypeStruct((b, nrow, w2.shape[1]), F32),
        compiler_params=_params(("arbitrary",), VMEM_LIMIT), name="nsa_compress")(kvr, pe, w1, w2)


def _softmax_tile(state, c, s_t, v_t, vis=None, cbias=None):
    m_ref, acc_ref = state
    m = m_ref[c]
    n_blk = 1 if vis is None else vis.shape[0]
    rows = s_t.shape[0] // n_blk
    parts = [s_t[b * rows:(b + 1) * rows] for b in range(n_blk)]
    tile_max = None
    for b in range(n_blk):
        mb = jnp.max(parts[b], axis=0, keepdims=True)
        if vis is not None:
            mb = jnp.where(vis[b:b + 1] > 0.5, mb, NEG_INF)
        tile_max = mb if tile_max is None else jnp.maximum(tile_max, mb)
    if cbias is not None:
        tile_max = tile_max + cbias
    m_new = jnp.maximum(m, tile_max)
    shift = m_new if cbias is None else m_new - cbias
    p_parts = []
    for b in range(n_blk):
        off = shift if vis is None else jnp.where(vis[b:b + 1] > 0.5, shift, -NEG_INF)
        p_parts.append(jnp.exp2(parts[b] - off))
    p_t = p_parts[0] if n_blk == 1 else jnp.concatenate(p_parts, axis=0)
    alpha = jnp.exp2(m - m_new)
    acc_ref[c] = alpha * acc_ref[c] + _dot(v_t, p_t.astype(BF16))
    m_ref[c] = m_new


def _run_chains(chains, logits_fn, update_fn):
    pending = {}
    for idx, c in enumerate(chains[:QK_LOOKAHEAD]):
        pending[c] = logits_fn(c, idx)
    for idx, c in enumerate(chains):
        update_fn(c, pending.pop(c), idx % QK_LOOKAHEAD)
        if idx + QK_LOOKAHEAD < len(chains):
            nxt = chains[idx + QK_LOOKAHEAD]
            pending[nxt] = logits_fn(nxt, idx % QK_LOOKAHEAD)


def _pair_loop(lo, hi, tiles_fn):
    odd = (hi - lo) % 2

    def single(_, carry):
        tiles_fn([lo])
        return carry

    def double(r, carry):
        jj = lo + odd + 2 * r
        tiles_fn([jj, jj + 1])
        return carry

    lax.fori_loop(0, odd, single, 0)
    lax.fori_loop(0, (hi - lo) // 2, double, 0)


def _stage_logits(s_ref, slot, s_t):
    s_ref[slot] = s_t


def _init_state(state):
    m_ref, acc_ref = state
    m_ref[...] = jnp.full(m_ref.shape, NEG_INF, F32)
    acc_ref[...] = jnp.zeros(acc_ref.shape, F32)


def _chain_out(state, c):
    _, acc_ref = state
    return acc_ref[c, 0:HEAD_DIM, :] / jnp.maximum(acc_ref[c, HEAD_DIM:HEAD_DIM + 1, :], TINY)


def _values_with_ones(tile, row0):
    v_t = tile.astype(F32).T[row0:row0 + HEAD_DIM]
    pad = lax.broadcasted_iota(jnp.int32, (V_ROWS - HEAD_DIM, tile.shape[0]), 0)
    return jnp.concatenate([v_t, jnp.where(pad == 0, 1.0, 0.0)], axis=0).astype(BF16)


def _rows(ref, kt, cols=slice(None)):
    start = kt * TQ if isinstance(kt, int) else pl.multiple_of(kt * TQ, TQ)
    return ref[pl.ds(start, TQ), cols]


def _store_head_queries(q_ref, qh_ref, n_pairs):
    row = lax.broadcasted_iota(jnp.int32, (LANES, TQ), 0)
    halves = (row < HEAD_DIM, row >= HEAD_DIM)
    for pp in range(n_pairs):
        q2t = q_ref[:, pp * LANES:(pp + 1) * LANES].astype(F32).T
        for j in range(2):
            qh_ref[2 * pp + j] = jnp.where(halves[j], q2t, 0.0).astype(BF16)


def _per_query_tile(step_fn, n_tiles):
    def kernel_fn(*refs):
        for i in range(n_tiles):
            pl.when(pl.program_id(1) == i)(functools.partial(step_fn, *refs, i=i))
    return kernel_fn


def _nsa_kernel(q_ref, g_ref, kvc_ref, ks_ref, vs_ref, kw_ref, vw_ref, bb_ref, far_ref, bc_ref, ovl_ref,
                o_ref, vst_ref, vwt_ref, qh_ref, gs_ref, oc_ref, sel_ref, m_ref, acc_ref, s_ref):
    i = pl.program_id(1)
    n_kt = vst_ref.shape[0]
    state = (m_ref, acc_ref)

    @pl.when(i == 0)
    def _values():
        for kt in range(n_kt):
            rows = slice(kt * TQ, (kt + 1) * TQ)
            vst_ref[kt] = _values_with_ones(vs_ref[rows, :], 0)
            vwt_ref[kt] = _values_with_ones(vw_ref[rows, :], 0)

    per_tile = TQ // SLC_BLOCK
    win_tiles = WINDOW // TQ

    def select_blocks():
        _store_head_queries(q_ref, qh_ref, N_PAIRS)
        gs_ref[...] = jax.nn.sigmoid(g_ref[...]).T
        kc2 = kvc_ref[:, 0:LANES].astype(BF16)
        vc_t = kvc_ref[:, LANES:2 * LANES].T[0:HEAD_DIM].astype(BF16)
        psums = []

        def cmp_logits(h, slot):
            s_ref[slot, 0:LANES, :] = _dot(kc2, qh_ref[h])

        def cmp_update(h, _s, slot):
            bias = bc_ref[h]
            s = s_ref[slot, 0:LANES, :] + bias
            m = jnp.maximum(jnp.max(s, axis=0, keepdims=True), 0.5 * NEG_INF)
            e = jnp.exp2(s - m)
            pc = e / jnp.maximum(jnp.sum(e, axis=0, keepdims=True), TINY)
            psums[:] = [pc if not psums else psums[0] + pc]
            oc_ref[h] = gs_ref[3 * h:3 * h + 1, :] * _dot(vc_t, pc.astype(BF16))

        _run_chains(list(range(N_HEADS)), cmp_logits, cmp_update)
        psum = psums[0]
        p_hi = psum.astype(BF16)
        p_lo = (psum - p_hi.astype(F32)).astype(BF16)
        imp = _dot(ovl_ref[...], p_hi) + _dot(ovl_ref[...], p_lo)
        n_slc = n_kt * per_tile
        imp = imp[0:n_slc, :]
        sidx = lax.broadcasted_iota(jnp.int32, (n_slc, TQ), 0)
        t = i * TQ + lax.broadcasted_iota(jnp.int32, (n_slc, TQ), 1)
        cur = lax.shift_right_logical(t, 6)
        eligible = sidx <= cur
        forced = (sidx == 0) | (sidx == cur) | (sidx == cur - 1)
        score = jnp.where(eligible, jnp.where(forced, FORCE_SCORE, 0.0), NEG_INF) + imp
        cnt = jnp.zeros((n_slc, TQ), jnp.int32)
        for sp in range(n_slc):
            row = score[sp:sp + 1, :]
            better = (row > score) | ((row == score) & (sp < sidx))
            cnt = cnt + better.astype(jnp.int32)
        chosen = jnp.where(cnt < min(SLC_TOPK, n_slc), 1.0, 0.0)
        for kt in range(n_kt):
            sel_ref[kt, 0:per_tile, :] = chosen[kt * per_tile:(kt + 1) * per_tile, :]
        _init_state(state)

    def near_tiles(n_tiles):
        def logits(c, slot):
            kind, jj, h = c
            _stage_logits(s_ref, slot, _dot(_rows(ks_ref if kind == "s" else kw_ref, i - jj), qh_ref[h]))

        def update(c, _s, slot):
            kind, jj, h = c
            kt = i - jj
            if kind == "s":
                _softmax_tile(state, h, s_ref[slot] + bb_ref[h, min(jj, 2)], vst_ref[kt],
                              vis=sel_ref[kt, 0:per_tile, :])
            else:
                _softmax_tile(state, N_HEADS + h, s_ref[slot] + bb_ref[h, 3 if jj == win_tiles else jj], vwt_ref[kt])

        _run_chains([(kind, jj, h) for jj in range(n_tiles) for h in range(N_HEADS) for kind in ("s", "w")],
                    logits, update)

    def far_tiles(jjs):
        kts = [i - jj for jj in jjs]
        _run_chains([(n, h) for n in range(len(jjs)) for h in range(N_HEADS)],
                    lambda c, slot: _dot(_rows(ks_ref, kts[c[0]]), qh_ref[c[1]]),
                    lambda c, s, slot: _softmax_tile(state, c[1], s, vst_ref[kts[c[0]]],
                                                     vis=sel_ref[kts[c[0]], 0:per_tile, :], cbias=far_ref[c[1]]))

    def front(n_tiles):
        select_blocks()
        near_tiles(min(n_tiles, win_tiles + 1))
        if n_tiles > win_tiles + 1:
            far_tiles(list(range(win_tiles + 1, n_tiles)))

    max_front = win_tiles + 1 + NSA_FRONT_FAR
    for n_tiles in range(1, max_front + 1):
        cond = (i >= n_tiles - 1) if n_tiles == max_front else (i == n_tiles - 1)
        pl.when(cond)(functools.partial(front, n_tiles))
    _pair_loop(jnp.minimum(i + 1, max_front), i + 1, far_tiles)

    for pp in range(N_PAIRS):
        hs = (2 * pp, 2 * pp + 1)
        outs = []
        for h in hs:
            g1 = gs_ref[3 * h + 1:3 * h + 2, :]
            g2 = gs_ref[3 * h + 2:3 * h + 3, :]
            outs.append(oc_ref[h] + g1 * _chain_out(state, h) + g2 * _chain_out(state, N_HEADS + h))
        o_ref[:, pp * LANES:(pp + 1) * LANES] = jnp.concatenate(outs, axis=0).T.astype(o_ref.dtype)


def _nsa_attention(q, g, kvc, ks2, vs2, kw2, vw2, bb, far, bc, ovl_t):
    b, s, _ = q.shape
    n_qt = s // TQ
    per_b = lambda bi, i: (bi, 0, 0)
    return pl.pallas_call(
        _nsa_kernel, grid=(b, n_qt),
        in_specs=[pl.BlockSpec((None, TQ, MAIN_W), lambda bi, i: (bi, i, 0)),
                  pl.BlockSpec((None, TQ, LANES), lambda bi, i: (bi, i, 0)),
                  pl.BlockSpec((None,) + kvc.shape[1:], per_b),
                  pl.BlockSpec((None, s, LANES), per_b), pl.BlockSpec((None, s, LANES), per_b),
                  pl.BlockSpec((None, s, LANES), per_b), pl.BlockSpec((None, s, LANES), per_b),
                  pl.BlockSpec(bb.shape, lambda bi, i: (0, 0, 0, 0)),
                  pl.BlockSpec(far.shape, lambda bi, i: (0, 0, 0)),
                  pl.BlockSpec((None,) + bc.shape[1:], lambda bi, i: (i, 0, 0, 0)),
                  pl.BlockSpec(ovl_t.shape, lambda bi, i: (0, 0))],
        out_specs=pl.BlockSpec((None, TQ, MAIN_W), lambda bi, i: (bi, i, 0)),
        out_shape=jax.ShapeDtypeStruct((b, s, MAIN_W), BF16),
        scratch_shapes=[pltpu.VMEM((n_qt, V_ROWS, TQ), BF16), pltpu.VMEM((n_qt, V_ROWS, TQ), BF16),
                        pltpu.VMEM((N_HEADS, LANES, TQ), BF16), pltpu.VMEM((LANES, TQ), F32),
                        pltpu.VMEM((N_HEADS, HEAD_DIM, TQ), F32), pltpu.VMEM((n_qt, 8, TQ), F32),
                        pltpu.VMEM((2 * N_HEADS, 1, TQ), F32),
                        pltpu.VMEM((2 * N_HEADS, V_ROWS, TQ), F32), pltpu.VMEM((QK_LOOKAHEAD, TQ, TQ), F32)],
        compiler_params=_params(("arbitrary", "arbitrary"), VMEM_LIMIT),
        name="nsa_attention")(q, g, kvc, ks2, vs2, kw2, vw2, bb, far, bc, ovl_t)


def _moba_step(q_ref, k_ref, v_ref, km_ref, bb_ref, far_ref, o_ref, vt_ref, qh_ref, sel_ref,
               m_ref, acc_ref, s_ref, *, i):
    n_blk = vt_ref.shape[1]
    state = (m_ref, acc_ref)
    heads = list(range(N_HEADS))

    if i == 0:
        for pp in range(N_PAIRS):
            for kt in range(n_blk):
                tile = v_ref[kt * TQ:(kt + 1) * TQ, pp * LANES:(pp + 1) * LANES]
                for j in range(2):
                    vt_ref[2 * pp + j, kt] = _values_with_ones(tile, j * HEAD_DIM)

    def gate_blocks():
        _store_head_queries(q_ref, qh_ref, N_PAIRS)
        blk = lax.broadcasted_iota(jnp.int32, (n_blk, TQ), 0)
        eligible = blk < i
        for pp in range(N_PAIRS):
            km = km_ref[:, pp * LANES:(pp + 1) * LANES].astype(BF16)
            for j in range(2):
                gate = _dot(km, qh_ref[2 * pp + j])[0:n_blk, :]
                gate = jnp.where(eligible, gate, NEG_INF)
                cnt = jnp.zeros((n_blk, TQ), jnp.int32)
                for n in range(n_blk):
                    row = gate[n:n + 1, :]
                    better = (row > gate) | ((row == gate) & (n < blk))
                    cnt = cnt + better.astype(jnp.int32)
                chosen = (cnt < min(MOBA_TOPK, n_blk - 1)) & eligible
                vis = jnp.where(chosen | (blk == i), 1.0, 0.0)
                for n in range(n_blk):
                    sel_ref[2 * pp + j, n] = vis[n:n + 1, :]
        _init_state(state)

    def tile_inputs(kt, h):
        pp = h // 2
        k = _rows(k_ref, kt, slice(pp * LANES, (pp + 1) * LANES))
        return k, vt_ref[h, kt], sel_ref[h, kt]

    def front(n_tiles):
        gate_blocks()

        def update(c, _s, slot):
            jj, h = c
            _, v_t, vis = tile_inputs(i - jj, h)
            _softmax_tile(state, h, s_ref[slot] + bb_ref[h, jj], v_t, vis=vis)

        def logits(c, slot):
            jj, h = c
            _stage_logits(s_ref, slot, _dot(tile_inputs(i - jj, h)[0], qh_ref[h]))

        _run_chains([(jj, h) for jj in range(n_tiles) for h in heads], logits, update)

    n_near = min(i, 1) + 1

    def far_tiles(jjs):
        kts = [i - jj for jj in jjs]

        def update(c, s, slot):
            _, v_t, vis = tile_inputs(kts[c[0]], c[1])
            _softmax_tile(state, c[1], s, v_t, vis=vis, cbias=far_ref[c[1]])

        _run_chains([(n, h) for n in range(len(jjs)) for h in heads],
                    lambda c, slot: _dot(tile_inputs(kts[c[0]], c[1])[0], qh_ref[c[1]]), update)

    front(n_near)
    if i + 1 > n_near:
        far_tiles(list(range(n_near, i + 1)))

    for pp in range(N_PAIRS):
        out_t = jnp.concatenate([_chain_out(state, 2 * pp), _chain_out(state, 2 * pp + 1)], axis=0)
        o_ref[:, pp * LANES:(pp + 1) * LANES] = out_t.T.astype(o_ref.dtype)


def _moba_attention(q, k, v, kmean, bb, far):
    b, s, _ = q.shape
    n_qt = s // TQ
    per_b = lambda bi, i: (bi, 0, 0)
    return pl.pallas_call(
        _per_query_tile(_moba_step, n_qt), grid=(b, n_qt),
        in_specs=[pl.BlockSpec((None, TQ, MAIN_W), lambda bi, i: (bi, i, 0)),
                  pl.BlockSpec((None, s, MAIN_W), per_b), pl.BlockSpec((None, s, MAIN_W), per_b),
                  pl.BlockSpec((None, LANES, MAIN_W), per_b),
                  pl.BlockSpec((N_HEADS, 2, TQ, TQ), lambda bi, i: (0, 0, 0, 0)),
                  pl.BlockSpec(far.shape, lambda bi, i: (0, 0, 0))],
        out_specs=pl.BlockSpec((None, TQ, MAIN_W), lambda bi, i: (bi, i, 0)),
        out_shape=jax.ShapeDtypeStruct((b, s, MAIN_W), BF16),
        scratch_shapes=[pltpu.VMEM((N_HEADS, n_qt, V_ROWS, TQ), BF16), pltpu.VMEM((N_HEADS, LANES, TQ), BF16),
                        pltpu.VMEM((N_HEADS, n_qt, 1, TQ), F32),
                        pltpu.VMEM((N_HEADS, 1, TQ), F32),
                        pltpu.VMEM((N_HEADS, V_ROWS, TQ), F32), pltpu.VMEM((QK_LOOKAHEAD, TQ, TQ), F32)],
        compiler_params=_params(("arbitrary", "arbitrary"), VMEM_LIMIT),
        name="moba_attention")(q, k, v, kmean, bb, far)


def _mem_kernel(q_ref, kv_ref, o_ref, qh_ref):
    n_sub = q_ref.shape[0] // TQ
    for sub in range(n_sub):
        _store_head_queries(q_ref.at[sub * TQ:(sub + 1) * TQ, :], qh_ref.at[sub], MEM_HEADS // 2)
    outs = {}

    def logits(c, slot):
        sub, h = c
        return _dot(kv_ref[:, (h // 2) * LANES:(h // 2 + 1) * LANES], qh_ref[sub, h])

    def update(c, s, slot):
        h = c[1]
        lanes = slice(MEM_W + (h // 2) * LANES, MEM_W + (h // 2 + 1) * LANES)
        v_t = kv_ref[:, lanes].astype(F32).T[(h % 2) * HEAD_DIM:(h % 2 + 1) * HEAD_DIM].astype(BF16)
        e = jnp.exp2(s - jnp.max(s, axis=0, keepdims=True))
        pr = e / jnp.sum(e, axis=0, keepdims=True)
        outs[c] = _dot(v_t, pr.astype(BF16))

    _run_chains([(sub, h) for sub in range(n_sub) for h in range(MEM_HEADS)], logits, update)
    for sub in range(n_sub):
        for pp in range(MEM_HEADS // 2):
            pair = jnp.concatenate([outs[(sub, 2 * pp)], outs[(sub, 2 * pp + 1)]], axis=0)
            o_ref[sub * TQ:(sub + 1) * TQ, pp * LANES:(pp + 1) * LANES] = pair.T.astype(o_ref.dtype)


MEM_TILES = 8


def _mem_attention(qm, mem_kv):
    b, s, _ = qm.shape
    n_mem = mem_kv.shape[1]
    rows = MEM_TILES * TQ
    return pl.pallas_call(
        _mem_kernel, grid=(b, s // rows),
        in_specs=[pl.BlockSpec((None, rows, MEM_W), lambda bi, i: (bi, i, 0)),
                  pl.BlockSpec((None, n_mem, 2 * MEM_W), lambda bi, i: (bi, 0, 0))],
        out_specs=pl.BlockSpec((None, rows, MEM_W), lambda bi, i: (bi, i, 0)),
        out_shape=jax.ShapeDtypeStruct((b, s, MEM_W), BF16),
        scratch_shapes=[pltpu.VMEM((MEM_TILES, MEM_HEADS, LANES, TQ), BF16)],
        compiler_params=_params(("arbitrary", "arbitrary")), name="mem_attention")(qm, mem_kv)


def _layer_norm(y, g, b):
    mu = jnp.mean(y, axis=-1, keepdims=True)
    var = jnp.mean(jnp.square(y - mu), axis=-1, keepdims=True)
    return (y - mu) * lax.rsqrt(var + LN_EPS) * g + b


def _mm_ln_kernel(*refs, n_in):
    a_refs, w_refs = refs[:n_in], refs[n_in:2 * n_in]
    x_ref, g_ref, b_ref, o_ref, y_ref = refs[2 * n_in:]

    @pl.when(pl.program_id(0) == 0)
    def _():
        y_ref[...] = jnp.zeros(y_ref.shape, F32)

    o_ref[...] = _layer_norm(ALPHA * x_ref[...] + y_ref[...], g_ref[...], b_ref[...])
    acc = _dot(a_refs[0][...], w_refs[0][...])
    for a_ref, w_ref in zip(a_refs[1:], w_refs[1:]):
        acc = acc + _dot(a_ref[...], w_ref[...])
    y_ref[...] = acc


def _mm_ln(acts, weights, x, g, b, tm, name):
    m, d = x.shape
    n_t = m // tm
    cur = lambda t: (jnp.minimum(t, n_t - 1), 0)
    prev = lambda t: (jnp.maximum(t - 1, 0), 0)
    full = lambda t: (0, 0)
    return pl.pallas_call(
        functools.partial(_mm_ln_kernel, n_in=len(acts)), grid=(n_t + 1,),
        in_specs=[pl.BlockSpec((tm, a.shape[1]), cur) for a in acts]
                 + [pl.BlockSpec(w.shape, full) for w in weights]
                 + [pl.BlockSpec((tm, d), prev), pl.BlockSpec((1, d), full), pl.BlockSpec((1, d), full)],
        out_specs=pl.BlockSpec((tm, d), prev),
        out_shape=jax.ShapeDtypeStruct((m, d), F32),
        scratch_shapes=[pltpu.VMEM((tm, d), F32)],
        compiler_params=_params(("arbitrary",), VMEM_LIMIT), name=name)(*acts, *weights, x, g, b)


FF_CHUNK = 256
FF_ROWS = 512
SUBLANES = 8


def _gelu_tanh(x):
    k = -2.0 * math.sqrt(2.0 / math.pi) * LOG2E
    w = x * (x * x * (0.044715 * k) + k)
    return x / (1.0 + jnp.exp2(w))


def _ffn_kernel(x_ref, wa_ref, wb_ref, cw_ref, cb_ref, wo_ref, wol_ref, g_ref, b_ref, o_ref, xb_ref, hid_ref, *,
                n_c):
    c = pl.program_id(1)

    @pl.when(c == 0)
    def _():
        xb_ref[...] = x_ref[...].astype(BF16)
        hid_ref[1] = jnp.zeros(hid_ref.shape[1:], BF16)
        o_ref[...] = jnp.zeros(o_ref.shape, F32)

    xb = xb_ref[...]
    a = _dot(xb, wa_ref[...])
    o_ref[...] += _dot(hid_ref[(c + 1) % 2], wo_ref[...])
    gate = _dot(xb, wb_ref[...])
    ext = jnp.concatenate([jnp.zeros((SUBLANES, a.shape[1]), F32), a], axis=0)
    a1 = pltpu.roll(ext, 1, 0)[SUBLANES:]
    a2 = pltpu.roll(ext, 2, 0)[SUBLANES:]
    conv = cw_ref[0:1, :] * a2 + cw_ref[1:2, :] * a1 + cw_ref[2:3, :] * a + cb_ref[...]
    hid_ref[c % 2] = (_gelu_tanh(conv) * gate).astype(BF16)

    @pl.when(c == n_c - 1)
    def _():
        for r0 in range(0, o_ref.shape[0], FF_ROWS):
            rows = slice(r0, r0 + FF_ROWS)
            y = o_ref[rows, :] + _dot(hid_ref[(n_c - 1) % 2, rows, :], wol_ref[...])
            o_ref[rows, :] = _layer_norm(ALPHA * x_ref[rows, :] + y, g_ref[...], b_ref[...])


def _ffn_ln(x, w_in, conv_w, conv_b, w_out, g, b):
    bsz, s, d = x.shape
    n_c = D_FF // FF_CHUNK
    prev = lambda c: jnp.maximum(c - 1, 0)
    return pl.pallas_call(
        functools.partial(_ffn_kernel, n_c=n_c), grid=(bsz, n_c),
        in_specs=[pl.BlockSpec((None, s, d), lambda bi, c: (bi, 0, 0)),
                  pl.BlockSpec((d, FF_CHUNK), lambda bi, c: (0, c)),
                  pl.BlockSpec((d, FF_CHUNK), lambda bi, c: (0, n_c + c)),
                  pl.BlockSpec((CONV_W, FF_CHUNK), lambda bi, c: (0, c)),
                  pl.BlockSpec((1, FF_CHUNK), lambda bi, c: (0, c)),
                  pl.BlockSpec((FF_CHUNK, d), lambda bi, c: (prev(c), 0)),
                  pl.BlockSpec((FF_CHUNK, d), lambda bi, c: (n_c - 1, 0)),
                  pl.BlockSpec((1, d), lambda bi, c: (0, 0)), pl.BlockSpec((1, d), lambda bi, c: (0, 0))],
        out_specs=pl.BlockSpec((None, s, d), lambda bi, c: (bi, 0, 0)),
        out_shape=jax.ShapeDtypeStruct((bsz, s, d), F32),
        scratch_shapes=[pltpu.VMEM((s, d), BF16), pltpu.VMEM((2, s, FF_CHUNK), BF16)],
        compiler_params=_params(("arbitrary", "arbitrary"), VMEM_LIMIT),
        name="ffn_ln")(x, w_in, w_in, conv_w, conv_b, w_out, w_out, g, b)


def _dup(w):
    return jnp.concatenate([w, w], axis=1)


def _cmp_weights(pe_k, w1_k, w2_k, pe_v, w1_v, w2_v):
    half = CMP_LEN // 2
    zk = jnp.zeros((half, HEAD_DIM, CMP_HIDDEN), F32)

    def w1_half(sl):
        wk = w1_k.reshape(CMP_LEN, HEAD_DIM, CMP_HIDDEN)[sl]
        wv = w1_v.reshape(CMP_LEN, HEAD_DIM, CMP_HIDDEN)[sl]
        top = jnp.concatenate([wk, zk], axis=2)
        bot = jnp.concatenate([zk, wv], axis=2)
        return jnp.concatenate([top, bot], axis=1).reshape(half * 2 * HEAD_DIM, 2 * CMP_HIDDEN)

    w1 = jnp.stack([w1_half(slice(0, half)), w1_half(slice(half, CMP_LEN))]).astype(BF16)
    pe = jnp.concatenate([pe_k, pe_v], axis=1)
    pe = jnp.stack([pe[:half].reshape(-1), pe[half:].reshape(-1)])
    zo = jnp.zeros((CMP_HIDDEN, 2 * HEAD_DIM), F32)
    w2 = jnp.concatenate([jnp.concatenate([_dup(w2_k), zo], axis=1),
                          jnp.concatenate([zo, _dup(w2_v)], axis=1)], axis=0).astype(BF16)
    return pe, w1, w2


def _overlap_matrix(seq):
    n_cmp = (seq - CMP_LEN) // CMP_STRIDE + 1
    n_slc = seq // SLC_BLOCK
    start = np.arange(LANES) * CMP_STRIDE
    bs = np.arange(LANES) * SLC_BLOCK
    ovl_t = ((start[None, :] < bs[:, None] + SLC_BLOCK) & (start[None, :] + CMP_LEN > bs[:, None])
             & (np.arange(LANES)[None, :] < n_cmp) & (np.arange(LANES)[:, None] < n_slc))
    return jnp.asarray(ovl_t, BF16)


def kernel(x, mem, rel_bias, a_w_in, a_cmp_pe_k, a_cmp_w1_k, a_cmp_w2_k, a_cmp_pe_v, a_cmp_w1_v, a_cmp_w2_v,
           a_w_mem_kv, a_w_out, shared_w_kv, b_w_in, b_w_mem_kv, b_w_out, ln1_g, ln1_b, ln2_g, ln2_b,
           ffn_w_in, ffn_conv_w, ffn_conv_b, ffn_w_out):
    bsz, seq, d = x.shape
    n_mem = mem.shape[1]
    m = bsz * seq
    assert (seq, d) == (2048, D_MODEL) and seq % TQ == 0
    n_qt = seq // TQ

    bb, bc, far = _bias_tables(rel_bias, n_qt)
    ovl_t = _overlap_matrix(seq)
    memf = mem.reshape(bsz * n_mem, d)
    xf = x.reshape(m, d)

    def ffn(xcur, layer):
        return _ffn_ln(xcur.reshape(bsz, seq, d), ffn_w_in[layer].astype(BF16), ffn_conv_w[layer],
                       ffn_conv_b[layer][None, :], ffn_w_out[layer].astype(BF16),
                       ln2_g[layer][None, :], ln2_b[layer][None, :]).reshape(m, d)

    def out_ln(o_main, o_mem, w_out, xcur, layer):
        w = w_out.astype(BF16)
        return _mm_ln([o_main.reshape(m, MAIN_W), o_mem.reshape(m, MEM_W)], [w[:MAIN_W], w[MAIN_W:]], xcur,
                      ln1_g[layer][None, :], ln1_b[layer][None, :], ROW_TILE, "outproj_ln")

    w = a_w_in[0]
    c0 = MAIN_W
    cols = [w[:, c0 + kk * HEAD_DIM:c0 + (kk + 1) * HEAD_DIM] for kk in range(6)]
    c_g = c0 + 6 * HEAD_DIM
    w_g = jnp.pad(w[:, c_g:c_g + 3 * N_HEADS], ((0, 0), (0, LANES - 3 * N_HEADS)))
    w_qm = w[:, c_g + 3 * N_HEADS:]
    weights = [w[:, :MAIN_W], jnp.concatenate(cols[0:2], axis=1), _dup(cols[2]), _dup(cols[3]),
               _dup(cols[4]), _dup(cols[5]), w_g, w_qm]
    weights = [wi.astype(BF16) for wi in weights]
    q, kvc_tok, ks2, vs2, kw2, vw2, g, qm = _proj(
        xf, weights, [BF16, BF16, BF16, BF16, BF16, BF16, F32, BF16], 2 * ROW_TILE, "proj_a",
        scales=[Q_SCALE, None, None, None, None, None, None, Q_SCALE])

    pe, w1, w2 = _cmp_weights(a_cmp_pe_k[0], a_cmp_w1_k[0], a_cmp_w2_k[0],
                              a_cmp_pe_v[0], a_cmp_w1_v[0], a_cmp_w2_v[0])
    kvc = _compress(kvc_tok.reshape(bsz, seq // (CMP_LEN // 2), (CMP_LEN // 2) * LANES), pe, w1, w2)

    r3 = lambda t: t.reshape(bsz, seq, t.shape[-1])
    o_main = _nsa_attention(r3(q), r3(g), kvc, r3(ks2), r3(vs2), r3(kw2), r3(vw2), bb, far, bc, ovl_t)
    (mkv,) = _proj(memf, [a_w_mem_kv[0].astype(BF16)], [BF16], 512, "proj_mem_a")
    o_mem = _mem_attention(r3(qm), mkv.reshape(bsz, n_mem, 2 * MEM_W))
    x1 = out_ln(o_main, o_mem, a_w_out[0], xf, 0)
    x1 = ffn(x1, 0)

    wb = b_w_in[0].astype(BF16)
    wkv = shared_w_kv.astype(BF16)
    q, qm, k, v, kmean = _proj_b(x1, wb[:, :MAIN_W], wb[:, MAIN_W:], wkv[:, :MAIN_W], wkv[:, MAIN_W:])
    n_blk = seq // MOBA_BLOCK
    kmean = jnp.pad(kmean.reshape(bsz, n_blk, MAIN_W), ((0, 0), (0, LANES - n_blk), (0, 0)))
    o_main = _moba_attention(r3(q), r3(k), r3(v), kmean, bb, far)
    (mkv,) = _proj(memf, [b_w_mem_kv[0].astype(BF16)], [BF16], 512, "proj_mem_b")
    o_mem = _mem_attention(r3(qm), mkv.reshape(bsz, n_mem, 2 * MEM_W))
    x2 = out_ln(o_main, o_mem, b_w_out[0], x1, 1)
    x2 = ffn(x2, 1)
    return x2.reshape(bsz, seq, d)
```
